```python
import math
import jax, jax.numpy as jnp
from jax import lax
import numpy as np

D_MODEL = 1024
BATCH = 32
SEQ = 2048
DEPTH = 2

CTX_LEN = 256
GRID_W = 64
F32 = jnp.float32
EPS = 1e-6

N_MOD = 9
D_FF = 256 * ((8 * D_MODEL // 3 + 255) // 256)
A_INNER = D_MODEL // 2
A_HEADS = 8
A_HEAD_DIM = A_INNER // A_HEADS
A_GROUPS = 2
A_STATE = 64
A_CONV = 5
A_CONV_DIM = A_INNER + 2 * A_GROUPS * A_STATE
A_COLS = A_INNER + A_CONV_DIM + 2 * A_HEADS
SSD_CHUNK = 64
B_WIDTH = D_MODEL // 4
B_GROUP = 16
B_NGROUPS = B_WIDTH // B_GROUP
B_STATE = 64
B_COLS = B_WIDTH
C_WIDTH = D_MODEL // 4
C_HEADS = 4
C_KEY = C_WIDTH // C_HEADS
C_VAL = C_WIDTH // C_HEADS
C_COLS = 5 * C_WIDTH
HG_CHUNK = 64

MIX_WIDTH = A_INNER + B_WIDTH + C_WIDTH
IN_COLS = A_COLS + B_COLS + C_COLS

kernel_name = "hybrid_ssd_s5_hgrn2_dit_block"


def rms_norm(x):
    xf = x.astype(F32)
    return (xf * lax.rsqrt(jnp.mean(xf * xf, axis=-1, keepdims=True) + EPS)).astype(x.dtype)


def modulate(h, shift, scale):
    return rms_norm(h) * (1 + scale) + shift


def swiglu(u, w_in, w_out):
    gate, up = jnp.split(u @ w_in, 2, axis=-1)
    return (jax.nn.silu(gate) * up) @ w_out


def masked_exp(diff, mask):
    return jnp.where(mask, jnp.exp(jnp.where(mask, diff, 0.0)), 0.0)


def _flip(t):
    return jnp.flip(t, axis=1)


def raster_to_column(t, rows):
    b, s, d = t.shape
    return t.reshape(b, rows, GRID_W, d).transpose(0, 2, 1, 3).reshape(b, s, d)


def column_to_raster(t, rows):
    b, s, d = t.shape
    return t.reshape(b, GRID_W, rows, d).transpose(0, 2, 1, 3).reshape(b, s, d)


def depthwise_conv(x, w, b):
    pad = A_CONV // 2
    y = lax.conv_general_dilated(x, w[:, None, :].astype(x.dtype), window_strides=(1,),
                                 padding=[(pad, pad)], dimension_numbers=('NWC', 'WIO', 'NWC'),
                                 feature_group_count=x.shape[-1])
    return y + b.astype(x.dtype)


def ssd_chunked(x, dt, a, bm, cm, s0):
    bsz, L, H, P = x.shape
    N = bm.shape[-1]
    Q = SSD_CHUNK
    nc = L // Q
    x = x.reshape(bsz, nc, Q, H, P)
    dt = dt.reshape(bsz, nc, Q, H)
    bm = bm.reshape(bsz, nc, Q, H, N)
    cm = cm.reshape(bsz, nc, Q, H, N)
    cum = jnp.cumsum(dt * a, axis=2)
    tri = jnp.tril(jnp.ones((Q, Q), dtype=bool))[None, None, :, :, None]
    decay = masked_exp(cum[:, :, :, None, :] - cum[:, :, None, :, :], tri)
    scores = jnp.einsum('bcihn,bcjhn->bcijh', cm, bm) * decay * dt[:, :, None, :, :]
    y_intra = jnp.einsum('bcijh,bcjhp->bcihp', scores, x)
    w_end = jnp.exp(cum[:, :, -1:, :] - cum) * dt
    chunk_states = jnp.einsum('bcjh,bcjhn,bcjhp->bchpn', w_end, bm, x)
    chunk_decay = jnp.exp(cum[:, :, -1, :])

    def step(s, inp):
        dec, cs = inp
        return s * dec[:, :, None, None] + cs, s

    s_fin, s_in = lax.scan(step, s0, (jnp.moveaxis(chunk_decay, 1, 0), jnp.moveaxis(chunk_states, 1, 0)))
    s_in = jnp.moveaxis(s_in, 0, 1)
    y_inter = jnp.einsum('bcihn,bchpn->bcihp', cm, s_in) * jnp.exp(cum)[..., None]
    return (y_intra + y_inter).reshape(bsz, L, H, P), s_fin


def mamba2_mixer(p, conv_w, conv_b, dt_bias, a_log, d_skip, norm_w, init):
    bsz, L, _ = p.shape
    z, xbc, dt_raw = jnp.split(p, [A_INNER, A_INNER + A_CONV_DIM], axis=-1)
    xbc = jax.nn.silu(depthwise_conv(xbc, conv_w, conv_b))
    xs, bm, cm = jnp.split(xbc, [A_INNER, A_INNER + A_GROUPS * A_STATE], axis=-1)
    rep = A_HEADS // A_GROUPS
    xs = xs.astype(F32).reshape(bsz, L, A_HEADS, A_HEAD_DIM)
    bm = jnp.repeat(bm.astype(F32).reshape(bsz, L, A_GROUPS, A_STATE), rep, axis=2)
    cm = jnp.repeat(cm.astype(F32).reshape(bsz, L, A_GROUPS, A_STATE), rep, axis=2)
    dt = jax.nn.softplus(dt_raw.astype(F32).reshape(bsz, L, 2, A_HEADS) + dt_bias.astype(F32))
    a = -jnp.exp(a_log.astype(F32))
    y_f, s_f = ssd_chunked(xs, dt[:, :, 0], a[0], bm, cm, init[0])
    y_b, s_b = ssd_chunked(_flip(xs), _flip(dt[:, :, 1]), a[1], _flip(bm), _flip(cm), init[1])
    y = y_f + _flip(y_b) + d_skip.astype(F32)[:, None] * xs
    y = y.reshape(bsz, L, A_INNER) * jax.nn.silu(z.astype(F32))
    y = rms_norm(y) * norm_w.astype(F32)
    return y.astype(p.dtype), jnp.stack([s_f, s_b])


def _complex_affine_combine(e1, e2):
    a1r, a1i, b1r, b1i = e1
    a2r, a2i, b2r, b2i = e2
    return (a1r * a2r - a1i * a2i, a1r * a2i + a1i * a2r,
            a2r * b1r - a2i * b1i + b2r, a2r * b1i + a2i * b1r + b2i)


def s5_scan(u, lam_re, lam_im, log_step, b_re, b_im, s0_re, s0_im):
    L = u.shape[1]
    lam_re = lam_re.astype(F32)
    lam_im = lam_im.astype(F32)
    step = jnp.exp(log_step.astype(F32))[:, None]
    mag = jnp.exp(lam_re * step)
    ar = mag * jnp.cos(lam_im * step)
    ai = mag * jnp.sin(lam_im * step)
    den = lam_re * lam_re + lam_im * lam_im
    nr = ar - 1.0
    kr = (nr * lam_re + ai * lam_im) / den
    ki = (ai * lam_re - nr * lam_im) / den
    b_re = b_re.astype(F32)
    b_im = b_im.astype(F32)
    br = kr[..., None] * b_re - ki[..., None] * b_im
    bi = kr[..., None] * b_im + ki[..., None] * b_re
    vr = jnp.einsum('gnc,blgc->blgn', br, u)
    vi = jnp.einsum('gnc,blgc->blgn', bi, u)
    vr = vr.at[:, 0].add(ar * s0_re - ai * s0_im)
    vi = vi.at[:, 0].add(ar * s0_im + ai * s0_re)
    shape_a = (1, L) + ar.shape
    elems = (jnp.broadcast_to(ar, shape_a), jnp.broadcast_to(ai, shape_a), vr, vi)
    _, _, xr, xi = lax.associative_scan(_complex_affine_combine, elems, axis=1)
    return xr, xi


def s5_mixer(p, lam_re, lam_im, log_step, b_re, b_im, c_re, c_im, d_skip, glu_w, glu_b, init_re, init_im):
    bsz, L, _ = p.shape
    pf = p.astype(F32)
    u = pf.reshape(bsz, L, B_NGROUPS, B_GROUP)
    c_re = c_re.astype(F32)
    c_im = c_im.astype(F32)
    xr_f, xi_f = s5_scan(u, lam_re[0], lam_im[0], log_step[0], b_re[0], b_im[0], init_re[0], init_im[0])
    xr_b, xi_b = s5_scan(_flip(u), lam_re[1], lam_im[1], log_step[1], b_re[1], b_im[1], init_re[1], init_im[1])
    y_f = jnp.einsum('gcn,blgn->blgc', c_re[0], xr_f) - jnp.einsum('gcn,blgn->blgc', c_im[0], xi_f)
    y_b = jnp.einsum('gcn,blgn->blgc', c_re[1], xr_b) - jnp.einsum('gcn,blgn->blgc', c_im[1], xi_b)
    y = (y_f + _flip(y_b)).reshape(bsz, L, B_WIDTH) + d_skip.astype(F32) * pf
    y = jax.nn.gelu(y)
    y = y * jax.nn.sigmoid(y @ glu_w.astype(F32) + glu_b.astype(F32))
    fin_re = jnp.stack([xr_f[:, -1], xr_b[:, -1]])
    fin_im = jnp.stack([xi_f[:, -1], xi_b[:, -1]])
    return y.astype(p.dtype), fin_re, fin_im


def hgrn2_chunked(q, log_f, k, v, s0):
    bsz, L, H, K = q.shape
    V = v.shape[-1]
    nc = L // HG_CHUNK

    def chunks(t):
        return jnp.moveaxis(t.reshape((bsz, nc, HG_CHUNK) + t.shape[2:]), 1, 0)

    tri = jnp.tril(jnp.ones((HG_CHUNK, HG_CHUNK), dtype=bool))[None, :, :, None, None]

    def step(s, inp):
        qc, lfc, kc, vc = inp
        cum = jnp.cumsum(lfc, axis=1)
        decay = masked_exp(cum[:, :, None] - cum[:, None, :], tri)
        scores = jnp.einsum('bihk,bjhk,bijhk->bijh', qc, kc, decay)
        o = jnp.einsum('bijh,bjhv->bihv', scores, vc) + jnp.einsum('bihk,bhkv->bihv', qc * jnp.exp(cum), s)
        w = kc * jnp.exp(cum[:, -1:] - cum)
        s_new = s * jnp.exp(cum[:, -1])[..., None] + jnp.einsum('bjhk,bjhv->bhkv', w, vc)
        return s_new, o

    s_fin, o = lax.scan(step, s0, (chunks(q), chunks(log_f), chunks(k), chunks(v)))
    return jnp.moveaxis(o, 0, 1).reshape(bsz, L, H, V), s_fin


def hgrn2_mixer(p, lower, norm_w, init):
    bsz, L, _ = p.shape
    q, f_raw, i, g = jnp.split(p, [C_WIDTH, 3 * C_WIDTH, 4 * C_WIDTH], axis=-1)
    q = jax.nn.silu(q.astype(F32)).reshape(bsz, L, C_HEADS, C_KEY)
    v = i.astype(F32).reshape(bsz, L, C_HEADS, C_VAL)
    f_raw = f_raw.astype(F32).reshape(bsz, L, 2, C_HEADS, C_KEY)
    lower = lower.astype(F32).reshape(2, C_HEADS, C_KEY)
    outs, finals = [], []
    for d in range(2):
        zf = f_raw[:, :, d]
        lb = lower[d]
        f = lb + (1.0 - lb) * jax.nn.sigmoid(zf)
        log_f = jnp.log(f)
        k = 1.0 - f
        if d == 0:
            o, s = hgrn2_chunked(q, log_f, k, v, init[0])
        else:
            o, s = hgrn2_chunked(_flip(q), _flip(log_f), _flip(k), _flip(v), init[1])
            o = _flip(o)
        outs.append(o)
        finals.append(s)
    o = rms_norm(outs[0] + outs[1]) * norm_w.astype(F32).reshape(C_HEADS, C_VAL)
    o = o.reshape(bsz, L, C_WIDTH) * jax.nn.silu(g.astype(F32))
    return o.astype(p.dtype), jnp.stack(finals)


def token_mixers(p_ctx, p_lat, conv_w, conv_b, dt_bias, a_log, a_d, a_norm_w,
                 lam_re, lam_im, log_step, b_re, b_im, c_re, c_im, s5_d, glu_w, glu_b,
                 lower, hg_norm_w):
    bsz = p_ctx.shape[0]
    cuts = [A_COLS, A_COLS + B_COLS]
    pa_c, pb_c, pc_c = jnp.split(p_ctx, cuts, axis=-1)
    pa_l, pb_l, pc_l = jnp.split(p_lat, cuts, axis=-1)
    za = jnp.zeros((2, bsz, A_HEADS, A_HEAD_DIM, A_STATE), F32)
    ya_c, sa = mamba2_mixer(pa_c, conv_w, conv_b, dt_bias, a_log, a_d, a_norm_w, za)
    ya_l, _ = mamba2_mixer(pa_l, conv_w, conv_b, dt_bias, a_log, a_d, a_norm_w, sa)
    zb = jnp.zeros((2, bsz, B_NGROUPS, B_STATE), F32)
    yb_c, sb_re, sb_im = s5_mixer(pb_c, lam_re, lam_im, log_step, b_re, b_im, c_re, c_im, s5_d, glu_w, glu_b, zb, zb)
    yb_l, _, _ = s5_mixer(pb_l, lam_re, lam_im, log_step, b_re, b_im, c_re, c_im, s5_d, glu_w, glu_b, sb_re, sb_im)
    zc = jnp.zeros((2, bsz, C_HEADS, C_KEY, C_VAL), F32)
    yc_c, sc = hgrn2_mixer(pc_c, lower, hg_norm_w, zc)
    yc_l, _ = hgrn2_mixer(pc_l, lower, hg_norm_w, sc)
    return (jnp.concatenate([ya_c, yb_c, yc_c], axis=-1), jnp.concatenate([ya_l, yb_l, yc_l], axis=-1))


def _fwd_setup_inputs(seed: int = 0) -> dict:
    key = jax.random.key(seed)
    ks = jax.random.split(key, 32)
    dm = D_MODEL

    def nrm(k, shape, scale):
        return scale * jax.random.normal(k, shape, F32)

    x = nrm(ks[0], (BATCH, SEQ, dm), 1.0)
    c = nrm(ks[1], (BATCH, dm), 1.0)
    ctx = nrm(ks[2], (BATCH, CTX_LEN, dm), 1.0)
    c_ctx = nrm(ks[3], (dm,), 1.0)
    mod_w = nrm(ks[4], (DEPTH, dm, N_MOD * dm), dm ** -0.5)
    mod_b = nrm(ks[5], (DEPTH, N_MOD * dm), 0.01)
    ffn_w_in = nrm(ks[6], (DEPTH, 2, dm, 2 * D_FF), dm ** -0.5)
    ffn_w_out = nrm(ks[7], (DEPTH, 2, D_FF, dm), D_FF ** -0.5)
    w_in = nrm(ks[8], (DEPTH, dm, IN_COLS), dm ** -0.5)
    w_out = nrm(ks[9], (DEPTH, MIX_WIDTH, dm), MIX_WIDTH ** -0.5)
    a_conv_w = nrm(ks[10], (DEPTH, A_CONV, A_CONV_DIM), A_CONV ** -0.5)
    a_conv_b = nrm(ks[11], (DEPTH, A_CONV_DIM), 0.01)
    dt0 = jnp.exp(jax.random.uniform(ks[12], (DEPTH, 2, A_HEADS), F32, math.log(1e-3), math.log(1e-1)))
    a_dt_bias = dt0 + jnp.log(-jnp.expm1(-dt0))
    a_log = jnp.log(jax.random.uniform(ks[13], (DEPTH, 2, A_HEADS), F32, 1.0, 16.0))
    a_d = 1.0 + nrm(ks[14], (DEPTH, A_HEADS), 0.01)
    a_norm_w = 1.0 + nrm(ks[15], (DEPTH, A_INNER), 0.01)
    s5_lam_re = -0.5 + nrm(ks[16], (DEPTH, 2, B_NGROUPS, B_STATE), 0.01)
    s5_lam_im = math.pi * jnp.arange(B_STATE, dtype=F32) + nrm(ks[17], (DEPTH, 2, B_NGROUPS, B_STATE), 0.01)
    s5_log_step = jax.random.uniform(ks[18], (DEPTH, 2, B_NGROUPS), F32, math.log(1e-3), math.log(1e-1))
    s5_b_re = nrm(ks[19], (DEPTH, 2, B_NGROUPS, B_STATE, B_GROUP), (2 * B_GROUP) ** -0.5)
    s5_b_im = nrm(ks[20], (DEPTH, 2, B_NGROUPS, B_STATE, B_GROUP), (2 * B_GROUP) ** -0.5)
    s5_c_re = nrm(ks[21], (DEPTH, 2, B_NGROUPS, B_GROUP, B_STATE), (2 * B_STATE) ** -0.5)
    s5_c_im = nrm(ks[22], (DEPTH, 2, B_NGROUPS, B_GROUP, B_STATE), (2 * B_STATE) ** -0.5)
    s5_d = nrm(ks[23], (DEPTH, B_WIDTH), 1.0)
    s5_glu_w = nrm(ks[24], (DEPTH, B_WIDTH, B_WIDTH), B_WIDTH ** -0.5)
    s5_glu_b = nrm(ks[25], (DEPTH, B_WIDTH), 0.01)
    hg_lb_logits = nrm(ks[26], (DEPTH, 2, C_WIDTH), 0.5)
    hg_norm_w = 1.0 + nrm(ks[27], (DEPTH, C_WIDTH), 0.01)
    final_norm_w = 1.0 + nrm(ks[28], (dm,), 0.01)
    return {"x": x, "c": c, "ctx": ctx, "c_ctx": c_ctx, "mod_w": mod_w, "mod_b": mod_b,
            "ffn_w_in": ffn_w_in, "ffn_w_out": ffn_w_out, "w_in": w_in, "w_out": w_out,
            "a_conv_w": a_conv_w, "a_conv_b": a_conv_b, "a_dt_bias": a_dt_bias, "a_log": a_log,
            "a_d": a_d, "a_norm_w": a_norm_w, "s5_lam_re": s5_lam_re, "s5_lam_im": s5_lam_im,
            "s5_log_step": s5_log_step, "s5_b_re": s5_b_re, "s5_b_im": s5_b_im, "s5_c_re": s5_c_re,
            "s5_c_im": s5_c_im, "s5_d": s5_d, "s5_glu_w": s5_glu_w, "s5_glu_b": s5_glu_b,
            "hg_lb_logits": hg_lb_logits, "hg_norm_w": hg_norm_w, "final_norm_w": final_norm_w}


def _fwd_reference(x, c, ctx, c_ctx, mod_w, mod_b, ffn_w_in, ffn_w_out, w_in, w_out,
              a_conv_w, a_conv_b, a_dt_bias, a_log, a_d, a_norm_w,
              s5_lam_re, s5_lam_im, s5_log_step, s5_b_re, s5_b_im, s5_c_re, s5_c_im,
              s5_d, s5_glu_w, s5_glu_b, hg_lb_logits, hg_norm_w, final_norm_w):
    bsz, seq, dm = x.shape
    rows = seq // GRID_W
    p_lb = jax.nn.softmax(hg_lb_logits.astype(F32), axis=0)
    lower_bounds = jnp.cumsum(p_lb, axis=0) - p_lb[:1]
    h_lat, h_ctx = x, ctx
    for l in range(DEPTH):
        last = l == DEPTH - 1
        col_major = l % 2 == 1
        m_lat = (jax.nn.silu(c) @ mod_w[l] + mod_b[l]).reshape(bsz, N_MOD, 1, dm)
        m_ctx = (jax.nn.silu(c_ctx) @ mod_w[l] + mod_b[l]).reshape(N_MOD, dm)
        h_lat = h_lat + 0.5 * m_lat[:, 2] * swiglu(modulate(h_lat, m_lat[:, 0], m_lat[:, 1]), ffn_w_in[l, 0], ffn_w_out[l, 0])
        h_ctx = h_ctx + 0.5 * m_ctx[2] * swiglu(modulate(h_ctx, m_ctx[0], m_ctx[1]), ffn_w_in[l, 0], ffn_w_out[l, 0])
        u_lat = modulate(h_lat, m_lat[:, 3], m_lat[:, 4])
        u_ctx = modulate(h_ctx, m_ctx[3], m_ctx[4])
        if col_major:
            u_lat = raster_to_column(u_lat, rows)
        mix_ctx, mix_lat = token_mixers(u_ctx @ w_in[l], u_lat @ w_in[l],
                                        a_conv_w[l], a_conv_b[l], a_dt_bias[l], a_log[l], a_d[l], a_norm_w[l],
                                        s5_lam_re[l], s5_lam_im[l], s5_log_step[l], s5_b_re[l], s5_b_im[l],
                                        s5_c_re[l], s5_c_im[l], s5_d[l], s5_glu_w[l], s5_glu_b[l],
                                        lower_bounds[l], hg_norm_w[l])
        y_lat = mix_lat @ w_out[l]
        if col_major:
            y_lat = column_to_raster(y_lat, rows)
        h_lat = h_lat + m_lat[:, 5] * y_lat
        h_lat = h_lat + 0.5 * m_lat[:, 8] * swiglu(modulate(h_lat, m_lat[:, 6], m_lat[:, 7]), ffn_w_in[l, 1], ffn_w_out[l, 1])
        if not last:
            h_ctx = h_ctx + m_ctx[5] * (mix_ctx @ w_out[l])
            h_ctx = h_ctx + 0.5 * m_ctx[8] * swiglu(modulate(h_ctx, m_ctx[6], m_ctx[7]), ffn_w_in[l, 1], ffn_w_out[l, 1])
    return rms_norm(h_lat) * final_norm_w


import jax as _jax
import jax.numpy as _jnp

TWIN_FORMAT = 'train_step'
FWD_PARAMS = ['x', 'c', 'ctx', 'c_ctx', 'mod_w', 'mod_b', 'ffn_w_in', 'ffn_w_out', 'w_in', 'w_out', 'a_conv_w', 'a_conv_b', 'a_dt_bias', 'a_log', 'a_d', 'a_norm_w', 's5_lam_re', 's5_lam_im', 's5_log_step', 's5_b_re', 's5_b_im', 's5_c_re', 's5_c_im', 's5_d', 's5_glu_w', 's5_glu_b', 'hg_lb_logits', 'hg_norm_w', 'final_norm_w']
TWIN_WEIGHTS = ['c_ctx', 'mod_w', 'mod_b', 'ffn_w_in', 'ffn_w_out', 'w_in', 'w_out', 'a_conv_w', 'a_conv_b', 'a_dt_bias', 'a_log', 'a_d', 'a_norm_w', 's5_lam_re', 's5_lam_im', 's5_log_step', 's5_b_re', 's5_b_im', 's5_c_re', 's5_c_im', 's5_d', 's5_glu_w', 's5_glu_b', 'hg_lb_logits', 'hg_norm_w', 'final_norm_w']
TWIN_DIFF_INPUT = 'x'
TWIN_INPUTS = ['x', 'c', 'ctx', 'c_ctx', 'mod_w', 'mod_b', 'ffn_w_in', 'ffn_w_out', 'w_in', 'w_out', 'a_conv_w', 'a_conv_b', 'a_dt_bias', 'a_log', 'a_d', 'a_norm_w', 's5_lam_re', 's5_lam_im', 's5_log_step', 's5_b_re', 's5_b_im', 's5_c_re', 's5_c_im', 's5_d', 's5_glu_w', 's5_glu_b', 'hg_lb_logits', 'hg_norm_w', 'final_norm_w', 'loss_target', 'm_c_ctx', 'm_mod_w', 'm_mod_b', 'm_ffn_w_in', 'm_ffn_w_out', 'm_w_in', 'm_w_out', 'm_a_conv_w', 'm_a_conv_b', 'm_a_dt_bias', 'm_a_log', 'm_a_d', 'm_a_norm_w', 'm_s5_lam_re', 'm_s5_lam_im', 'm_s5_log_step', 'm_s5_b_re', 'm_s5_b_im', 'm_s5_c_re', 'm_s5_c_im', 'm_s5_d', 'm_s5_glu_w', 'm_s5_glu_b', 'm_hg_lb_logits', 'm_hg_norm_w', 'm_final_norm_w', 'v_c_ctx', 'v_mod_w', 'v_mod_b', 'v_ffn_w_in', 'v_ffn_w_out', 'v_w_in', 'v_w_out', 'v_a_conv_w', 'v_a_conv_b', 'v_a_dt_bias', 'v_a_log', 'v_a_d', 'v_a_norm_w', 'v_s5_lam_re', 'v_s5_lam_im', 'v_s5_log_step', 'v_s5_b_re', 'v_s5_b_im', 'v_s5_c_re', 'v_s5_c_im', 'v_s5_d', 'v_s5_glu_w', 'v_s5_glu_b', 'v_hg_lb_logits', 'v_hg_norm_w', 'v_final_norm_w']
TWIN_OUTPUTS = ['loss', 'grad_x', 'grad_c_ctx', 'grad_mod_w', 'grad_mod_b', 'grad_ffn_w_in', 'grad_ffn_w_out', 'grad_w_in', 'grad_w_out', 'grad_a_conv_w', 'grad_a_conv_b', 'grad_a_dt_bias', 'grad_a_log', 'grad_a_d', 'grad_a_norm_w', 'grad_s5_lam_re', 'grad_s5_lam_im', 'grad_s5_log_step', 'grad_s5_b_re', 'grad_s5_b_im', 'grad_s5_c_re', 'grad_s5_c_im', 'grad_s5_d', 'grad_s5_glu_w', 'grad_s5_glu_b', 'grad_hg_lb_logits', 'grad_hg_norm_w', 'grad_final_norm_w', 'delta_c_ctx', 'delta_mod_w', 'delta_mod_b', 'delta_ffn_w_in', 'delta_ffn_w_out', 'delta_w_in', 'delta_w_out', 'delta_a_conv_w', 'delta_a_conv_b', 'delta_a_dt_bias', 'delta_a_log', 'delta_a_d', 'delta_a_norm_w', 'delta_s5_lam_re', 'delta_s5_lam_im', 'delta_s5_log_step', 'delta_s5_b_re', 'delta_s5_b_im', 'delta_s5_c_re', 'delta_s5_c_im', 'delta_s5_d', 'delta_s5_glu_w', 'delta_s5_glu_b', 'delta_hg_lb_logits', 'delta_hg_norm_w', 'delta_final_norm_w', 'new_m_c_ctx', 'new_m_mod_w', 'new_m_mod_b', 'new_m_ffn_w_in', 'new_m_ffn_w_out', 'new_m_w_in', 'new_m_w_out', 'new_m_a_conv_w', 'new_m_a_conv_b', 'new_m_a_dt_bias', 'new_m_a_log', 'new_m_a_d', 'new_m_a_norm_w', 'new_m_s5_lam_re', 'new_m_s5_lam_im', 'new_m_s5_log_step', 'new_m_s5_b_re', 'new_m_s5_b_im', 'new_m_s5_c_re', 'new_m_s5_c_im', 'new_m_s5_d', 'new_m_s5_glu_w', 'new_m_s5_glu_b', 'new_m_hg_lb_logits', 'new_m_hg_norm_w', 'new_m_final_norm_w', 'new_v_c_ctx', 'new_v_mod_w', 'new_v_mod_b', 'new_v_ffn_w_in', 'new_v_ffn_w_out', 'new_v_w_in', 'new_v_w_out', 'new_v_a_conv_w', 'new_v_a_conv_b', 'new_v_a_dt_bias', 'new_v_a_log', 'new_v_a_d', 'new_v_a_norm_w', 'new_v_s5_lam_re', 'new_v_s5_lam_im', 'new_v_s5_log_step', 'new_v_s5_b_re', 'new_v_s5_b_im', 'new_v_s5_c_re', 'new_v_s5_c_im', 'new_v_s5_d', 'new_v_s5_glu_w', 'new_v_s5_glu_b', 'new_v_hg_lb_logits', 'new_v_hg_norm_w', 'new_v_final_norm_w']
TWIN_LEAF_KINDS = {'loss': 'loss', 'grad_x': 'grad_x', 'grad_c_ctx': 'grad_w', 'grad_mod_w': 'grad_w', 'grad_mod_b': 'grad_w', 'grad_ffn_w_in': 'grad_w', 'grad_ffn_w_out': 'grad_w', 'grad_w_in': 'grad_w', 'grad_w_out': 'grad_w', 'grad_a_conv_w': 'grad_w', 'grad_a_conv_b': 'grad_w', 'grad_a_dt_bias': 'grad_w', 'grad_a_log': 'grad_w', 'grad_a_d': 'grad_w', 'grad_a_norm_w': 'grad_w', 'grad_s5_lam_re': 'grad_w', 'grad_s5_lam_im': 'grad_w', 'grad_s5_log_step': 'grad_w', 'grad_s5_b_re': 'grad_w', 'grad_s5_b_im': 'grad_w', 'grad_s5_c_re': 'grad_w', 'grad_s5_c_im': 'grad_w', 'grad_s5_d': 'grad_w', 'grad_s5_glu_w': 'grad_w', 'grad_s5_glu_b': 'grad_w', 'grad_hg_lb_logits': 'grad_w', 'grad_hg_norm_w': 'grad_w', 'grad_final_norm_w': 'grad_w', 'delta_c_ctx': 'delta_w', 'delta_mod_w': 'delta_w', 'delta_mod_b': 'delta_w', 'delta_ffn_w_in': 'delta_w', 'delta_ffn_w_out': 'delta_w', 'delta_w_in': 'delta_w', 'delta_w_out': 'delta_w', 'delta_a_conv_w': 'delta_w', 'delta_a_conv_b': 'delta_w', 'delta_a_dt_bias': 'delta_w', 'delta_a_log': 'delta_w', 'delta_a_d': 'delta_w', 'delta_a_norm_w': 'delta_w', 'delta_s5_lam_re': 'delta_w', 'delta_s5_lam_im': 'delta_w', 'delta_s5_log_step': 'delta_w', 'delta_s5_b_re': 'delta_w', 'delta_s5_b_im': 'delta_w', 'delta_s5_c_re': 'delta_w', 'delta_s5_c_im': 'delta_w', 'delta_s5_d': 'delta_w', 'delta_s5_glu_w': 'delta_w', 'delta_s5_glu_b': 'delta_w', 'delta_hg_lb_logits': 'delta_w', 'delta_hg_norm_w': 'delta_w', 'delta_final_norm_w': 'delta_w', 'new_m_c_ctx': 'new_m', 'new_m_mod_w': 'new_m', 'new_m_mod_b': 'new_m', 'new_m_ffn_w_in': 'new_m', 'new_m_ffn_w_out': 'new_m', 'new_m_w_in': 'new_m', 'new_m_w_out': 'new_m', 'new_m_a_conv_w': 'new_m', 'new_m_a_conv_b': 'new_m', 'new_m_a_dt_bias': 'new_m', 'new_m_a_log': 'new_m', 'new_m_a_d': 'new_m', 'new_m_a_norm_w': 'new_m', 'new_m_s5_lam_re': 'new_m', 'new_m_s5_lam_im': 'new_m', 'new_m_s5_log_step': 'new_m', 'new_m_s5_b_re': 'new_m', 'new_m_s5_b_im': 'new_m', 'new_m_s5_c_re': 'new_m', 'new_m_s5_c_im': 'new_m', 'new_m_s5_d': 'new_m', 'new_m_s5_glu_w': 'new_m', 'new_m_s5_glu_b': 'new_m', 'new_m_hg_lb_logits': 'new_m', 'new_m_hg_norm_w': 'new_m', 'new_m_final_norm_w': 'new_m', 'new_v_c_ctx': 'new_v', 'new_v_mod_w': 'new_v', 'new_v_mod_b': 'new_v', 'new_v_ffn_w_in': 'new_v', 'new_v_ffn_w_out': 'new_v', 'new_v_w_in': 'new_v', 'new_v_w_out': 'new_v', 'new_v_a_conv_w': 'new_v', 'new_v_a_conv_b': 'new_v', 'new_v_a_dt_bias': 'new_v', 'new_v_a_log': 'new_v', 'new_v_a_d': 'new_v', 'new_v_a_norm_w': 'new_v', 'new_v_s5_lam_re': 'new_v', 'new_v_s5_lam_im': 'new_v', 'new_v_s5_log_step': 'new_v', 'new_v_s5_b_re': 'new_v', 'new_v_s5_b_im': 'new_v', 'new_v_s5_c_re': 'new_v', 'new_v_s5_c_im': 'new_v', 'new_v_s5_d': 'new_v', 'new_v_s5_glu_w': 'new_v', 'new_v_s5_glu_b': 'new_v', 'new_v_hg_lb_logits': 'new_v', 'new_v_hg_norm_w': 'new_v', 'new_v_final_norm_w': 'new_v'}


def _forward(args):
    return _fwd_reference(*[args[k] for k in FWD_PARAMS])


def _output_shape():
    out = _jax.eval_shape(lambda: _forward(_fwd_setup_inputs(0)))
    return out.shape, out.dtype

N_MICROBATCH = 1
ADAM_LR = 0.001
ADAM_B1 = 0.9
ADAM_B2 = 0.999
ADAM_EPS = 1e-08
ADAM_WD = 0.01
ADAM_STEP = 10
PER_EXAMPLE_BATCH_AXIS = {'x': 0, 'c': 0, 'ctx': 0, 'loss_target': 0}
SHARED_INPUTS = []
_WEIGHT_DTYPES = {'c_ctx': _jnp.float32, 'mod_w': _jnp.float32, 'mod_b': _jnp.float32, 'ffn_w_in': _jnp.float32, 'ffn_w_out': _jnp.float32, 'w_in': _jnp.float32, 'w_out': _jnp.float32, 'a_conv_w': _jnp.float32, 'a_conv_b': _jnp.float32, 'a_dt_bias': _jnp.float32, 'a_log': _jnp.float32, 'a_d': _jnp.float32, 'a_norm_w': _jnp.float32, 's5_lam_re': _jnp.float32, 's5_lam_im': _jnp.float32, 's5_log_step': _jnp.float32, 's5_b_re': _jnp.float32, 's5_b_im': _jnp.float32, 's5_c_re': _jnp.float32, 's5_c_im': _jnp.float32, 's5_d': _jnp.float32, 's5_glu_w': _jnp.float32, 's5_glu_b': _jnp.float32, 'hg_lb_logits': _jnp.float32, 'hg_norm_w': _jnp.float32, 'final_norm_w': _jnp.float32}
MOMENT_SCALE = {'c_ctx': 1.630211e-02, 'mod_w': 7.802807e-02, 'mod_b': 1.306243e-01, 'ffn_w_in': 4.634071e-02, 'ffn_w_out': 7.593358e-02, 'w_in': 1.077990e-01, 'w_out': 1.136084e-01, 'a_conv_w': 1.281480e-01, 'a_conv_b': 1.546843e-01, 'a_dt_bias': 1.419068e-01, 'a_log': 9.541766e-01, 'a_d': 6.068323e-01, 'a_norm_w': 1.639538e-01, 's5_lam_re': 7.917223e-03, 's5_lam_im': 9.852266e-03, 's5_log_step': 2.233444e+00, 's5_b_re': 3.707198e-03, 's5_b_im': 4.536951e-03, 's5_c_re': 8.454905e-03, 's5_c_im': 8.686113e-03, 's5_d': 7.099687e-02, 's5_glu_w': 2.778775e-02, 's5_glu_b': 2.703917e-02, 'hg_lb_logits': 3.811888e-03, 'hg_norm_w': 1.121288e-01, 'final_norm_w': 6.511725e+01}


def _to_microbatches(a, axis):
    t = _jnp.moveaxis(a, axis, 0)
    t = t.reshape((N_MICROBATCH, t.shape[0] // N_MICROBATCH) + t.shape[1:])
    return _jnp.moveaxis(t, 1, axis + 1)


def setup_inputs(seed: int = 0) -> dict:
    inp = _fwd_setup_inputs(seed)
    key = _jax.random.fold_in(_jax.random.key(seed), 7919)
    shape, _ = _output_shape()
    out = dict(inp)
    out["loss_target"] = _jax.random.normal(_jax.random.fold_in(key, 0), shape, _jnp.float32)
    for i, name in enumerate(TWIN_WEIGHTS):
        w = inp[name].astype(_jnp.float32)
        if MOMENT_SCALE is None:
            s = _jnp.sqrt(_jnp.mean(_jnp.square(w)) + 1e-30)
        else:
            s = MOMENT_SCALE[name]
        km, kv = _jax.random.split(_jax.random.fold_in(key, i + 1))
        out[name] = w
        out["m_" + name] = s * _jax.random.normal(km, w.shape, _jnp.float32)
        out["v_" + name] = (s * s) * _jax.random.uniform(kv, w.shape, _jnp.float32, 0.5, 1.5)
    if N_MICROBATCH > 1:
        for name, axis in PER_EXAMPLE_BATCH_AXIS.items():
            out[name] = _to_microbatches(out[name], axis)
    return {'x': out['x'], 'c': out['c'], 'ctx': out['ctx'], 'c_ctx': out['c_ctx'], 'mod_w': out['mod_w'], 'mod_b': out['mod_b'], 'ffn_w_in': out['ffn_w_in'], 'ffn_w_out': out['ffn_w_out'], 'w_in': out['w_in'], 'w_out': out['w_out'], 'a_conv_w': out['a_conv_w'], 'a_conv_b': out['a_conv_b'], 'a_dt_bias': out['a_dt_bias'], 'a_log': out['a_log'], 'a_d': out['a_d'], 'a_norm_w': out['a_norm_w'], 's5_lam_re': out['s5_lam_re'], 's5_lam_im': out['s5_lam_im'], 's5_log_step': out['s5_log_step'], 's5_b_re': out['s5_b_re'], 's5_b_im': out['s5_b_im'], 's5_c_re': out['s5_c_re'], 's5_c_im': out['s5_c_im'], 's5_d': out['s5_d'], 's5_glu_w': out['s5_glu_w'], 's5_glu_b': out['s5_glu_b'], 'hg_lb_logits': out['hg_lb_logits'], 'hg_norm_w': out['hg_norm_w'], 'final_norm_w': out['final_norm_w'], 'loss_target': out['loss_target'], 'm_c_ctx': out['m_c_ctx'], 'm_mod_w': out['m_mod_w'], 'm_mod_b': out['m_mod_b'], 'm_ffn_w_in': out['m_ffn_w_in'], 'm_ffn_w_out': out['m_ffn_w_out'], 'm_w_in': out['m_w_in'], 'm_w_out': out['m_w_out'], 'm_a_conv_w': out['m_a_conv_w'], 'm_a_conv_b': out['m_a_conv_b'], 'm_a_dt_bias': out['m_a_dt_bias'], 'm_a_log': out['m_a_log'], 'm_a_d': out['m_a_d'], 'm_a_norm_w': out['m_a_norm_w'], 'm_s5_lam_re': out['m_s5_lam_re'], 'm_s5_lam_im': out['m_s5_lam_im'], 'm_s5_log_step': out['m_s5_log_step'], 'm_s5_b_re': out['m_s5_b_re'], 'm_s5_b_im': out['m_s5_b_im'], 'm_s5_c_re': out['m_s5_c_re'], 'm_s5_c_im': out['m_s5_c_im'], 'm_s5_d': out['m_s5_d'], 'm_s5_glu_w': out['m_s5_glu_w'], 'm_s5_glu_b': out['m_s5_glu_b'], 'm_hg_lb_logits': out['m_hg_lb_logits'], 'm_hg_norm_w': out['m_hg_norm_w'], 'm_final_norm_w': out['m_final_norm_w'], 'v_c_ctx': out['v_c_ctx'], 'v_mod_w': out['v_mod_w'], 'v_mod_b': out['v_mod_b'], 'v_ffn_w_in': out['v_ffn_w_in'], 'v_ffn_w_out': out['v_ffn_w_out'], 'v_w_in': out['v_w_in'], 'v_w_out': out['v_w_out'], 'v_a_conv_w': out['v_a_conv_w'], 'v_a_conv_b': out['v_a_conv_b'], 'v_a_dt_bias': out['v_a_dt_bias'], 'v_a_log': out['v_a_log'], 'v_a_d': out['v_a_d'], 'v_a_norm_w': out['v_a_norm_w'], 'v_s5_lam_re': out['v_s5_lam_re'], 'v_s5_lam_im': out['v_s5_lam_im'], 'v_s5_log_step': out['v_s5_log_step'], 'v_s5_b_re': out['v_s5_b_re'], 'v_s5_b_im': out['v_s5_b_im'], 'v_s5_c_re': out['v_s5_c_re'], 'v_s5_c_im': out['v_s5_c_im'], 'v_s5_d': out['v_s5_d'], 'v_s5_glu_w': out['v_s5_glu_w'], 'v_s5_glu_b': out['v_s5_glu_b'], 'v_hg_lb_logits': out['v_hg_lb_logits'], 'v_hg_norm_w': out['v_hg_norm_w'], 'v_final_norm_w': out['v_final_norm_w']}


def _loss(weights, diff, rest, loss_target):
    with _jax.named_scope("forward"):
        args = {**rest, TWIN_DIFF_INPUT: diff, **{k: w.astype(_WEIGHT_DTYPES[k]) for k, w in weights.items()}}
        y = _forward(args)
    with _jax.named_scope("loss_head"):
        err = _jnp.square(y.astype(_jnp.float32) - loss_target)
        return 0.5 * _jnp.sum(_jnp.mean(err, axis=-1)) if err.ndim else 0.5 * err


def _adamw(w, g, m, v):
    m = ADAM_B1 * m + (1.0 - ADAM_B1) * g
    v = ADAM_B2 * v + (1.0 - ADAM_B2) * _jnp.square(g)
    m_hat = m / (1.0 - ADAM_B1 ** ADAM_STEP)
    v_hat = v / (1.0 - ADAM_B2 ** ADAM_STEP)
    delta = -ADAM_LR * (m_hat / (_jnp.sqrt(v_hat) + ADAM_EPS) + ADAM_WD * w)
    return delta, m, v


def reference(x, c, ctx, c_ctx, mod_w, mod_b, ffn_w_in, ffn_w_out, w_in, w_out, a_conv_w, a_conv_b, a_dt_bias, a_log, a_d, a_norm_w, s5_lam_re, s5_lam_im, s5_log_step, s5_b_re, s5_b_im, s5_c_re, s5_c_im, s5_d, s5_glu_w, s5_glu_b, hg_lb_logits, hg_norm_w, final_norm_w, loss_target, m_c_ctx, m_mod_w, m_mod_b, m_ffn_w_in, m_ffn_w_out, m_w_in, m_w_out, m_a_conv_w, m_a_conv_b, m_a_dt_bias, m_a_log, m_a_d, m_a_norm_w, m_s5_lam_re, m_s5_lam_im, m_s5_log_step, m_s5_b_re, m_s5_b_im, m_s5_c_re, m_s5_c_im, m_s5_d, m_s5_glu_w, m_s5_glu_b, m_hg_lb_logits, m_hg_norm_w, m_final_norm_w, v_c_ctx, v_mod_w, v_mod_b, v_ffn_w_in, v_ffn_w_out, v_w_in, v_w_out, v_a_conv_w, v_a_conv_b, v_a_dt_bias, v_a_log, v_a_d, v_a_norm_w, v_s5_lam_re, v_s5_lam_im, v_s5_log_step, v_s5_b_re, v_s5_b_im, v_s5_c_re, v_s5_c_im, v_s5_d, v_s5_glu_w, v_s5_glu_b, v_hg_lb_logits, v_hg_norm_w, v_final_norm_w):
    given = dict(x=x, c=c, ctx=ctx, c_ctx=c_ctx, mod_w=mod_w, mod_b=mod_b, ffn_w_in=ffn_w_in, ffn_w_out=ffn_w_out, w_in=w_in, w_out=w_out, a_conv_w=a_conv_w, a_conv_b=a_conv_b, a_dt_bias=a_dt_bias, a_log=a_log, a_d=a_d, a_norm_w=a_norm_w, s5_lam_re=s5_lam_re, s5_lam_im=s5_lam_im, s5_log_step=s5_log_step, s5_b_re=s5_b_re, s5_b_im=s5_b_im, s5_c_re=s5_c_re, s5_c_im=s5_c_im, s5_d=s5_d, s5_glu_w=s5_glu_w, s5_glu_b=s5_glu_b, hg_lb_logits=hg_lb_logits, hg_norm_w=hg_norm_w, final_norm_w=final_norm_w, loss_target=loss_target, m_c_ctx=m_c_ctx, m_mod_w=m_mod_w, m_mod_b=m_mod_b, m_ffn_w_in=m_ffn_w_in, m_ffn_w_out=m_ffn_w_out, m_w_in=m_w_in, m_w_out=m_w_out, m_a_conv_w=m_a_conv_w, m_a_conv_b=m_a_conv_b, m_a_dt_bias=m_a_dt_bias, m_a_log=m_a_log, m_a_d=m_a_d, m_a_norm_w=m_a_norm_w, m_s5_lam_re=m_s5_lam_re, m_s5_lam_im=m_s5_lam_im, m_s5_log_step=m_s5_log_step, m_s5_b_re=m_s5_b_re, m_s5_b_im=m_s5_b_im, m_s5_c_re=m_s5_c_re, m_s5_c_im=m_s5_c_im, m_s5_d=m_s5_d, m_s5_glu_w=m_s5_glu_w, m_s5_glu_b=m_s5_glu_b, m_hg_lb_logits=m_hg_lb_logits, m_hg_norm_w=m_hg_norm_w, m_final_norm_w=m_final_norm_w, v_c_ctx=v_c_ctx, v_mod_w=v_mod_w, v_mod_b=v_mod_b, v_ffn_w_in=v_ffn_w_in, v_ffn_w_out=v_ffn_w_out, v_w_in=v_w_in, v_w_out=v_w_out, v_a_conv_w=v_a_conv_w, v_a_conv_b=v_a_conv_b, v_a_dt_bias=v_a_dt_bias, v_a_log=v_a_log, v_a_d=v_a_d, v_a_norm_w=v_a_norm_w, v_s5_lam_re=v_s5_lam_re, v_s5_lam_im=v_s5_lam_im, v_s5_log_step=v_s5_log_step, v_s5_b_re=v_s5_b_re, v_s5_b_im=v_s5_b_im, v_s5_c_re=v_s5_c_re, v_s5_c_im=v_s5_c_im, v_s5_d=v_s5_d, v_s5_glu_w=v_s5_glu_w, v_s5_glu_b=v_s5_glu_b, v_hg_lb_logits=v_hg_lb_logits, v_hg_norm_w=v_hg_norm_w, v_final_norm_w=v_final_norm_w)
    weights = {n: given[n] for n in TWIN_WEIGHTS}
    shared = {n: given[n] for n in SHARED_INPUTS}
    per_example = {n: given[n] for n in ['x', 'c', 'ctx']}
    grad_fn = _jax.value_and_grad(_loss, argnums=(0, 1))

    def one_microbatch(ex, loss_target):
        ex = dict(ex)
        diff = ex.pop(TWIN_DIFF_INPUT)
        return grad_fn(weights, diff, {**shared, **ex}, loss_target)

    if N_MICROBATCH == 1:
        loss, (grad_w, grad_x) = one_microbatch(per_example, given["loss_target"])
    else:
        def body(carry, xs):
            loss_sum, grad_sum = carry
            l_k, (gw_k, gx_k) = one_microbatch(xs[0], xs[1])
            with _jax.named_scope("update"):
                return (loss_sum + l_k, _jax.tree.map(_jnp.add, grad_sum, gw_k)), gx_k

        init = (_jnp.zeros((), _jnp.float32), _jax.tree.map(_jnp.zeros_like, weights))
        (loss, grad_w), grad_x = _jax.lax.scan(body, init, (per_example, given["loss_target"]))
    with _jax.named_scope("update"):
        delta_w, new_m, new_v = {}, {}, {}
        for n in TWIN_WEIGHTS:
            delta_w[n], new_m[n], new_v[n] = _adamw(weights[n], grad_w[n], given["m_" + n], given["v_" + n])
    return (loss, grad_x, *[grad_w[n] for n in TWIN_WEIGHTS], *[delta_w[n] for n in TWIN_WEIGHTS],
            *[new_m[n] for n in TWIN_WEIGHTS], *[new_v[n] for n in TWIN_WEIGHTS])
```

```python
import functools
import math

import jax
import jax.numpy as jnp
from jax import lax
from jax.experimental import pallas as pl
from jax.experimental.pallas import tpu as pltpu

F32 = jnp.float32
BF16 = jnp.bfloat16
MESH_AXES = ("x", "y", "c")
N_DEV = 8
MESH_ID = pl.DeviceIdType.MESH
VMEM_LIMIT = 48 * 1024 * 1024

D_MODEL = 1024
DEPTH = 2
GRID_W = 64
EPS = 1e-6
N_MOD = 9
D_FF = 2816
A_INNER = 512
A_HEADS = 8
A_HEAD_DIM = 64
A_GROUPS = 2
A_STATE = 64
A_CONV = 5
A_CONV_DIM = A_INNER + 2 * A_GROUPS * A_STATE
A_COLS = A_INNER + A_CONV_DIM + 2 * A_HEADS
B_WIDTH = 256
B_GROUP = 16
B_NGROUPS = 16
B_STATE = 64
B_COLS = B_WIDTH
C_WIDTH = 256
C_HEADS = 4
C_KEY = 64
C_VAL = 64
C_COLS = 5 * C_WIDTH
IN_PAD = 3072

ADAM_LR = 0.001
ADAM_B1 = 0.9
ADAM_B2 = 0.999
ADAM_EPS = 1e-08
ADAM_WD = 0.01
ADAM_STEP = 10

WEIGHTS = ['c_ctx', 'mod_w', 'mod_b', 'ffn_w_in', 'ffn_w_out', 'w_in', 'w_out', 'a_conv_w', 'a_conv_b', 'a_dt_bias',
           'a_log', 'a_d', 'a_norm_w', 's5_lam_re', 's5_lam_im', 's5_log_step', 's5_b_re', 's5_b_im', 's5_c_re',
           's5_c_im', 's5_d', 's5_glu_w', 's5_glu_b', 'hg_lb_logits', 'hg_norm_w', 'final_norm_w']
BIG = {'ffn_w_in': 3, 'ffn_w_out': 2, 'w_in': 2, 'w_out': 1}
SMALL_SHARDED = {'a_conv_w': 2, 's5_glu_w': 1, 'hg_lb_logits': 2}
SMALL = [n for n in WEIGHTS if n not in BIG and n != 'mod_w']


def _tile(d, prefs):
    for p in prefs:
        if d % p == 0:
            return p
    return d


def _pack(arrs, width, row_mult):
    flat = jnp.concatenate([a.reshape(-1) for a in arrs])
    pad = (-flat.shape[0]) % (width * row_mult)
    if pad:
        flat = jnp.concatenate([flat, jnp.zeros((pad,), flat.dtype)])
    return flat.reshape(-1, width)


def _pack8(arrs, width, row_mult):
    flat = jnp.concatenate([a.reshape(N_DEV, -1) for a in arrs], axis=1)
    pad = (-flat.shape[1]) % (width * row_mult)
    if pad:
        flat = jnp.concatenate([flat, jnp.zeros((N_DEV, pad), flat.dtype)], axis=1)
    return flat.reshape(N_DEV, -1, width)


def _unpack(buf, shapes):
    lead = buf.shape[:-2]
    flat = buf.reshape(lead + (-1,))
    out, off = [], 0
    for s in shapes:
        n = math.prod(s)
        out.append(flat[..., off:off + n].reshape(lead + tuple(s)))
        off += n
    return out


def _assemble(g, axis):
    t = jnp.moveaxis(g, 0, axis)
    s = t.shape
    return t.reshape(s[:axis] + (s[axis] * s[axis + 1],) + s[axis + 2:])


def _split(full, axis):
    s = full.shape
    t = full.reshape(s[:axis] + (N_DEV, s[axis] // N_DEV) + s[axis + 1:])
    return jnp.moveaxis(t, axis, 0)


def all_gather(x, name):
    def body(x_ref, out_ref, send_sems, recv_sems, local_sem):
        ax, ay, ac = lax.axis_index("x"), lax.axis_index("y"), lax.axis_index("c")
        me, sibling = (ax, ay, ac), (ax, ay, 1 - ac)
        chips = [(1 - ax, ay), (ax, 1 - ay), (1 - ax, 1 - ay)]

        def slot(px, py, pc):
            return out_ref.at[4 * px + 2 * py + pc]

        def copy(k, block, to, src=None):
            return pltpu.make_async_remote_copy(
                src_ref=slot(*block) if src is None else src, dst_ref=slot(*block),
                send_sem=send_sems.at[k], recv_sem=recv_sems.at[k], device_id=to, device_id_type=MESH_ID)

        mine = pltpu.make_async_copy(x_ref, slot(*me), local_sem)
        mine.start()
        first = [copy(0, me, sibling, src=x_ref)]
        first += [copy(1 + j, me, (*chip, ac), src=x_ref) for j, chip in enumerate(chips)]
        for cp in first:
            cp.start()
        passed = [copy(4 + j, (*chip, ac), sibling) for j, chip in enumerate(chips)]
        for j, chip in enumerate(chips):
            copy(1 + j, (*chip, ac), me).wait_recv()
            passed[j].start()
        copy(0, sibling, me).wait_recv()
        for j, chip in enumerate(chips):
            copy(4 + j, (*chip, 1 - ac), me).wait_recv()
        for cp in first + passed:
            cp.wait_send()
        mine.wait()

    return pl.pallas_call(
        body, name=name,
        out_shape=jax.ShapeDtypeStruct((N_DEV,) + x.shape, x.dtype),
        in_specs=[pl.BlockSpec(memory_space=pl.ANY)],
        out_specs=pl.BlockSpec(memory_space=pl.ANY),
        scratch_shapes=[pltpu.SemaphoreType.DMA((7,)), pltpu.SemaphoreType.DMA((7,)), pltpu.SemaphoreType.DMA(())],
    )(x)


def all_to_all(g, name):
    def body(g_ref, out_ref, send_sems, recv_sems, local_sem):
        ax, ay, ac = lax.axis_index("x"), lax.axis_index("y"), lax.axis_index("c")
        my = 4 * ax + 2 * ay + ac
        local = pltpu.make_async_copy(g_ref.at[my], out_ref.at[my], local_sem)
        local.start()
        peers = []
        for r in range(1, N_DEV):
            px = 1 - ax if r & 4 else ax
            py = 1 - ay if r & 2 else ay
            pc = 1 - ac if r & 1 else ac
            peers.append((px, py, pc))

        def copy(k, peer):
            return pltpu.make_async_remote_copy(
                src_ref=g_ref.at[4 * peer[0] + 2 * peer[1] + peer[2]], dst_ref=out_ref.at[my],
                send_sem=send_sems.at[k], recv_sem=recv_sems.at[k], device_id=peer, device_id_type=MESH_ID)

        def arrival(k, peer):
            slot = 4 * peer[0] + 2 * peer[1] + peer[2]
            return pltpu.make_async_remote_copy(
                src_ref=g_ref.at[slot], dst_ref=out_ref.at[slot],
                send_sem=send_sems.at[k], recv_sem=recv_sems.at[k], device_id=peer, device_id_type=MESH_ID)

        sends = [copy(k, p) for k, p in enumerate(peers)]
        for cp in sends:
            cp.start()
        for k, p in enumerate(peers):
            arrival(k, p).wait_recv()
        for cp in sends:
            cp.wait_send()
        local.wait()

    return pl.pallas_call(
        body, name=name,
        out_shape=jax.ShapeDtypeStruct(g.shape, g.dtype),
        in_specs=[pl.BlockSpec(memory_space=pl.ANY)],
        out_specs=pl.BlockSpec(memory_space=pl.ANY),
        scratch_shapes=[pltpu.SemaphoreType.DMA((7,)), pltpu.SemaphoreType.DMA((7,)), pltpu.SemaphoreType.DMA(())],
    )(g)


def matmul(a, b, *, ta=False, tb=False, out_dtype=F32, name="mm"):
    m, k = (a.shape[1], a.shape[0]) if ta else a.shape
    n = b.shape[0] if tb else b.shape[1]
    assert (b.shape[1] if tb else b.shape[0]) == k, (a.shape, b.shape, ta, tb)
    tm = _tile(m, (512, 256, 128))
    tn = _tile(n, (1408, 1024, 512, 384, 256, 128))
    tk = _tile(k, (1024, 512, 256, 128))
    nk = k // tk
    dims = (((0 if ta else 1,), (1 if tb else 0,)), ((), ()))

    def body(a_ref, b_ref, o_ref, acc_ref):
        step = pl.program_id(2)

        @pl.when(step == 0)
        def _():
            acc_ref[...] = jnp.zeros_like(acc_ref)

        acc_ref[...] += lax.dot_general(a_ref[...].astype(BF16), b_ref[...].astype(BF16), dims,
                                        preferred_element_type=F32)

        @pl.when(step == nk - 1)
        def _():
            o_ref[...] = acc_ref[...].astype(out_dtype)

    a_spec = pl.BlockSpec((tk, tm), lambda i, j, s: (s, i)) if ta else pl.BlockSpec((tm, tk), lambda i, j, s: (i, s))
    b_spec = pl.BlockSpec((tn, tk), lambda i, j, s: (j, s)) if tb else pl.BlockSpec((tk, tn), lambda i, j, s: (s, j))
    return pl.pallas_call(
        body, name=name,
        out_shape=jax.ShapeDtypeStruct((m, n), out_dtype),
        grid=(m // tm, n // tn, nk),
        in_specs=[a_spec, b_spec],
        out_specs=pl.BlockSpec((tm, tn), lambda i, j, s: (i, j)),
        scratch_shapes=[pltpu.VMEM((tm, tn), F32)],
        compiler_params=pltpu.CompilerParams(dimension_semantics=("parallel", "parallel", "arbitrary"),
                                             vmem_limit_bytes=VMEM_LIMIT),
    )(a, b)


@jax.custom_vjp
def mm(x, w):
    return matmul(x, w, name="mm_fwd")


def _mm_fwd(x, w):
    return matmul(x, w, name="mm_fwd"), (x, w)


def _mm_bwd(res, dy):
    x, w = res
    dx = matmul(dy, w, tb=True, out_dtype=x.dtype, name="mm_dx")
    dw = matmul(x, dy, ta=True, out_dtype=w.dtype, name="mm_dw")
    return dx, dw


mm.defvjp(_mm_fwd, _mm_bwd)


def sum_leading(x, name):
    n, r, c = x.shape
    tr = _tile(r, (256, 128, 64, 32, 16, 8))

    def body(x_ref, o_ref):
        acc = x_ref[0].astype(F32)
        for i in range(1, n):
            acc = acc + x_ref[i].astype(F32)
        o_ref[...] = acc

    return pl.pallas_call(
        body, name=name,
        out_shape=jax.ShapeDtypeStruct((r, c), F32),
        grid=(r // tr,),
        in_specs=[pl.BlockSpec((n, tr, c), lambda i: (0, i, 0))],
        out_specs=pl.BlockSpec((tr, c), lambda i: (i, 0)),
        compiler_params=pltpu.CompilerParams(dimension_semantics=("parallel",), vmem_limit_bytes=VMEM_LIMIT),
    )(x)


def adamw(w, g, m, v, name):
    r, c = w.shape
    tr = _tile(r, (256, 128, 64, 32, 16, 8))

    def body(w_ref, g_ref, m_ref, v_ref, d_ref, mo_ref, vo_ref):
        gv = g_ref[...]
        mv = ADAM_B1 * m_ref[...] + (1.0 - ADAM_B1) * gv
        vv = ADAM_B2 * v_ref[...] + (1.0 - ADAM_B2) * jnp.square(gv)
        m_hat = mv / (1.0 - ADAM_B1 ** ADAM_STEP)
        v_hat = vv / (1.0 - ADAM_B2 ** ADAM_STEP)
        d_ref[...] = -ADAM_LR * (m_hat / (jnp.sqrt(v_hat) + ADAM_EPS) + ADAM_WD * w_ref[...])
        mo_ref[...] = mv
        vo_ref[...] = vv

    spec = pl.BlockSpec((tr, c), lambda i: (i, 0))
    return pl.pallas_call(
        body, name=name,
        out_shape=(jax.ShapeDtypeStruct((r, c), F32),) * 3,
        grid=(r // tr,),
        in_specs=[spec] * 4,
        out_specs=(spec,) * 3,
        compiler_params=pltpu.CompilerParams(dimension_semantics=("parallel",), vmem_limit_bytes=VMEM_LIMIT),
    )(w, g, m, v)


ROW_TILE = 256


def _row_tile(t):
    return _tile(t, (ROW_TILE, 128, 64, 32, 16, 8))


def _group_call(body, name, ins, in_kinds, out_shapes, out_kinds, tt):
    g, t = ins[0].shape[:2]

    def spec(kind, shape):
        if kind == 'tok':
            return pl.BlockSpec((1, tt, shape[-1]), lambda i, j: (i, j, 0))
        return pl.BlockSpec((1, 1, shape[-1]), lambda i, j: (i, 0, 0))

    return pl.pallas_call(
        body, name=name,
        out_shape=tuple(jax.ShapeDtypeStruct(s, F32) for s in out_shapes),
        grid=(g, t // tt),
        in_specs=[spec(k, a.shape) for k, a in zip(in_kinds, ins)],
        out_specs=tuple(spec(k, s) for k, s in zip(out_kinds, out_shapes)),
        compiler_params=pltpu.CompilerParams(dimension_semantics=("parallel", "arbitrary"),
                                             vmem_limit_bytes=VMEM_LIMIT),
    )(*ins)


def _accumulate(ref, val):
    @pl.when(pl.program_id(1) == 0)
    def _():
        ref[...] = jnp.zeros_like(ref)

    ref[0] += jnp.sum(val, axis=0, keepdims=True)


@jax.custom_vjp
def modulate_g(h, shift, scale):
    return _modulate_fwd(h, shift, scale)


def _modulate_fwd(h, shift, scale):
    def body(h_ref, sh_ref, sc_ref, o_ref):
        hv = h_ref[0]
        r = lax.rsqrt(jnp.mean(hv * hv, axis=-1, keepdims=True) + EPS)
        o_ref[0] = hv * r * (1.0 + sc_ref[0]) + sh_ref[0]

    return _group_call(body, "modulate_fwd", [h, shift, scale], ['tok', 'vec', 'vec'], [h.shape], ['tok'],
                       _row_tile(h.shape[1]))[0]


def _modulate_vjp_fwd(h, shift, scale):
    return _modulate_fwd(h, shift, scale), (h, scale)


def _modulate_vjp_bwd(res, du):
    h, scale = res

    def body(h_ref, sc_ref, du_ref, dh_ref, dsh_ref, dsc_ref):
        hv, dv = h_ref[0], du_ref[0]
        r = lax.rsqrt(jnp.mean(hv * hv, axis=-1, keepdims=True) + EPS)
        hn = hv * r
        dn = dv * (1.0 + sc_ref[0])
        dh_ref[0] = r * (dn - hn * jnp.mean(dn * hn, axis=-1, keepdims=True))
        _accumulate(dsh_ref, dv)
        _accumulate(dsc_ref, dv * hn)

    dh, dsh, dsc = _group_call(body, "modulate_bwd", [h, scale, du], ['tok', 'vec', 'tok'],
                               [h.shape, scale.shape, scale.shape], ['tok', 'acc', 'acc'], _row_tile(h.shape[1]))
    return dh, dsh, dsc


modulate_g.defvjp(_modulate_vjp_fwd, _modulate_vjp_bwd)


def _gated_add_call(h, y, gate, coef):
    def body(h_ref, y_ref, g_ref, o_ref):
        o_ref[0] = h_ref[0] + coef * g_ref[0] * y_ref[0]

    return _group_call(body, "gated_add_fwd", [h, y, gate], ['tok', 'tok', 'vec'], [h.shape], ['tok'],
                       _row_tile(h.shape[1]))[0]


def _gated_add_bwd_call(y, gate, dout, coef):
    def body(y_ref, g_ref, d_ref, dy_ref, dg_ref):
        dv = d_ref[0]
        dy_ref[0] = coef * g_ref[0] * dv
        _accumulate(dg_ref, coef * dv * y_ref[0])

    return _group_call(body, "gated_add_bwd", [y, gate, dout], ['tok', 'vec', 'tok'], [y.shape, gate.shape],
                       ['tok', 'acc'], _row_tile(y.shape[1]))


@functools.partial(jax.custom_vjp, nondiff_argnums=(3,))
def gated_add(h, y, gate, coef):
    return _gated_add_call(h, y, gate, coef)


def _gated_add_vjp_fwd(h, y, gate, coef):
    return _gated_add_call(h, y, gate, coef), (y, gate)


def _gated_add_vjp_bwd(coef, res, dout):
    y, gate = res
    dy, dg = _gated_add_bwd_call(y, gate, dout, coef)
    return dout, dy, dg


gated_add.defvjp(_gated_add_vjp_fwd, _gated_add_vjp_bwd)


@jax.custom_vjp
def swiglu_act(hid):
    return _swiglu_act_fwd(hid)


def _swiglu_act_fwd(hid):
    f = hid.shape[-1] // 2

    def body(h_ref, o_ref):
        gate, up = h_ref[0, :, 0:f], h_ref[0, :, f:2 * f]
        o_ref[0] = gate * jax.nn.sigmoid(gate) * up

    return _group_call(body, "swiglu_fwd", [hid], ['tok'], [hid.shape[:2] + (f,)], ['tok'],
                       _tile(hid.shape[1], (128, 64, 32, 16, 8)))[0]


def _swiglu_vjp_fwd(hid):
    return _swiglu_act_fwd(hid), (hid,)


def _swiglu_vjp_bwd(res, da):
    (hid,) = res
    f = hid.shape[-1] // 2

    def body(h_ref, da_ref, d_ref):
        gate, up, dv = h_ref[0, :, 0:f], h_ref[0, :, f:2 * f], da_ref[0]
        s = jax.nn.sigmoid(gate)
        d_ref[0, :, 0:f] = dv * up * (s * (1.0 + gate * (1.0 - s)))
        d_ref[0, :, f:2 * f] = dv * (gate * s)

    return (_group_call(body, "swiglu_bwd", [hid, da], ['tok', 'tok'], [hid.shape], ['tok'],
                        _tile(hid.shape[1], (128, 64, 32, 16, 8)))[0],)


swiglu_act.defvjp(_swiglu_vjp_fwd, _swiglu_vjp_bwd)


@jax.custom_vjp
def flip_rows(x):
    return _flip_rows_call(x)


def _flip_rows_call(x):
    n, length, c = x.shape
    tb = _tile(length, (256, 128, 64, 32, 16, 8))
    nb = length // tb

    def body(x_ref, o_ref):
        xv = x_ref[0]
        ii = lax.broadcasted_iota(jnp.int32, (tb, tb), 0)
        jj = lax.broadcasted_iota(jnp.int32, (tb, tb), 1)
        rev = (ii + jj == tb - 1).astype(BF16)
        hi = xv.astype(BF16)
        r1 = xv - hi.astype(F32)
        mid = r1.astype(BF16)
        lo = (r1 - mid.astype(F32)).astype(BF16)
        dot = functools.partial(jnp.dot, preferred_element_type=F32)
        o_ref[0] = (dot(rev, hi) + dot(rev, mid)) + dot(rev, lo)

    return pl.pallas_call(
        body, name="flip_rows",
        out_shape=jax.ShapeDtypeStruct(x.shape, F32),
        grid=(n, nb),
        in_specs=[pl.BlockSpec((1, tb, c), lambda i, j: (i, j, 0))],
        out_specs=pl.BlockSpec((1, tb, c), lambda i, j: (i, nb - 1 - j, 0)),
        compiler_params=pltpu.CompilerParams(dimension_semantics=("parallel", "parallel"),
                                             vmem_limit_bytes=VMEM_LIMIT),
    )(x)


flip_rows.defvjp(lambda x: (_flip_rows_call(x), None), lambda _, dy: (_flip_rows_call(dy),))


def _flip_time(t, axis):
    s = t.shape
    lead = math.prod(s[:axis])
    return flip_rows(t.reshape(lead, s[axis], -1)).reshape(s)


def rms_norm(x):
    return x * lax.rsqrt(jnp.mean(x * x, axis=-1, keepdims=True) + EPS)


def raster_to_column(t, rows):
    b, s, d = t.shape
    return t.reshape(b, rows, GRID_W, d).transpose(0, 2, 1, 3).reshape(b, s, d)


def column_to_raster(t, rows):
    b, s, d = t.shape
    return t.reshape(b, GRID_W, rows, d).transpose(0, 2, 1, 3).reshape(b, s, d)


def depthwise_conv(x, w, b):
    pad = A_CONV // 2
    y = lax.conv_general_dilated(x, w[:, None, :], window_strides=(1,), padding=[(pad, pad)],
                                 dimension_numbers=('NWC', 'WIO', 'NWC'), feature_group_count=x.shape[-1])
    return y + b


S5_STATES = B_NGROUPS * B_STATE
S5_ROWS = 8
S5_STEPS_FWD = 64
S5_STEPS_BWD = 32


def _s5_scan_fwd(u2, bd2, cd2, ar8, ai8):
    rows, width = u2.shape
    ns = S5_STATES
    tr = S5_ROWS * S5_STEPS_FWD
    assert rows % tr == 0

    def body(u_ref, bd_ref, cd_ref, ar_ref, ai_ref, y_ref, x_ref, st_ref):
        @pl.when(pl.program_id(0) == 0)
        def _():
            st_ref[...] = jnp.zeros_like(st_ref)

        x_ref[...] = jnp.dot(u_ref[...].astype(BF16), bd_ref[...], preferred_element_type=F32)
        ar, ai = ar_ref[...], ai_ref[...]

        def step(t, carry):
            xr, xi = carry
            r = pl.ds(pl.multiple_of(t * S5_ROWS, S5_ROWS), S5_ROWS)
            nr = ar * xr - ai * xi + x_ref[r, 0:ns]
            ni = ar * xi + ai * xr + x_ref[r, ns:2 * ns]
            x_ref[r, 0:ns] = nr
            x_ref[r, ns:2 * ns] = ni
            return nr, ni

        xr, xi = lax.fori_loop(0, S5_STEPS_FWD, step, (st_ref[:, 0:ns], st_ref[:, ns:2 * ns]), unroll=4)
        st_ref[:, 0:ns] = xr
        st_ref[:, ns:2 * ns] = xi
        y_ref[...] = jnp.dot(x_ref[...].astype(BF16), cd_ref[...], preferred_element_type=F32)

    whole = lambda shape: pl.BlockSpec(shape, lambda i: (0, 0))
    return pl.pallas_call(
        body, name="s5_scan_fwd",
        out_shape=(jax.ShapeDtypeStruct((rows, width), F32), jax.ShapeDtypeStruct((rows, 2 * ns), F32)),
        grid=(rows // tr,),
        in_specs=[pl.BlockSpec((tr, width), lambda i: (i, 0)), whole(bd2.shape), whole(cd2.shape),
                  whole(ar8.shape), whole(ai8.shape)],
        out_specs=(pl.BlockSpec((tr, width), lambda i: (i, 0)), pl.BlockSpec((tr, 2 * ns), lambda i: (i, 0))),
        scratch_shapes=[pltpu.VMEM((S5_ROWS, 2 * ns), F32)],
        compiler_params=pltpu.CompilerParams(dimension_semantics=("arbitrary",), vmem_limit_bytes=VMEM_LIMIT),
    )(u2, bd2, cd2, ar8, ai8)


def _s5_scan_bwd(dy, x, u2, bd2, cd2, ar8, ai8):
    rows, width = u2.shape
    ns = S5_STATES
    steps = S5_STEPS_BWD
    tr = S5_ROWS * steps
    nblk = rows // tr
    assert rows % tr == 0
    nt = (((1,), (1,)), ((), ()))
    tn = (((0,), (0,)), ((), ()))

    def body(dy_ref, x_ref, xp_ref, u_ref, bd_ref, cd_ref, ar_ref, ai_ref,
             du_ref, dbd_ref, dcd_ref, dar_ref, dai_ref, g_ref, st_ref):
        k = pl.program_id(0)

        @pl.when(k == 0)
        def _():
            st_ref[...] = jnp.zeros_like(st_ref)
            dbd_ref[...] = jnp.zeros_like(dbd_ref)
            dcd_ref[...] = jnp.zeros_like(dcd_ref)
            dar_ref[...] = jnp.zeros_like(dar_ref)
            dai_ref[...] = jnp.zeros_like(dai_ref)

        dyb = dy_ref[...].astype(BF16)
        g_ref[...] = lax.dot_general(dyb, cd_ref[...], nt, preferred_element_type=F32)
        ar, ai = ar_ref[...], ai_ref[...]

        def adjoint(r, carry, xpr, xpi):
            gr_n, gi_n, dar, dai = carry
            gr = g_ref[r, 0:ns] + ar * gr_n + ai * gi_n
            gi = g_ref[r, ns:2 * ns] - ai * gr_n + ar * gi_n
            g_ref[r, 0:ns] = gr
            g_ref[r, ns:2 * ns] = gi
            return gr, gi, dar + gr * xpr + gi * xpi, dai + gi * xpr - gr * xpi

        def step(i, carry):
            t = steps - 1 - i
            r = pl.ds(pl.multiple_of(t * S5_ROWS, S5_ROWS), S5_ROWS)
            rp = pl.ds(pl.multiple_of((t - 1) * S5_ROWS, S5_ROWS), S5_ROWS)
            return adjoint(r, carry, x_ref[rp, 0:ns], x_ref[rp, ns:2 * ns])

        zero = jnp.zeros((S5_ROWS, ns), F32)
        carry = lax.fori_loop(0, steps - 1, step, (st_ref[:, 0:ns], st_ref[:, ns:2 * ns], zero, zero), unroll=2)
        has_prev = (k < nblk - 1).astype(F32)
        gr, gi, dar, dai = adjoint(pl.ds(0, S5_ROWS), carry, xp_ref[:, 0:ns] * has_prev, xp_ref[:, ns:2 * ns] * has_prev)
        st_ref[:, 0:ns] = gr
        st_ref[:, ns:2 * ns] = gi
        dar_ref[...] += dar
        dai_ref[...] += dai
        gb = g_ref[...].astype(BF16)
        du_ref[...] = lax.dot_general(gb, bd_ref[...], nt, preferred_element_type=F32)
        dbd_ref[...] += lax.dot_general(u_ref[...].astype(BF16), gb, tn, preferred_element_type=F32)
        dcd_ref[...] += lax.dot_general(x_ref[...].astype(BF16), dyb, tn, preferred_element_type=F32)

    whole = lambda shape: pl.BlockSpec(shape, lambda k: (0, 0))
    rev = lambda k: (nblk - 1 - k, 0)
    prev = lambda k: (jnp.maximum((nblk - 1 - k) * steps - 1, 0), 0)
    return pl.pallas_call(
        body, name="s5_scan_bwd",
        out_shape=(jax.ShapeDtypeStruct((rows, width), F32), jax.ShapeDtypeStruct(bd2.shape, F32),
                   jax.ShapeDtypeStruct(cd2.shape, F32), jax.ShapeDtypeStruct(ar8.shape, F32),
                   jax.ShapeDtypeStruct(ai8.shape, F32)),
        grid=(nblk,),
        in_specs=[pl.BlockSpec((tr, width), rev), pl.BlockSpec((tr, 2 * ns), rev),
                  pl.BlockSpec((S5_ROWS, 2 * ns), prev), pl.BlockSpec((tr, width), rev),
                  whole(bd2.shape), whole(cd2.shape), whole(ar8.shape), whole(ai8.shape)],
        out_specs=(pl.BlockSpec((tr, width), rev), whole(bd2.shape), whole(cd2.shape), whole(ar8.shape),
                   whole(ai8.shape)),
        scratch_shapes=[pltpu.VMEM((tr, 2 * ns), F32), pltpu.VMEM((S5_ROWS, 2 * ns), F32)],
        compiler_params=pltpu.CompilerParams(dimension_semantics=("arbitrary",), vmem_limit_bytes=VMEM_LIMIT),
    )(dy, x, x, u2, bd2, cd2, ar8, ai8)


@jax.custom_vjp
def s5_core(u2, bd2, cd2, ar8, ai8):
    return _s5_scan_fwd(u2, bd2.astype(BF16), cd2.astype(BF16), ar8, ai8)[0]


def _s5_core_fwd(u2, bd2, cd2, ar8, ai8):
    bd2, cd2 = bd2.astype(BF16), cd2.astype(BF16)
    y, x = _s5_scan_fwd(u2, bd2, cd2, ar8, ai8)
    return y, (x, u2, bd2, cd2, ar8, ai8)


def _s5_core_bwd(res, dy):
    return _s5_scan_bwd(dy, *res)


s5_core.defvjp(_s5_core_fwd, _s5_core_bwd)


def s5_mixers(p_ctx, p_lat, lam_re, lam_im, log_step, b_re, b_im, c_re, c_im, d_skip, glu_w, glu_b):
    bsz = p_ctx.shape[0]
    assert 2 * bsz == S5_ROWS
    eye = jnp.eye(B_NGROUPS, dtype=F32)
    bds, cds, ars, ais = [], [], [], []
    for d in range(2):
        step = jnp.exp(log_step[d])[:, None]
        mag = jnp.exp(lam_re[d] * step)
        ar = mag * jnp.cos(lam_im[d] * step)
        ai = mag * jnp.sin(lam_im[d] * step)
        den = lam_re[d] * lam_re[d] + lam_im[d] * lam_im[d]
        nr = ar - 1.0
        kr = (nr * lam_re[d] + ai * lam_im[d]) / den
        ki = (ai * lam_re[d] - nr * lam_im[d]) / den
        br = kr[..., None] * b_re[d] - ki[..., None] * b_im[d]
        bi = kr[..., None] * b_im[d] + ki[..., None] * b_re[d]
        blk = lambda w: jnp.einsum('gnc,gh->gchn', w, eye).reshape(B_WIDTH, S5_STATES)
        bds.append(jnp.concatenate([blk(br), blk(bi)], axis=1))
        blk_c = lambda w: jnp.einsum('gcn,gh->gnhc', w, eye).reshape(S5_STATES, B_WIDTH)
        cds.append(jnp.concatenate([blk_c(c_re[d]), -blk_c(c_im[d])], axis=0))
        ars.append(jnp.broadcast_to(ar.reshape(1, S5_STATES), (bsz, S5_STATES)))
        ais.append(jnp.broadcast_to(ai.reshape(1, S5_STATES), (bsz, S5_STATES)))
    bd2 = jnp.concatenate(bds, axis=0)
    cd2 = jnp.concatenate(cds, axis=1)
    ar8 = jnp.concatenate(ars, axis=0)
    ai8 = jnp.concatenate(ais, axis=0)

    def rows_of(p):
        ut = jnp.swapaxes(p, 0, 1)
        z = jnp.zeros_like(ut)
        return jnp.concatenate([jnp.concatenate([ut, z], axis=-1), jnp.concatenate([z, _flip_time(ut, 0)], axis=-1)], axis=1)

    lc = p_ctx.shape[1]
    u2 = jnp.concatenate([rows_of(p_ctx), rows_of(p_lat)], axis=0)
    y2 = s5_core(u2.reshape(-1, 2 * B_WIDTH), bd2, cd2, ar8, ai8).reshape(u2.shape)

    def finish(y2p, p):
        y = y2p[:, :bsz, :B_WIDTH] + _flip_time(y2p[:, bsz:, B_WIDTH:], 0)
        y = jnp.swapaxes(y, 0, 1) + d_skip * p
        y = jax.nn.gelu(y)
        gate = mm(y.reshape(-1, B_WIDTH), glu_w).reshape(y.shape)
        return y * jax.nn.sigmoid(gate + glu_b)

    return finish(y2[:lc], p_ctx), finish(y2[lc:], p_lat)


GLA_CHUNK = 64
GLA_SUB = 16
GLA_NB = 4
NT_DIMS = (((1,), (1,)), ((), ()))
TN_DIMS = (((0,), (0,)), ((), ()))


def _bdot(a, b, dims=(((1,), (0,)), ((), ()))):
    return lax.dot_general(a.astype(BF16), b.astype(BF16), dims, preferred_element_type=F32)


def _hdot(a, b, dims=(((1,), (0,)), ((), ()))):
    ah, bh = a.astype(BF16), b.astype(BF16)
    al, bl = (a - ah.astype(F32)).astype(BF16), (b - bh.astype(F32)).astype(BF16)
    dot = functools.partial(lax.dot_general, dimension_numbers=dims, preferred_element_type=F32)
    return dot(ah, bh) + (dot(ah, bl) + dot(al, bh))


def _gla_scores(q, k, cum, cumr, tri):
    n = GLA_CHUNK
    if cumr is not None:
        decay = jnp.where(tri, jnp.exp(jnp.where(tri, cum - cumr, 0.0)), 0.0)
        return _bdot(q, k, NT_DIMS) * decay, decay
    rows = lax.broadcasted_iota(jnp.int32, (n, 1), 0)
    parts = []
    for i in range(n // GLA_SUB):
        lo, hi = i * GLA_SUB, (i + 1) * GLA_SUB
        ref = cum[lo - 1:lo, :] if i else jnp.zeros_like(cum[0:1, :])
        qt = q[lo:hi] * jnp.exp(cum[lo:hi] - ref)
        seen = rows < hi
        kh = jnp.where(seen, k * jnp.exp(jnp.where(seen, ref - cum, 0.0)), 0.0)
        parts.append(_bdot(qt, kh, NT_DIMS))
    return jnp.where(tri, jnp.concatenate(parts, axis=0), 0.0), None


def _gla_fwd(q, k, cum, cumr, v):
    nseq, length, dk = q.shape
    nc = length // GLA_CHUNK
    scalar = cumr is not None

    def body(*refs):
        if scalar:
            q_ref, k_ref, cum_ref, cumr_ref, v_ref, o_ref, s_ref, st_ref = refs
        else:
            q_ref, k_ref, cum_ref, v_ref, o_ref, s_ref, st_ref = refs

        @pl.when(pl.program_id(1) == 0)
        def _():
            st_ref[...] = jnp.zeros_like(st_ref)

        ii = lax.broadcasted_iota(jnp.int32, (GLA_CHUNK, GLA_CHUNK), 0)
        jj = lax.broadcasted_iota(jnp.int32, (GLA_CHUNK, GLA_CHUNK), 1)
        tri = jj <= ii
        for b in range(GLA_NB):
            qv, kv, cv, vv, st = q_ref[b], k_ref[b], cum_ref[b], v_ref[b], st_ref[b]
            s_ref[b, 0] = st
            a, _ = _gla_scores(qv, kv, cv, cumr_ref[b] if scalar else None, tri)
            o_ref[b] = _bdot(qv * jnp.exp(cv), st, NT_DIMS) + _bdot(a, vv)
            last = cv[GLA_CHUNK - 1:GLA_CHUNK, :]
            st_ref[b] = st * jnp.exp(last) + _bdot(vv, kv * jnp.exp(last - cv), TN_DIMS)

    seq = pl.BlockSpec((GLA_NB, GLA_CHUNK, dk), lambda n, c: (n, c, 0))
    state = pl.BlockSpec((GLA_NB, 1, dk, dk), lambda n, c: (n, c, 0, 0))
    ins = [q, k, cum] + ([cumr] if scalar else []) + [v]
    return pl.pallas_call(
        body, name="gla_fwd_scalar" if scalar else "gla_fwd",
        out_shape=(jax.ShapeDtypeStruct((nseq, length, dk), F32), jax.ShapeDtypeStruct((nseq, nc, dk, dk), F32)),
        grid=(nseq // GLA_NB, nc),
        in_specs=[seq] * len(ins),
        out_specs=(seq, state),
        scratch_shapes=[pltpu.VMEM((GLA_NB, dk, dk), F32)],
        compiler_params=pltpu.CompilerParams(dimension_semantics=("parallel", "arbitrary"),
                                             vmem_limit_bytes=VMEM_LIMIT),
    )(*ins)


def _gla_bwd(do, q, k, cum, cumr, v, states):
    nseq, length, dk = q.shape
    nc = length // GLA_CHUNK
    scalar = cumr is not None
    n = GLA_CHUNK

    def body(*refs):
        if scalar:
            do_ref, q_ref, k_ref, cum_ref, cumr_ref, v_ref, s_ref, dq_ref, dk_ref, dc_ref, dcr_ref, dv_ref, dst_ref = refs
        else:
            do_ref, q_ref, k_ref, cum_ref, v_ref, s_ref, dq_ref, dk_ref, dc_ref, dv_ref, dst_ref = refs

        @pl.when(pl.program_id(1) == 0)
        def _():
            dst_ref[...] = jnp.zeros_like(dst_ref)

        ii = lax.broadcasted_iota(jnp.int32, (n, n), 0)
        jj = lax.broadcasted_iota(jnp.int32, (n, n), 1)
        tri = jj <= ii
        rows = lax.broadcasted_iota(jnp.int32, (n, 1), 0)
        for b in range(GLA_NB):
            dov, qv, kv, cv, vv, st, dst = do_ref[b], q_ref[b], k_ref[b], cum_ref[b], v_ref[b], s_ref[b, 0], dst_ref[b]
            e = jnp.exp(cv)
            qe = qv * e
            last = cv[n - 1:n, :]
            w = jnp.exp(last - cv)
            kw = kv * w
            el = jnp.exp(last)
            d_qe = _hdot(dov, st)
            d_kw = _hdot(vv, dst)
            dv = _bdot(kw, dst, NT_DIMS)
            d_last = jnp.sum(st * dst, axis=0, keepdims=True) * el + jnp.sum(d_kw * kw, axis=0, keepdims=True)
            dst_ref[b] = dst * el + _bdot(dov, qe, TN_DIMS)
            dq = d_qe * e
            dkk = d_kw * w
            dc = d_qe * qe - d_kw * kw + jnp.where(rows == n - 1, d_last, 0.0)
            da = jnp.where(tri, _hdot(dov, vv, NT_DIMS), 0.0)
            if scalar:
                a, decay = _gla_scores(qv, kv, cv, cumr_ref[b], tri)
                dg = da * decay
                dq = dq + _bdot(dg, kv)
                dkk = dkk + _bdot(dg, qv, TN_DIMS)
                p = da * a
                dc = dc + p
                dcr_ref[b] = -p
            else:
                a_parts, dq_parts = [], []
                for i in range(n // GLA_SUB):
                    lo, hi = i * GLA_SUB, (i + 1) * GLA_SUB
                    ref = cv[lo - 1:lo, :] if i else jnp.zeros_like(cv[0:1, :])
                    eq = jnp.exp(cv[lo:hi] - ref)
                    qt = qv[lo:hi] * eq
                    seen = rows < hi
                    ek = jnp.where(seen, jnp.exp(jnp.where(seen, ref - cv, 0.0)), 0.0)
                    kh = kv * ek
                    a_parts.append(_bdot(qt, kh, NT_DIMS))
                    dqt = _hdot(da[lo:hi], kh)
                    dkh = _hdot(da[lo:hi], qt, TN_DIMS)
                    dq_parts.append((dqt * eq, dqt * qt))
                    dkk = dkk + dkh * ek
                    dc = dc - dkh * kh
                a = jnp.where(tri, jnp.concatenate(a_parts, axis=0), 0.0)
                dq = dq + jnp.concatenate([p[0] for p in dq_parts], axis=0)
                dc = dc + jnp.concatenate([p[1] for p in dq_parts], axis=0)
            dv_ref[b] = dv + _bdot(a, dov, TN_DIMS)
            dq_ref[b] = dq
            dk_ref[b] = dkk
            dc_ref[b] = dc

    seq = pl.BlockSpec((GLA_NB, n, dk), lambda s, c: (s, nc - 1 - c, 0))
    state = pl.BlockSpec((GLA_NB, 1, dk, dk), lambda s, c: (s, nc - 1 - c, 0, 0))
    ins = [do, q, k, cum] + ([cumr] if scalar else []) + [v]
    n_out = 5 if scalar else 4
    return pl.pallas_call(
        body, name="gla_bwd_scalar" if scalar else "gla_bwd",
        out_shape=(jax.ShapeDtypeStruct((nseq, length, dk), F32),) * n_out,
        grid=(nseq // GLA_NB, nc),
        in_specs=[seq] * len(ins) + [state],
        out_specs=(seq,) * n_out,
        scratch_shapes=[pltpu.VMEM((GLA_NB, dk, dk), F32)],
        compiler_params=pltpu.CompilerParams(dimension_semantics=("parallel", "arbitrary"),
                                             vmem_limit_bytes=VMEM_LIMIT),
    )(*ins, states)


@jax.custom_vjp
def gla(q, k, cum, v):
    return _gla_fwd(q, k, cum, None, v)[0]


def _gla_vjp_fwd(q, k, cum, v):
    o, states = _gla_fwd(q, k, cum, None, v)
    return o, (q, k, cum, v, states)


def _gla_vjp_bwd(res, do):
    q, k, cum, v, states = res
    return _gla_bwd(do, q, k, cum, None, v, states)


gla.defvjp(_gla_vjp_fwd, _gla_vjp_bwd)


@jax.custom_vjp
def gla_scalar(q, k, cum, cumr, v):
    return _gla_fwd(q, k, cum, cumr, v)[0]


def _gla_scalar_vjp_fwd(q, k, cum, cumr, v):
    o, states = _gla_fwd(q, k, cum, cumr, v)
    return o, (q, k, cum, cumr, v, states)


def _gla_scalar_vjp_bwd(res, do):
    q, k, cum, cumr, v, states = res
    return _gla_bwd(do, q, k, cum, cumr, v, states)


gla_scalar.defvjp(_gla_scalar_vjp_fwd, _gla_scalar_vjp_bwd)


def _chunk_cumsum(g):
    s = g.shape
    return jnp.cumsum(g.reshape(s[:-2] + (s[-2] // GLA_CHUNK, GLA_CHUNK, s[-1])), axis=-2).reshape(s)


def _both_ways(t_ctx, t_lat, flip):
    parts = [_flip_time(t, 1) if flip else t for t in (t_ctx, t_lat)]
    return jnp.swapaxes(jnp.concatenate(parts, axis=1), 1, 2)


def _undo_ways(o, lc, flip):
    o = jnp.swapaxes(o, 1, 2)
    parts = (o[:, :lc], o[:, lc:])
    return tuple(_flip_time(t, 1) if flip else t for t in parts)


def hgrn2_mixers(p_ctx, p_lat, lower, norm_w):
    bsz, lc = p_ctx.shape[:2]
    lower = lower.reshape(2, C_HEADS, C_KEY)

    def heads(p, lo, hi):
        return p[..., lo:hi].reshape(p.shape[:2] + (C_HEADS, -1))

    q_c, q_l = (jax.nn.silu(heads(p, 0, C_WIDTH)) for p in (p_ctx, p_lat))
    v_c, v_l = (heads(p, 3 * C_WIDTH, 4 * C_WIDTH) for p in (p_ctx, p_lat))
    qs, ks, cums, vs = [], [], [], []
    for d in range(2):
        f_c, f_l = (lower[d] + (1.0 - lower[d]) * jax.nn.sigmoid(heads(p, (1 + d) * C_WIDTH, (2 + d) * C_WIDTH))
                    for p in (p_ctx, p_lat))
        qs.append(_both_ways(q_c, q_l, d))
        vs.append(_both_ways(v_c, v_l, d))
        ks.append(_both_ways(1.0 - f_c, 1.0 - f_l, d))
        cums.append(_chunk_cumsum(_both_ways(jnp.log(f_c), jnp.log(f_l), d)))
    flat = lambda ts: jnp.stack(ts).reshape((-1,) + ts[0].shape[2:])
    o = gla(flat(qs), flat(ks), flat(cums), flat(vs)).reshape((2, bsz, C_HEADS, -1, C_VAL))
    f_c, f_l = _undo_ways(o[0], lc, False)
    b_c, b_l = _undo_ways(o[1], lc, True)
    outs = []
    for o_sum, p in ((f_c + b_c, p_ctx), (f_l + b_l, p_lat)):
        o_n = rms_norm(o_sum) * norm_w.reshape(C_HEADS, C_VAL)
        outs.append(o_n.reshape(p.shape[:2] + (C_WIDTH,)) * jax.nn.silu(p[..., 4 * C_WIDTH:]))
    return tuple(outs)


def ssd_mixers(p_ctx, p_lat, conv_w, conv_b, dt_bias, a_log, d_skip, norm_w):
    bsz, lc = p_ctx.shape[:2]
    rep = A_HEADS // A_GROUPS
    a = -jnp.exp(a_log)
    xs, bs, cs, dts, zs = [], [], [], [], []
    for p in (p_ctx, p_lat):
        z, xbc, dt_raw = jnp.split(p, [A_INNER, A_INNER + A_CONV_DIM], axis=-1)
        xbc = jax.nn.silu(depthwise_conv(xbc, conv_w, conv_b))
        x_, b_, c_ = jnp.split(xbc, [A_INNER, A_INNER + A_GROUPS * A_STATE], axis=-1)
        shp = p.shape[:2]
        xs.append(x_.reshape(shp + (A_HEADS, A_HEAD_DIM)))
        bs.append(jnp.repeat(b_.reshape(shp + (A_GROUPS, A_STATE)), rep, axis=2))
        cs.append(jnp.repeat(c_.reshape(shp + (A_GROUPS, A_STATE)), rep, axis=2))
        dts.append(jax.nn.softplus(dt_raw.reshape(shp + (2, A_HEADS)) + dt_bias))
        zs.append(z)
    qs, ks, cums, cumrs, vs = [], [], [], [], []
    for d in range(2):
        dt_c, dt_l = (dt[:, :, d, :, None] for dt in dts)
        qs.append(_both_ways(cs[0], cs[1], d))
        vs.append(_both_ways(xs[0], xs[1], d))
        ks.append(_both_ways(bs[0] * dt_c, bs[1] * dt_l, d))
        cum = _chunk_cumsum(_both_ways(dt_c * a[d][:, None], dt_l * a[d][:, None], d))
        cums.append(jnp.broadcast_to(cum, cum.shape[:-1] + (A_STATE,)))
        by_chunk = cum.reshape(cum.shape[:2] + (-1, 1, GLA_CHUNK))
        cumrs.append(jnp.broadcast_to(by_chunk, by_chunk.shape[:3] + (GLA_CHUNK, GLA_CHUNK)).reshape(cums[-1].shape))
    flat = lambda ts: jnp.stack(ts).reshape((-1,) + ts[0].shape[2:])
    o = gla_scalar(flat(qs), flat(ks), flat(cums), flat(cumrs), flat(vs)).reshape((2, bsz, A_HEADS, -1, A_HEAD_DIM))
    f_c, f_l = _undo_ways(o[0], lc, False)
    b_c, b_l = _undo_ways(o[1], lc, True)
    outs = []
    for y, x_, z in ((f_c + b_c, xs[0], zs[0]), (f_l + b_l, xs[1], zs[1])):
        y = y + d_skip[:, None] * x_
        y = y.reshape(z.shape) * jax.nn.silu(z)
        outs.append(rms_norm(y) * norm_w)
    return tuple(outs)


def token_mixers(p_ctx, p_lat, W, l, lower):
    def cut(p):
        return p[..., :A_COLS], p[..., 1408:1408 + B_COLS], p[..., 1664:1664 + C_COLS]

    pa_c, pb_c, pc_c = cut(p_ctx)
    pa_l, pb_l, pc_l = cut(p_lat)
    ya_c, ya_l = ssd_mixers(pa_c, pa_l, W['a_conv_w'][l], W['a_conv_b'][l], W['a_dt_bias'][l], W['a_log'][l],
                            W['a_d'][l], W['a_norm_w'][l])
    yb_c, yb_l = s5_mixers(pb_c, pb_l, W['s5_lam_re'][l], W['s5_lam_im'][l], W['s5_log_step'][l], W['s5_b_re'][l],
                           W['s5_b_im'][l], W['s5_c_re'][l], W['s5_c_im'][l], W['s5_d'][l], W['s5_glu_w'][l],
                           W['s5_glu_b'][l])
    yc_c, yc_l = hgrn2_mixers(pc_c, pc_l, lower, W['hg_norm_w'][l])
    return (jnp.concatenate([ya_c, yb_c, yc_c], axis=-1), jnp.concatenate([ya_l, yb_l, yc_l], axis=-1))


def _pad_w_in(w):
    z = functools.partial(jnp.zeros, dtype=w.dtype)
    return jnp.concatenate([w[:, :A_COLS], z((D_MODEL, 1408 - A_COLS)), w[:, A_COLS:], z((D_MODEL, IN_PAD - 2944))],
                           axis=1)


def _mm3(t, w):
    g, tt, k = t.shape
    return mm(t.reshape(g * tt, k), w).reshape(g, tt, -1)


def _ffn(h, mg, first, w_in, w_out):
    u = modulate_g(h, mg[:, first:first + 1], mg[:, first + 1:first + 2])
    f = _mm3(swiglu_act(_mm3(u, w_in)), w_out)
    return gated_add(h, f, mg[:, first + 2:first + 3], 0.5)


def local_loss(x, W, m_lat, m_ctx, ctx, target):
    bsz, seq, dm = x.shape
    lc = ctx.shape[1]
    tg = bsz * lc
    assert seq % tg == 0
    gl = seq // tg
    ng = bsz * gl
    rows = seq // GRID_W
    p_lb = jax.nn.softmax(W['hg_lb_logits'], axis=0)
    lower_bounds = jnp.cumsum(p_lb, axis=0) - p_lb[:1]
    h = jnp.concatenate([x.reshape(ng, tg, dm), ctx.reshape(1, tg, dm)], axis=0)
    for l in range(DEPTH):
        last = l == DEPTH - 1
        col_major = l % 2 == 1
        mg = jnp.concatenate([jnp.repeat(m_lat[l], gl, axis=0), m_ctx[l][None]], axis=0)
        h = _ffn(h, mg, 0, W['ffn_w_in'][l][0], W['ffn_w_out'][l][0])
        u = modulate_g(h, mg[:, 3:4], mg[:, 4:5])
        if col_major:
            u_lat = raster_to_column(u[:ng].reshape(bsz, seq, dm), rows)
            u = jnp.concatenate([u_lat.reshape(ng, tg, dm), u[ng:]], axis=0)
        p = _mm3(u, _pad_w_in(W['w_in'][l]))
        mix_ctx, mix_lat = token_mixers(p[ng].reshape(bsz, lc, -1), p[:ng].reshape(bsz, seq, -1), W, l,
                                        lower_bounds[l])
        if last:
            h, mg = h[:ng], mg[:ng]
            y_lat = _mm3(mix_lat.reshape(ng, tg, dm), W['w_out'][l])
            y_ctx = None
        else:
            y = _mm3(jnp.concatenate([mix_lat.reshape(ng, tg, dm), mix_ctx.reshape(1, tg, dm)], axis=0), W['w_out'][l])
            y_lat, y_ctx = y[:ng], y[ng:]
        if col_major:
            y_lat = column_to_raster(y_lat.reshape(bsz, seq, dm), rows).reshape(ng, tg, dm)
        y = y_lat if y_ctx is None else jnp.concatenate([y_lat, y_ctx], axis=0)
        h = gated_add(h, y, mg[:, 5:6], 1.0)
        h = _ffn(h, mg, 6, W['ffn_w_in'][l][1], W['ffn_w_out'][l][1])
    y = rms_norm(h[:ng].reshape(bsz, seq, dm)) * W['final_norm_w']
    err = jnp.square(y - target)
    return 0.5 * jnp.sum(jnp.mean(err, axis=-1))


def _pad_rows(a, rows):
    return jnp.concatenate([a, jnp.zeros((rows - a.shape[0],) + a.shape[1:], a.dtype)], axis=0)


def kernel(x, c, ctx, c_ctx, mod_w, mod_b, ffn_w_in, ffn_w_out, w_in, w_out, a_conv_w, a_conv_b, a_dt_bias, a_log, a_d, a_norm_w, s5_lam_re, s5_lam_im, s5_log_step, s5_b_re, s5_b_im, s5_c_re, s5_c_im, s5_d, s5_glu_w, s5_glu_b, hg_lb_logits, hg_norm_w, final_norm_w, loss_target, m_c_ctx, m_mod_w, m_mod_b, m_ffn_w_in, m_ffn_w_out, m_w_in, m_w_out, m_a_conv_w, m_a_conv_b, m_a_dt_bias, m_a_log, m_a_d, m_a_norm_w, m_s5_lam_re, m_s5_lam_im, m_s5_log_step, m_s5_b_re, m_s5_b_im, m_s5_c_re, m_s5_c_im, m_s5_d, m_s5_glu_w, m_s5_glu_b, m_hg_lb_logits, m_hg_norm_w, m_final_norm_w, v_c_ctx, v_mod_w, v_mod_b, v_ffn_w_in, v_ffn_w_out, v_w_in, v_w_out, v_a_conv_w, v_a_conv_b, v_a_dt_bias, v_a_log, v_a_d, v_a_norm_w, v_s5_lam_re, v_s5_lam_im, v_s5_log_step, v_s5_b_re, v_s5_b_im, v_s5_c_re, v_s5_c_im, v_s5_d, v_s5_glu_w, v_s5_glu_b, v_hg_lb_logits, v_hg_norm_w, v_final_norm_w):
    given = dict(locals())
    w_loc = {n: given[n] for n in WEIGHTS}
    m_loc = {n: given["m_" + n] for n in WEIGHTS}
    v_loc = {n: given["v_" + n] for n in WEIGHTS}
    bsz = x.shape[0]
    me = 4 * lax.axis_index("x") + 2 * lax.axis_index("y") + lax.axis_index("c")
    big_shapes = [w_loc[n].shape for n in BIG]

    small_sh = [c] + [w_loc[n] for n in SMALL_SHARDED]
    g1 = _unpack(all_gather(_pack(small_sh, 128, 8), "gather_small"), [a.shape for a in small_sh])
    c_all = g1[0].reshape(N_DEV * bsz, D_MODEL)
    gb = all_gather(_pack([w_loc[n].astype(BF16) for n in BIG], D_MODEL, 256), "gather_weights")
    gathered = dict(zip(BIG, _unpack(gb, big_shapes)))
    W = {'ffn_w_in': [[_assemble(gathered['ffn_w_in'][:, l, i], 1) for i in range(2)] for l in range(DEPTH)],
         'ffn_w_out': [[_assemble(gathered['ffn_w_out'][:, l, i], 0) for i in range(2)] for l in range(DEPTH)],
         'w_in': [_assemble(gathered['w_in'][:, l], 1) for l in range(DEPTH)],
         'w_out': [_assemble(gathered['w_out'][:, l], 0) for l in range(DEPTH)]}
    for (n, ax), t in zip(SMALL_SHARDED.items(), g1[1:]):
        W[n] = _assemble(t, ax)
    for n in SMALL:
        if n not in SMALL_SHARDED and n not in ('c_ctx', 'mod_b'):
            W[n] = w_loc[n]

    n_rows = N_DEV * bsz + 1
    pad_rows = 8 * ((n_rows + 7) // 8)
    c_rows = _pad_rows(jnp.concatenate([c_all, c_ctx[None]], axis=0), pad_rows)
    sc = jax.nn.silu(c_rows)
    mods_sh = jnp.stack([matmul(sc, mod_w[l], name="mod_fwd") for l in range(DEPTH)])
    mods = _assemble(all_gather(mods_sh, "gather_mods"), 2) + mod_b[:, None, :]
    m_lat = lax.dynamic_slice_in_dim(mods, me * bsz, bsz, axis=1).reshape(DEPTH, bsz, N_MOD, D_MODEL)
    m_ctx = mods[:, n_rows - 1].reshape(DEPTH, N_MOD, D_MODEL)

    loss_loc, (grad_x, gW, gm_lat, gm_ctx) = jax.value_and_grad(local_loss, argnums=(0, 1, 2, 3))(
        x, W, m_lat, m_ctx, ctx, loss_target)
    loss = lax.psum(loss_loc, MESH_AXES)

    dm_loc = jnp.concatenate([gm_lat.reshape(DEPTH, bsz, -1), gm_ctx.reshape(DEPTH, 1, -1)], axis=1)
    dm_all = all_gather(dm_loc, "gather_dmods")
    dm_ex = jnp.moveaxis(dm_all[:, :, :bsz], 0, 1).reshape(DEPTH, N_DEV * bsz, -1)
    ncol = N_MOD * D_MODEL
    dm_cx = sum_leading(dm_all[:, :, bsz].reshape(N_DEV, DEPTH * ncol // 128, 128), "sum_dmods_ctx")
    dm_cx = dm_cx.reshape(DEPTH, 1, ncol)
    dm_rows = jnp.concatenate([dm_ex, dm_cx, jnp.zeros((DEPTH, pad_rows - n_rows, ncol), F32)], axis=1)
    grad_mod_b = sum_leading(jnp.moveaxis(dm_rows, 1, 0).reshape(pad_rows, DEPTH * ncol // 128, 128),
                             "sum_mod_b").reshape(DEPTH, ncol)
    my_cols = ncol // N_DEV
    dm_mine = lax.dynamic_slice_in_dim(dm_rows, me * my_cols, my_cols, axis=2)
    grad_mod_w = jnp.stack([matmul(sc, dm_mine[l], ta=True, name="mod_dw") for l in range(DEPTH)])
    dm_cx_mine = lax.dynamic_slice_in_dim(dm_cx, me * my_cols, my_cols, axis=2)
    g_sc_ctx = sum(matmul(_pad_rows(dm_cx_mine[l], 8), mod_w[l], tb=True, name="mod_dc")[0] for l in range(DEPTH))

    small_full = [n for n in SMALL if n not in ('c_ctx', 'mod_b')]
    part = [gW[n] for n in small_full] + [g_sc_ctx]
    red = sum_leading(all_gather(_pack(part, 128, 8), "gather_small_grads"), "sum_small_grads")
    red = _unpack(red, [a.shape for a in part])
    grads = dict(zip(small_full, red[:-1]))
    sig = jax.nn.sigmoid(c_ctx)
    grads['c_ctx'] = red[-1] * (sig * (1.0 + c_ctx * (1.0 - sig)))
    grads['mod_b'] = grad_mod_b
    for n, ax in SMALL_SHARDED.items():
        size = w_loc[n].shape[ax]
        grads[n] = lax.dynamic_slice_in_dim(grads[n], me * size, size, axis=ax)
    grads['mod_w'] = grad_mod_w

    by_dev = {'ffn_w_in': jnp.stack([jnp.stack([_split(g, 1) for g in gl], axis=1) for gl in gW['ffn_w_in']], axis=1),
              'ffn_w_out': jnp.stack([jnp.stack([_split(g, 0) for g in gl], axis=1) for gl in gW['ffn_w_out']], axis=1),
              'w_in': jnp.stack([_split(g, 1) for g in gW['w_in']], axis=1),
              'w_out': jnp.stack([_split(g, 0) for g in gW['w_out']], axis=1)}
    gsend = _pack8([by_dev[n] for n in BIG], D_MODEL, 256)
    g_big = sum_leading(all_to_all(gsend, "exchange_grads"), "sum_grads")
    for n, t in zip(BIG, _unpack(g_big, big_shapes)):
        grads[n] = t

    delta, new_m, new_v = {}, {}, {}

    def update(names, width, row_mult, tag):
        packed = [_pack([d[n] for n in names], width, row_mult) for d in (w_loc, grads, m_loc, v_loc)]
        outs = adamw(*packed, name="adamw_" + tag)
        shapes = [w_loc[n].shape for n in names]
        for res, out in zip((delta, new_m, new_v), outs):
            res.update(zip(names, _unpack(out, shapes)))

    update(list(BIG), D_MODEL, 256, "matmul_weights")
    update(['mod_w'], mod_w.shape[-1], 256, "mod_w")
    update(SMALL, 128, 256, "small")
    return (loss, grad_x, *[grads[n] for n in WEIGHTS], *[delta[n] for n in WEIGHTS],
            *[new_m[n] for n in WEIGHTS], *[new_v[n] for n in WEIGHTS])
```

```python
import functools
import math

import jax
import jax.numpy as jnp
from jax import lax
from jax.experimental import pallas as pl
from jax.experimental.pallas import tpu as pltpu

F32 = jnp.float32
BF16 = jnp.bfloat16
MESH_AXES = ("x", "y", "c")
N_DEV = 8
MESH_ID = pl.DeviceIdType.MESH
VMEM_LIMIT = 48 * 1024 * 1024

D_MODEL = 1024
DEPTH = 2
GRID_W = 64
EPS = 1e-6
N_MOD = 9
D_FF = 2816
A_INNER = 512
A_HEADS = 8
A_HEAD_DIM = 64
A_GROUPS = 2
A_STATE = 64
A_CONV = 5
A_CONV_DIM = A_INNER + 2 * A_GROUPS * A_STATE
A_COLS = A_INNER + A_CONV_DIM + 2 * A_HEADS
B_WIDTH = 256
B_GROUP = 16
B_NGROUPS = 16
B_STATE = 64
B_COLS = B_WIDTH
C_WIDTH = 256
C_HEADS = 4
C_KEY = 64
C_VAL = 64
C_COLS = 5 * C_WIDTH
IN_PAD = 3072

ADAM_LR = 0.001
ADAM_B1 = 0.9
ADAM_B2 = 0.999
ADAM_EPS = 1e-08
ADAM_WD = 0.01
ADAM_STEP = 10

WEIGHTS = ['c_ctx', 'mod_w', 'mod_b', 'ffn_w_in', 'ffn_w_out', 'w_in', 'w_out', 'a_conv_w', 'a_conv_b', 'a_dt_bias',
           'a_log', 'a_d', 'a_norm_w', 's5_lam_re', 's5_lam_im', 's5_log_step', 's5_b_re', 's5_b_im', 's5_c_re',
           's5_c_im', 's5_d', 's5_glu_w', 's5_glu_b', 'hg_lb_logits', 'hg_norm_w', 'final_norm_w']
BIG = {'ffn_w_in': 3, 'ffn_w_out': 2, 'w_in': 2, 'w_out': 1}
SMALL_SHARDED = {'a_conv_w': 2, 's5_glu_w': 1, 'hg_lb_logits': 2}
SMALL = [n for n in WEIGHTS if n not in BIG and n != 'mod_w']


def _tile(d, prefs):
    for p in prefs:
        if d % p == 0:
            return p
    return d


def _pack(arrs, width, row_mult):
    flat = jnp.concatenate([a.reshape(-1) for a in arrs])
    pad = (-flat.shape[0]) % (width * row_mult)
    if pad:
        flat = jnp.concatenate([flat, jnp.zeros((pad,), flat.dtype)])
    return flat.reshape(-1, width)


def _as_2d(shape):
    return (math.prod(shape[:-1]), shape[-1])


def _unpack(buf, shapes):
    lead = buf.shape[:-2]
    flat = buf.reshape(lead + (-1,))
    out, off = [], 0
    for s in shapes:
        n = math.prod(s)
        out.append(flat[..., off:off + n].reshape(lead + tuple(s)))
        off += n
    return out


def _assemble(g, axis):
    t = jnp.moveaxis(g, 0, axis)
    s = t.shape
    return t.reshape(s[:axis] + (s[axis] * s[axis + 1],) + s[axis + 2:])


def _split(full, axis):
    s = full.shape
    t = full.reshape(s[:axis] + (N_DEV, s[axis] // N_DEV) + s[axis + 1:])
    return jnp.moveaxis(t, axis, 0)


def all_gather(xs, name):
    nt = len(xs)

    def body(*refs):
        x_refs, out_refs = refs[:nt], refs[nt:2 * nt]
        send_sems, recv_sems, local_sems = refs[2 * nt:]
        ax, ay, ac = lax.axis_index("x"), lax.axis_index("y"), lax.axis_index("c")
        me, sibling = (ax, ay, ac), (ax, ay, 1 - ac)
        chips = [(1 - ax, ay), (ax, 1 - ay), (1 - ax, 1 - ay)]

        def slot(t, px, py, pc):
            return out_refs[t].at[4 * px + 2 * py + pc]

        def copy(t, k, block, to, src=None):
            return pltpu.make_async_remote_copy(
                src_ref=slot(t, *block) if src is None else src, dst_ref=slot(t, *block),
                send_sem=send_sems.at[t, k], recv_sem=recv_sems.at[t, k], device_id=to, device_id_type=MESH_ID)

        mine = [pltpu.make_async_copy(x_refs[t], slot(t, *me), local_sems.at[t]) for t in range(nt)]
        for cp in mine:
            cp.start()
        first = []
        for t in range(nt):
            first.append(copy(t, 0, me, sibling, src=x_refs[t]))
            first += [copy(t, 1 + j, me, (*chip, ac), src=x_refs[t]) for j, chip in enumerate(chips)]
        for cp in first:
            cp.start()
        passed = []
        for j, chip in enumerate(chips):
            for t in range(nt):
                copy(t, 1 + j, (*chip, ac), me).wait_recv()
                passed.append(copy(t, 4 + j, (*chip, ac), sibling))
                passed[-1].start()
        for t in range(nt):
            copy(t, 0, sibling, me).wait_recv()
            for j, chip in enumerate(chips):
                copy(t, 4 + j, (*chip, 1 - ac), me).wait_recv()
        for cp in first + passed:
            cp.wait_send()
        for cp in mine:
            cp.wait()

    return pl.pallas_call(
        body, name=name,
        out_shape=tuple(jax.ShapeDtypeStruct((N_DEV,) + x.shape, x.dtype) for x in xs),
        in_specs=[pl.BlockSpec(memory_space=pl.ANY)] * nt,
        out_specs=tuple(pl.BlockSpec(memory_space=pl.ANY) for _ in xs),
        scratch_shapes=[pltpu.SemaphoreType.DMA((nt, 7)), pltpu.SemaphoreType.DMA((nt, 7)),
                        pltpu.SemaphoreType.DMA((nt,))],
    )(*xs)


def all_to_all(gs, name):
    nt = len(gs)

    def body(*refs):
        g_refs, out_refs = refs[:nt], refs[nt:2 * nt]
        send_sems, recv_sems, local_sems = refs[2 * nt:]
        ax, ay, ac = lax.axis_index("x"), lax.axis_index("y"), lax.axis_index("c")
        my = 4 * ax + 2 * ay + ac
        local = [pltpu.make_async_copy(g_refs[t].at[my], out_refs[t].at[my], local_sems.at[t]) for t in range(nt)]
        for cp in local:
            cp.start()
        peers = []
        for r in range(1, N_DEV):
            px = 1 - ax if r & 4 else ax
            py = 1 - ay if r & 2 else ay
            pc = 1 - ac if r & 1 else ac
            peers.append((px, py, pc))

        def copy(t, k, peer):
            return pltpu.make_async_remote_copy(
                src_ref=g_refs[t].at[4 * peer[0] + 2 * peer[1] + peer[2]], dst_ref=out_refs[t].at[my],
                send_sem=send_sems.at[t, k], recv_sem=recv_sems.at[t, k], device_id=peer, device_id_type=MESH_ID)

        def arrival(t, k, peer):
            slot = 4 * peer[0] + 2 * peer[1] + peer[2]
            return pltpu.make_async_remote_copy(
                src_ref=g_refs[t].at[slot], dst_ref=out_refs[t].at[slot],
                send_sem=send_sems.at[t, k], recv_sem=recv_sems.at[t, k], device_id=peer, device_id_type=MESH_ID)

        sends = [copy(t, k, p) for t in range(nt) for k, p in enumerate(peers)]
        for cp in sends:
            cp.start()
        for t in range(nt):
            for k, p in enumerate(peers):
                arrival(t, k, p).wait_recv()
        for cp in sends:
            cp.wait_send()
        for cp in local:
            cp.wait()

    return pl.pallas_call(
        body, name=name,
        out_shape=tuple(jax.ShapeDtypeStruct(g.shape, g.dtype) for g in gs),
        in_specs=[pl.BlockSpec(memory_space=pl.ANY)] * nt,
        out_specs=tuple(pl.BlockSpec(memory_space=pl.ANY) for _ in gs),
        scratch_shapes=[pltpu.SemaphoreType.DMA((nt, 7)), pltpu.SemaphoreType.DMA((nt, 7)),
                        pltpu.SemaphoreType.DMA((nt,))],
    )(*gs)


def matmul(a, b, *, ta=False, tb=False, out_dtype=F32, name="mm"):
    m, k = (a.shape[1], a.shape[0]) if ta else a.shape
    n = b.shape[0] if tb else b.shape[1]
    assert (b.shape[1] if tb else b.shape[0]) == k, (a.shape, b.shape, ta, tb)
    tm = _tile(m, (512, 256, 128))
    tn = _tile(n, (1408, 1024, 512, 384, 256, 128))
    tk = _tile(k, (1024, 512, 256, 128))
    nk = k // tk
    dims = (((0 if ta else 1,), (1 if tb else 0,)), ((), ()))

    def body(a_ref, b_ref, o_ref, acc_ref):
        step = pl.program_id(2)

        @pl.when(step == 0)
        def _():
            acc_ref[...] = jnp.zeros_like(acc_ref)

        acc_ref[...] += lax.dot_general(a_ref[...].astype(BF16), b_ref[...].astype(BF16), dims,
                                        preferred_element_type=F32)

        @pl.when(step == nk - 1)
        def _():
            o_ref[...] = acc_ref[...].astype(out_dtype)

    a_spec = pl.BlockSpec((tk, tm), lambda i, j, s: (s, i)) if ta else pl.BlockSpec((tm, tk), lambda i, j, s: (i, s))
    b_spec = pl.BlockSpec((tn, tk), lambda i, j, s: (j, s)) if tb else pl.BlockSpec((tk, tn), lambda i, j, s: (s, j))
    return pl.pallas_call(
        body, name=name,
        out_shape=jax.ShapeDtypeStruct((m, n), out_dtype),
        grid=(m // tm, n // tn, nk),
        in_specs=[a_spec, b_spec],
        out_specs=pl.BlockSpec((tm, tn), lambda i, j, s: (i, j)),
        scratch_shapes=[pltpu.VMEM((tm, tn), F32)],
        compiler_params=pltpu.CompilerParams(dimension_semantics=("parallel", "parallel", "arbitrary"),
                                             vmem_limit_bytes=VMEM_LIMIT),
    )(a, b)


@jax.custom_vjp
def mm(x, w):
    return matmul(x, w, name="mm_fwd")


def _mm_fwd(x, w):
    return matmul(x, w, name="mm_fwd"), (x, w)


def _mm_bwd(res, dy):
    x, w = res
    dx = matmul(dy, w, tb=True, out_dtype=x.dtype, name="mm_dx")
    dw = matmul(x, dy, ta=True, out_dtype=w.dtype, name="mm_dw")
    return dx, dw


mm.defvjp(_mm_fwd, _mm_bwd)


def sum_leading(x, name):
    n, r, c = x.shape
    tr = _tile(r, (256, 128, 64, 32, 16, 8))

    def body(x_ref, o_ref):
        acc = x_ref[0].astype(F32)
        for i in range(1, n):
            acc = acc + x_ref[i].astype(F32)
        o_ref[...] = acc

    return pl.pallas_call(
        body, name=name,
        out_shape=jax.ShapeDtypeStruct((r, c), F32),
        grid=(r // tr,),
        in_specs=[pl.BlockSpec((n, tr, c), lambda i: (0, i, 0))],
        out_specs=pl.BlockSpec((tr, c), lambda i: (i, 0)),
        compiler_params=pltpu.CompilerParams(dimension_semantics=("parallel",), vmem_limit_bytes=VMEM_LIMIT),
    )(x)


def adamw(w, g, m, v, name):
    r, c = w.shape
    tr = _tile(r, (256, 128, 64, 32, 16, 8))

    def body(w_ref, g_ref, m_ref, v_ref, d_ref, mo_ref, vo_ref):
        gv = g_ref[...]
        mv = ADAM_B1 * m_ref[...] + (1.0 - ADAM_B1) * gv
        vv = ADAM_B2 * v_ref[...] + (1.0 - ADAM_B2) * jnp.square(gv)
        m_hat = mv / (1.0 - ADAM_B1 ** ADAM_STEP)
        v_hat = vv / (1.0 - ADAM_B2 ** ADAM_STEP)
        d_ref[...] = -ADAM_LR * (m_hat / (jnp.sqrt(v_hat) + ADAM_EPS) + ADAM_WD * w_ref[...])
        mo_ref[...] = mv
        vo_ref[...] = vv

    spec = pl.BlockSpec((tr, c), lambda i: (i, 0))
    return pl.pallas_call(
        body, name=name,
        out_shape=(jax.ShapeDtypeStruct((r, c), F32),) * 3,
        grid=(r // tr,),
        in_specs=[spec] * 4,
        out_specs=(spec,) * 3,
        compiler_params=pltpu.CompilerParams(dimension_semantics=("parallel",), vmem_limit_bytes=VMEM_LIMIT),
    )(w, g, m, v)


ROW_TILE = 256


def _row_tile(t):
    return _tile(t, (ROW_TILE, 128, 64, 32, 16, 8))


def _group_call(body, name, ins, in_kinds, out_shapes, out_kinds, tt):
    g, t = ins[0].shape[:2]

    def spec(kind, shape):
        if kind == 'tok':
            return pl.BlockSpec((1, tt, shape[-1]), lambda i, j: (i, j, 0))
        return pl.BlockSpec((1, 1, shape[-1]), lambda i, j: (i, 0, 0))

    return pl.pallas_call(
        body, name=name,
        out_shape=tuple(jax.ShapeDtypeStruct(s, F32) for s in out_shapes),
        grid=(g, t // tt),
        in_specs=[spec(k, a.shape) for k, a in zip(in_kinds, ins)],
        out_specs=tuple(spec(k, s) for k, s in zip(out_kinds, out_shapes)),
        compiler_params=pltpu.CompilerParams(dimension_semantics=("parallel", "arbitrary"),
                                             vmem_limit_bytes=VMEM_LIMIT),
    )(*ins)


def _accumulate(ref, val):
    @pl.when(pl.program_id(1) == 0)
    def _():
        ref[...] = jnp.zeros_like(ref)

    ref[0] += jnp.sum(val, axis=0, keepdims=True)


@jax.custom_vjp
def modulate_g(h, shift, scale):
    return _modulate_fwd(h, shift, scale)


def _modulate_fwd(h, shift, scale):
    def body(h_ref, sh_ref, sc_ref, o_ref):
        hv = h_ref[0]
        r = lax.rsqrt(jnp.mean(hv * hv, axis=-1, keepdims=True) + EPS)
        o_ref[0] = hv * r * (1.0 + sc_ref[0]) + sh_ref[0]

    return _group_call(body, "modulate_fwd", [h, shift, scale], ['tok', 'vec', 'vec'], [h.shape], ['tok'],
                       _row_tile(h.shape[1]))[0]


def _modulate_vjp_fwd(h, shift, scale):
    return _modulate_fwd(h, shift, scale), (h, scale)


def _modulate_vjp_bwd(res, du):
    h, scale = res

    def body(h_ref, sc_ref, du_ref, dh_ref, dsh_ref, dsc_ref):
        hv, dv = h_ref[0], du_ref[0]
        r = lax.rsqrt(jnp.mean(hv * hv, axis=-1, keepdims=True) + EPS)
        hn = hv * r
        dn = dv * (1.0 + sc_ref[0])
        dh_ref[0] = r * (dn - hn * jnp.mean(dn * hn, axis=-1, keepdims=True))
        _accumulate(dsh_ref, dv)
        _accumulate(dsc_ref, dv * hn)

    dh, dsh, dsc = _group_call(body, "modulate_bwd", [h, scale, du], ['tok', 'vec', 'tok'],
                               [h.shape, scale.shape, scale.shape], ['tok', 'acc', 'acc'], _row_tile(h.shape[1]))
    return dh, dsh, dsc


modulate_g.defvjp(_modulate_vjp_fwd, _modulate_vjp_bwd)


def _gated_add_call(h, y, gate, coef):
    def body(h_ref, y_ref, g_ref, o_ref):
        o_ref[0] = h_ref[0] + coef * g_ref[0] * y_ref[0]

    return _group_call(body, "gated_add_fwd", [h, y, gate], ['tok', 'tok', 'vec'], [h.shape], ['tok'],
                       _row_tile(h.shape[1]))[0]


def _gated_add_bwd_call(y, gate, dout, coef):
    def body(y_ref, g_ref, d_ref, dy_ref, dg_ref):
        dv = d_ref[0]
        dy_ref[0] = coef * g_ref[0] * dv
        _accumulate(dg_ref, coef * dv * y_ref[0])

    return _group_call(body, "gated_add_bwd", [y, gate, dout], ['tok', 'vec', 'tok'], [y.shape, gate.shape],
                       ['tok', 'acc'], _row_tile(y.shape[1]))


@functools.partial(jax.custom_vjp, nondiff_argnums=(3,))
def gated_add(h, y, gate, coef):
    return _gated_add_call(h, y, gate, coef)


def _gated_add_vjp_fwd(h, y, gate, coef):
    return _gated_add_call(h, y, gate, coef), (y, gate)


def _gated_add_vjp_bwd(coef, res, dout):
    y, gate = res
    dy, dg = _gated_add_bwd_call(y, gate, dout, coef)
    return dout, dy, dg


gated_add.defvjp(_gated_add_vjp_fwd, _gated_add_vjp_bwd)


@jax.custom_vjp
def swiglu_act(hid):
    return _swiglu_act_fwd(hid)


def _swiglu_act_fwd(hid):
    f = hid.shape[-1] // 2

    def body(h_ref, o_ref):
        gate, up = h_ref[0, :, 0:f], h_ref[0, :, f:2 * f]
        o_ref[0] = gate * jax.nn.sigmoid(gate) * up

    return _group_call(body, "swiglu_fwd", [hid], ['tok'], [hid.shape[:2] + (f,)], ['tok'],
                       _tile(hid.shape[1], (128, 64, 32, 16, 8)))[0]


def _swiglu_vjp_fwd(hid):
    return _swiglu_act_fwd(hid), (hid,)


def _swiglu_vjp_bwd(res, da):
    (hid,) = res
    f = hid.shape[-1] // 2

    def body(h_ref, da_ref, d_ref):
        gate, up, dv = h_ref[0, :, 0:f], h_ref[0, :, f:2 * f], da_ref[0]
        s = jax.nn.sigmoid(gate)
        d_ref[0, :, 0:f] = dv * up * (s * (1.0 + gate * (1.0 - s)))
        d_ref[0, :, f:2 * f] = dv * (gate * s)

    return (_group_call(body, "swiglu_bwd", [hid, da], ['tok', 'tok'], [hid.shape], ['tok'],
                        _tile(hid.shape[1], (128, 64, 32, 16, 8)))[0],)


swiglu_act.defvjp(_swiglu_vjp_fwd, _swiglu_vjp_bwd)


@jax.custom_vjp
def flip_rows(x):
    return _flip_rows_call(x)


def _flip_rows_call(x):
    n, length, c = x.shape
    tb = _tile(length, (256, 128, 64, 32, 16, 8))
    nb = length // tb

    def body(x_ref, o_ref):
        xv = x_ref[0]
        ii = lax.broadcasted_iota(jnp.int32, (tb, tb), 0)
        jj = lax.broadcasted_iota(jnp.int32, (tb, tb), 1)
        rev = (ii + jj == tb - 1).astype(BF16)
        hi = xv.astype(BF16)
        r1 = xv - hi.astype(F32)
        mid = r1.astype(BF16)
        lo = (r1 - mid.astype(F32)).astype(BF16)
        dot = functools.partial(jnp.dot, preferred_element_type=F32)
        o_ref[0] = (dot(rev, hi) + dot(rev, mid)) + dot(rev, lo)

    return pl.pallas_call(
        body, name="flip_rows",
        out_shape=jax.ShapeDtypeStruct(x.shape, F32),
        grid=(n, nb),
        in_specs=[pl.BlockSpec((1, tb, c), lambda i, j: (i, j, 0))],
        out_specs=pl.BlockSpec((1, tb, c), lambda i, j: (i, nb - 1 - j, 0)),
        compiler_params=pltpu.CompilerParams(dimension_semantics=("parallel", "parallel"),
                                             vmem_limit_bytes=VMEM_LIMIT),
    )(x)


flip_rows.defvjp(lambda x: (_flip_rows_call(x), None), lambda _, dy: (_flip_rows_call(dy),))


def _flip_time(t, axis):
    s = t.shape
    lead = math.prod(s[:axis])
    return flip_rows(t.reshape(lead, s[axis], -1)).reshape(s)


def rms_norm(x):
    return x * lax.rsqrt(jnp.mean(x * x, axis=-1, keepdims=True) + EPS)


def raster_to_column(t, rows):
    b, s, d = t.shape
    return t.reshape(b, rows, GRID_W, d).transpose(0, 2, 1, 3).reshape(b, s, d)


def column_to_raster(t, rows):
    b, s, d = t.shape
    return t.reshape(b, GRID_W, rows, d).transpose(0, 2, 1, 3).reshape(b, s, d)


def depthwise_conv(x, w, b):
    pad = A_CONV // 2
    y = lax.conv_general_dilated(x, w[:, None, :], window_strides=(1,), padding=[(pad, pad)],
                                 dimension_numbers=('NWC', 'WIO', 'NWC'), feature_group_count=x.shape[-1])
    return y + b


S5_STATES = B_NGROUPS * B_STATE
S5_ROWS = 8
S5_STEPS_FWD = 64
S5_STEPS_BWD = 32


def _s5_scan_fwd(u2, bd2, cd2, ar8, ai8):
    rows, width = u2.shape
    ns = S5_STATES
    tr = S5_ROWS * S5_STEPS_FWD
    assert rows % tr == 0

    def body(u_ref, bd_ref, cd_ref, ar_ref, ai_ref, y_ref, x_ref, st_ref):
        @pl.when(pl.program_id(0) == 0)
        def _():
            st_ref[...] = jnp.zeros_like(st_ref)

        x_ref[...] = jnp.dot(u_ref[...].astype(BF16), bd_ref[...], preferred_element_type=F32)
        ar, ai = ar_ref[...], ai_ref[...]

        def step(t, carry):
            xr, xi = carry
            r = pl.ds(pl.multiple_of(t * S5_ROWS, S5_ROWS), S5_ROWS)
            nr = ar * xr - ai * xi + x_ref[r, 0:ns]
            ni = ar * xi + ai * xr + x_ref[r, ns:2 * ns]
            x_ref[r, 0:ns] = nr
            x_ref[r, ns:2 * ns] = ni
            return nr, ni

        xr, xi = lax.fori_loop(0, S5_STEPS_FWD, step, (st_ref[:, 0:ns], st_ref[:, ns:2 * ns]), unroll=4)
        st_ref[:, 0:ns] = xr
        st_ref[:, ns:2 * ns] = xi
        y_ref[...] = jnp.dot(x_ref[...].astype(BF16), cd_ref[...], preferred_element_type=F32)

    whole = lambda shape: pl.BlockSpec(shape, lambda i: (0, 0))
    return pl.pallas_call(
        body, name="s5_scan_fwd",
        out_shape=(jax.ShapeDtypeStruct((rows, width), F32), jax.ShapeDtypeStruct((rows, 2 * ns), F32)),
        grid=(rows // tr,),
        in_specs=[pl.BlockSpec((tr, width), lambda i: (i, 0)), whole(bd2.shape), whole(cd2.shape),
                  whole(ar8.shape), whole(ai8.shape)],
        out_specs=(pl.BlockSpec((tr, width), lambda i: (i, 0)), pl.BlockSpec((tr, 2 * ns), lambda i: (i, 0))),
        scratch_shapes=[pltpu.VMEM((S5_ROWS, 2 * ns), F32)],
        compiler_params=pltpu.CompilerParams(dimension_semantics=("arbitrary",), vmem_limit_bytes=VMEM_LIMIT),
    )(u2, bd2, cd2, ar8, ai8)


def _s5_scan_bwd(dy, x, u2, bd2, cd2, ar8, ai8):
    rows, width = u2.shape
    ns = S5_STATES
    steps = S5_STEPS_BWD
    tr = S5_ROWS * steps
    nblk = rows // tr
    assert rows % tr == 0
    nt = (((1,), (1,)), ((), ()))
    tn = (((0,), (0,)), ((), ()))

    def body(dy_ref, x_ref, xp_ref, u_ref, bd_ref, cd_ref, ar_ref, ai_ref,
             du_ref, dbd_ref, dcd_ref, dar_ref, dai_ref, g_ref, st_ref):
        k = pl.program_id(0)

        @pl.when(k == 0)
        def _():
            st_ref[...] = jnp.zeros_like(st_ref)
            dbd_ref[...] = jnp.zeros_like(dbd_ref)
            dcd_ref[...] = jnp.zeros_like(dcd_ref)
            dar_ref[...] = jnp.zeros_like(dar_ref)
            dai_ref[...] = jnp.zeros_like(dai_ref)

        dyb = dy_ref[...].astype(BF16)
        g_ref[...] = lax.dot_general(dyb, cd_ref[...], nt, preferred_element_type=F32)
        ar, ai = ar_ref[...], ai_ref[...]

        def adjoint(r, carry, xpr, xpi):
            gr_n, gi_n, dar, dai = carry
            gr = g_ref[r, 0:ns] + ar * gr_n + ai * gi_n
            gi = g_ref[r, ns:2 * ns] - ai * gr_n + ar * gi_n
            g_ref[r, 0:ns] = gr
            g_ref[r, ns:2 * ns] = gi
            return gr, gi, dar + gr * xpr + gi * xpi, dai + gi * xpr - gr * xpi

        def step(i, carry):
            t = steps - 1 - i
            r = pl.ds(pl.multiple_of(t * S5_ROWS, S5_ROWS), S5_ROWS)
            rp = pl.ds(pl.multiple_of((t - 1) * S5_ROWS, S5_ROWS), S5_ROWS)
            return adjoint(r, carry, x_ref[rp, 0:ns], x_ref[rp, ns:2 * ns])

        zero = jnp.zeros((S5_ROWS, ns), F32)
        carry = lax.fori_loop(0, steps - 1, step, (st_ref[:, 0:ns], st_ref[:, ns:2 * ns], zero, zero), unroll=2)
        has_prev = (k < nblk - 1).astype(F32)
        gr, gi, dar, dai = adjoint(pl.ds(0, S5_ROWS), carry, xp_ref[:, 0:ns] * has_prev, xp_ref[:, ns:2 * ns] * has_prev)
        st_ref[:, 0:ns] = gr
        st_ref[:, ns:2 * ns] = gi
        dar_ref[...] += dar
        dai_ref[...] += dai
        gb = g_ref[...].astype(BF16)
        du_ref[...] = lax.dot_general(gb, bd_ref[...], nt, preferred_element_type=F32)
        dbd_ref[...] += lax.dot_general(u_ref[...].astype(BF16), gb, tn, preferred_element_type=F32)
        dcd_ref[...] += lax.dot_general(x_ref[...].astype(BF16), dyb, tn, preferred_element_type=F32)

    whole = lambda shape: pl.BlockSpec(shape, lambda k: (0, 0))
    rev = lambda k: (nblk - 1 - k, 0)
    prev = lambda k: (jnp.maximum((nblk - 1 - k) * steps - 1, 0), 0)
    return pl.pallas_call(
        body, name="s5_scan_bwd",
        out_shape=(jax.ShapeDtypeStruct((rows, width), F32), jax.ShapeDtypeStruct(bd2.shape, F32),
                   jax.ShapeDtypeStruct(cd2.shape, F32), jax.ShapeDtypeStruct(ar8.shape, F32),
                   jax.ShapeDtypeStruct(ai8.shape, F32)),
        grid=(nblk,),
        in_specs=[pl.BlockSpec((tr, width), rev), pl.BlockSpec((tr, 2 * ns), rev),
                  pl.BlockSpec((S5_ROWS, 2 * ns), prev), pl.BlockSpec((tr, width), rev),
                  whole(bd2.shape), whole(cd2.shape), whole(ar8.shape), whole(ai8.shape)],
        out_specs=(pl.BlockSpec((tr, width), rev), whole(bd2.shape), whole(cd2.shape), whole(ar8.shape),
                   whole(ai8.shape)),
        scratch_shapes=[pltpu.VMEM((tr, 2 * ns), F32), pltpu.VMEM((S5_ROWS, 2 * ns), F32)],
        compiler_params=pltpu.CompilerParams(dimension_semantics=("arbitrary",), vmem_limit_bytes=VMEM_LIMIT),
    )(dy, x, x, u2, bd2, cd2, ar8, ai8)


@jax.custom_vjp
def s5_core(u2, bd2, cd2, ar8, ai8):
    return _s5_scan_fwd(u2, bd2.astype(BF16), cd2.astype(BF16), ar8, ai8)[0]


def _s5_core_fwd(u2, bd2, cd2, ar8, ai8):
    bd2, cd2 = bd2.astype(BF16), cd2.astype(BF16)
    y, x = _s5_scan_fwd(u2, bd2, cd2, ar8, ai8)
    return y, (x, u2, bd2, cd2, ar8, ai8)


def _s5_core_bwd(res, dy):
    return _s5_scan_bwd(dy, *res)


s5_core.defvjp(_s5_core_fwd, _s5_core_bwd)


def s5_mixers(p_ctx, p_lat, lam_re, lam_im, log_step, b_re, b_im, c_re, c_im, d_skip, glu_w, glu_b):
    bsz = p_ctx.shape[0]
    assert 2 * bsz == S5_ROWS
    eye = jnp.eye(B_NGROUPS, dtype=F32)
    bds, cds, ars, ais = [], [], [], []
    for d in range(2):
        step = jnp.exp(log_step[d])[:, None]
        mag = jnp.exp(lam_re[d] * step)
        ar = mag * jnp.cos(lam_im[d] * step)
        ai = mag * jnp.sin(lam_im[d] * step)
        den = lam_re[d] * lam_re[d] + lam_im[d] * lam_im[d]
        nr = ar - 1.0
        kr = (nr * lam_re[d] + ai * lam_im[d]) / den
        ki = (ai * lam_re[d] - nr * lam_im[d]) / den
        br = kr[..., None] * b_re[d] - ki[..., None] * b_im[d]
        bi = kr[..., None] * b_im[d] + ki[..., None] * b_re[d]
        blk = lambda w: jnp.einsum('gnc,gh->gchn', w, eye).reshape(B_WIDTH, S5_STATES)
        bds.append(jnp.concatenate([blk(br), blk(bi)], axis=1))
        blk_c = lambda w: jnp.einsum('gcn,gh->gnhc', w, eye).reshape(S5_STATES, B_WIDTH)
        cds.append(jnp.concatenate([blk_c(c_re[d]), -blk_c(c_im[d])], axis=0))
        ars.append(jnp.broadcast_to(ar.reshape(1, S5_STATES), (bsz, S5_STATES)))
        ais.append(jnp.broadcast_to(ai.reshape(1, S5_STATES), (bsz, S5_STATES)))
    bd2 = jnp.concatenate(bds, axis=0)
    cd2 = jnp.concatenate(cds, axis=1)
    ar8 = jnp.concatenate(ars, axis=0)
    ai8 = jnp.concatenate(ais, axis=0)

    def rows_of(p):
        ut = jnp.swapaxes(p, 0, 1)
        z = jnp.zeros_like(ut)
        return jnp.concatenate([jnp.concatenate([ut, z], axis=-1), jnp.concatenate([z, _flip_time(ut, 0)], axis=-1)], axis=1)

    lc = p_ctx.shape[1]
    u2 = jnp.concatenate([rows_of(p_ctx), rows_of(p_lat)], axis=0)
    y2 = s5_core(u2.reshape(-1, 2 * B_WIDTH), bd2, cd2, ar8, ai8).reshape(u2.shape)

    def finish(y2p, p):
        y = y2p[:, :bsz, :B_WIDTH] + _flip_time(y2p[:, bsz:, B_WIDTH:], 0)
        y = jnp.swapaxes(y, 0, 1) + d_skip * p
        y = jax.nn.gelu(y)
        gate = mm(y.reshape(-1, B_WIDTH), glu_w).reshape(y.shape)
        return y * jax.nn.sigmoid(gate + glu_b)

    return finish(y2[:lc], p_ctx), finish(y2[lc:], p_lat)


GLA_CHUNK = 64
GLA_SUB = 16
GLA_NB = 4
NT_DIMS = (((1,), (1,)), ((), ()))
TN_DIMS = (((0,), (0,)), ((), ()))


def _bdot(a, b, dims=(((1,), (0,)), ((), ()))):
    return lax.dot_general(a.astype(BF16), b.astype(BF16), dims, preferred_element_type=F32)


def _hdot(a, b, dims=(((1,), (0,)), ((), ()))):
    ah, bh = a.astype(BF16), b.astype(BF16)
    al, bl = (a - ah.astype(F32)).astype(BF16), (b - bh.astype(F32)).astype(BF16)
    dot = functools.partial(lax.dot_general, dimension_numbers=dims, preferred_element_type=F32)
    return dot(ah, bh) + (dot(ah, bl) + dot(al, bh))


def _gla_scores(q, k, cum, cumr, tri):
    n = GLA_CHUNK
    if cumr is not None:
        decay = jnp.where(tri, jnp.exp(jnp.where(tri, cum - cumr, 0.0)), 0.0)
        return _bdot(q, k, NT_DIMS) * decay, decay
    rows = lax.broadcasted_iota(jnp.int32, (n, 1), 0)
    parts = []
    for i in range(n // GLA_SUB):
        lo, hi = i * GLA_SUB, (i + 1) * GLA_SUB
        ref = cum[lo - 1:lo, :] if i else jnp.zeros_like(cum[0:1, :])
        qt = q[lo:hi] * jnp.exp(cum[lo:hi] - ref)
        seen = rows < hi
        kh = jnp.where(seen, k * jnp.exp(jnp.where(seen, ref - cum, 0.0)), 0.0)
        parts.append(_bdot(qt, kh, NT_DIMS))
    return jnp.where(tri, jnp.concatenate(parts, axis=0), 0.0), None


def _gla_fwd(q, k, cum, cumr, v):
    nseq, length, dk = q.shape
    nc = length // GLA_CHUNK
    scalar = cumr is not None

    def body(*refs):
        if scalar:
            q_ref, k_ref, cum_ref, cumr_ref, v_ref, o_ref, s_ref, st_ref = refs
        else:
            q_ref, k_ref, cum_ref, v_ref, o_ref, s_ref, st_ref = refs

        @pl.when(pl.program_id(1) == 0)
        def _():
            st_ref[...] = jnp.zeros_like(st_ref)

        ii = lax.broadcasted_iota(jnp.int32, (GLA_CHUNK, GLA_CHUNK), 0)
        jj = lax.broadcasted_iota(jnp.int32, (GLA_CHUNK, GLA_CHUNK), 1)
        tri = jj <= ii
        for b in range(GLA_NB):
            qv, kv, cv, vv, st = q_ref[b], k_ref[b], cum_ref[b], v_ref[b], st_ref[b]
            s_ref[b, 0] = st
            a, _ = _gla_scores(qv, kv, cv, cumr_ref[b] if scalar else None, tri)
            o_ref[b] = _bdot(qv * jnp.exp(cv), st, NT_DIMS) + _bdot(a, vv)
            last = cv[GLA_CHUNK - 1:GLA_CHUNK, :]
            st_ref[b] = st * jnp.exp(last) + _bdot(vv, kv * jnp.exp(last - cv), TN_DIMS)

    seq = pl.BlockSpec((GLA_NB, GLA_CHUNK, dk), lambda n, c: (n, c, 0))
    state = pl.BlockSpec((GLA_NB, 1, dk, dk), lambda n, c: (n, c, 0, 0))
    ins = [q, k, cum] + ([cumr] if scalar else []) + [v]
    return pl.pallas_call(
        body, name="gla_fwd_scalar" if scalar else "gla_fwd",
        out_shape=(jax.ShapeDtypeStruct((nseq, length, dk), F32), jax.ShapeDtypeStruct((nseq, nc, dk, dk), F32)),
        grid=(nseq // GLA_NB, nc),
        in_specs=[seq] * len(ins),
        out_specs=(seq, state),
        scratch_shapes=[pltpu.VMEM((GLA_NB, dk, dk), F32)],
        compiler_params=pltpu.CompilerParams(dimension_semantics=("parallel", "arbitrary"),
                                             vmem_limit_bytes=VMEM_LIMIT),
    )(*ins)


def _gla_bwd(do, q, k, cum, cumr, v, states):
    nseq, length, dk = q.shape
    nc = length // GLA_CHUNK
    scalar = cumr is not None
    n = GLA_CHUNK

    def body(*refs):
        if scalar:
            do_ref, q_ref, k_ref, cum_ref, cumr_ref, v_ref, s_ref, dq_ref, dk_ref, dc_ref, dcr_ref, dv_ref, dst_ref = refs
        else:
            do_ref, q_ref, k_ref, cum_ref, v_ref, s_ref, dq_ref, dk_ref, dc_ref, dv_ref, dst_ref = refs

        @pl.when(pl.program_id(1) == 0)
        def _():
            dst_ref[...] = jnp.zeros_like(dst_ref)

        ii = lax.broadcasted_iota(jnp.int32, (n, n), 0)
        jj = lax.broadcasted_iota(jnp.int32, (n, n), 1)
        tri = jj <= ii
        rows = lax.broadcasted_iota(jnp.int32, (n, 1), 0)
        for b in range(GLA_NB):
            dov, qv, kv, cv, vv, st, dst = do_ref[b], q_ref[b], k_ref[b], cum_ref[b], v_ref[b], s_ref[b, 0], dst_ref[b]
            e = jnp.exp(cv)
            qe = qv * e
            last = cv[n - 1:n, :]
            w = jnp.exp(last - cv)
            kw = kv * w
            el = jnp.exp(last)
            d_qe = _hdot(dov, st)
            d_kw = _hdot(vv, dst)
            dv = _bdot(kw, dst, NT_DIMS)
            d_last = jnp.sum(st * dst, axis=0, keepdims=True) * el + jnp.sum(d_kw * kw, axis=0, keepdims=True)
            dst_ref[b] = dst * el + _bdot(dov, qe, TN_DIMS)
            dq = d_qe * e
            dkk = d_kw * w
            dc = d_qe * qe - d_kw * kw + jnp.where(rows == n - 1, d_last, 0.0)
            da = jnp.where(tri, _hdot(dov, vv, NT_DIMS), 0.0)
            if scalar:
                a, decay = _gla_scores(qv, kv, cv, cumr_ref[b], tri)
                dg = da * decay
                dq = dq + _bdot(dg, kv)
                dkk = dkk + _bdot(dg, qv, TN_DIMS)
                p = da * a
                dc = dc + p
                dcr_ref[b] = -p
            else:
                a_parts, dq_parts = [], []
                for i in range(n // GLA_SUB):
                    lo, hi = i * GLA_SUB, (i + 1) * GLA_SUB
                    ref = cv[lo - 1:lo, :] if i else jnp.zeros_like(cv[0:1, :])
                    eq = jnp.exp(cv[lo:hi] - ref)
                    qt = qv[lo:hi] * eq
                    seen = rows < hi
                    ek = jnp.where(seen, jnp.exp(jnp.where(seen, ref - cv, 0.0)), 0.0)
                    kh = kv * ek
                    a_parts.append(_bdot(qt, kh, NT_DIMS))
                    dqt = _hdot(da[lo:hi], kh)
                    dkh = _hdot(da[lo:hi], qt, TN_DIMS)
                    dq_parts.append((dqt * eq, dqt * qt))
                    dkk = dkk + dkh * ek
                    dc = dc - dkh * kh
                a = jnp.where(tri, jnp.concatenate(a_parts, axis=0), 0.0)
                dq = dq + jnp.concatenate([p[0] for p in dq_parts], axis=0)
                dc = dc + jnp.concatenate([p[1] for p in dq_parts], axis=0)
            dv_ref[b] = dv + _bdot(a, dov, TN_DIMS)
            dq_ref[b] = dq
            dk_ref[b] = dkk
            dc_ref[b] = dc

    seq = pl.BlockSpec((GLA_NB, n, dk), lambda s, c: (s, nc - 1 - c, 0))
    state = pl.BlockSpec((GLA_NB, 1, dk, dk), lambda s, c: (s, nc - 1 - c, 0, 0))
    ins = [do, q, k, cum] + ([cumr] if scalar else []) + [v]
    n_out = 5 if scalar else 4
    return pl.pallas_call(
        body, name="gla_bwd_scalar" if scalar else "gla_bwd",
        out_shape=(jax.ShapeDtypeStruct((nseq, length, dk), F32),) * n_out,
        grid=(nseq // GLA_NB, nc),
        in_specs=[seq] * len(ins) + [state],
        out_specs=(seq,) * n_out,
        scratch_shapes=[pltpu.VMEM((GLA_NB, dk, dk), F32)],
        compiler_params=pltpu.CompilerParams(dimension_semantics=("parallel", "arbitrary"),
                                             vmem_limit_bytes=VMEM_LIMIT),
    )(*ins, states)


@jax.custom_vjp
def gla(q, k, cum, v):
    return _gla_fwd(q, k, cum, None, v)[0]


def _gla_vjp_fwd(q, k, cum, v):
    o, states = _gla_fwd(q, k, cum, None, v)
    return o, (q, k, cum, v, states)


def _gla_vjp_bwd(res, do):
    q, k, cum, v, states = res
    return _gla_bwd(do, q, k, cum, None, v, states)


gla.defvjp(_gla_vjp_fwd, _gla_vjp_bwd)


@jax.custom_vjp
def gla_scalar(q, k, cum, cumr, v):
    return _gla_fwd(q, k, cum, cumr, v)[0]


def _gla_scalar_vjp_fwd(q, k, cum, cumr, v):
    o, states = _gla_fwd(q, k, cum, cumr, v)
    return o, (q, k, cum, cumr, v, states)


def _gla_scalar_vjp_bwd(res, do):
    q, k, cum, cumr, v, states = res
    return _gla_bwd(do, q, k, cum, cumr, v, states)


gla_scalar.defvjp(_gla_scalar_vjp_fwd, _gla_scalar_vjp_bwd)


def _chunk_cumsum(g):
    s = g.shape
    return jnp.cumsum(g.reshape(s[:-2] + (s[-2] // GLA_CHUNK, GLA_CHUNK, s[-1])), axis=-2).reshape(s)


def _both_ways(t_ctx, t_lat, flip):
    parts = [_flip_time(t, 1) if flip else t for t in (t_ctx, t_lat)]
    return jnp.swapaxes(jnp.concatenate(parts, axis=1), 1, 2)


def _undo_ways(o, lc, flip):
    o = jnp.swapaxes(o, 1, 2)
    parts = (o[:, :lc], o[:, lc:])
    return tuple(_flip_time(t, 1) if flip else t for t in parts)


def hgrn2_mixers(p_ctx, p_lat, lower, norm_w):
    bsz, lc = p_ctx.shape[:2]
    lower = lower.reshape(2, C_HEADS, C_KEY)

    def heads(p, lo, hi):
        return p[..., lo:hi].reshape(p.shape[:2] + (C_HEADS, -1))

    q_c, q_l = (jax.nn.silu(heads(p, 0, C_WIDTH)) for p in (p_ctx, p_lat))
    v_c, v_l = (heads(p, 3 * C_WIDTH, 4 * C_WIDTH) for p in (p_ctx, p_lat))
    qs, ks, cums, vs = [], [], [], []
    for d in range(2):
        f_c, f_l = (lower[d] + (1.0 - lower[d]) * jax.nn.sigmoid(heads(p, (1 + d) * C_WIDTH, (2 + d) * C_WIDTH))
                    for p in (p_ctx, p_lat))
        qs.append(_both_ways(q_c, q_l, d))
        vs.append(_both_ways(v_c, v_l, d))
        ks.append(_both_ways(1.0 - f_c, 1.0 - f_l, d))
        cums.append(_chunk_cumsum(_both_ways(jnp.log(f_c), jnp.log(f_l), d)))
    flat = lambda ts: jnp.stack(ts).reshape((-1,) + ts[0].shape[2:])
    o = gla(flat(qs), flat(ks), flat(cums), flat(vs)).reshape((2, bsz, C_HEADS, -1, C_VAL))
    f_c, f_l = _undo_ways(o[0], lc, False)
    b_c, b_l = _undo_ways(o[1], lc, True)
    outs = []
    for o_sum, p in ((f_c + b_c, p_ctx), (f_l + b_l, p_lat)):
        o_n = rms_norm(o_sum) * norm_w.reshape(C_HEADS, C_VAL)
        outs.append(o_n.reshape(p.shape[:2] + (C_WIDTH,)) * jax.nn.silu(p[..., 4 * C_WIDTH:]))
    return tuple(outs)


def ssd_mixers(p_ctx, p_lat, conv_w, conv_b, dt_bias, a_log, d_skip, norm_w):
    bsz, lc = p_ctx.shape[:2]
    rep = A_HEADS // A_GROUPS
    a = -jnp.exp(a_log)
    xs, bs, cs, dts, zs = [], [], [], [], []
    for p in (p_ctx, p_lat):
        z, xbc, dt_raw = jnp.split(p, [A_INNER, A_INNER + A_CONV_DIM], axis=-1)
        xbc = jax.nn.silu(depthwise_conv(xbc, conv_w, conv_b))
        x_, b_, c_ = jnp.split(xbc, [A_INNER, A_INNER + A_GROUPS * A_STATE], axis=-1)
        shp = p.shape[:2]
        xs.append(x_.reshape(shp + (A_HEADS, A_HEAD_DIM)))
        bs.append(jnp.repeat(b_.reshape(shp + (A_GROUPS, A_STATE)), rep, axis=2))
        cs.append(jnp.repeat(c_.reshape(shp + (A_GROUPS, A_STATE)), rep, axis=2))
        dts.append(jax.nn.softplus(dt_raw.reshape(shp + (2, A_HEADS)) + dt_bias))
        zs.append(z)
    qs, ks, cums, cumrs, vs = [], [], [], [], []
    for d in range(2):
        qs.append(_both_ways(cs[0], cs[1], d))
        vs.append(_both_ways(xs[0], xs[1], d))
        ks.append(_both_ways(bs[0] * dts[0][:, :, d, :, None], bs[1] * dts[1][:, :, d, :, None], d))
        adt = [_flip_time(t, 1) if d else t for t in (dt[:, :, d, :] * a[d] for dt in dts)]
        adt = jnp.swapaxes(jnp.concatenate(adt, axis=1), 1, 2)
        by_chunk = jnp.cumsum(adt.reshape(adt.shape[:2] + (-1, GLA_CHUNK)), axis=-1)
        full = adt.shape + (A_STATE,)
        cums.append(jnp.broadcast_to(by_chunk.reshape(adt.shape)[..., None], full))
        cumrs.append(jnp.broadcast_to(by_chunk[:, :, :, None, :], by_chunk.shape[:3] + (GLA_CHUNK, GLA_CHUNK)).reshape(full))
    flat = lambda ts: jnp.stack(ts).reshape((-1,) + ts[0].shape[2:])
    o = gla_scalar(flat(qs), flat(ks), flat(cums), flat(cumrs), flat(vs)).reshape((2, bsz, A_HEADS, -1, A_HEAD_DIM))
    f_c, f_l = _undo_ways(o[0], lc, False)
    b_c, b_l = _undo_ways(o[1], lc, True)
    outs = []
    for y, x_, z in ((f_c + b_c, xs[0], zs[0]), (f_l + b_l, xs[1], zs[1])):
        y = y + d_skip[:, None] * x_
        y = y.reshape(z.shape) * jax.nn.silu(z)
        outs.append(rms_norm(y) * norm_w)
    return tuple(outs)


def token_mixers(p_ctx, p_lat, W, l, lower):
    def cut(p):
        return p[..., :A_COLS], p[..., 1408:1408 + B_COLS], p[..., 1664:1664 + C_COLS]

    pa_c, pb_c, pc_c = cut(p_ctx)
    pa_l, pb_l, pc_l = cut(p_lat)
    ya_c, ya_l = ssd_mixers(pa_c, pa_l, W['a_conv_w'][l], W['a_conv_b'][l], W['a_dt_bias'][l], W['a_log'][l],
                            W['a_d'][l], W['a_norm_w'][l])
    yb_c, yb_l = s5_mixers(pb_c, pb_l, W['s5_lam_re'][l], W['s5_lam_im'][l], W['s5_log_step'][l], W['s5_b_re'][l],
                           W['s5_b_im'][l], W['s5_c_re'][l], W['s5_c_im'][l], W['s5_d'][l], W['s5_glu_w'][l],
                           W['s5_glu_b'][l])
    yc_c, yc_l = hgrn2_mixers(pc_c, pc_l, lower, W['hg_norm_w'][l])
    return (jnp.concatenate([ya_c, yb_c, yc_c], axis=-1), jnp.concatenate([ya_l, yb_l, yc_l], axis=-1))


def _pad_w_in(w):
    z = functools.partial(jnp.zeros, dtype=w.dtype)
    return jnp.concatenate([w[:, :A_COLS], z((D_MODEL, 1408 - A_COLS)), w[:, A_COLS:], z((D_MODEL, IN_PAD - 2944))],
                           axis=1)


def _mm3(t, w):
    g, tt, k = t.shape
    return mm(t.reshape(g * tt, k), w).reshape(g, tt, -1)


def _ffn(h, mg, first, w_in, w_out):
    u = modulate_g(h, mg[:, first:first + 1], mg[:, first + 1:first + 2])
    f = _mm3(swiglu_act(_mm3(u, w_in)), w_out)
    return gated_add(h, f, mg[:, first + 2:first + 3], 0.5)


def local_loss(x, W, m_lat, m_ctx, ctx, target):
    bsz, seq, dm = x.shape
    lc = ctx.shape[1]
    tg = bsz * lc
    assert seq % tg == 0
    gl = seq // tg
    ng = bsz * gl
    rows = seq // GRID_W
    p_lb = jax.nn.softmax(W['hg_lb_logits'], axis=0)
    lower_bounds = jnp.cumsum(p_lb, axis=0) - p_lb[:1]
    h = jnp.concatenate([x.reshape(ng, tg, dm), ctx.reshape(1, tg, dm)], axis=0)
    for l in range(DEPTH):
        last = l == DEPTH - 1
        col_major = l % 2 == 1
        mg = jnp.concatenate([jnp.repeat(m_lat[l], gl, axis=0), m_ctx[l][None]], axis=0)
        h = _ffn(h, mg, 0, W['ffn_w_in'][l][0], W['ffn_w_out'][l][0])
        u = modulate_g(h, mg[:, 3:4], mg[:, 4:5])
        if col_major:
            u_lat = raster_to_column(u[:ng].reshape(bsz, seq, dm), rows)
            u = jnp.concatenate([u_lat.reshape(ng, tg, dm), u[ng:]], axis=0)
        p = _mm3(u, _pad_w_in(W['w_in'][l]))
        mix_ctx, mix_lat = token_mixers(p[ng].reshape(bsz, lc, -1), p[:ng].reshape(bsz, seq, -1), W, l,
                                        lower_bounds[l])
        if last:
            h, mg = h[:ng], mg[:ng]
            y_lat = _mm3(mix_lat.reshape(ng, tg, dm), W['w_out'][l])
            y_ctx = None
        else:
            y = _mm3(jnp.concatenate([mix_lat.reshape(ng, tg, dm), mix_ctx.reshape(1, tg, dm)], axis=0), W['w_out'][l])
            y_lat, y_ctx = y[:ng], y[ng:]
        if col_major:
            y_lat = column_to_raster(y_lat.reshape(bsz, seq, dm), rows).reshape(ng, tg, dm)
        y = y_lat if y_ctx is None else jnp.concatenate([y_lat, y_ctx], axis=0)
        h = gated_add(h, y, mg[:, 5:6], 1.0)
        h = _ffn(h, mg, 6, W['ffn_w_in'][l][1], W['ffn_w_out'][l][1])
    y = rms_norm(h[:ng].reshape(bsz, seq, dm)) * W['final_norm_w']
    err = jnp.square(y - target)
    return 0.5 * jnp.sum(jnp.mean(err, axis=-1))


def _pad_rows(a, rows):
    return jnp.concatenate([a, jnp.zeros((rows - a.shape[0],) + a.shape[1:], a.dtype)], axis=0)


def kernel(x, c, ctx, c_ctx, mod_w, mod_b, ffn_w_in, ffn_w_out, w_in, w_out, a_conv_w, a_conv_b, a_dt_bias, a_log, a_d, a_norm_w, s5_lam_re, s5_lam_im, s5_log_step, s5_b_re, s5_b_im, s5_c_re, s5_c_im, s5_d, s5_glu_w, s5_glu_b, hg_lb_logits, hg_norm_w, final_norm_w, loss_target, m_c_ctx, m_mod_w, m_mod_b, m_ffn_w_in, m_ffn_w_out, m_w_in, m_w_out, m_a_conv_w, m_a_conv_b, m_a_dt_bias, m_a_log, m_a_d, m_a_norm_w, m_s5_lam_re, m_s5_lam_im, m_s5_log_step, m_s5_b_re, m_s5_b_im, m_s5_c_re, m_s5_c_im, m_s5_d, m_s5_glu_w, m_s5_glu_b, m_hg_lb_logits, m_hg_norm_w, m_final_norm_w, v_c_ctx, v_mod_w, v_mod_b, v_ffn_w_in, v_ffn_w_out, v_w_in, v_w_out, v_a_conv_w, v_a_conv_b, v_a_dt_bias, v_a_log, v_a_d, v_a_norm_w, v_s5_lam_re, v_s5_lam_im, v_s5_log_step, v_s5_b_re, v_s5_b_im, v_s5_c_re, v_s5_c_im, v_s5_d, v_s5_glu_w, v_s5_glu_b, v_hg_lb_logits, v_hg_norm_w, v_final_norm_w):
    given = dict(locals())
    w_loc = {n: given[n] for n in WEIGHTS}
    m_loc = {n: given["m_" + n] for n in WEIGHTS}
    v_loc = {n: given["v_" + n] for n in WEIGHTS}
    bsz = x.shape[0]
    me = 4 * lax.axis_index("x") + 2 * lax.axis_index("y") + lax.axis_index("c")

    small_sh = [c] + [w_loc[n] for n in SMALL_SHARDED]
    g1 = _unpack(all_gather([_pack(small_sh, 128, 8)], "gather_small")[0], [a.shape for a in small_sh])
    c_all = g1[0].reshape(N_DEV * bsz, D_MODEL)
    gathered = dict(zip(BIG, all_gather([w_loc[n].astype(BF16) for n in BIG], "gather_weights")))
    W = {'ffn_w_in': [[_assemble(gathered['ffn_w_in'][:, l, i], 1) for i in range(2)] for l in range(DEPTH)],
         'ffn_w_out': [[_assemble(gathered['ffn_w_out'][:, l, i], 0) for i in range(2)] for l in range(DEPTH)],
         'w_in': [_assemble(gathered['w_in'][:, l], 1) for l in range(DEPTH)],
         'w_out': [_assemble(gathered['w_out'][:, l], 0) for l in range(DEPTH)]}
    for (n, ax), t in zip(SMALL_SHARDED.items(), g1[1:]):
        W[n] = _assemble(t, ax)
    for n in SMALL:
        if n not in SMALL_SHARDED and n not in ('c_ctx', 'mod_b'):
            W[n] = w_loc[n]

    n_rows = N_DEV * bsz + 1
    pad_rows = 8 * ((n_rows + 7) // 8)
    c_rows = _pad_rows(jnp.concatenate([c_all, c_ctx[None]], axis=0), pad_rows)
    sc = jax.nn.silu(c_rows)
    mods_sh = jnp.stack([matmul(sc, mod_w[l], name="mod_fwd") for l in range(DEPTH)])
    mods = _assemble(all_gather([mods_sh], "gather_mods")[0], 2) + mod_b[:, None, :]
    m_lat = lax.dynamic_slice_in_dim(mods, me * bsz, bsz, axis=1).reshape(DEPTH, bsz, N_MOD, D_MODEL)
    m_ctx = mods[:, n_rows - 1].reshape(DEPTH, N_MOD, D_MODEL)

    loss_loc, (grad_x, gW, gm_lat, gm_ctx) = jax.value_and_grad(local_loss, argnums=(0, 1, 2, 3))(
        x, W, m_lat, m_ctx, ctx, loss_target)
    loss = lax.psum(loss_loc, MESH_AXES)

    dm_loc = jnp.concatenate([gm_lat.reshape(DEPTH, bsz, -1), gm_ctx.reshape(DEPTH, 1, -1)], axis=1)
    (dm_all,) = all_gather([dm_loc], "gather_dmods")
    dm_ex = jnp.moveaxis(dm_all[:, :, :bsz], 0, 1).reshape(DEPTH, N_DEV * bsz, -1)
    ncol = N_MOD * D_MODEL
    dm_cx = sum_leading(dm_all[:, :, bsz].reshape(N_DEV, DEPTH * ncol // 128, 128), "sum_dmods_ctx")
    dm_cx = dm_cx.reshape(DEPTH, 1, ncol)
    dm_rows = jnp.concatenate([dm_ex, dm_cx, jnp.zeros((DEPTH, pad_rows - n_rows, ncol), F32)], axis=1)
    grad_mod_b = sum_leading(jnp.moveaxis(dm_rows, 1, 0).reshape(pad_rows, DEPTH * ncol // 128, 128),
                             "sum_mod_b").reshape(DEPTH, ncol)
    my_cols = ncol // N_DEV
    dm_mine = lax.dynamic_slice_in_dim(dm_rows, me * my_cols, my_cols, axis=2)
    grad_mod_w = jnp.stack([matmul(sc, dm_mine[l], ta=True, name="mod_dw") for l in range(DEPTH)])
    dm_cx_mine = lax.dynamic_slice_in_dim(dm_cx, me * my_cols, my_cols, axis=2)
    g_sc_ctx = sum(matmul(_pad_rows(dm_cx_mine[l], 8), mod_w[l], tb=True, name="mod_dc")[0] for l in range(DEPTH))

    small_full = [n for n in SMALL if n not in ('c_ctx', 'mod_b')]
    part = [gW[n] for n in small_full] + [g_sc_ctx]
    red = sum_leading(all_gather([_pack(part, 128, 8)], "gather_small_grads")[0], "sum_small_grads")
    red = _unpack(red, [a.shape for a in part])
    grads = dict(zip(small_full, red[:-1]))
    sig = jax.nn.sigmoid(c_ctx)
    grads['c_ctx'] = red[-1] * (sig * (1.0 + c_ctx * (1.0 - sig)))
    grads['mod_b'] = grad_mod_b
    for n, ax in SMALL_SHARDED.items():
        size = w_loc[n].shape[ax]
        grads[n] = lax.dynamic_slice_in_dim(grads[n], me * size, size, axis=ax)
    grads['mod_w'] = grad_mod_w

    by_dev = {'ffn_w_in': jnp.stack([jnp.stack([_split(g, 1) for g in gl], axis=1) for gl in gW['ffn_w_in']], axis=1),
              'ffn_w_out': jnp.stack([jnp.stack([_split(g, 0) for g in gl], axis=1) for gl in gW['ffn_w_out']], axis=1),
              'w_in': jnp.stack([_split(g, 1) for g in gW['w_in']], axis=1),
              'w_out': jnp.stack([_split(g, 0) for g in gW['w_out']], axis=1)}
    for n, t in zip(BIG, all_to_all([by_dev[n] for n in BIG], "exchange_grads")):
        grads[n] = sum_leading(t.reshape((N_DEV,) + _as_2d(t.shape[1:])), "sum_grads").reshape(t.shape[1:])

    delta, new_m, new_v = {}, {}, {}

    for n in list(BIG) + ['mod_w']:
        outs = adamw(*[d[n].reshape(_as_2d(d[n].shape)) for d in (w_loc, grads, m_loc, v_loc)], name="adamw_" + n)
        delta[n], new_m[n], new_v[n] = (o.reshape(w_loc[n].shape) for o in outs)

    def update(names, width, row_mult, tag):
        packed = [_pack([d[n] for n in names], width, row_mult) for d in (w_loc, grads, m_loc, v_loc)]
        outs = adamw(*packed, name="adamw_" + tag)
        shapes = [w_loc[n].shape for n in names]
        for res, out in zip((delta, new_m, new_v), outs):
            res.update(zip(names, _unpack(out, shapes)))

    update(SMALL, 128, 256, "small")
    return (loss, grad_x, *[grads[n] for n in WEIGHTS], *[delta[n] for n in WEIGHTS],
            *[new_m[n] for n in WEIGHTS], *[new_v[n] for n in WEIGHTS])
```

```python
import functools
import math

import jax
import jax.numpy as jnp
from jax import lax
from jax.experimental import pallas as pl
from jax.experimental.pallas import tpu as pltpu

F32 = jnp.float32
BF16 = jnp.bfloat16
MESH_AXES = ("x", "y", "c")
N_DEV = 8
MESH_ID = pl.DeviceIdType.MESH
VMEM_LIMIT = 48 * 1024 * 1024

D_MODEL = 1024
DEPTH = 2
GRID_W = 64
EPS = 1e-6
N_MOD = 9
D_FF = 2816
A_INNER = 512
A_HEADS = 8
A_HEAD_DIM = 64
A_GROUPS = 2
A_STATE = 64
A_CONV = 5
A_CONV_DIM = A_INNER + 2 * A_GROUPS * A_STATE
A_COLS = A_INNER + A_CONV_DIM + 2 * A_HEADS
B_WIDTH = 256
B_GROUP = 16
B_NGROUPS = 16
B_STATE = 64
B_COLS = B_WIDTH
C_WIDTH = 256
C_HEADS = 4
C_KEY = 64
C_VAL = 64
C_COLS = 5 * C_WIDTH
IN_PAD = 3072

ADAM_LR = 0.001
ADAM_B1 = 0.9
ADAM_B2 = 0.999
ADAM_EPS = 1e-08
ADAM_WD = 0.01
ADAM_STEP = 10

WEIGHTS = ['c_ctx', 'mod_w', 'mod_b', 'ffn_w_in', 'ffn_w_out', 'w_in', 'w_out', 'a_conv_w', 'a_conv_b', 'a_dt_bias',
           'a_log', 'a_d', 'a_norm_w', 's5_lam_re', 's5_lam_im', 's5_log_step', 's5_b_re', 's5_b_im', 's5_c_re',
           's5_c_im', 's5_d', 's5_glu_w', 's5_glu_b', 'hg_lb_logits', 'hg_norm_w', 'final_norm_w']
BIG = {'ffn_w_in': 3, 'ffn_w_out': 2, 'w_in': 2, 'w_out': 1}
SMALL_SHARDED = {'a_conv_w': 2, 's5_glu_w': 1, 'hg_lb_logits': 2}
SMALL = [n for n in WEIGHTS if n not in BIG and n != 'mod_w']


def _tile(d, prefs):
    for p in prefs:
        if d % p == 0:
            return p
    return d


def _pack(arrs, width, row_mult):
    flat = jnp.concatenate([a.reshape(-1) for a in arrs])
    pad = (-flat.shape[0]) % (width * row_mult)
    if pad:
        flat = jnp.concatenate([flat, jnp.zeros((pad,), flat.dtype)])
    return flat.reshape(-1, width)


def _as_2d(shape):
    return (math.prod(shape[:-1]), shape[-1])


def _unpack(buf, shapes):
    lead = buf.shape[:-2]
    flat = buf.reshape(lead + (-1,))
    out, off = [], 0
    for s in shapes:
        n = math.prod(s)
        out.append(flat[..., off:off + n].reshape(lead + tuple(s)))
        off += n
    return out


def _assemble(g, axis):
    t = jnp.moveaxis(g, 0, axis)
    s = t.shape
    return t.reshape(s[:axis] + (s[axis] * s[axis + 1],) + s[axis + 2:])


def _split(full, axis):
    s = full.shape
    t = full.reshape(s[:axis] + (N_DEV, s[axis] // N_DEV) + s[axis + 1:])
    return jnp.moveaxis(t, axis, 0)


def all_gather(xs, name):
    nt = len(xs)

    def body(*refs):
        x_refs, out_refs = refs[:nt], refs[nt:2 * nt]
        send_sems, recv_sems, local_sems = refs[2 * nt:]
        ax, ay, ac = lax.axis_index("x"), lax.axis_index("y"), lax.axis_index("c")
        me, sibling = (ax, ay, ac), (ax, ay, 1 - ac)
        chips = [(1 - ax, ay), (ax, 1 - ay), (1 - ax, 1 - ay)]

        def slot(t, px, py, pc):
            return out_refs[t].at[4 * px + 2 * py + pc]

        def copy(t, k, block, to, src=None):
            return pltpu.make_async_remote_copy(
                src_ref=slot(t, *block) if src is None else src, dst_ref=slot(t, *block),
                send_sem=send_sems.at[t, k], recv_sem=recv_sems.at[t, k], device_id=to, device_id_type=MESH_ID)

        mine = [pltpu.make_async_copy(x_refs[t], slot(t, *me), local_sems.at[t]) for t in range(nt)]
        for cp in mine:
            cp.start()
        first = []
        for t in range(nt):
            first.append(copy(t, 0, me, sibling, src=x_refs[t]))
            first += [copy(t, 1 + j, me, (*chip, ac), src=x_refs[t]) for j, chip in enumerate(chips)]
        for cp in first:
            cp.start()
        passed = []
        for j, chip in enumerate(chips):
            for t in range(nt):
                copy(t, 1 + j, (*chip, ac), me).wait_recv()
                passed.append(copy(t, 4 + j, (*chip, ac), sibling))
                passed[-1].start()
        for t in range(nt):
            copy(t, 0, sibling, me).wait_recv()
            for j, chip in enumerate(chips):
                copy(t, 4 + j, (*chip, 1 - ac), me).wait_recv()
        for cp in first + passed:
            cp.wait_send()
        for cp in mine:
            cp.wait()

    return pl.pallas_call(
        body, name=name,
        out_shape=tuple(jax.ShapeDtypeStruct((N_DEV,) + x.shape, x.dtype) for x in xs),
        in_specs=[pl.BlockSpec(memory_space=pl.ANY)] * nt,
        out_specs=tuple(pl.BlockSpec(memory_space=pl.ANY) for _ in xs),
        scratch_shapes=[pltpu.SemaphoreType.DMA((nt, 7)), pltpu.SemaphoreType.DMA((nt, 7)),
                        pltpu.SemaphoreType.DMA((nt,))],
    )(*xs)


def all_to_all(gs, name):
    nt = len(gs)

    def body(*refs):
        g_refs, out_refs = refs[:nt], refs[nt:2 * nt]
        send_sems, recv_sems, local_sems = refs[2 * nt:]
        ax, ay, ac = lax.axis_index("x"), lax.axis_index("y"), lax.axis_index("c")
        my = 4 * ax + 2 * ay + ac
        local = [pltpu.make_async_copy(g_refs[t].at[my], out_refs[t].at[my], local_sems.at[t]) for t in range(nt)]
        for cp in local:
            cp.start()
        peers = []
        for r in range(1, N_DEV):
            px = 1 - ax if r & 4 else ax
            py = 1 - ay if r & 2 else ay
            pc = 1 - ac if r & 1 else ac
            peers.append((px, py, pc))

        def copy(t, k, peer):
            return pltpu.make_async_remote_copy(
                src_ref=g_refs[t].at[4 * peer[0] + 2 * peer[1] + peer[2]], dst_ref=out_refs[t].at[my],
                send_sem=send_sems.at[t, k], recv_sem=recv_sems.at[t, k], device_id=peer, device_id_type=MESH_ID)

        def arrival(t, k, peer):
            slot = 4 * peer[0] + 2 * peer[1] + peer[2]
            return pltpu.make_async_remote_copy(
                src_ref=g_refs[t].at[slot], dst_ref=out_refs[t].at[slot],
                send_sem=send_sems.at[t, k], recv_sem=recv_sems.at[t, k], device_id=peer, device_id_type=MESH_ID)

        sends = [copy(t, k, p) for t in range(nt) for k, p in enumerate(peers)]
        for cp in sends:
            cp.start()
        for t in range(nt):
            for k, p in enumerate(peers):
                arrival(t, k, p).wait_recv()
        for cp in sends:
            cp.wait_send()
        for cp in local:
            cp.wait()

    return pl.pallas_call(
        body, name=name,
        out_shape=tuple(jax.ShapeDtypeStruct(g.shape, g.dtype) for g in gs),
        in_specs=[pl.BlockSpec(memory_space=pl.ANY)] * nt,
        out_specs=tuple(pl.BlockSpec(memory_space=pl.ANY) for _ in gs),
        scratch_shapes=[pltpu.SemaphoreType.DMA((nt, 7)), pltpu.SemaphoreType.DMA((nt, 7)),
                        pltpu.SemaphoreType.DMA((nt,))],
    )(*gs)


def matmul(a, b, *, ta=False, tb=False, out_dtype=F32, name="mm"):
    m, k = (a.shape[1], a.shape[0]) if ta else a.shape
    n = b.shape[0] if tb else b.shape[1]
    assert (b.shape[1] if tb else b.shape[0]) == k, (a.shape, b.shape, ta, tb)
    tm = _tile(m, (1024, 512, 256, 128))
    tn = _tile(n, (1408, 1024, 512, 384, 256, 128))
    tk = _tile(k, (1024, 512, 256, 128))
    nk = k // tk
    dims = (((0 if ta else 1,), (1 if tb else 0,)), ((), ()))

    def body(a_ref, b_ref, o_ref, acc_ref):
        step = pl.program_id(2)

        @pl.when(step == 0)
        def _():
            acc_ref[...] = jnp.zeros_like(acc_ref)

        acc_ref[...] += lax.dot_general(a_ref[...].astype(BF16), b_ref[...].astype(BF16), dims,
                                        preferred_element_type=F32)

        @pl.when(step == nk - 1)
        def _():
            o_ref[...] = acc_ref[...].astype(out_dtype)

    a_spec = pl.BlockSpec((tk, tm), lambda i, j, s: (s, i)) if ta else pl.BlockSpec((tm, tk), lambda i, j, s: (i, s))
    b_spec = pl.BlockSpec((tn, tk), lambda i, j, s: (j, s)) if tb else pl.BlockSpec((tk, tn), lambda i, j, s: (s, j))
    return pl.pallas_call(
        body, name=name,
        out_shape=jax.ShapeDtypeStruct((m, n), out_dtype),
        grid=(m // tm, n // tn, nk),
        in_specs=[a_spec, b_spec],
        out_specs=pl.BlockSpec((tm, tn), lambda i, j, s: (i, j)),
        scratch_shapes=[pltpu.VMEM((tm, tn), F32)],
        compiler_params=pltpu.CompilerParams(dimension_semantics=("parallel", "parallel", "arbitrary"),
                                             vmem_limit_bytes=VMEM_LIMIT),
    )(a, b)


@jax.custom_vjp
def mm(x, w):
    return matmul(x, w, name="mm_fwd")


def _mm_fwd(x, w):
    return matmul(x, w, name="mm_fwd"), (x, w)


def _mm_bwd(res, dy):
    x, w = res
    dx = matmul(dy, w, tb=True, out_dtype=x.dtype, name="mm_dx")
    dw = matmul(x, dy, ta=True, out_dtype=w.dtype, name="mm_dw")
    return dx, dw


mm.defvjp(_mm_fwd, _mm_bwd)


def sum_leading(x, name):
    n, r, c = x.shape
    tr = _tile(r, (256, 128, 64, 32, 16, 8))

    def body(x_ref, o_ref):
        acc = x_ref[0].astype(F32)
        for i in range(1, n):
            acc = acc + x_ref[i].astype(F32)
        o_ref[...] = acc

    return pl.pallas_call(
        body, name=name,
        out_shape=jax.ShapeDtypeStruct((r, c), F32),
        grid=(r // tr,),
        in_specs=[pl.BlockSpec((n, tr, c), lambda i: (0, i, 0))],
        out_specs=pl.BlockSpec((tr, c), lambda i: (i, 0)),
        compiler_params=pltpu.CompilerParams(dimension_semantics=("parallel",), vmem_limit_bytes=VMEM_LIMIT),
    )(x)


def adamw(w, g, m, v, name):
    r, c = w.shape
    tr = _tile(r, (256, 128, 64, 32, 16, 8))

    def body(w_ref, g_ref, m_ref, v_ref, d_ref, mo_ref, vo_ref):
        gv = g_ref[...]
        mv = ADAM_B1 * m_ref[...] + (1.0 - ADAM_B1) * gv
        vv = ADAM_B2 * v_ref[...] + (1.0 - ADAM_B2) * jnp.square(gv)
        m_hat = mv / (1.0 - ADAM_B1 ** ADAM_STEP)
        v_hat = vv / (1.0 - ADAM_B2 ** ADAM_STEP)
        d_ref[...] = -ADAM_LR * (m_hat / (jnp.sqrt(v_hat) + ADAM_EPS) + ADAM_WD * w_ref[...])
        mo_ref[...] = mv
        vo_ref[...] = vv

    spec = pl.BlockSpec((tr, c), lambda i: (i, 0))
    return pl.pallas_call(
        body, name=name,
        out_shape=(jax.ShapeDtypeStruct((r, c), F32),) * 3,
        grid=(r // tr,),
        in_specs=[spec] * 4,
        out_specs=(spec,) * 3,
        compiler_params=pltpu.CompilerParams(dimension_semantics=("parallel",), vmem_limit_bytes=VMEM_LIMIT),
    )(w, g, m, v)


ROW_TILE = 256


def _row_tile(t):
    return _tile(t, (ROW_TILE, 128, 64, 32, 16, 8))


def _group_call(body, name, ins, in_kinds, out_shapes, out_kinds, tt, out_dtypes=None):
    g, t = ins[0].shape[:2]

    def spec(kind, shape):
        if kind == 'tok':
            return pl.BlockSpec((1, tt, shape[-1]), lambda i, j: (i, j, 0))
        return pl.BlockSpec((1, 1, shape[-1]), lambda i, j: (i, 0, 0))

    return pl.pallas_call(
        body, name=name,
        out_shape=tuple(jax.ShapeDtypeStruct(s, d) for s, d in zip(out_shapes, out_dtypes or [F32] * len(out_shapes))),
        grid=(g, t // tt),
        in_specs=[spec(k, a.shape) for k, a in zip(in_kinds, ins)],
        out_specs=tuple(spec(k, s) for k, s in zip(out_kinds, out_shapes)),
        compiler_params=pltpu.CompilerParams(dimension_semantics=("parallel", "arbitrary"),
                                             vmem_limit_bytes=VMEM_LIMIT),
    )(*ins)


def _accumulate(ref, val):
    @pl.when(pl.program_id(1) == 0)
    def _():
        ref[...] = jnp.zeros_like(ref)

    ref[0] += jnp.sum(val, axis=0, keepdims=True)


def _modulate_fwd(h, shift, scale, out_dtype):
    def body(h_ref, sh_ref, sc_ref, o_ref):
        hv = h_ref[0]
        r = lax.rsqrt(jnp.mean(hv * hv, axis=-1, keepdims=True) + EPS)
        o_ref[0] = (hv * r * (1.0 + sc_ref[0]) + sh_ref[0]).astype(out_dtype)

    return _group_call(body, "modulate_fwd", [h, shift, scale], ['tok', 'vec', 'vec'], [h.shape], ['tok'],
                       _row_tile(h.shape[1]), [out_dtype])[0]


def _modulate_bwd(h, scale, du):
    def body(h_ref, sc_ref, du_ref, dh_ref, dsh_ref, dsc_ref):
        hv, dv = h_ref[0], du_ref[0]
        r = lax.rsqrt(jnp.mean(hv * hv, axis=-1, keepdims=True) + EPS)
        hn = hv * r
        dn = dv * (1.0 + sc_ref[0])
        dh_ref[0] = r * (dn - hn * jnp.mean(dn * hn, axis=-1, keepdims=True))
        _accumulate(dsh_ref, dv)
        _accumulate(dsc_ref, dv * hn)

    return _group_call(body, "modulate_bwd", [h, scale, du], ['tok', 'vec', 'tok'],
                       [h.shape, scale.shape, scale.shape], ['tok', 'acc', 'acc'], _row_tile(h.shape[1]))


def _rows(t):
    return t.reshape(-1, t.shape[-1])


@functools.partial(jax.custom_vjp, nondiff_argnums=(4,))
def modmm(h, shift, scale, w, out_dtype):
    return _modmm_fwd(h, shift, scale, w, out_dtype)[0]


def _modmm_fwd(h, shift, scale, w, out_dtype):
    u = _modulate_fwd(h, shift, scale, BF16)
    y = matmul(_rows(u), w, out_dtype=out_dtype, name="mm_fwd").reshape(h.shape[:2] + (-1,))
    return y, (h, scale, u, w)


def _modmm_bwd(out_dtype, res, dy):
    h, scale, u, w = res
    du = matmul(_rows(dy), w, tb=True, name="mm_dx").reshape(h.shape)
    dw = matmul(_rows(u), _rows(dy), ta=True, out_dtype=w.dtype, name="mm_dw")
    dh, dsh, dsc = _modulate_bwd(h, scale, du)
    return dh, dsh, dsc, dw


modmm.defvjp(_modmm_fwd, _modmm_bwd)


def _gated_add_call(h, y, gate, coef):
    def body(h_ref, y_ref, g_ref, o_ref):
        o_ref[0] = h_ref[0] + coef * g_ref[0] * y_ref[0]

    return _group_call(body, "gated_add_fwd", [h, y, gate], ['tok', 'tok', 'vec'], [h.shape], ['tok'],
                       _row_tile(h.shape[1]))[0]


def _gated_add_bwd_call(y, gate, dout, coef):
    def body(y_ref, g_ref, d_ref, dy_ref, dg_ref):
        dv = d_ref[0]
        dy_ref[0] = coef * g_ref[0] * dv
        _accumulate(dg_ref, coef * dv * y_ref[0])

    return _group_call(body, "gated_add_bwd", [y, gate, dout], ['tok', 'vec', 'tok'], [y.shape, gate.shape],
                       ['tok', 'acc'], _row_tile(y.shape[1]))


@functools.partial(jax.custom_vjp, nondiff_argnums=(3,))
def gated_add(h, y, gate, coef):
    return _gated_add_call(h, y, gate, coef)


def _gated_add_vjp_fwd(h, y, gate, coef):
    return _gated_add_call(h, y, gate, coef), (y, gate)


def _gated_add_vjp_bwd(coef, res, dout):
    y, gate = res
    dy, dg = _gated_add_bwd_call(y, gate, dout, coef)
    return dout, dy, dg


gated_add.defvjp(_gated_add_vjp_fwd, _gated_add_vjp_bwd)


def _swiglu_fwd(hid):
    f = hid.shape[-1] // 2

    def body(h_ref, o_ref):
        gate, up = h_ref[0, :, 0:f].astype(F32), h_ref[0, :, f:2 * f].astype(F32)
        o_ref[0] = (gate * jax.nn.sigmoid(gate) * up).astype(BF16)

    return _group_call(body, "swiglu_fwd", [hid], ['tok'], [hid.shape[:2] + (f,)], ['tok'],
                       _tile(hid.shape[1], (128, 64, 32, 16)), [BF16])[0]


def _swiglu_bwd(hid, da):
    f = hid.shape[-1] // 2

    def body(h_ref, da_ref, d_ref):
        gate, up, dv = h_ref[0, :, 0:f].astype(F32), h_ref[0, :, f:2 * f].astype(F32), da_ref[0]
        s = jax.nn.sigmoid(gate)
        d_ref[0, :, 0:f] = (dv * up * (s * (1.0 + gate * (1.0 - s)))).astype(hid.dtype)
        d_ref[0, :, f:2 * f] = (dv * (gate * s)).astype(hid.dtype)

    return _group_call(body, "swiglu_bwd", [hid, da], ['tok', 'tok'], [hid.shape], ['tok'],
                       _tile(hid.shape[1], (128, 64, 32, 16)), [hid.dtype])[0]


@jax.custom_vjp
def swiglu_mm(hid, w):
    return _swiglu_mm_fwd(hid, w)[0]


def _swiglu_mm_fwd(hid, w):
    act = _swiglu_fwd(hid)
    y = matmul(_rows(act), w, name="mm_fwd").reshape(hid.shape[:2] + (-1,))
    return y, (hid, act, w)


def _swiglu_mm_bwd(res, dy):
    hid, act, w = res
    da = matmul(_rows(dy), w, tb=True, name="mm_dx").reshape(act.shape)
    dw = matmul(_rows(act), _rows(dy), ta=True, out_dtype=w.dtype, name="mm_dw")
    return _swiglu_bwd(hid, da), dw


swiglu_mm.defvjp(_swiglu_mm_fwd, _swiglu_mm_bwd)


@jax.custom_vjp
def flip_rows(x):
    return _flip_rows_call(x)


def _flip_rows_call(x):
    n, length, c = x.shape
    tb = _tile(length, (256, 128, 64, 32, 16, 8))
    nb = length // tb

    def body(x_ref, o_ref):
        xv = x_ref[0]
        ii = lax.broadcasted_iota(jnp.int32, (tb, tb), 0)
        jj = lax.broadcasted_iota(jnp.int32, (tb, tb), 1)
        rev = (ii + jj == tb - 1).astype(BF16)
        hi = xv.astype(BF16)
        r1 = xv - hi.astype(F32)
        mid = r1.astype(BF16)
        lo = (r1 - mid.astype(F32)).astype(BF16)
        dot = functools.partial(jnp.dot, preferred_element_type=F32)
        o_ref[0] = (dot(rev, hi) + dot(rev, mid)) + dot(rev, lo)

    return pl.pallas_call(
        body, name="flip_rows",
        out_shape=jax.ShapeDtypeStruct(x.shape, F32),
        grid=(n, nb),
        in_specs=[pl.BlockSpec((1, tb, c), lambda i, j: (i, j, 0))],
        out_specs=pl.BlockSpec((1, tb, c), lambda i, j: (i, nb - 1 - j, 0)),
        compiler_params=pltpu.CompilerParams(dimension_semantics=("parallel", "parallel"),
                                             vmem_limit_bytes=VMEM_LIMIT),
    )(x)


flip_rows.defvjp(lambda x: (_flip_rows_call(x), None), lambda _, dy: (_flip_rows_call(dy),))


def _flip_time(t, axis):
    s = t.shape
    lead = math.prod(s[:axis])
    return flip_rows(t.reshape(lead, s[axis], -1)).reshape(s)


def rms_norm(x):
    return x * lax.rsqrt(jnp.mean(x * x, axis=-1, keepdims=True) + EPS)


def raster_to_column(t, rows):
    b, s, d = t.shape
    return t.reshape(b, rows, GRID_W, d).transpose(0, 2, 1, 3).reshape(b, s, d)


def column_to_raster(t, rows):
    b, s, d = t.shape
    return t.reshape(b, GRID_W, rows, d).transpose(0, 2, 1, 3).reshape(b, s, d)


def depthwise_conv(x, w, b):
    pad = A_CONV // 2
    y = lax.conv_general_dilated(x, w[:, None, :], window_strides=(1,), padding=[(pad, pad)],
                                 dimension_numbers=('NWC', 'WIO', 'NWC'), feature_group_count=x.shape[-1])
    return y + b


S5_STATES = B_NGROUPS * B_STATE
S5_ROWS = 8
S5_STEPS_FWD = 64
S5_STEPS_BWD = 32


def _s5_scan_fwd(u2, bd2, cd2, ar8, ai8):
    rows, width = u2.shape
    ns = S5_STATES
    tr = S5_ROWS * S5_STEPS_FWD
    assert rows % tr == 0

    def body(u_ref, bd_ref, cd_ref, ar_ref, ai_ref, y_ref, x_ref, st_ref):
        @pl.when(pl.program_id(0) == 0)
        def _():
            st_ref[...] = jnp.zeros_like(st_ref)

        x_ref[...] = jnp.dot(u_ref[...].astype(BF16), bd_ref[...], preferred_element_type=F32)
        ar, ai = ar_ref[...], ai_ref[...]

        def step(t, carry):
            xr, xi = carry
            r = pl.ds(pl.multiple_of(t * S5_ROWS, S5_ROWS), S5_ROWS)
            nr = ar * xr - ai * xi + x_ref[r, 0:ns]
            ni = ar * xi + ai * xr + x_ref[r, ns:2 * ns]
            x_ref[r, 0:ns] = nr
            x_ref[r, ns:2 * ns] = ni
            return nr, ni

        xr, xi = lax.fori_loop(0, S5_STEPS_FWD, step, (st_ref[:, 0:ns], st_ref[:, ns:2 * ns]), unroll=4)
        st_ref[:, 0:ns] = xr
        st_ref[:, ns:2 * ns] = xi
        y_ref[...] = jnp.dot(x_ref[...].astype(BF16), cd_ref[...], preferred_element_type=F32)

    whole = lambda shape: pl.BlockSpec(shape, lambda i: (0, 0))
    return pl.pallas_call(
        body, name="s5_scan_fwd",
        out_shape=(jax.ShapeDtypeStruct((rows, width), F32), jax.ShapeDtypeStruct((rows, 2 * ns), F32)),
        grid=(rows // tr,),
        in_specs=[pl.BlockSpec((tr, width), lambda i: (i, 0)), whole(bd2.shape), whole(cd2.shape),
                  whole(ar8.shape), whole(ai8.shape)],
        out_specs=(pl.BlockSpec((tr, width), lambda i: (i, 0)), pl.BlockSpec((tr, 2 * ns), lambda i: (i, 0))),
        scratch_shapes=[pltpu.VMEM((S5_ROWS, 2 * ns), F32)],
        compiler_params=pltpu.CompilerParams(dimension_semantics=("arbitrary",), vmem_limit_bytes=VMEM_LIMIT),
    )(u2, bd2, cd2, ar8, ai8)


def _s5_scan_bwd(dy, x, u2, bd2, cd2, ar8, ai8):
    rows, width = u2.shape
    ns = S5_STATES
    steps = S5_STEPS_BWD
    tr = S5_ROWS * steps
    nblk = rows // tr
    assert rows % tr == 0
    nt = (((1,), (1,)), ((), ()))
    tn = (((0,), (0,)), ((), ()))

    def body(dy_ref, x_ref, xp_ref, u_ref, bd_ref, cd_ref, ar_ref, ai_ref,
             du_ref, dbd_ref, dcd_ref, dar_ref, dai_ref, g_ref, st_ref):
        k = pl.program_id(0)

        @pl.when(k == 0)
        def _():
            st_ref[...] = jnp.zeros_like(st_ref)
            dbd_ref[...] = jnp.zeros_like(dbd_ref)
            dcd_ref[...] = jnp.zeros_like(dcd_ref)
            dar_ref[...] = jnp.zeros_like(dar_ref)
            dai_ref[...] = jnp.zeros_like(dai_ref)

        dyb = dy_ref[...].astype(BF16)
        g_ref[...] = lax.dot_general(dyb, cd_ref[...], nt, preferred_element_type=F32)
        ar, ai = ar_ref[...], ai_ref[...]

        def adjoint(r, carry, xpr, xpi):
            gr_n, gi_n, dar, dai = carry
            gr = g_ref[r, 0:ns] + ar * gr_n + ai * gi_n
            gi = g_ref[r, ns:2 * ns] - ai * gr_n + ar * gi_n
            g_ref[r, 0:ns] = gr
            g_ref[r, ns:2 * ns] = gi
            return gr, gi, dar + gr * xpr + gi * xpi, dai + gi * xpr - gr * xpi

        def step(i, carry):
            t = steps - 1 - i
            r = pl.ds(pl.multiple_of(t * S5_ROWS, S5_ROWS), S5_ROWS)
            rp = pl.ds(pl.multiple_of((t - 1) * S5_ROWS, S5_ROWS), S5_ROWS)
            return adjoint(r, carry, x_ref[rp, 0:ns], x_ref[rp, ns:2 * ns])

        zero = jnp.zeros((S5_ROWS, ns), F32)
        carry = lax.fori_loop(0, steps - 1, step, (st_ref[:, 0:ns], st_ref[:, ns:2 * ns], zero, zero), unroll=2)
        has_prev = (k < nblk - 1).astype(F32)
        gr, gi, dar, dai = adjoint(pl.ds(0, S5_ROWS), carry, xp_ref[:, 0:ns] * has_prev, xp_ref[:, ns:2 * ns] * has_prev)
        st_ref[:, 0:ns] = gr
        st_ref[:, ns:2 * ns] = gi
        dar_ref[...] += dar
        dai_ref[...] += dai
        gb = g_ref[...].astype(BF16)
        du_ref[...] = lax.dot_general(gb, bd_ref[...], nt, preferred_element_type=F32)
        dbd_ref[...] += lax.dot_general(u_ref[...].astype(BF16), gb, tn, preferred_element_type=F32)
        dcd_ref[...] += lax.dot_general(x_ref[...].astype(BF16), dyb, tn, preferred_element_type=F32)

    whole = lambda shape: pl.BlockSpec(shape, lambda k: (0, 0))
    rev = lambda k: (nblk - 1 - k, 0)
    prev = lambda k: (jnp.maximum((nblk - 1 - k) * steps - 1, 0), 0)
    return pl.pallas_call(
        body, name="s5_scan_bwd",
        out_shape=(jax.ShapeDtypeStruct((rows, width), F32), jax.ShapeDtypeStruct(bd2.shape, F32),
                   jax.ShapeDtypeStruct(cd2.shape, F32), jax.ShapeDtypeStruct(ar8.shape, F32),
                   jax.ShapeDtypeStruct(ai8.shape, F32)),
        grid=(nblk,),
        in_specs=[pl.BlockSpec((tr, width), rev), pl.BlockSpec((tr, 2 * ns), rev),
                  pl.BlockSpec((S5_ROWS, 2 * ns), prev), pl.BlockSpec((tr, width), rev),
                  whole(bd2.shape), whole(cd2.shape), whole(ar8.shape), whole(ai8.shape)],
        out_specs=(pl.BlockSpec((tr, width), rev), whole(bd2.shape), whole(cd2.shape), whole(ar8.shape),
                   whole(ai8.shape)),
        scratch_shapes=[pltpu.VMEM((tr, 2 * ns), F32), pltpu.VMEM((S5_ROWS, 2 * ns), F32)],
        compiler_params=pltpu.CompilerParams(dimension_semantics=("arbitrary",), vmem_limit_bytes=VMEM_LIMIT),
    )(dy, x, x, u2, bd2, cd2, ar8, ai8)


@jax.custom_vjp
def s5_core(u2, bd2, cd2, ar8, ai8):
    return _s5_scan_fwd(u2, bd2.astype(BF16), cd2.astype(BF16), ar8, ai8)[0]


def _s5_core_fwd(u2, bd2, cd2, ar8, ai8):
    bd2, cd2 = bd2.astype(BF16), cd2.astype(BF16)
    y, x = _s5_scan_fwd(u2, bd2, cd2, ar8, ai8)
    return y, (x, u2, bd2, cd2, ar8, ai8)


def _s5_core_bwd(res, dy):
    return _s5_scan_bwd(dy, *res)


s5_core.defvjp(_s5_core_fwd, _s5_core_bwd)


def s5_mixers(p_ctx, p_lat, lam_re, lam_im, log_step, b_re, b_im, c_re, c_im, d_skip, glu_w, glu_b):
    bsz = p_ctx.shape[0]
    assert 2 * bsz == S5_ROWS
    eye = jnp.eye(B_NGROUPS, dtype=F32)
    bds, cds, ars, ais = [], [], [], []
    for d in range(2):
        step = jnp.exp(log_step[d])[:, None]
        mag = jnp.exp(lam_re[d] * step)
        ar = mag * jnp.cos(lam_im[d] * step)
        ai = mag * jnp.sin(lam_im[d] * step)
        den = lam_re[d] * lam_re[d] + lam_im[d] * lam_im[d]
        nr = ar - 1.0
        kr = (nr * lam_re[d] + ai * lam_im[d]) / den
        ki = (ai * lam_re[d] - nr * lam_im[d]) / den
        br = kr[..., None] * b_re[d] - ki[..., None] * b_im[d]
        bi = kr[..., None] * b_im[d] + ki[..., None] * b_re[d]
        blk = lambda w: jnp.einsum('gnc,gh->gchn', w, eye).reshape(B_WIDTH, S5_STATES)
        bds.append(jnp.concatenate([blk(br), blk(bi)], axis=1))
        blk_c = lambda w: jnp.einsum('gcn,gh->gnhc', w, eye).reshape(S5_STATES, B_WIDTH)
        cds.append(jnp.concatenate([blk_c(c_re[d]), -blk_c(c_im[d])], axis=0))
        ars.append(jnp.broadcast_to(ar.reshape(1, S5_STATES), (bsz, S5_STATES)))
        ais.append(jnp.broadcast_to(ai.reshape(1, S5_STATES), (bsz, S5_STATES)))
    bd2 = jnp.concatenate(bds, axis=0)
    cd2 = jnp.concatenate(cds, axis=1)
    ar8 = jnp.concatenate(ars, axis=0)
    ai8 = jnp.concatenate(ais, axis=0)

    def rows_of(p):
        ut = jnp.swapaxes(p, 0, 1)
        z = jnp.zeros_like(ut)
        return jnp.concatenate([jnp.concatenate([ut, z], axis=-1), jnp.concatenate([z, _flip_time(ut, 0)], axis=-1)], axis=1)

    lc = p_ctx.shape[1]
    u2 = jnp.concatenate([rows_of(p_ctx), rows_of(p_lat)], axis=0)
    y2 = s5_core(u2.reshape(-1, 2 * B_WIDTH), bd2, cd2, ar8, ai8).reshape(u2.shape)

    def finish(y2p, p):
        y = y2p[:, :bsz, :B_WIDTH] + _flip_time(y2p[:, bsz:, B_WIDTH:], 0)
        y = jnp.swapaxes(y, 0, 1) + d_skip * p
        y = jax.nn.gelu(y)
        gate = mm(y.reshape(-1, B_WIDTH), glu_w).reshape(y.shape)
        return y * jax.nn.sigmoid(gate + glu_b)

    return finish(y2[:lc], p_ctx), finish(y2[lc:], p_lat)


GLA_CHUNK = 64
GLA_SUB = 16
GLA_NB = 4
NT_DIMS = (((1,), (1,)), ((), ()))
TN_DIMS = (((0,), (0,)), ((), ()))


def _bdot(a, b, dims=(((1,), (0,)), ((), ()))):
    return lax.dot_general(a.astype(BF16), b.astype(BF16), dims, preferred_element_type=F32)


def _hdot(a, b, dims=(((1,), (0,)), ((), ()))):
    ah, bh = a.astype(BF16), b.astype(BF16)
    al, bl = (a - ah.astype(F32)).astype(BF16), (b - bh.astype(F32)).astype(BF16)
    dot = functools.partial(lax.dot_general, dimension_numbers=dims, preferred_element_type=F32)
    return dot(ah, bh) + (dot(ah, bl) + dot(al, bh))


def _gla_scores(q, k, cum, cumr, tri):
    n = GLA_CHUNK
    if cumr is not None:
        decay = jnp.where(tri, jnp.exp(jnp.where(tri, cum - cumr, 0.0)), 0.0)
        return _bdot(q, k, NT_DIMS) * decay, decay
    rows = lax.broadcasted_iota(jnp.int32, (n, 1), 0)
    parts = []
    for i in range(n // GLA_SUB):
        lo, hi = i * GLA_SUB, (i + 1) * GLA_SUB
        ref = cum[lo - 1:lo, :] if i else jnp.zeros_like(cum[0:1, :])
        qt = q[lo:hi] * jnp.exp(cum[lo:hi] - ref)
        seen = rows < hi
        kh = jnp.where(seen, k * jnp.exp(jnp.where(seen, ref - cum, 0.0)), 0.0)
        parts.append(_bdot(qt, kh, NT_DIMS))
    return jnp.where(tri, jnp.concatenate(parts, axis=0), 0.0), None


def _gla_fwd(q, k, cum, cumr, v):
    nseq, length, dk = q.shape
    nc = length // GLA_CHUNK
    scalar = cumr is not None

    def body(*refs):
        if scalar:
            q_ref, k_ref, cum_ref, cumr_ref, v_ref, o_ref, s_ref, st_ref = refs
        else:
            q_ref, k_ref, cum_ref, v_ref, o_ref, s_ref, st_ref = refs

        @pl.when(pl.program_id(1) == 0)
        def _():
            st_ref[...] = jnp.zeros_like(st_ref)

        ii = lax.broadcasted_iota(jnp.int32, (GLA_CHUNK, GLA_CHUNK), 0)
        jj = lax.broadcasted_iota(jnp.int32, (GLA_CHUNK, GLA_CHUNK), 1)
        tri = jj <= ii
        for b in range(GLA_NB):
            qv, kv, cv, vv, st = q_ref[b], k_ref[b], cum_ref[b], v_ref[b], st_ref[b]
            s_ref[b, 0] = st
            a, _ = _gla_scores(qv, kv, cv, cumr_ref[b] if scalar else None, tri)
            o_ref[b] = _bdot(qv * jnp.exp(cv), st, NT_DIMS) + _bdot(a, vv)
            last = cv[GLA_CHUNK - 1:GLA_CHUNK, :]
            st_ref[b] = st * jnp.exp(last) + _bdot(vv, kv * jnp.exp(last - cv), TN_DIMS)

    seq = pl.BlockSpec((GLA_NB, GLA_CHUNK, dk), lambda n, c: (n, c, 0))
    state = pl.BlockSpec((GLA_NB, 1, dk, dk), lambda n, c: (n, c, 0, 0))
    ins = [q, k, cum] + ([cumr] if scalar else []) + [v]
    return pl.pallas_call(
        body, name="gla_fwd_scalar" if scalar else "gla_fwd",
        out_shape=(jax.ShapeDtypeStruct((nseq, length, dk), F32), jax.ShapeDtypeStruct((nseq, nc, dk, dk), F32)),
        grid=(nseq // GLA_NB, nc),
        in_specs=[seq] * len(ins),
        out_specs=(seq, state),
        scratch_shapes=[pltpu.VMEM((GLA_NB, dk, dk), F32)],
        compiler_params=pltpu.CompilerParams(dimension_semantics=("parallel", "arbitrary"),
                                             vmem_limit_bytes=VMEM_LIMIT),
    )(*ins)


def _gla_bwd(do, q, k, cum, cumr, v, states):
    nseq, length, dk = q.shape
    nc = length // GLA_CHUNK
    scalar = cumr is not None
    n = GLA_CHUNK

    def body(*refs):
        if scalar:
            do_ref, q_ref, k_ref, cum_ref, cumr_ref, v_ref, s_ref, dq_ref, dk_ref, dc_ref, dcr_ref, dv_ref, dst_ref = refs
        else:
            do_ref, q_ref, k_ref, cum_ref, v_ref, s_ref, dq_ref, dk_ref, dc_ref, dv_ref, dst_ref = refs

        @pl.when(pl.program_id(1) == 0)
        def _():
            dst_ref[...] = jnp.zeros_like(dst_ref)

        ii = lax.broadcasted_iota(jnp.int32, (n, n), 0)
        jj = lax.broadcasted_iota(jnp.int32, (n, n), 1)
        tri = jj <= ii
        rows = lax.broadcasted_iota(jnp.int32, (n, 1), 0)
        for b in range(GLA_NB):
            dov, qv, kv, cv, vv, st, dst = do_ref[b], q_ref[b], k_ref[b], cum_ref[b], v_ref[b], s_ref[b, 0], dst_ref[b]
            e = jnp.exp(cv)
            qe = qv * e
            last = cv[n - 1:n, :]
            w = jnp.exp(last - cv)
            kw = kv * w
            el = jnp.exp(last)
            d_qe = _hdot(dov, st)
            d_kw = _hdot(vv, dst)
            dv = _bdot(kw, dst, NT_DIMS)
            d_last = jnp.sum(st * dst, axis=0, keepdims=True) * el + jnp.sum(d_kw * kw, axis=0, keepdims=True)
            dst_ref[b] = dst * el + _bdot(dov, qe, TN_DIMS)
            dq = d_qe * e
            dkk = d_kw * w
            dc = d_qe * qe - d_kw * kw + jnp.where(rows == n - 1, d_last, 0.0)
            da = jnp.where(tri, _hdot(dov, vv, NT_DIMS), 0.0)
            if scalar:
                a, decay = _gla_scores(qv, kv, cv, cumr_ref[b], tri)
                dg = da * decay
                dq = dq + _bdot(dg, kv)
                dkk = dkk + _bdot(dg, qv, TN_DIMS)
                p = da * a
                dc = dc + p
                dcr_ref[b] = -p
            else:
                a_parts, dq_parts = [], []
                for i in range(n // GLA_SUB):
                    lo, hi = i * GLA_SUB, (i + 1) * GLA_SUB
                    ref = cv[lo - 1:lo, :] if i else jnp.zeros_like(cv[0:1, :])
                    eq = jnp.exp(cv[lo:hi] - ref)
                    qt = qv[lo:hi] * eq
                    seen = rows < hi
                    ek = jnp.where(seen, jnp.exp(jnp.where(seen, ref - cv, 0.0)), 0.0)
                    kh = kv * ek
                    a_parts.append(_bdot(qt, kh, NT_DIMS))
                    dqt = _hdot(da[lo:hi], kh)
                    dkh = _hdot(da[lo:hi], qt, TN_DIMS)
                    dq_parts.append((dqt * eq, dqt * qt))
                    dkk = dkk + dkh * ek
                    dc = dc - dkh * kh
                a = jnp.where(tri, jnp.concatenate(a_parts, axis=0), 0.0)
                dq = dq + jnp.concatenate([p[0] for p in dq_parts], axis=0)
                dc = dc + jnp.concatenate([p[1] for p in dq_parts], axis=0)
            dv_ref[b] = dv + _bdot(a, dov, TN_DIMS)
            dq_ref[b] = dq
            dk_ref[b] = dkk
            dc_ref[b] = dc

    seq = pl.BlockSpec((GLA_NB, n, dk), lambda s, c: (s, nc - 1 - c, 0))
    state = pl.BlockSpec((GLA_NB, 1, dk, dk), lambda s, c: (s, nc - 1 - c, 0, 0))
    ins = [do, q, k, cum] + ([cumr] if scalar else []) + [v]
    n_out = 5 if scalar else 4
    return pl.pallas_call(
        body, name="gla_bwd_scalar" if scalar else "gla_bwd",
        out_shape=(jax.ShapeDtypeStruct((nseq, length, dk), F32),) * n_out,
        grid=(nseq // GLA_NB, nc),
        in_specs=[seq] * len(ins) + [state],
        out_specs=(seq,) * n_out,
        scratch_shapes=[pltpu.VMEM((GLA_NB, dk, dk), F32)],
        compiler_params=pltpu.CompilerParams(dimension_semantics=("parallel", "arbitrary"),
                                             vmem_limit_bytes=VMEM_LIMIT),
    )(*ins, states)


@jax.custom_vjp
def gla(q, k, cum, v):
    return _gla_fwd(q, k, cum, None, v)[0]


def _gla_vjp_fwd(q, k, cum, v):
    o, states = _gla_fwd(q, k, cum, None, v)
    return o, (q, k, cum, v, states)


def _gla_vjp_bwd(res, do):
    q, k, cum, v, states = res
    return _gla_bwd(do, q, k, cum, None, v, states)


gla.defvjp(_gla_vjp_fwd, _gla_vjp_bwd)


@jax.custom_vjp
def gla_scalar(q, k, cum, cumr, v):
    return _gla_fwd(q, k, cum, cumr, v)[0]


def _gla_scalar_vjp_fwd(q, k, cum, cumr, v):
    o, states = _gla_fwd(q, k, cum, cumr, v)
    return o, (q, k, cum, cumr, v, states)


def _gla_scalar_vjp_bwd(res, do):
    q, k, cum, cumr, v, states = res
    return _gla_bwd(do, q, k, cum, cumr, v, states)


gla_scalar.defvjp(_gla_scalar_vjp_fwd, _gla_scalar_vjp_bwd)


def _chunk_cumsum(g):
    s = g.shape
    return jnp.cumsum(g.reshape(s[:-2] + (s[-2] // GLA_CHUNK, GLA_CHUNK, s[-1])), axis=-2).reshape(s)


def _both_ways(t_ctx, t_lat, flip):
    parts = [_flip_time(t, 1) if flip else t for t in (t_ctx, t_lat)]
    return jnp.swapaxes(jnp.concatenate(parts, axis=1), 1, 2)


def _undo_ways(o, lc, flip):
    o = jnp.swapaxes(o, 1, 2)
    parts = (o[:, :lc], o[:, lc:])
    return tuple(_flip_time(t, 1) if flip else t for t in parts)


def hgrn2_mixers(p_ctx, p_lat, lower, norm_w):
    bsz, lc = p_ctx.shape[:2]
    lower = lower.reshape(2, C_HEADS, C_KEY)

    def heads(p, lo, hi):
        return p[..., lo:hi].reshape(p.shape[:2] + (C_HEADS, -1))

    q_c, q_l = (jax.nn.silu(heads(p, 0, C_WIDTH)) for p in (p_ctx, p_lat))
    v_c, v_l = (heads(p, 3 * C_WIDTH, 4 * C_WIDTH) for p in (p_ctx, p_lat))
    qs, ks, cums, vs = [], [], [], []
    for d in range(2):
        f_c, f_l = (lower[d] + (1.0 - lower[d]) * jax.nn.sigmoid(heads(p, (1 + d) * C_WIDTH, (2 + d) * C_WIDTH))
                    for p in (p_ctx, p_lat))
        qs.append(_both_ways(q_c, q_l, d))
        vs.append(_both_ways(v_c, v_l, d))
        ks.append(_both_ways(1.0 - f_c, 1.0 - f_l, d))
        cums.append(_chunk_cumsum(_both_ways(jnp.log(f_c), jnp.log(f_l), d)))
    flat = lambda t: t.reshape((-1,) + t.shape[2:])
    o = [gla(flat(qs[d]), flat(ks[d]), flat(cums[d]), flat(vs[d])).reshape(qs[d].shape) for d in range(2)]
    f_c, f_l = _undo_ways(o[0], lc, False)
    b_c, b_l = _undo_ways(o[1], lc, True)
    outs = []
    for o_sum, p in ((f_c + b_c, p_ctx), (f_l + b_l, p_lat)):
        o_n = rms_norm(o_sum) * norm_w.reshape(C_HEADS, C_VAL)
        outs.append(o_n.reshape(p.shape[:2] + (C_WIDTH,)) * jax.nn.silu(p[..., 4 * C_WIDTH:]))
    return tuple(outs)


def ssd_mixers(p_ctx, p_lat, conv_w, conv_b, dt_bias, a_log, d_skip, norm_w):
    bsz, lc = p_ctx.shape[:2]
    rep = A_HEADS // A_GROUPS
    a = -jnp.exp(a_log)
    xs, bs, cs, dts, zs = [], [], [], [], []
    for p in (p_ctx, p_lat):
        z, xbc, dt_raw = jnp.split(p, [A_INNER, A_INNER + A_CONV_DIM], axis=-1)
        xbc = jax.nn.silu(depthwise_conv(xbc, conv_w, conv_b))
        x_, b_, c_ = jnp.split(xbc, [A_INNER, A_INNER + A_GROUPS * A_STATE], axis=-1)
        shp = p.shape[:2]
        xs.append(x_.reshape(shp + (A_HEADS, A_HEAD_DIM)))
        bs.append(jnp.repeat(b_.reshape(shp + (A_GROUPS, A_STATE)), rep, axis=2))
        cs.append(jnp.repeat(c_.reshape(shp + (A_GROUPS, A_STATE)), rep, axis=2))
        dts.append(jax.nn.softplus(dt_raw.reshape(shp + (2, A_HEADS)) + dt_bias))
        zs.append(z)
    qs, ks, cums, cumrs, vs = [], [], [], [], []
    for d in range(2):
        qs.append(_both_ways(cs[0], cs[1], d))
        vs.append(_both_ways(xs[0], xs[1], d))
        ks.append(_both_ways(bs[0] * dts[0][:, :, d, :, None], bs[1] * dts[1][:, :, d, :, None], d))
        adt = [_flip_time(t, 1) if d else t for t in (dt[:, :, d, :] * a[d] for dt in dts)]
        adt = jnp.swapaxes(jnp.concatenate(adt, axis=1), 1, 2)
        by_chunk = jnp.cumsum(adt.reshape(adt.shape[:2] + (-1, GLA_CHUNK)), axis=-1)
        full = adt.shape + (A_STATE,)
        cums.append(jnp.broadcast_to(by_chunk.reshape(adt.shape)[..., None], full))
        cumrs.append(jnp.broadcast_to(by_chunk[:, :, :, None, :], by_chunk.shape[:3] + (GLA_CHUNK, GLA_CHUNK)).reshape(full))
    flat = lambda t: t.reshape((-1,) + t.shape[2:])
    o = [gla_scalar(flat(qs[d]), flat(ks[d]), flat(cums[d]), flat(cumrs[d]), flat(vs[d])).reshape(qs[d].shape)
         for d in range(2)]
    f_c, f_l = _undo_ways(o[0], lc, False)
    b_c, b_l = _undo_ways(o[1], lc, True)
    outs = []
    for y, x_, z in ((f_c + b_c, xs[0], zs[0]), (f_l + b_l, xs[1], zs[1])):
        y = y + d_skip[:, None] * x_
        y = y.reshape(z.shape) * jax.nn.silu(z)
        outs.append(rms_norm(y) * norm_w)
    return tuple(outs)


def token_mixers(p_ctx, p_lat, W, l, lower):
    def cut(p):
        return p[..., :A_COLS], p[..., 1408:1408 + B_COLS], p[..., 1664:1664 + C_COLS]

    pa_c, pb_c, pc_c = cut(p_ctx)
    pa_l, pb_l, pc_l = cut(p_lat)
    ya_c, ya_l = ssd_mixers(pa_c, pa_l, W['a_conv_w'][l], W['a_conv_b'][l], W['a_dt_bias'][l], W['a_log'][l],
                            W['a_d'][l], W['a_norm_w'][l])
    yb_c, yb_l = s5_mixers(pb_c, pb_l, W['s5_lam_re'][l], W['s5_lam_im'][l], W['s5_log_step'][l], W['s5_b_re'][l],
                           W['s5_b_im'][l], W['s5_c_re'][l], W['s5_c_im'][l], W['s5_d'][l], W['s5_glu_w'][l],
                           W['s5_glu_b'][l])
    yc_c, yc_l = hgrn2_mixers(pc_c, pc_l, lower, W['hg_norm_w'][l])
    return (jnp.concatenate([ya_c, yb_c, yc_c], axis=-1), jnp.concatenate([ya_l, yb_l, yc_l], axis=-1))


def _pad_w_in(w):
    z = functools.partial(jnp.zeros, dtype=w.dtype)
    return jnp.concatenate([w[:, :A_COLS], z((D_MODEL, 1408 - A_COLS)), w[:, A_COLS:], z((D_MODEL, IN_PAD - 2944))],
                           axis=1)


def _mm3(t, w):
    g, tt, k = t.shape
    return mm(t.reshape(g * tt, k), w).reshape(g, tt, -1)


def _ffn(h, mg, first, w_in, w_out):
    hid = modmm(h, mg[:, first:first + 1], mg[:, first + 1:first + 2], w_in, BF16)
    return gated_add(h, swiglu_mm(hid, w_out), mg[:, first + 2:first + 3], 0.5)


def local_loss(x, W, m_lat, m_ctx, ctx, target):
    bsz, seq, dm = x.shape
    lc = ctx.shape[1]
    tg = bsz * lc
    assert seq % tg == 0
    gl = seq // tg
    ng = bsz * gl
    rows = seq // GRID_W
    p_lb = jax.nn.softmax(W['hg_lb_logits'], axis=0)
    lower_bounds = jnp.cumsum(p_lb, axis=0) - p_lb[:1]
    h = jnp.concatenate([x.reshape(ng, tg, dm), ctx.reshape(1, tg, dm)], axis=0)
    for l in range(DEPTH):
        last = l == DEPTH - 1
        col_major = l % 2 == 1
        mg = jnp.concatenate([jnp.repeat(m_lat[l], gl, axis=0), m_ctx[l][None]], axis=0)
        h = _ffn(h, mg, 0, W['ffn_w_in'][l][0], W['ffn_w_out'][l][0])
        hp = h
        if col_major:
            h_lat = raster_to_column(h[:ng].reshape(bsz, seq, dm), rows)
            hp = jnp.concatenate([h_lat.reshape(ng, tg, dm), h[ng:]], axis=0)
        p = modmm(hp, mg[:, 3:4], mg[:, 4:5], _pad_w_in(W['w_in'][l]), F32)
        mix_ctx, mix_lat = token_mixers(p[ng].reshape(bsz, lc, -1), p[:ng].reshape(bsz, seq, -1), W, l,
                                        lower_bounds[l])
        if last:
            h, mg = h[:ng], mg[:ng]
            y_lat = _mm3(mix_lat.reshape(ng, tg, dm), W['w_out'][l])
            y_ctx = None
        else:
            y = _mm3(jnp.concatenate([mix_lat.reshape(ng, tg, dm), mix_ctx.reshape(1, tg, dm)], axis=0), W['w_out'][l])
            y_lat, y_ctx = y[:ng], y[ng:]
        if col_major:
            y_lat = column_to_raster(y_lat.reshape(bsz, seq, dm), rows).reshape(ng, tg, dm)
        y = y_lat if y_ctx is None else jnp.concatenate([y_lat, y_ctx], axis=0)
        h = gated_add(h, y, mg[:, 5:6], 1.0)
        h = _ffn(h, mg, 6, W['ffn_w_in'][l][1], W['ffn_w_out'][l][1])
    y = rms_norm(h[:ng].reshape(bsz, seq, dm)) * W['final_norm_w']
    err = jnp.square(y - target)
    return 0.5 * jnp.sum(jnp.mean(err, axis=-1))


def _pad_rows(a, rows):
    return jnp.concatenate([a, jnp.zeros((rows - a.shape[0],) + a.shape[1:], a.dtype)], axis=0)


def kernel(x, c, ctx, c_ctx, mod_w, mod_b, ffn_w_in, ffn_w_out, w_in, w_out, a_conv_w, a_conv_b, a_dt_bias, a_log, a_d, a_norm_w, s5_lam_re, s5_lam_im, s5_log_step, s5_b_re, s5_b_im, s5_c_re, s5_c_im, s5_d, s5_glu_w, s5_glu_b, hg_lb_logits, hg_norm_w, final_norm_w, loss_target, m_c_ctx, m_mod_w, m_mod_b, m_ffn_w_in, m_ffn_w_out, m_w_in, m_w_out, m_a_conv_w, m_a_conv_b, m_a_dt_bias, m_a_log, m_a_d, m_a_norm_w, m_s5_lam_re, m_s5_lam_im, m_s5_log_step, m_s5_b_re, m_s5_b_im, m_s5_c_re, m_s5_c_im, m_s5_d, m_s5_glu_w, m_s5_glu_b, m_hg_lb_logits, m_hg_norm_w, m_final_norm_w, v_c_ctx, v_mod_w, v_mod_b, v_ffn_w_in, v_ffn_w_out, v_w_in, v_w_out, v_a_conv_w, v_a_conv_b, v_a_dt_bias, v_a_log, v_a_d, v_a_norm_w, v_s5_lam_re, v_s5_lam_im, v_s5_log_step, v_s5_b_re, v_s5_b_im, v_s5_c_re, v_s5_c_im, v_s5_d, v_s5_glu_w, v_s5_glu_b, v_hg_lb_logits, v_hg_norm_w, v_final_norm_w):
    given = dict(locals())
    w_loc = {n: given[n] for n in WEIGHTS}
    m_loc = {n: given["m_" + n] for n in WEIGHTS}
    v_loc = {n: given["v_" + n] for n in WEIGHTS}
    bsz = x.shape[0]
    me = 4 * lax.axis_index("x") + 2 * lax.axis_index("y") + lax.axis_index("c")

    small_sh = [c] + [w_loc[n] for n in SMALL_SHARDED]
    g1 = _unpack(all_gather([_pack(small_sh, 128, 8)], "gather_small")[0], [a.shape for a in small_sh])
    c_all = g1[0].reshape(N_DEV * bsz, D_MODEL)
    gathered = dict(zip(BIG, all_gather([w_loc[n].astype(BF16) for n in BIG], "gather_weights")))
    W = {'ffn_w_in': [[_assemble(gathered['ffn_w_in'][:, l, i], 1) for i in range(2)] for l in range(DEPTH)],
         'ffn_w_out': [[_assemble(gathered['ffn_w_out'][:, l, i], 0) for i in range(2)] for l in range(DEPTH)],
         'w_in': [_assemble(gathered['w_in'][:, l], 1) for l in range(DEPTH)],
         'w_out': [_assemble(gathered['w_out'][:, l], 0) for l in range(DEPTH)]}
    for (n, ax), t in zip(SMALL_SHARDED.items(), g1[1:]):
        W[n] = _assemble(t, ax)
    for n in SMALL:
        if n not in SMALL_SHARDED and n not in ('c_ctx', 'mod_b'):
            W[n] = w_loc[n]

    n_rows = N_DEV * bsz + 1
    pad_rows = 8 * ((n_rows + 7) // 8)
    c_rows = _pad_rows(jnp.concatenate([c_all, c_ctx[None]], axis=0), pad_rows)
    sc = jax.nn.silu(c_rows)
    mods_sh = jnp.stack([matmul(sc, mod_w[l], name="mod_fwd") for l in range(DEPTH)])
    mods = _assemble(all_gather([mods_sh], "gather_mods")[0], 2) + mod_b[:, None, :]
    m_lat = lax.dynamic_slice_in_dim(mods, me * bsz, bsz, axis=1).reshape(DEPTH, bsz, N_MOD, D_MODEL)
    m_ctx = mods[:, n_rows - 1].reshape(DEPTH, N_MOD, D_MODEL)

    loss_loc, (grad_x, gW, gm_lat, gm_ctx) = jax.value_and_grad(local_loss, argnums=(0, 1, 2, 3))(
        x, W, m_lat, m_ctx, ctx, loss_target)
    loss = lax.psum(loss_loc, MESH_AXES)

    dm_loc = jnp.concatenate([gm_lat.reshape(DEPTH, bsz, -1), gm_ctx.reshape(DEPTH, 1, -1)], axis=1)
    (dm_all,) = all_gather([dm_loc], "gather_dmods")
    dm_ex = jnp.moveaxis(dm_all[:, :, :bsz], 0, 1).reshape(DEPTH, N_DEV * bsz, -1)
    ncol = N_MOD * D_MODEL
    dm_cx = sum_leading(dm_all[:, :, bsz].reshape(N_DEV, DEPTH * ncol // 128, 128), "sum_dmods_ctx")
    dm_cx = dm_cx.reshape(DEPTH, 1, ncol)
    dm_rows = jnp.concatenate([dm_ex, dm_cx, jnp.zeros((DEPTH, pad_rows - n_rows, ncol), F32)], axis=1)
    grad_mod_b = sum_leading(jnp.moveaxis(dm_rows, 1, 0).reshape(pad_rows, DEPTH * ncol // 128, 128),
                             "sum_mod_b").reshape(DEPTH, ncol)
    my_cols = ncol // N_DEV
    dm_mine = lax.dynamic_slice_in_dim(dm_rows, me * my_cols, my_cols, axis=2)
    grad_mod_w = jnp.stack([matmul(sc, dm_mine[l], ta=True, name="mod_dw") for l in range(DEPTH)])
    dm_cx_mine = lax.dynamic_slice_in_dim(dm_cx, me * my_cols, my_cols, axis=2)
    g_sc_ctx = sum(matmul(_pad_rows(dm_cx_mine[l], 8), mod_w[l], tb=True, name="mod_dc")[0] for l in range(DEPTH))

    small_full = [n for n in SMALL if n not in ('c_ctx', 'mod_b')]
    part = [gW[n] for n in small_full] + [g_sc_ctx]
    red = sum_leading(all_gather([_pack(part, 128, 8)], "gather_small_grads")[0], "sum_small_grads")
    red = _unpack(red, [a.shape for a in part])
    grads = dict(zip(small_full, red[:-1]))
    sig = jax.nn.sigmoid(c_ctx)
    grads['c_ctx'] = red[-1] * (sig * (1.0 + c_ctx * (1.0 - sig)))
    grads['mod_b'] = grad_mod_b
    for n, ax in SMALL_SHARDED.items():
        size = w_loc[n].shape[ax]
        grads[n] = lax.dynamic_slice_in_dim(grads[n], me * size, size, axis=ax)
    grads['mod_w'] = grad_mod_w

    by_dev = {'ffn_w_in': jnp.stack([jnp.stack([_split(g, 1) for g in gl], axis=1) for gl in gW['ffn_w_in']], axis=1),
              'ffn_w_out': jnp.stack([jnp.stack([_split(g, 0) for g in gl], axis=1) for gl in gW['ffn_w_out']], axis=1),
              'w_in': jnp.stack([_split(g, 1) for g in gW['w_in']], axis=1),
              'w_out': jnp.stack([_split(g, 0) for g in gW['w_out']], axis=1)}
    for n, t in zip(BIG, all_to_all([by_dev[n] for n in BIG], "exchange_grads")):
        grads[n] = sum_leading(t.reshape((N_DEV,) + _as_2d(t.shape[1:])), "sum_grads").reshape(t.shape[1:])

    delta, new_m, new_v = {}, {}, {}

    for n in list(BIG) + ['mod_w']:
        outs = adamw(*[d[n].reshape(_as_2d(d[n].shape)) for d in (w_loc, grads, m_loc, v_loc)], name="adamw_" + n)
        delta[n], new_m[n], new_v[n] = (o.reshape(w_loc[n].shape) for o in outs)

    def update(names, width, row_mult, tag):
        packed = [_pack([d[n] for n in names], width, row_mult) for d in (w_loc, grads, m_loc, v_loc)]
        outs = adamw(*packed, name="adamw_" + tag)
        shapes = [w_loc[n].shape for n in names]
        for res, out in zip((delta, new_m, new_v), outs):
            res.update(zip(names, _unpack(out, shapes)))

    update(SMALL, 128, 256, "small")
    return (loss, grad_x, *[grads[n] for n in WEIGHTS], *[delta[n] for n in WEIGHTS],
            *[new_m[n] for n in WEIGHTS], *[new_v[n] for n in WEIGHTS])
```

```python
import functools
import math

import jax
import jax.numpy as jnp
from jax import lax
from jax.experimental import pallas as pl
from jax.experimental.pallas import tpu as pltpu

F32 = jnp.float32
BF16 = jnp.bfloat16
MESH_AXES = ("x", "y", "c")
N_DEV = 8
MESH_ID = pl.DeviceIdType.MESH
VMEM_LIMIT = 48 * 1024 * 1024

D_MODEL = 1024
DEPTH = 2
GRID_W = 64
EPS = 1e-6
N_MOD = 9
D_FF = 2816
A_INNER = 512
A_HEADS = 8
A_HEAD_DIM = 64
A_GROUPS = 2
A_STATE = 64
A_CONV = 5
A_CONV_DIM = A_INNER + 2 * A_GROUPS * A_STATE
A_COLS = A_INNER + A_CONV_DIM + 2 * A_HEADS
B_WIDTH = 256
B_GROUP = 16
B_NGROUPS = 16
B_STATE = 64
B_COLS = B_WIDTH
C_WIDTH = 256
C_HEADS = 4
C_KEY = 64
C_VAL = 64
C_COLS = 5 * C_WIDTH
IN_PAD = 3072

ADAM_LR = 0.001
ADAM_B1 = 0.9
ADAM_B2 = 0.999
ADAM_EPS = 1e-08
ADAM_WD = 0.01
ADAM_STEP = 10

WEIGHTS = ['c_ctx', 'mod_w', 'mod_b', 'ffn_w_in', 'ffn_w_out', 'w_in', 'w_out', 'a_conv_w', 'a_conv_b', 'a_dt_bias',
           'a_log', 'a_d', 'a_norm_w', 's5_lam_re', 's5_lam_im', 's5_log_step', 's5_b_re', 's5_b_im', 's5_c_re',
           's5_c_im', 's5_d', 's5_glu_w', 's5_glu_b', 'hg_lb_logits', 'hg_norm_w', 'final_norm_w']
BIG = {'ffn_w_in': 3, 'ffn_w_out': 2, 'w_in': 2, 'w_out': 1}
SMALL_SHARDED = {'a_conv_w': 2, 's5_glu_w': 1, 'hg_lb_logits': 2}
SMALL = [n for n in WEIGHTS if n not in BIG and n != 'mod_w']


def _tile(d, prefs):
    for p in prefs:
        if d % p == 0:
            return p
    return d


def _pack(arrs, width, row_mult):
    flat = jnp.concatenate([a.reshape(-1) for a in arrs])
    pad = (-flat.shape[0]) % (width * row_mult)
    if pad:
        flat = jnp.concatenate([flat, jnp.zeros((pad,), flat.dtype)])
    return flat.reshape(-1, width)


def _as_2d(shape):
    return (math.prod(shape[:-1]), shape[-1])


def _unpack(buf, shapes):
    lead = buf.shape[:-2]
    flat = buf.reshape(lead + (-1,))
    out, off = [], 0
    for s in shapes:
        n = math.prod(s)
        out.append(flat[..., off:off + n].reshape(lead + tuple(s)))
        off += n
    return out


def _assemble(g, axis):
    t = jnp.moveaxis(g, 0, axis)
    s = t.shape
    return t.reshape(s[:axis] + (s[axis] * s[axis + 1],) + s[axis + 2:])


def _split(full, axis):
    s = full.shape
    t = full.reshape(s[:axis] + (N_DEV, s[axis] // N_DEV) + s[axis + 1:])
    return jnp.moveaxis(t, axis, 0)


def all_gather(xs, name):
    nt = len(xs)

    def body(*refs):
        x_refs, out_refs = refs[:nt], refs[nt:2 * nt]
        send_sems, recv_sems, local_sems = refs[2 * nt:]
        ax, ay, ac = lax.axis_index("x"), lax.axis_index("y"), lax.axis_index("c")
        me, sibling = (ax, ay, ac), (ax, ay, 1 - ac)
        chips = [(1 - ax, ay), (ax, 1 - ay), (1 - ax, 1 - ay)]

        def slot(t, px, py, pc):
            return out_refs[t].at[4 * px + 2 * py + pc]

        def copy(t, k, block, to, src=None):
            return pltpu.make_async_remote_copy(
                src_ref=slot(t, *block) if src is None else src, dst_ref=slot(t, *block),
                send_sem=send_sems.at[t, k], recv_sem=recv_sems.at[t, k], device_id=to, device_id_type=MESH_ID)

        mine = [pltpu.make_async_copy(x_refs[t], slot(t, *me), local_sems.at[t]) for t in range(nt)]
        for cp in mine:
            cp.start()
        first = []
        for t in range(nt):
            first.append(copy(t, 0, me, sibling, src=x_refs[t]))
            first += [copy(t, 1 + j, me, (*chip, ac), src=x_refs[t]) for j, chip in enumerate(chips)]
        for cp in first:
            cp.start()
        passed = []
        for j, chip in enumerate(chips):
            for t in range(nt):
                copy(t, 1 + j, (*chip, ac), me).wait_recv()
                passed.append(copy(t, 4 + j, (*chip, ac), sibling))
                passed[-1].start()
        for t in range(nt):
            copy(t, 0, sibling, me).wait_recv()
            for j, chip in enumerate(chips):
                copy(t, 4 + j, (*chip, 1 - ac), me).wait_recv()
        for cp in first + passed:
            cp.wait_send()
        for cp in mine:
            cp.wait()

    return pl.pallas_call(
        body, name=name,
        out_shape=tuple(jax.ShapeDtypeStruct((N_DEV,) + x.shape, x.dtype) for x in xs),
        in_specs=[pl.BlockSpec(memory_space=pl.ANY)] * nt,
        out_specs=tuple(pl.BlockSpec(memory_space=pl.ANY) for _ in xs),
        scratch_shapes=[pltpu.SemaphoreType.DMA((nt, 7)), pltpu.SemaphoreType.DMA((nt, 7)),
                        pltpu.SemaphoreType.DMA((nt,))],
    )(*xs)


def all_to_all(gs, name):
    nt = len(gs)

    def body(*refs):
        g_refs, out_refs = refs[:nt], refs[nt:2 * nt]
        send_sems, recv_sems, local_sems = refs[2 * nt:]
        ax, ay, ac = lax.axis_index("x"), lax.axis_index("y"), lax.axis_index("c")
        my = 4 * ax + 2 * ay + ac
        local = [pltpu.make_async_copy(g_refs[t].at[my], out_refs[t].at[my], local_sems.at[t]) for t in range(nt)]
        for cp in local:
            cp.start()
        peers = []
        for r in range(1, N_DEV):
            px = 1 - ax if r & 4 else ax
            py = 1 - ay if r & 2 else ay
            pc = 1 - ac if r & 1 else ac
            peers.append((px, py, pc))

        def copy(t, k, peer):
            return pltpu.make_async_remote_copy(
                src_ref=g_refs[t].at[4 * peer[0] + 2 * peer[1] + peer[2]], dst_ref=out_refs[t].at[my],
                send_sem=send_sems.at[t, k], recv_sem=recv_sems.at[t, k], device_id=peer, device_id_type=MESH_ID)

        def arrival(t, k, peer):
            slot = 4 * peer[0] + 2 * peer[1] + peer[2]
            return pltpu.make_async_remote_copy(
                src_ref=g_refs[t].at[slot], dst_ref=out_refs[t].at[slot],
                send_sem=send_sems.at[t, k], recv_sem=recv_sems.at[t, k], device_id=peer, device_id_type=MESH_ID)

        sends = [copy(t, k, p) for t in range(nt) for k, p in enumerate(peers)]
        for cp in sends:
            cp.start()
        for t in range(nt):
            for k, p in enumerate(peers):
                arrival(t, k, p).wait_recv()
        for cp in sends:
            cp.wait_send()
        for cp in local:
            cp.wait()

    return pl.pallas_call(
        body, name=name,
        out_shape=tuple(jax.ShapeDtypeStruct(g.shape, g.dtype) for g in gs),
        in_specs=[pl.BlockSpec(memory_space=pl.ANY)] * nt,
        out_specs=tuple(pl.BlockSpec(memory_space=pl.ANY) for _ in gs),
        scratch_shapes=[pltpu.SemaphoreType.DMA((nt, 7)), pltpu.SemaphoreType.DMA((nt, 7)),
                        pltpu.SemaphoreType.DMA((nt,))],
    )(*gs)


def matmul(a, b, *, ta=False, tb=False, out_dtype=F32, name="mm"):
    m, k = (a.shape[1], a.shape[0]) if ta else a.shape
    n = b.shape[0] if tb else b.shape[1]
    assert (b.shape[1] if tb else b.shape[0]) == k, (a.shape, b.shape, ta, tb)
    tm = _tile(m, (1024, 512, 256, 128))
    tn = _tile(n, (1408, 1024, 512, 384, 256, 128))
    tk = _tile(k, (1024, 512, 256, 128))
    nk = k // tk
    dims = (((0 if ta else 1,), (1 if tb else 0,)), ((), ()))

    def body(a_ref, b_ref, o_ref, acc_ref):
        step = pl.program_id(2)

        @pl.when(step == 0)
        def _():
            acc_ref[...] = jnp.zeros_like(acc_ref)

        acc_ref[...] += lax.dot_general(a_ref[...].astype(BF16), b_ref[...].astype(BF16), dims,
                                        preferred_element_type=F32)

        @pl.when(step == nk - 1)
        def _():
            o_ref[...] = acc_ref[...].astype(out_dtype)

    a_spec = pl.BlockSpec((tk, tm), lambda i, j, s: (s, i)) if ta else pl.BlockSpec((tm, tk), lambda i, j, s: (i, s))
    b_spec = pl.BlockSpec((tn, tk), lambda i, j, s: (j, s)) if tb else pl.BlockSpec((tk, tn), lambda i, j, s: (s, j))
    return pl.pallas_call(
        body, name=name,
        out_shape=jax.ShapeDtypeStruct((m, n), out_dtype),
        grid=(m // tm, n // tn, nk),
        in_specs=[a_spec, b_spec],
        out_specs=pl.BlockSpec((tm, tn), lambda i, j, s: (i, j)),
        scratch_shapes=[pltpu.VMEM((tm, tn), F32)],
        compiler_params=pltpu.CompilerParams(dimension_semantics=("parallel", "parallel", "arbitrary"),
                                             vmem_limit_bytes=VMEM_LIMIT),
    )(a, b)


@jax.custom_vjp
def mm(x, w):
    return matmul(x, w, name="mm_fwd")


def _mm_fwd(x, w):
    return matmul(x, w, name="mm_fwd"), (x, w)


def _mm_bwd(res, dy):
    x, w = res
    dx = matmul(dy, w, tb=True, out_dtype=x.dtype, name="mm_dx")
    dw = matmul(x, dy, ta=True, out_dtype=w.dtype, name="mm_dw")
    return dx, dw


mm.defvjp(_mm_fwd, _mm_bwd)


def sum_leading(x, name):
    n, r, c = x.shape
    tr = _tile(r, (256, 128, 64, 32, 16, 8))

    def body(x_ref, o_ref):
        acc = x_ref[0].astype(F32)
        for i in range(1, n):
            acc = acc + x_ref[i].astype(F32)
        o_ref[...] = acc

    return pl.pallas_call(
        body, name=name,
        out_shape=jax.ShapeDtypeStruct((r, c), F32),
        grid=(r // tr,),
        in_specs=[pl.BlockSpec((n, tr, c), lambda i: (0, i, 0))],
        out_specs=pl.BlockSpec((tr, c), lambda i: (i, 0)),
        compiler_params=pltpu.CompilerParams(dimension_semantics=("parallel",), vmem_limit_bytes=VMEM_LIMIT),
    )(x)


def adamw(w, g, m, v, name):
    r, c = w.shape
    tr = _tile(r, (256, 128, 64, 32, 16, 8))

    def body(w_ref, g_ref, m_ref, v_ref, d_ref, mo_ref, vo_ref):
        gv = g_ref[...]
        mv = ADAM_B1 * m_ref[...] + (1.0 - ADAM_B1) * gv
        vv = ADAM_B2 * v_ref[...] + (1.0 - ADAM_B2) * jnp.square(gv)
        m_hat = mv / (1.0 - ADAM_B1 ** ADAM_STEP)
        v_hat = vv / (1.0 - ADAM_B2 ** ADAM_STEP)
        d_ref[...] = -ADAM_LR * (m_hat / (jnp.sqrt(v_hat) + ADAM_EPS) + ADAM_WD * w_ref[...])
        mo_ref[...] = mv
        vo_ref[...] = vv

    spec = pl.BlockSpec((tr, c), lambda i: (i, 0))
    return pl.pallas_call(
        body, name=name,
        out_shape=(jax.ShapeDtypeStruct((r, c), F32),) * 3,
        grid=(r // tr,),
        in_specs=[spec] * 4,
        out_specs=(spec,) * 3,
        compiler_params=pltpu.CompilerParams(dimension_semantics=("parallel",), vmem_limit_bytes=VMEM_LIMIT),
    )(w, g, m, v)


ROW_TILE = 256


def _row_tile(t):
    return _tile(t, (ROW_TILE, 128, 64, 32, 16, 8))


def _group_call(body, name, ins, in_kinds, out_shapes, out_kinds, tt, out_dtypes=None):
    g, t = ins[0].shape[:2]

    def spec(kind, shape):
        if kind == 'tok':
            return pl.BlockSpec((1, tt, shape[-1]), lambda i, j: (i, j, 0))
        return pl.BlockSpec((1, 1, shape[-1]), lambda i, j: (i, 0, 0))

    return pl.pallas_call(
        body, name=name,
        out_shape=tuple(jax.ShapeDtypeStruct(s, d) for s, d in zip(out_shapes, out_dtypes or [F32] * len(out_shapes))),
        grid=(g, t // tt),
        in_specs=[spec(k, a.shape) for k, a in zip(in_kinds, ins)],
        out_specs=tuple(spec(k, s) for k, s in zip(out_kinds, out_shapes)),
        compiler_params=pltpu.CompilerParams(dimension_semantics=("parallel", "arbitrary"),
                                             vmem_limit_bytes=VMEM_LIMIT),
    )(*ins)


def _accumulate(ref, val):
    @pl.when(pl.program_id(1) == 0)
    def _():
        ref[...] = jnp.zeros_like(ref)

    ref[0] += jnp.sum(val, axis=0, keepdims=True)


def _modulate_fwd(h, shift, scale, out_dtype):
    def body(h_ref, sh_ref, sc_ref, o_ref):
        hv = h_ref[0]
        r = lax.rsqrt(jnp.mean(hv * hv, axis=-1, keepdims=True) + EPS)
        o_ref[0] = (hv * r * (1.0 + sc_ref[0]) + sh_ref[0]).astype(out_dtype)

    return _group_call(body, "modulate_fwd", [h, shift, scale], ['tok', 'vec', 'vec'], [h.shape], ['tok'],
                       _row_tile(h.shape[1]), [out_dtype])[0]


def _modulate_bwd(h, scale, du):
    def body(h_ref, sc_ref, du_ref, dh_ref, dsh_ref, dsc_ref):
        hv, dv = h_ref[0], du_ref[0]
        r = lax.rsqrt(jnp.mean(hv * hv, axis=-1, keepdims=True) + EPS)
        hn = hv * r
        dn = dv * (1.0 + sc_ref[0])
        dh_ref[0] = r * (dn - hn * jnp.mean(dn * hn, axis=-1, keepdims=True))
        _accumulate(dsh_ref, dv)
        _accumulate(dsc_ref, dv * hn)

    return _group_call(body, "modulate_bwd", [h, scale, du], ['tok', 'vec', 'tok'],
                       [h.shape, scale.shape, scale.shape], ['tok', 'acc', 'acc'], _row_tile(h.shape[1]))


def _rows(t):
    return t.reshape(-1, t.shape[-1])


@functools.partial(jax.custom_vjp, nondiff_argnums=(4,))
def modmm(h, shift, scale, w, out_dtype):
    return _modmm_fwd(h, shift, scale, w, out_dtype)[0]


def _modmm_fwd(h, shift, scale, w, out_dtype):
    u = _modulate_fwd(h, shift, scale, BF16)
    y = matmul(_rows(u), w, out_dtype=out_dtype, name="mm_fwd").reshape(h.shape[:2] + (-1,))
    return y, (h, scale, u, w)


def _modmm_bwd(out_dtype, res, dy):
    h, scale, u, w = res
    du = matmul(_rows(dy), w, tb=True, name="mm_dx").reshape(h.shape)
    dw = matmul(_rows(u), _rows(dy), ta=True, out_dtype=w.dtype, name="mm_dw")
    dh, dsh, dsc = _modulate_bwd(h, scale, du)
    return dh, dsh, dsc, dw


modmm.defvjp(_modmm_fwd, _modmm_bwd)


def _gated_add_call(h, y, gate, coef):
    def body(h_ref, y_ref, g_ref, o_ref):
        o_ref[0] = h_ref[0] + coef * g_ref[0] * y_ref[0]

    return _group_call(body, "gated_add_fwd", [h, y, gate], ['tok', 'tok', 'vec'], [h.shape], ['tok'],
                       _row_tile(h.shape[1]))[0]


def _gated_add_bwd_call(y, gate, dout, coef):
    def body(y_ref, g_ref, d_ref, dy_ref, dg_ref):
        dv = d_ref[0]
        dy_ref[0] = coef * g_ref[0] * dv
        _accumulate(dg_ref, coef * dv * y_ref[0])

    return _group_call(body, "gated_add_bwd", [y, gate, dout], ['tok', 'vec', 'tok'], [y.shape, gate.shape],
                       ['tok', 'acc'], _row_tile(y.shape[1]))


@functools.partial(jax.custom_vjp, nondiff_argnums=(3,))
def gated_add(h, y, gate, coef):
    return _gated_add_call(h, y, gate, coef)


def _gated_add_vjp_fwd(h, y, gate, coef):
    return _gated_add_call(h, y, gate, coef), (y, gate)


def _gated_add_vjp_bwd(coef, res, dout):
    y, gate = res
    dy, dg = _gated_add_bwd_call(y, gate, dout, coef)
    return dout, dy, dg


gated_add.defvjp(_gated_add_vjp_fwd, _gated_add_vjp_bwd)


def _swiglu_fwd(hid):
    f = hid.shape[-1] // 2

    def body(h_ref, o_ref):
        gate, up = h_ref[0, :, 0:f].astype(F32), h_ref[0, :, f:2 * f].astype(F32)
        o_ref[0] = (gate * jax.nn.sigmoid(gate) * up).astype(BF16)

    return _group_call(body, "swiglu_fwd", [hid], ['tok'], [hid.shape[:2] + (f,)], ['tok'],
                       _tile(hid.shape[1], (128, 64, 32, 16)), [BF16])[0]


def _swiglu_bwd(hid, da):
    f = hid.shape[-1] // 2

    def body(h_ref, da_ref, d_ref):
        gate, up, dv = h_ref[0, :, 0:f].astype(F32), h_ref[0, :, f:2 * f].astype(F32), da_ref[0]
        s = jax.nn.sigmoid(gate)
        d_ref[0, :, 0:f] = (dv * up * (s * (1.0 + gate * (1.0 - s)))).astype(hid.dtype)
        d_ref[0, :, f:2 * f] = (dv * (gate * s)).astype(hid.dtype)

    return _group_call(body, "swiglu_bwd", [hid, da], ['tok', 'tok'], [hid.shape], ['tok'],
                       _tile(hid.shape[1], (128, 64, 32, 16)), [hid.dtype])[0]


@jax.custom_vjp
def swiglu_mm(hid, w):
    return _swiglu_mm_fwd(hid, w)[0]


def _swiglu_mm_fwd(hid, w):
    act = _swiglu_fwd(hid)
    y = matmul(_rows(act), w, name="mm_fwd").reshape(hid.shape[:2] + (-1,))
    return y, (hid, act, w)


def _swiglu_mm_bwd(res, dy):
    hid, act, w = res
    da = matmul(_rows(dy), w, tb=True, name="mm_dx").reshape(act.shape)
    dw = matmul(_rows(act), _rows(dy), ta=True, out_dtype=w.dtype, name="mm_dw")
    return _swiglu_bwd(hid, da), dw


swiglu_mm.defvjp(_swiglu_mm_fwd, _swiglu_mm_bwd)


@jax.custom_vjp
def flip_rows(x):
    return _flip_rows_call(x)


def _flip_rows_call(x):
    n, length, c = x.shape
    tb = _tile(length, (256, 128, 64, 32, 16, 8))
    nb = length // tb

    def body(x_ref, o_ref):
        xv = x_ref[0]
        ii = lax.broadcasted_iota(jnp.int32, (tb, tb), 0)
        jj = lax.broadcasted_iota(jnp.int32, (tb, tb), 1)
        rev = (ii + jj == tb - 1).astype(BF16)
        hi = xv.astype(BF16)
        r1 = xv - hi.astype(F32)
        mid = r1.astype(BF16)
        lo = (r1 - mid.astype(F32)).astype(BF16)
        dot = functools.partial(jnp.dot, preferred_element_type=F32)
        o_ref[0] = (dot(rev, hi) + dot(rev, mid)) + dot(rev, lo)

    return pl.pallas_call(
        body, name="flip_rows",
        out_shape=jax.ShapeDtypeStruct(x.shape, F32),
        grid=(n, nb),
        in_specs=[pl.BlockSpec((1, tb, c), lambda i, j: (i, j, 0))],
        out_specs=pl.BlockSpec((1, tb, c), lambda i, j: (i, nb - 1 - j, 0)),
        compiler_params=pltpu.CompilerParams(dimension_semantics=("parallel", "parallel"),
                                             vmem_limit_bytes=VMEM_LIMIT),
    )(x)


flip_rows.defvjp(lambda x: (_flip_rows_call(x), None), lambda _, dy: (_flip_rows_call(dy),))


def _flip_time(t, axis):
    s = t.shape
    lead = math.prod(s[:axis])
    return flip_rows(t.reshape(lead, s[axis], -1)).reshape(s)


def rms_norm(x):
    return x * lax.rsqrt(jnp.mean(x * x, axis=-1, keepdims=True) + EPS)


def raster_to_column(t, rows):
    b, s, d = t.shape
    return t.reshape(b, rows, GRID_W, d).transpose(0, 2, 1, 3).reshape(b, s, d)


def column_to_raster(t, rows):
    b, s, d = t.shape
    return t.reshape(b, GRID_W, rows, d).transpose(0, 2, 1, 3).reshape(b, s, d)


def depthwise_conv(x, w, b):
    pad = A_CONV // 2
    y = lax.conv_general_dilated(x, w[:, None, :], window_strides=(1,), padding=[(pad, pad)],
                                 dimension_numbers=('NWC', 'WIO', 'NWC'), feature_group_count=x.shape[-1])
    return y + b


S5_STATES = B_NGROUPS * B_STATE
S5_ROWS = 8
S5_STEPS_FWD = 64
S5_STEPS_BWD = 32


def _s5_scan_fwd(u2, bd2, cd2, ar8, ai8):
    rows, width = u2.shape
    ns = S5_STATES
    tr = S5_ROWS * S5_STEPS_FWD
    assert rows % tr == 0

    def body(u_ref, bd_ref, cd_ref, ar_ref, ai_ref, y_ref, x_ref, st_ref):
        @pl.when(pl.program_id(0) == 0)
        def _():
            st_ref[...] = jnp.zeros_like(st_ref)

        x_ref[...] = jnp.dot(u_ref[...].astype(BF16), bd_ref[...], preferred_element_type=F32)
        ar, ai = ar_ref[...], ai_ref[...]

        def step(t, carry):
            xr, xi = carry
            r = pl.ds(pl.multiple_of(t * S5_ROWS, S5_ROWS), S5_ROWS)
            nr = ar * xr - ai * xi + x_ref[r, 0:ns]
            ni = ar * xi + ai * xr + x_ref[r, ns:2 * ns]
            x_ref[r, 0:ns] = nr
            x_ref[r, ns:2 * ns] = ni
            return nr, ni

        xr, xi = lax.fori_loop(0, S5_STEPS_FWD, step, (st_ref[:, 0:ns], st_ref[:, ns:2 * ns]), unroll=4)
        st_ref[:, 0:ns] = xr
        st_ref[:, ns:2 * ns] = xi
        y_ref[...] = jnp.dot(x_ref[...].astype(BF16), cd_ref[...], preferred_element_type=F32)

    whole = lambda shape: pl.BlockSpec(shape, lambda i: (0, 0))
    return pl.pallas_call(
        body, name="s5_scan_fwd",
        out_shape=(jax.ShapeDtypeStruct((rows, width), F32), jax.ShapeDtypeStruct((rows, 2 * ns), F32)),
        grid=(rows // tr,),
        in_specs=[pl.BlockSpec((tr, width), lambda i: (i, 0)), whole(bd2.shape), whole(cd2.shape),
                  whole(ar8.shape), whole(ai8.shape)],
        out_specs=(pl.BlockSpec((tr, width), lambda i: (i, 0)), pl.BlockSpec((tr, 2 * ns), lambda i: (i, 0))),
        scratch_shapes=[pltpu.VMEM((S5_ROWS, 2 * ns), F32)],
        compiler_params=pltpu.CompilerParams(dimension_semantics=("arbitrary",), vmem_limit_bytes=VMEM_LIMIT),
    )(u2, bd2, cd2, ar8, ai8)


def _s5_scan_bwd(dy, x, u2, bd2, cd2, ar8, ai8):
    rows, width = u2.shape
    ns = S5_STATES
    steps = S5_STEPS_BWD
    tr = S5_ROWS * steps
    nblk = rows // tr
    assert rows % tr == 0
    nt = (((1,), (1,)), ((), ()))
    tn = (((0,), (0,)), ((), ()))

    def body(dy_ref, x_ref, xp_ref, u_ref, bd_ref, cd_ref, ar_ref, ai_ref,
             du_ref, dbd_ref, dcd_ref, dar_ref, dai_ref, g_ref, st_ref):
        k = pl.program_id(0)

        @pl.when(k == 0)
        def _():
            st_ref[...] = jnp.zeros_like(st_ref)
            dbd_ref[...] = jnp.zeros_like(dbd_ref)
            dcd_ref[...] = jnp.zeros_like(dcd_ref)
            dar_ref[...] = jnp.zeros_like(dar_ref)
            dai_ref[...] = jnp.zeros_like(dai_ref)

        dyb = dy_ref[...].astype(BF16)
        g_ref[...] = lax.dot_general(dyb, cd_ref[...], nt, preferred_element_type=F32)
        ar, ai = ar_ref[...], ai_ref[...]

        def adjoint(r, carry, xpr, xpi):
            gr_n, gi_n, dar, dai = carry
            gr = g_ref[r, 0:ns] + ar * gr_n + ai * gi_n
            gi = g_ref[r, ns:2 * ns] - ai * gr_n + ar * gi_n
            g_ref[r, 0:ns] = gr
            g_ref[r, ns:2 * ns] = gi
            return gr, gi, dar + gr * xpr + gi * xpi, dai + gi * xpr - gr * xpi

        def step(i, carry):
            t = steps - 1 - i
            r = pl.ds(pl.multiple_of(t * S5_ROWS, S5_ROWS), S5_ROWS)
            rp = pl.ds(pl.multiple_of((t - 1) * S5_ROWS, S5_ROWS), S5_ROWS)
            return adjoint(r, carry, x_ref[rp, 0:ns], x_ref[rp, ns:2 * ns])

        zero = jnp.zeros((S5_ROWS, ns), F32)
        carry = lax.fori_loop(0, steps - 1, step, (st_ref[:, 0:ns], st_ref[:, ns:2 * ns], zero, zero), unroll=2)
        has_prev = (k < nblk - 1).astype(F32)
        gr, gi, dar, dai = adjoint(pl.ds(0, S5_ROWS), carry, xp_ref[:, 0:ns] * has_prev, xp_ref[:, ns:2 * ns] * has_prev)
        st_ref[:, 0:ns] = gr
        st_ref[:, ns:2 * ns] = gi
        dar_ref[...] += dar
        dai_ref[...] += dai
        gb = g_ref[...].astype(BF16)
        du_ref[...] = lax.dot_general(gb, bd_ref[...], nt, preferred_element_type=F32)
        dbd_ref[...] += lax.dot_general(u_ref[...].astype(BF16), gb, tn, preferred_element_type=F32)
        dcd_ref[...] += lax.dot_general(x_ref[...].astype(BF16), dyb, tn, preferred_element_type=F32)

    whole = lambda shape: pl.BlockSpec(shape, lambda k: (0, 0))
    rev = lambda k: (nblk - 1 - k, 0)
    prev = lambda k: (jnp.maximum((nblk - 1 - k) * steps - 1, 0), 0)
    return pl.pallas_call(
        body, name="s5_scan_bwd",
        out_shape=(jax.ShapeDtypeStruct((rows, width), F32), jax.ShapeDtypeStruct(bd2.shape, F32),
                   jax.ShapeDtypeStruct(cd2.shape, F32), jax.ShapeDtypeStruct(ar8.shape, F32),
                   jax.ShapeDtypeStruct(ai8.shape, F32)),
        grid=(nblk,),
        in_specs=[pl.BlockSpec((tr, width), rev), pl.BlockSpec((tr, 2 * ns), rev),
                  pl.BlockSpec((S5_ROWS, 2 * ns), prev), pl.BlockSpec((tr, width), rev),
                  whole(bd2.shape), whole(cd2.shape), whole(ar8.shape), whole(ai8.shape)],
        out_specs=(pl.BlockSpec((tr, width), rev), whole(bd2.shape), whole(cd2.shape), whole(ar8.shape),
                   whole(ai8.shape)),
        scratch_shapes=[pltpu.VMEM((tr, 2 * ns), F32), pltpu.VMEM((S5_ROWS, 2 * ns), F32)],
        compiler_params=pltpu.CompilerParams(dimension_semantics=("arbitrary",), vmem_limit_bytes=VMEM_LIMIT),
    )(dy, x, x, u2, bd2, cd2, ar8, ai8)


@jax.custom_vjp
def s5_core(u2, bd2, cd2, ar8, ai8):
    return _s5_scan_fwd(u2, bd2.astype(BF16), cd2.astype(BF16), ar8, ai8)[0]


def _s5_core_fwd(u2, bd2, cd2, ar8, ai8):
    bd2, cd2 = bd2.astype(BF16), cd2.astype(BF16)
    y, x = _s5_scan_fwd(u2, bd2, cd2, ar8, ai8)
    return y, (x, u2, bd2, cd2, ar8, ai8)


def _s5_core_bwd(res, dy):
    return _s5_scan_bwd(dy, *res)


s5_core.defvjp(_s5_core_fwd, _s5_core_bwd)


def s5_mixers(p_ctx, p_lat, lam_re, lam_im, log_step, b_re, b_im, c_re, c_im, d_skip, glu_w, glu_b):
    bsz = p_ctx.shape[0]
    assert 2 * bsz == S5_ROWS
    eye = jnp.eye(B_NGROUPS, dtype=F32)
    bds, cds, ars, ais = [], [], [], []
    for d in range(2):
        step = jnp.exp(log_step[d])[:, None]
        mag = jnp.exp(lam_re[d] * step)
        ar = mag * jnp.cos(lam_im[d] * step)
        ai = mag * jnp.sin(lam_im[d] * step)
        den = lam_re[d] * lam_re[d] + lam_im[d] * lam_im[d]
        nr = ar - 1.0
        kr = (nr * lam_re[d] + ai * lam_im[d]) / den
        ki = (ai * lam_re[d] - nr * lam_im[d]) / den
        br = kr[..., None] * b_re[d] - ki[..., None] * b_im[d]
        bi = kr[..., None] * b_im[d] + ki[..., None] * b_re[d]
        blk = lambda w: jnp.einsum('gnc,gh->gchn', w, eye).reshape(B_WIDTH, S5_STATES)
        bds.append(jnp.concatenate([blk(br), blk(bi)], axis=1))
        blk_c = lambda w: jnp.einsum('gcn,gh->gnhc', w, eye).reshape(S5_STATES, B_WIDTH)
        cds.append(jnp.concatenate([blk_c(c_re[d]), -blk_c(c_im[d])], axis=0))
        ars.append(jnp.broadcast_to(ar.reshape(1, S5_STATES), (bsz, S5_STATES)))
        ais.append(jnp.broadcast_to(ai.reshape(1, S5_STATES), (bsz, S5_STATES)))
    bd2 = jnp.concatenate(bds, axis=0)
    cd2 = jnp.concatenate(cds, axis=1)
    ar8 = jnp.concatenate(ars, axis=0)
    ai8 = jnp.concatenate(ais, axis=0)

    def rows_of(p):
        ut = jnp.swapaxes(p, 0, 1)
        z = jnp.zeros_like(ut)
        return jnp.concatenate([jnp.concatenate([ut, z], axis=-1), jnp.concatenate([z, _flip_time(ut, 0)], axis=-1)], axis=1)

    lc = p_ctx.shape[1]
    u2 = jnp.concatenate([rows_of(p_ctx), rows_of(p_lat)], axis=0)
    y2 = s5_core(u2.reshape(-1, 2 * B_WIDTH), bd2, cd2, ar8, ai8).reshape(u2.shape)

    def finish(y2p, p):
        y = y2p[:, :bsz, :B_WIDTH] + _flip_time(y2p[:, bsz:, B_WIDTH:], 0)
        y = jnp.swapaxes(y, 0, 1) + d_skip * p
        y = jax.nn.gelu(y)
        gate = mm(y.reshape(-1, B_WIDTH), glu_w).reshape(y.shape)
        return y * jax.nn.sigmoid(gate + glu_b)

    return finish(y2[:lc], p_ctx), finish(y2[lc:], p_lat)


GLA_CHUNK = 64
GLA_SUB = 16
NT_DIMS = (((1,), (1,)), ((), ()))
TN_DIMS = (((0,), (0,)), ((), ()))


def _bdot(a, b, dims=(((1,), (0,)), ((), ()))):
    return lax.dot_general(a.astype(BF16), b.astype(BF16), dims, preferred_element_type=F32)


def _hdot(a, b, dims=(((1,), (0,)), ((), ()))):
    ah, bh = a.astype(BF16), b.astype(BF16)
    al, bl = (a - ah.astype(F32)).astype(BF16), (b - bh.astype(F32)).astype(BF16)
    dot = functools.partial(lax.dot_general, dimension_numbers=dims, preferred_element_type=F32)
    return dot(ah, bh) + (dot(ah, bl) + dot(al, bh))


def _gla_scores(q, k, cum, cumr, tri):
    n = GLA_CHUNK
    if cumr is not None:
        decay = jnp.where(tri, jnp.exp(jnp.where(tri, cum - cumr, 0.0)), 0.0)
        return _bdot(q, k, NT_DIMS) * decay, decay
    rows = lax.broadcasted_iota(jnp.int32, (n, 1), 0)
    parts = []
    for i in range(n // GLA_SUB):
        lo, hi = i * GLA_SUB, (i + 1) * GLA_SUB
        ref = cum[lo - 1:lo, :] if i else jnp.zeros_like(cum[0:1, :])
        qt = q[lo:hi] * jnp.exp(cum[lo:hi] - ref)
        seen = rows < hi
        kh = jnp.where(seen, k * jnp.exp(jnp.where(seen, ref - cum, 0.0)), 0.0)
        parts.append(_bdot(qt, kh, NT_DIMS))
    return jnp.where(tri, jnp.concatenate(parts, axis=0), 0.0), None


def _gla_fwd(q, k, cum, cumr, v):
    bsz, length, width = q.shape
    dk = GLA_CHUNK
    nh = width // dk
    nc = length // GLA_CHUNK
    scalar = cumr is not None

    def body(*refs):
        if scalar:
            q_ref, k_ref, cum_ref, cumr_ref, v_ref, o_ref, s_ref, st_ref = refs
        else:
            q_ref, k_ref, cum_ref, v_ref, o_ref, s_ref, st_ref = refs

        @pl.when(pl.program_id(1) == 0)
        def _():
            st_ref[...] = jnp.zeros_like(st_ref)

        ii = lax.broadcasted_iota(jnp.int32, (GLA_CHUNK, GLA_CHUNK), 0)
        jj = lax.broadcasted_iota(jnp.int32, (GLA_CHUNK, GLA_CHUNK), 1)
        tri = jj <= ii
        qa, ka, ca, va = q_ref[0], k_ref[0], cum_ref[0], v_ref[0]
        cra = cumr_ref[0] if scalar else None
        outs = []
        for h in range(nh):
            sl = slice(h * dk, (h + 1) * dk)
            qv, kv, cv, vv, st = qa[:, sl], ka[:, sl], ca[:, sl], va[:, sl], st_ref[h]
            s_ref[0, 0, h] = st
            a, _ = _gla_scores(qv, kv, cv, cra[:, sl] if scalar else None, tri)
            outs.append(_bdot(qv * jnp.exp(cv), st, NT_DIMS) + _bdot(a, vv))
            last = cv[GLA_CHUNK - 1:GLA_CHUNK, :]
            st_ref[h] = st * jnp.exp(last) + _bdot(vv, kv * jnp.exp(last - cv), TN_DIMS)
        o_ref[0] = jnp.concatenate(outs, axis=1)

    seq = pl.BlockSpec((1, GLA_CHUNK, width), lambda n, c: (n, c, 0))
    state = pl.BlockSpec((1, 1, nh, dk, dk), lambda n, c: (n, c, 0, 0, 0))
    ins = [q, k, cum] + ([cumr] if scalar else []) + [v]
    return pl.pallas_call(
        body, name="gla_fwd_scalar" if scalar else "gla_fwd",
        out_shape=(jax.ShapeDtypeStruct((bsz, length, width), F32), jax.ShapeDtypeStruct((bsz, nc, nh, dk, dk), F32)),
        grid=(bsz, nc),
        in_specs=[seq] * len(ins),
        out_specs=(seq, state),
        scratch_shapes=[pltpu.VMEM((nh, dk, dk), F32)],
        compiler_params=pltpu.CompilerParams(dimension_semantics=("parallel", "arbitrary"),
                                             vmem_limit_bytes=VMEM_LIMIT),
    )(*ins)


def _gla_bwd(do, q, k, cum, cumr, v, states):
    bsz, length, width = q.shape
    nc = length // GLA_CHUNK
    scalar = cumr is not None
    n = GLA_CHUNK
    dk = GLA_CHUNK
    nh = width // dk

    def body(*refs):
        if scalar:
            do_ref, q_ref, k_ref, cum_ref, cumr_ref, v_ref, s_ref, dq_ref, dk_ref, dc_ref, dcr_ref, dv_ref, dst_ref = refs
        else:
            do_ref, q_ref, k_ref, cum_ref, v_ref, s_ref, dq_ref, dk_ref, dc_ref, dv_ref, dst_ref = refs

        @pl.when(pl.program_id(1) == 0)
        def _():
            dst_ref[...] = jnp.zeros_like(dst_ref)

        ii = lax.broadcasted_iota(jnp.int32, (n, n), 0)
        jj = lax.broadcasted_iota(jnp.int32, (n, n), 1)
        tri = jj <= ii
        rows = lax.broadcasted_iota(jnp.int32, (n, 1), 0)
        doa, qa, ka, ca, va = do_ref[0], q_ref[0], k_ref[0], cum_ref[0], v_ref[0]
        cra = cumr_ref[0] if scalar else None
        dqs, dks, dcs, dcrs, dvs = [], [], [], [], []
        for h in range(nh):
            sl = slice(h * dk, (h + 1) * dk)
            dov, qv, kv, cv, vv, st, dst = doa[:, sl], qa[:, sl], ka[:, sl], ca[:, sl], va[:, sl], s_ref[0, 0, h], dst_ref[h]
            e = jnp.exp(cv)
            qe = qv * e
            last = cv[n - 1:n, :]
            w = jnp.exp(last - cv)
            kw = kv * w
            el = jnp.exp(last)
            d_qe = _hdot(dov, st)
            d_kw = _hdot(vv, dst)
            dv = _bdot(kw, dst, NT_DIMS)
            d_last = jnp.sum(st * dst, axis=0, keepdims=True) * el + jnp.sum(d_kw * kw, axis=0, keepdims=True)
            dst_ref[h] = dst * el + _bdot(dov, qe, TN_DIMS)
            dq = d_qe * e
            dkk = d_kw * w
            dc = d_qe * qe - d_kw * kw + jnp.where(rows == n - 1, d_last, 0.0)
            da = jnp.where(tri, _hdot(dov, vv, NT_DIMS), 0.0)
            if scalar:
                a, decay = _gla_scores(qv, kv, cv, cra[:, sl], tri)
                dg = da * decay
                dq = dq + _bdot(dg, kv)
                dkk = dkk + _bdot(dg, qv, TN_DIMS)
                p = da * a
                dc = dc + p
                dcrs.append(-p)
            else:
                a_parts, dq_parts = [], []
                for i in range(n // GLA_SUB):
                    lo, hi = i * GLA_SUB, (i + 1) * GLA_SUB
                    ref = cv[lo - 1:lo, :] if i else jnp.zeros_like(cv[0:1, :])
                    eq = jnp.exp(cv[lo:hi] - ref)
                    qt = qv[lo:hi] * eq
                    seen = rows < hi
                    ek = jnp.where(seen, jnp.exp(jnp.where(seen, ref - cv, 0.0)), 0.0)
                    kh = kv * ek
                    a_parts.append(_bdot(qt, kh, NT_DIMS))
                    dqt = _hdot(da[lo:hi], kh)
                    dkh = _hdot(da[lo:hi], qt, TN_DIMS)
                    dq_parts.append((dqt * eq, dqt * qt))
                    dkk = dkk + dkh * ek
                    dc = dc - dkh * kh
                a = jnp.where(tri, jnp.concatenate(a_parts, axis=0), 0.0)
                dq = dq + jnp.concatenate([p[0] for p in dq_parts], axis=0)
                dc = dc + jnp.concatenate([p[1] for p in dq_parts], axis=0)
            dvs.append(dv + _bdot(a, dov, TN_DIMS))
            dqs.append(dq)
            dks.append(dkk)
            dcs.append(dc)
        cat = functools.partial(jnp.concatenate, axis=1)
        dq_ref[0], dk_ref[0], dc_ref[0], dv_ref[0] = cat(dqs), cat(dks), cat(dcs), cat(dvs)
        if scalar:
            dcr_ref[0] = cat(dcrs)

    seq = pl.BlockSpec((1, n, width), lambda s, c: (s, nc - 1 - c, 0))
    state = pl.BlockSpec((1, 1, nh, dk, dk), lambda s, c: (s, nc - 1 - c, 0, 0, 0))
    ins = [do, q, k, cum] + ([cumr] if scalar else []) + [v]
    n_out = 5 if scalar else 4
    return pl.pallas_call(
        body, name="gla_bwd_scalar" if scalar else "gla_bwd",
        out_shape=(jax.ShapeDtypeStruct((bsz, length, width), F32),) * n_out,
        grid=(bsz, nc),
        in_specs=[seq] * len(ins) + [state],
        out_specs=(seq,) * n_out,
        scratch_shapes=[pltpu.VMEM((nh, dk, dk), F32)],
        compiler_params=pltpu.CompilerParams(dimension_semantics=("parallel", "arbitrary"),
                                             vmem_limit_bytes=VMEM_LIMIT),
    )(*ins, states)


@jax.custom_vjp
def gla(q, k, cum, v):
    return _gla_fwd(q, k, cum, None, v)[0]


def _gla_vjp_fwd(q, k, cum, v):
    o, states = _gla_fwd(q, k, cum, None, v)
    return o, (q, k, cum, v, states)


def _gla_vjp_bwd(res, do):
    q, k, cum, v, states = res
    return _gla_bwd(do, q, k, cum, None, v, states)


gla.defvjp(_gla_vjp_fwd, _gla_vjp_bwd)


@jax.custom_vjp
def gla_scalar(q, k, cum, cumr, v):
    return _gla_fwd(q, k, cum, cumr, v)[0]


def _gla_scalar_vjp_fwd(q, k, cum, cumr, v):
    o, states = _gla_fwd(q, k, cum, cumr, v)
    return o, (q, k, cum, cumr, v, states)


def _gla_scalar_vjp_bwd(res, do):
    q, k, cum, cumr, v, states = res
    return _gla_bwd(do, q, k, cum, cumr, v, states)


gla_scalar.defvjp(_gla_scalar_vjp_fwd, _gla_scalar_vjp_bwd)


def _chunk_cumsum(g):
    s = g.shape
    return jnp.cumsum(g.reshape(s[:-2] + (s[-2] // GLA_CHUNK, GLA_CHUNK, s[-1])), axis=-2).reshape(s)


def _both_ways(t_ctx, t_lat, flip):
    parts = [t.reshape(t.shape[:2] + (-1,)) for t in (t_ctx, t_lat)]
    return jnp.concatenate([_flip_time(t, 1) if flip else t for t in parts], axis=1)


def _undo_ways(o, lc, flip, nh):
    parts = (o[:, :lc], o[:, lc:])
    return tuple((_flip_time(t, 1) if flip else t).reshape(t.shape[:2] + (nh, -1)) for t in parts)


def hgrn2_mixers(p_ctx, p_lat, lower, norm_w):
    bsz, lc = p_ctx.shape[:2]
    lower = lower.reshape(2, C_HEADS, C_KEY)

    def heads(p, lo, hi):
        return p[..., lo:hi].reshape(p.shape[:2] + (C_HEADS, -1))

    q_c, q_l = (jax.nn.silu(heads(p, 0, C_WIDTH)) for p in (p_ctx, p_lat))
    v_c, v_l = (heads(p, 3 * C_WIDTH, 4 * C_WIDTH) for p in (p_ctx, p_lat))
    qs, ks, cums, vs = [], [], [], []
    for d in range(2):
        f_c, f_l = (lower[d] + (1.0 - lower[d]) * jax.nn.sigmoid(heads(p, (1 + d) * C_WIDTH, (2 + d) * C_WIDTH))
                    for p in (p_ctx, p_lat))
        qs.append(_both_ways(q_c, q_l, d))
        vs.append(_both_ways(v_c, v_l, d))
        ks.append(_both_ways(1.0 - f_c, 1.0 - f_l, d))
        cums.append(_chunk_cumsum(_both_ways(jnp.log(f_c), jnp.log(f_l), d)))
    o = [gla(qs[d], ks[d], cums[d], vs[d]) for d in range(2)]
    f_c, f_l = _undo_ways(o[0], lc, False, C_HEADS)
    b_c, b_l = _undo_ways(o[1], lc, True, C_HEADS)
    outs = []
    for o_sum, p in ((f_c + b_c, p_ctx), (f_l + b_l, p_lat)):
        o_n = rms_norm(o_sum) * norm_w.reshape(C_HEADS, C_VAL)
        outs.append(o_n.reshape(p.shape[:2] + (C_WIDTH,)) * jax.nn.silu(p[..., 4 * C_WIDTH:]))
    return tuple(outs)


def ssd_mixers(p_ctx, p_lat, conv_w, conv_b, dt_bias, a_log, d_skip, norm_w):
    bsz, lc = p_ctx.shape[:2]
    rep = A_HEADS // A_GROUPS
    a = -jnp.exp(a_log)
    xs, bs, cs, dts, zs = [], [], [], [], []
    for p in (p_ctx, p_lat):
        z, xbc, dt_raw = jnp.split(p, [A_INNER, A_INNER + A_CONV_DIM], axis=-1)
        xbc = jax.nn.silu(depthwise_conv(xbc, conv_w, conv_b))
        x_, b_, c_ = jnp.split(xbc, [A_INNER, A_INNER + A_GROUPS * A_STATE], axis=-1)
        shp = p.shape[:2]
        xs.append(x_.reshape(shp + (A_HEADS, A_HEAD_DIM)))
        bs.append(jnp.repeat(b_.reshape(shp + (A_GROUPS, A_STATE)), rep, axis=2))
        cs.append(jnp.repeat(c_.reshape(shp + (A_GROUPS, A_STATE)), rep, axis=2))
        dts.append(jax.nn.softplus(dt_raw.reshape(shp + (2, A_HEADS)) + dt_bias))
        zs.append(z)
    qs, ks, cums, cumrs, vs = [], [], [], [], []
    for d in range(2):
        qs.append(_both_ways(cs[0], cs[1], d))
        vs.append(_both_ways(xs[0], xs[1], d))
        ks.append(_both_ways(bs[0] * dts[0][:, :, d, :, None], bs[1] * dts[1][:, :, d, :, None], d))
        adt = [_flip_time(t, 1) if d else t for t in (dt[:, :, d, :] * a[d] for dt in dts)]
        adt = jnp.concatenate(adt, axis=1)
        nb, lt = adt.shape[:2]
        by_chunk = jnp.cumsum(adt.reshape(nb, lt // GLA_CHUNK, GLA_CHUNK, A_HEADS), axis=2)
        cums.append(jnp.broadcast_to(by_chunk.reshape(nb, lt, A_HEADS, 1), (nb, lt, A_HEADS, A_STATE)).reshape(nb, lt, -1))
        along = jnp.swapaxes(by_chunk, 2, 3)[:, :, None]
        cumrs.append(jnp.broadcast_to(along, (nb, lt // GLA_CHUNK, GLA_CHUNK, A_HEADS, GLA_CHUNK)).reshape(nb, lt, -1))
    o = [gla_scalar(qs[d], ks[d], cums[d], cumrs[d], vs[d]) for d in range(2)]
    f_c, f_l = _undo_ways(o[0], lc, False, A_HEADS)
    b_c, b_l = _undo_ways(o[1], lc, True, A_HEADS)
    outs = []
    for y, x_, z in ((f_c + b_c, xs[0], zs[0]), (f_l + b_l, xs[1], zs[1])):
        y = y + d_skip[:, None] * x_
        y = y.reshape(z.shape) * jax.nn.silu(z)
        outs.append(rms_norm(y) * norm_w)
    return tuple(outs)


def token_mixers(p_ctx, p_lat, W, l, lower):
    def cut(p):
        return p[..., :A_COLS], p[..., 1408:1408 + B_COLS], p[..., 1664:1664 + C_COLS]

    pa_c, pb_c, pc_c = cut(p_ctx)
    pa_l, pb_l, pc_l = cut(p_lat)
    ya_c, ya_l = ssd_mixers(pa_c, pa_l, W['a_conv_w'][l], W['a_conv_b'][l], W['a_dt_bias'][l], W['a_log'][l],
                            W['a_d'][l], W['a_norm_w'][l])
    yb_c, yb_l = s5_mixers(pb_c, pb_l, W['s5_lam_re'][l], W['s5_lam_im'][l], W['s5_log_step'][l], W['s5_b_re'][l],
                           W['s5_b_im'][l], W['s5_c_re'][l], W['s5_c_im'][l], W['s5_d'][l], W['s5_glu_w'][l],
                           W['s5_glu_b'][l])
    yc_c, yc_l = hgrn2_mixers(pc_c, pc_l, lower, W['hg_norm_w'][l])
    return (jnp.concatenate([ya_c, yb_c, yc_c], axis=-1), jnp.concatenate([ya_l, yb_l, yc_l], axis=-1))


def _pad_w_in(w):
    z = functools.partial(jnp.zeros, dtype=w.dtype)
    return jnp.concatenate([w[:, :A_COLS], z((D_MODEL, 1408 - A_COLS)), w[:, A_COLS:], z((D_MODEL, IN_PAD - 2944))],
                           axis=1)


def _mm3(t, w):
    g, tt, k = t.shape
    return mm(t.reshape(g * tt, k), w).reshape(g, tt, -1)


def _ffn(h, mg, first, w_in, w_out):
    hid = modmm(h, mg[:, first:first + 1], mg[:, first + 1:first + 2], w_in, BF16)
    return gated_add(h, swiglu_mm(hid, w_out), mg[:, first + 2:first + 3], 0.5)


def local_loss(x, W, m_lat, m_ctx, ctx, target):
    bsz, seq, dm = x.shape
    lc = ctx.shape[1]
    tg = bsz * lc
    assert seq % tg == 0
    gl = seq // tg
    ng = bsz * gl
    rows = seq // GRID_W
    p_lb = jax.nn.softmax(W['hg_lb_logits'], axis=0)
    lower_bounds = jnp.cumsum(p_lb, axis=0) - p_lb[:1]
    h = jnp.concatenate([x.reshape(ng, tg, dm), ctx.reshape(1, tg, dm)], axis=0)
    for l in range(DEPTH):
        last = l == DEPTH - 1
        col_major = l % 2 == 1
        mg = jnp.concatenate([jnp.repeat(m_lat[l], gl, axis=0), m_ctx[l][None]], axis=0)
        h = _ffn(h, mg, 0, W['ffn_w_in'][l][0], W['ffn_w_out'][l][0])
        hp = h
        if col_major:
            h_lat = raster_to_column(h[:ng].reshape(bsz, seq, dm), rows)
            hp = jnp.concatenate([h_lat.reshape(ng, tg, dm), h[ng:]], axis=0)
        p = modmm(hp, mg[:, 3:4], mg[:, 4:5], _pad_w_in(W['w_in'][l]), F32)
        mix_ctx, mix_lat = token_mixers(p[ng].reshape(bsz, lc, -1), p[:ng].reshape(bsz, seq, -1), W, l,
                                        lower_bounds[l])
        if last:
            h, mg = h[:ng], mg[:ng]
            y_lat = _mm3(mix_lat.reshape(ng, tg, dm), W['w_out'][l])
            y_ctx = None
        else:
            y = _mm3(jnp.concatenate([mix_lat.reshape(ng, tg, dm), mix_ctx.reshape(1, tg, dm)], axis=0), W['w_out'][l])
            y_lat, y_ctx = y[:ng], y[ng:]
        if col_major:
            y_lat = column_to_raster(y_lat.reshape(bsz, seq, dm), rows).reshape(ng, tg, dm)
        y = y_lat if y_ctx is None else jnp.concatenate([y_lat, y_ctx], axis=0)
        h = gated_add(h, y, mg[:, 5:6], 1.0)
        h = _ffn(h, mg, 6, W['ffn_w_in'][l][1], W['ffn_w_out'][l][1])
    y = rms_norm(h[:ng].reshape(bsz, seq, dm)) * W['final_norm_w']
    err = jnp.square(y - target)
    return 0.5 * jnp.sum(jnp.mean(err, axis=-1))


def _pad_rows(a, rows):
    return jnp.concatenate([a, jnp.zeros((rows - a.shape[0],) + a.shape[1:], a.dtype)], axis=0)


def kernel(x, c, ctx, c_ctx, mod_w, mod_b, ffn_w_in, ffn_w_out, w_in, w_out, a_conv_w, a_conv_b, a_dt_bias, a_log, a_d, a_norm_w, s5_lam_re, s5_lam_im, s5_log_step, s5_b_re, s5_b_im, s5_c_re, s5_c_im, s5_d, s5_glu_w, s5_glu_b, hg_lb_logits, hg_norm_w, final_norm_w, loss_target, m_c_ctx, m_mod_w, m_mod_b, m_ffn_w_in, m_ffn_w_out, m_w_in, m_w_out, m_a_conv_w, m_a_conv_b, m_a_dt_bias, m_a_log, m_a_d, m_a_norm_w, m_s5_lam_re, m_s5_lam_im, m_s5_log_step, m_s5_b_re, m_s5_b_im, m_s5_c_re, m_s5_c_im, m_s5_d, m_s5_glu_w, m_s5_glu_b, m_hg_lb_logits, m_hg_norm_w, m_final_norm_w, v_c_ctx, v_mod_w, v_mod_b, v_ffn_w_in, v_ffn_w_out, v_w_in, v_w_out, v_a_conv_w, v_a_conv_b, v_a_dt_bias, v_a_log, v_a_d, v_a_norm_w, v_s5_lam_re, v_s5_lam_im, v_s5_log_step, v_s5_b_re, v_s5_b_im, v_s5_c_re, v_s5_c_im, v_s5_d, v_s5_glu_w, v_s5_glu_b, v_hg_lb_logits, v_hg_norm_w, v_final_norm_w):
    given = dict(locals())
    w_loc = {n: given[n] for n in WEIGHTS}
    m_loc = {n: given["m_" + n] for n in WEIGHTS}
    v_loc = {n: given["v_" + n] for n in WEIGHTS}
    bsz = x.shape[0]
    me = 4 * lax.axis_index("x") + 2 * lax.axis_index("y") + lax.axis_index("c")

    small_sh = [c] + [w_loc[n] for n in SMALL_SHARDED]
    g1 = _unpack(all_gather([_pack(small_sh, 128, 8)], "gather_small")[0], [a.shape for a in small_sh])
    c_all = g1[0].reshape(N_DEV * bsz, D_MODEL)
    gathered = dict(zip(BIG, all_gather([w_loc[n].astype(BF16) for n in BIG], "gather_weights")))
    W = {'ffn_w_in': [[_assemble(gathered['ffn_w_in'][:, l, i], 1) for i in range(2)] for l in range(DEPTH)],
         'ffn_w_out': [[_assemble(gathered['ffn_w_out'][:, l, i], 0) for i in range(2)] for l in range(DEPTH)],
         'w_in': [_assemble(gathered['w_in'][:, l], 1) for l in range(DEPTH)],
         'w_out': [_assemble(gathered['w_out'][:, l], 0) for l in range(DEPTH)]}
    for (n, ax), t in zip(SMALL_SHARDED.items(), g1[1:]):
        W[n] = _assemble(t, ax)
    for n in SMALL:
        if n not in SMALL_SHARDED and n not in ('c_ctx', 'mod_b'):
            W[n] = w_loc[n]

    n_rows = N_DEV * bsz + 1
    pad_rows = 8 * ((n_rows + 7) // 8)
    c_rows = _pad_rows(jnp.concatenate([c_all, c_ctx[None]], axis=0), pad_rows)
    sc = jax.nn.silu(c_rows)
    mods_sh = jnp.stack([matmul(sc, mod_w[l], name="mod_fwd") for l in range(DEPTH)])
    mods = _assemble(all_gather([mods_sh], "gather_mods")[0], 2) + mod_b[:, None, :]
    m_lat = lax.dynamic_slice_in_dim(mods, me * bsz, bsz, axis=1).reshape(DEPTH, bsz, N_MOD, D_MODEL)
    m_ctx = mods[:, n_rows - 1].reshape(DEPTH, N_MOD, D_MODEL)

    loss_loc, (grad_x, gW, gm_lat, gm_ctx) = jax.value_and_grad(local_loss, argnums=(0, 1, 2, 3))(
        x, W, m_lat, m_ctx, ctx, loss_target)
    loss = lax.psum(loss_loc, MESH_AXES)

    dm_loc = jnp.concatenate([gm_lat.reshape(DEPTH, bsz, -1), gm_ctx.reshape(DEPTH, 1, -1)], axis=1)
    (dm_all,) = all_gather([dm_loc], "gather_dmods")
    dm_ex = jnp.moveaxis(dm_all[:, :, :bsz], 0, 1).reshape(DEPTH, N_DEV * bsz, -1)
    ncol = N_MOD * D_MODEL
    dm_cx = sum_leading(dm_all[:, :, bsz].reshape(N_DEV, DEPTH * ncol // 128, 128), "sum_dmods_ctx")
    dm_cx = dm_cx.reshape(DEPTH, 1, ncol)
    dm_rows = jnp.concatenate([dm_ex, dm_cx, jnp.zeros((DEPTH, pad_rows - n_rows, ncol), F32)], axis=1)
    grad_mod_b = sum_leading(jnp.moveaxis(dm_rows, 1, 0).reshape(pad_rows, DEPTH * ncol // 128, 128),
                             "sum_mod_b").reshape(DEPTH, ncol)
    my_cols = ncol // N_DEV
    dm_mine = lax.dynamic_slice_in_dim(dm_rows, me * my_cols, my_cols, axis=2)
    grad_mod_w = jnp.stack([matmul(sc, dm_mine[l], ta=True, name="mod_dw") for l in range(DEPTH)])
    dm_cx_mine = lax.dynamic_slice_in_dim(dm_cx, me * my_cols, my_cols, axis=2)
    g_sc_ctx = sum(matmul(_pad_rows(dm_cx_mine[l], 8), mod_w[l], tb=True, name="mod_dc")[0] for l in range(DEPTH))

    small_full = [n for n in SMALL if n not in ('c_ctx', 'mod_b')]
    part = [gW[n] for n in small_full] + [g_sc_ctx]
    red = sum_leading(all_gather([_pack(part, 128, 8)], "gather_small_grads")[0], "sum_small_grads")
    red = _unpack(red, [a.shape for a in part])
    grads = dict(zip(small_full, red[:-1]))
    sig = jax.nn.sigmoid(c_ctx)
    grads['c_ctx'] = red[-1] * (sig * (1.0 + c_ctx * (1.0 - sig)))
    grads['mod_b'] = grad_mod_b
    for n, ax in SMALL_SHARDED.items():
        size = w_loc[n].shape[ax]
        grads[n] = lax.dynamic_slice_in_dim(grads[n], me * size, size, axis=ax)
    grads['mod_w'] = grad_mod_w

    by_dev = {'ffn_w_in': jnp.stack([jnp.stack([_split(g, 1) for g in gl], axis=1) for gl in gW['ffn_w_in']], axis=1),
              'ffn_w_out': jnp.stack([jnp.stack([_split(g, 0) for g in gl], axis=1) for gl in gW['ffn_w_out']], axis=1),
              'w_in': jnp.stack([_split(g, 1) for g in gW['w_in']], axis=1),
              'w_out': jnp.stack([_split(g, 0) for g in gW['w_out']], axis=1)}
    for n, t in zip(BIG, all_to_all([by_dev[n] for n in BIG], "exchange_grads")):
        grads[n] = sum_leading(t.reshape((N_DEV,) + _as_2d(t.shape[1:])), "sum_grads").reshape(t.shape[1:])

    delta, new_m, new_v = {}, {}, {}

    for n in list(BIG) + ['mod_w']:
        outs = adamw(*[d[n].reshape(_as_2d(d[n].shape)) for d in (w_loc, grads, m_loc, v_loc)], name="adamw_" + n)
        delta[n], new_m[n], new_v[n] = (o.reshape(w_loc[n].shape) for o in outs)

    def update(names, width, row_mult, tag):
        packed = [_pack([d[n] for n in names], width, row_mult) for d in (w_loc, grads, m_loc, v_loc)]
        outs = adamw(*packed, name="adamw_" + tag)
        shapes = [w_loc[n].shape for n in names]
        for res, out in zip((delta, new_m, new_v), outs):
            res.update(zip(names, _unpack(out, shapes)))

    update(SMALL, 128, 256, "small")
    return (loss, grad_x, *[grads[n] for n in WEIGHTS], *[delta[n] for n in WEIGHTS],
            *[new_m[n] for n in WEIGHTS], *[new_v[n] for n in WEIGHTS])
```

```python
import functools
import math

import jax
import jax.numpy as jnp
from jax import lax
from jax.experimental import pallas as pl
from jax.experimental.pallas import tpu as pltpu

F32 = jnp.float32
BF16 = jnp.bfloat16
MESH_AXES = ("x", "y", "c")
N_DEV = 8
MESH_ID = pl.DeviceIdType.MESH
VMEM_LIMIT = 48 * 1024 * 1024

D_MODEL = 1024
DEPTH = 2
GRID_W = 64
EPS = 1e-6
N_MOD = 9
D_FF = 2816
A_INNER = 512
A_HEADS = 8
A_HEAD_DIM = 64
A_GROUPS = 2
A_STATE = 64
A_CONV = 5
A_CONV_DIM = A_INNER + 2 * A_GROUPS * A_STATE
A_COLS = A_INNER + A_CONV_DIM + 2 * A_HEADS
B_WIDTH = 256
B_GROUP = 16
B_NGROUPS = 16
B_STATE = 64
B_COLS = B_WIDTH
C_WIDTH = 256
C_HEADS = 4
C_KEY = 64
C_VAL = 64
C_COLS = 5 * C_WIDTH
IN_PAD = 3072

ADAM_LR = 0.001
ADAM_B1 = 0.9
ADAM_B2 = 0.999
ADAM_EPS = 1e-08
ADAM_WD = 0.01
ADAM_STEP = 10

WEIGHTS = ['c_ctx', 'mod_w', 'mod_b', 'ffn_w_in', 'ffn_w_out', 'w_in', 'w_out', 'a_conv_w', 'a_conv_b', 'a_dt_bias',
           'a_log', 'a_d', 'a_norm_w', 's5_lam_re', 's5_lam_im', 's5_log_step', 's5_b_re', 's5_b_im', 's5_c_re',
           's5_c_im', 's5_d', 's5_glu_w', 's5_glu_b', 'hg_lb_logits', 'hg_norm_w', 'final_norm_w']
BIG = {'ffn_w_in': 3, 'ffn_w_out': 2, 'w_in': 2, 'w_out': 1}
SMALL_SHARDED = {'a_conv_w': 2, 's5_glu_w': 1, 'hg_lb_logits': 2}
SMALL = [n for n in WEIGHTS if n not in BIG and n != 'mod_w']


def _tile(d, prefs):
    for p in prefs:
        if d % p == 0:
            return p
    return d


def _pack(arrs, width, row_mult):
    flat = jnp.concatenate([a.reshape(-1) for a in arrs])
    pad = (-flat.shape[0]) % (width * row_mult)
    if pad:
        flat = jnp.concatenate([flat, jnp.zeros((pad,), flat.dtype)])
    return flat.reshape(-1, width)


def _as_2d(shape):
    return (math.prod(shape[:-1]), shape[-1])


def _unpack(buf, shapes):
    lead = buf.shape[:-2]
    flat = buf.reshape(lead + (-1,))
    out, off = [], 0
    for s in shapes:
        n = math.prod(s)
        out.append(flat[..., off:off + n].reshape(lead + tuple(s)))
        off += n
    return out


def _assemble(g, axis):
    t = jnp.moveaxis(g, 0, axis)
    s = t.shape
    return t.reshape(s[:axis] + (s[axis] * s[axis + 1],) + s[axis + 2:])


def _split(full, axis):
    s = full.shape
    t = full.reshape(s[:axis] + (N_DEV, s[axis] // N_DEV) + s[axis + 1:])
    return jnp.moveaxis(t, axis, 0)


def all_gather(xs, name):
    nt = len(xs)

    def body(*refs):
        x_refs, out_refs = refs[:nt], refs[nt:2 * nt]
        send_sems, recv_sems, local_sems = refs[2 * nt:]
        ax, ay, ac = lax.axis_index("x"), lax.axis_index("y"), lax.axis_index("c")
        me, sibling = (ax, ay, ac), (ax, ay, 1 - ac)
        chips = [(1 - ax, ay), (ax, 1 - ay), (1 - ax, 1 - ay)]

        def slot(t, px, py, pc):
            return out_refs[t].at[4 * px + 2 * py + pc]

        def copy(t, k, block, to, src=None):
            return pltpu.make_async_remote_copy(
                src_ref=slot(t, *block) if src is None else src, dst_ref=slot(t, *block),
                send_sem=send_sems.at[t, k], recv_sem=recv_sems.at[t, k], device_id=to, device_id_type=MESH_ID)

        mine = [pltpu.make_async_copy(x_refs[t], slot(t, *me), local_sems.at[t]) for t in range(nt)]
        for cp in mine:
            cp.start()
        first = []
        for t in range(nt):
            first.append(copy(t, 0, me, sibling, src=x_refs[t]))
            first += [copy(t, 1 + j, me, (*chip, ac), src=x_refs[t]) for j, chip in enumerate(chips)]
        for cp in first:
            cp.start()
        passed = []
        for j, chip in enumerate(chips):
            for t in range(nt):
                copy(t, 1 + j, (*chip, ac), me).wait_recv()
                passed.append(copy(t, 4 + j, (*chip, ac), sibling))
                passed[-1].start()
        for t in range(nt):
            copy(t, 0, sibling, me).wait_recv()
            for j, chip in enumerate(chips):
                copy(t, 4 + j, (*chip, 1 - ac), me).wait_recv()
        for cp in first + passed:
            cp.wait_send()
        for cp in mine:
            cp.wait()

    return pl.pallas_call(
        body, name=name,
        out_shape=tuple(jax.ShapeDtypeStruct((N_DEV,) + x.shape, x.dtype) for x in xs),
        in_specs=[pl.BlockSpec(memory_space=pl.ANY)] * nt,
        out_specs=tuple(pl.BlockSpec(memory_space=pl.ANY) for _ in xs),
        scratch_shapes=[pltpu.SemaphoreType.DMA((nt, 7)), pltpu.SemaphoreType.DMA((nt, 7)),
                        pltpu.SemaphoreType.DMA((nt,))],
    )(*xs)


def all_to_all(gs, name):
    nt = len(gs)

    def body(*refs):
        g_refs, out_refs = refs[:nt], refs[nt:2 * nt]
        send_sems, recv_sems, local_sems = refs[2 * nt:]
        ax, ay, ac = lax.axis_index("x"), lax.axis_index("y"), lax.axis_index("c")
        my = 4 * ax + 2 * ay + ac
        local = [pltpu.make_async_copy(g_refs[t].at[my], out_refs[t].at[my], local_sems.at[t]) for t in range(nt)]
        for cp in local:
            cp.start()
        peers = []
        for r in range(1, N_DEV):
            px = 1 - ax if r & 4 else ax
            py = 1 - ay if r & 2 else ay
            pc = 1 - ac if r & 1 else ac
            peers.append((px, py, pc))

        def copy(t, k, peer):
            return pltpu.make_async_remote_copy(
                src_ref=g_refs[t].at[4 * peer[0] + 2 * peer[1] + peer[2]], dst_ref=out_refs[t].at[my],
                send_sem=send_sems.at[t, k], recv_sem=recv_sems.at[t, k], device_id=peer, device_id_type=MESH_ID)

        def arrival(t, k, peer):
            slot = 4 * peer[0] + 2 * peer[1] + peer[2]
            return pltpu.make_async_remote_copy(
                src_ref=g_refs[t].at[slot], dst_ref=out_refs[t].at[slot],
                send_sem=send_sems.at[t, k], recv_sem=recv_sems.at[t, k], device_id=peer, device_id_type=MESH_ID)

        sends = [copy(t, k, p) for t in range(nt) for k, p in enumerate(peers)]
        for cp in sends:
            cp.start()
        for t in range(nt):
            for k, p in enumerate(peers):
                arrival(t, k, p).wait_recv()
        for cp in sends:
            cp.wait_send()
        for cp in local:
            cp.wait()

    return pl.pallas_call(
        body, name=name,
        out_shape=tuple(jax.ShapeDtypeStruct(g.shape, g.dtype) for g in gs),
        in_specs=[pl.BlockSpec(memory_space=pl.ANY)] * nt,
        out_specs=tuple(pl.BlockSpec(memory_space=pl.ANY) for _ in gs),
        scratch_shapes=[pltpu.SemaphoreType.DMA((nt, 7)), pltpu.SemaphoreType.DMA((nt, 7)),
                        pltpu.SemaphoreType.DMA((nt,))],
    )(*gs)


def matmul(a, b, *, ta=False, tb=False, out_dtype=F32, name="mm"):
    m, k = (a.shape[1], a.shape[0]) if ta else a.shape
    n = b.shape[0] if tb else b.shape[1]
    assert (b.shape[1] if tb else b.shape[0]) == k, (a.shape, b.shape, ta, tb)
    tm = _tile(m, (1024, 512, 256, 128))
    tn = _tile(n, (1408, 1024, 512, 384, 256, 128))
    tk = _tile(k, (1024, 512, 256, 128))
    nk = k // tk
    dims = (((0 if ta else 1,), (1 if tb else 0,)), ((), ()))

    def body(a_ref, b_ref, o_ref, acc_ref):
        step = pl.program_id(2)

        @pl.when(step == 0)
        def _():
            acc_ref[...] = jnp.zeros_like(acc_ref)

        acc_ref[...] += lax.dot_general(a_ref[...].astype(BF16), b_ref[...].astype(BF16), dims,
                                        preferred_element_type=F32)

        @pl.when(step == nk - 1)
        def _():
            o_ref[...] = acc_ref[...].astype(out_dtype)

    a_spec = pl.BlockSpec((tk, tm), lambda i, j, s: (s, i)) if ta else pl.BlockSpec((tm, tk), lambda i, j, s: (i, s))
    b_spec = pl.BlockSpec((tn, tk), lambda i, j, s: (j, s)) if tb else pl.BlockSpec((tk, tn), lambda i, j, s: (s, j))
    return pl.pallas_call(
        body, name=name,
        out_shape=jax.ShapeDtypeStruct((m, n), out_dtype),
        grid=(m // tm, n // tn, nk),
        in_specs=[a_spec, b_spec],
        out_specs=pl.BlockSpec((tm, tn), lambda i, j, s: (i, j)),
        scratch_shapes=[pltpu.VMEM((tm, tn), F32)],
        compiler_params=pltpu.CompilerParams(dimension_semantics=("parallel", "parallel", "arbitrary"),
                                             vmem_limit_bytes=VMEM_LIMIT),
    )(a, b)


@jax.custom_vjp
def mm(x, w):
    return matmul(x, w, name="mm_fwd")


def _mm_fwd(x, w):
    return matmul(x, w, name="mm_fwd"), (x, w)


def _mm_bwd(res, dy):
    x, w = res
    dx = matmul(dy, w, tb=True, out_dtype=x.dtype, name="mm_dx")
    dw = matmul(x, dy, ta=True, out_dtype=w.dtype, name="mm_dw")
    return dx, dw


mm.defvjp(_mm_fwd, _mm_bwd)


def sum_leading(x, name):
    n, r, c = x.shape
    tr = _tile(r, (256, 128, 64, 32, 16, 8))

    def body(x_ref, o_ref):
        acc = x_ref[0].astype(F32)
        for i in range(1, n):
            acc = acc + x_ref[i].astype(F32)
        o_ref[...] = acc

    return pl.pallas_call(
        body, name=name,
        out_shape=jax.ShapeDtypeStruct((r, c), F32),
        grid=(r // tr,),
        in_specs=[pl.BlockSpec((n, tr, c), lambda i: (0, i, 0))],
        out_specs=pl.BlockSpec((tr, c), lambda i: (i, 0)),
        compiler_params=pltpu.CompilerParams(dimension_semantics=("parallel",), vmem_limit_bytes=VMEM_LIMIT),
    )(x)


def adamw(w, g, m, v, name):
    r, c = w.shape
    tr = _tile(r, (256, 128, 64, 32, 16, 8))

    def body(w_ref, g_ref, m_ref, v_ref, d_ref, mo_ref, vo_ref):
        gv = g_ref[...]
        mv = ADAM_B1 * m_ref[...] + (1.0 - ADAM_B1) * gv
        vv = ADAM_B2 * v_ref[...] + (1.0 - ADAM_B2) * jnp.square(gv)
        m_hat = mv / (1.0 - ADAM_B1 ** ADAM_STEP)
        v_hat = vv / (1.0 - ADAM_B2 ** ADAM_STEP)
        d_ref[...] = -ADAM_LR * (m_hat / (jnp.sqrt(v_hat) + ADAM_EPS) + ADAM_WD * w_ref[...])
        mo_ref[...] = mv
        vo_ref[...] = vv

    spec = pl.BlockSpec((tr, c), lambda i: (i, 0))
    return pl.pallas_call(
        body, name=name,
        out_shape=(jax.ShapeDtypeStruct((r, c), F32),) * 3,
        grid=(r // tr,),
        in_specs=[spec] * 4,
        out_specs=(spec,) * 3,
        compiler_params=pltpu.CompilerParams(dimension_semantics=("parallel",), vmem_limit_bytes=VMEM_LIMIT),
    )(w, g, m, v)


ROW_TILE = 256


def _row_tile(t):
    return _tile(t, (ROW_TILE, 128, 64, 32, 16, 8))


def _group_call(body, name, ins, in_kinds, out_shapes, out_kinds, tt, out_dtypes=None):
    g, t = ins[0].shape[:2]

    def spec(kind, shape):
        if kind == 'tok':
            return pl.BlockSpec((1, tt, shape[-1]), lambda i, j: (i, j, 0))
        return pl.BlockSpec((1, 1, shape[-1]), lambda i, j: (i, 0, 0))

    return pl.pallas_call(
        body, name=name,
        out_shape=tuple(jax.ShapeDtypeStruct(s, d) for s, d in zip(out_shapes, out_dtypes or [F32] * len(out_shapes))),
        grid=(g, t // tt),
        in_specs=[spec(k, a.shape) for k, a in zip(in_kinds, ins)],
        out_specs=tuple(spec(k, s) for k, s in zip(out_kinds, out_shapes)),
        compiler_params=pltpu.CompilerParams(dimension_semantics=("parallel", "arbitrary"),
                                             vmem_limit_bytes=VMEM_LIMIT),
    )(*ins)


def _accumulate(ref, val):
    @pl.when(pl.program_id(1) == 0)
    def _():
        ref[...] = jnp.zeros_like(ref)

    ref[0] += jnp.sum(val, axis=0, keepdims=True)


def _modulate_fwd(h, shift, scale, out_dtype):
    def body(h_ref, sh_ref, sc_ref, o_ref):
        hv = h_ref[0]
        r = lax.rsqrt(jnp.mean(hv * hv, axis=-1, keepdims=True) + EPS)
        o_ref[0] = (hv * r * (1.0 + sc_ref[0]) + sh_ref[0]).astype(out_dtype)

    return _group_call(body, "modulate_fwd", [h, shift, scale], ['tok', 'vec', 'vec'], [h.shape], ['tok'],
                       _row_tile(h.shape[1]), [out_dtype])[0]


def _modulate_bwd(h, scale, du):
    def body(h_ref, sc_ref, du_ref, dh_ref, dsh_ref, dsc_ref):
        hv, dv = h_ref[0], du_ref[0]
        r = lax.rsqrt(jnp.mean(hv * hv, axis=-1, keepdims=True) + EPS)
        hn = hv * r
        dn = dv * (1.0 + sc_ref[0])
        dh_ref[0] = r * (dn - hn * jnp.mean(dn * hn, axis=-1, keepdims=True))
        _accumulate(dsh_ref, dv)
        _accumulate(dsc_ref, dv * hn)

    return _group_call(body, "modulate_bwd", [h, scale, du], ['tok', 'vec', 'tok'],
                       [h.shape, scale.shape, scale.shape], ['tok', 'acc', 'acc'], _row_tile(h.shape[1]))


def _rows(t):
    return t.reshape(-1, t.shape[-1])


@functools.partial(jax.custom_vjp, nondiff_argnums=(4,))
def modmm(h, shift, scale, w, out_dtype):
    return _modmm_fwd(h, shift, scale, w, out_dtype)[0]


def _modmm_fwd(h, shift, scale, w, out_dtype):
    u = _modulate_fwd(h, shift, scale, BF16)
    y = matmul(_rows(u), w, out_dtype=out_dtype, name="mm_fwd").reshape(h.shape[:2] + (-1,))
    return y, (h, scale, u, w)


def _modmm_bwd(out_dtype, res, dy):
    h, scale, u, w = res
    du = matmul(_rows(dy), w, tb=True, name="mm_dx").reshape(h.shape)
    dw = matmul(_rows(u), _rows(dy), ta=True, out_dtype=w.dtype, name="mm_dw")
    dh, dsh, dsc = _modulate_bwd(h, scale, du)
    return dh, dsh, dsc, dw


modmm.defvjp(_modmm_fwd, _modmm_bwd)


def _gated_add_call(h, y, gate, coef):
    def body(h_ref, y_ref, g_ref, o_ref):
        o_ref[0] = h_ref[0] + coef * g_ref[0] * y_ref[0]

    return _group_call(body, "gated_add_fwd", [h, y, gate], ['tok', 'tok', 'vec'], [h.shape], ['tok'],
                       _row_tile(h.shape[1]))[0]


def _gated_add_bwd_call(y, gate, dout, coef):
    def body(y_ref, g_ref, d_ref, dy_ref, dg_ref):
        dv = d_ref[0]
        dy_ref[0] = coef * g_ref[0] * dv
        _accumulate(dg_ref, coef * dv * y_ref[0])

    return _group_call(body, "gated_add_bwd", [y, gate, dout], ['tok', 'vec', 'tok'], [y.shape, gate.shape],
                       ['tok', 'acc'], _row_tile(y.shape[1]))


@functools.partial(jax.custom_vjp, nondiff_argnums=(3,))
def gated_add(h, y, gate, coef):
    return _gated_add_call(h, y, gate, coef)


def _gated_add_vjp_fwd(h, y, gate, coef):
    return _gated_add_call(h, y, gate, coef), (y, gate)


def _gated_add_vjp_bwd(coef, res, dout):
    y, gate = res
    dy, dg = _gated_add_bwd_call(y, gate, dout, coef)
    return dout, dy, dg


gated_add.defvjp(_gated_add_vjp_fwd, _gated_add_vjp_bwd)


def _swiglu_fwd(hid):
    f = hid.shape[-1] // 2

    def body(h_ref, o_ref):
        gate, up = h_ref[0, :, 0:f].astype(F32), h_ref[0, :, f:2 * f].astype(F32)
        o_ref[0] = (gate * jax.nn.sigmoid(gate) * up).astype(BF16)

    return _group_call(body, "swiglu_fwd", [hid], ['tok'], [hid.shape[:2] + (f,)], ['tok'],
                       _tile(hid.shape[1], (128, 64, 32, 16)), [BF16])[0]


def _swiglu_bwd(hid, da):
    f = hid.shape[-1] // 2

    def body(h_ref, da_ref, d_ref):
        gate, up, dv = h_ref[0, :, 0:f].astype(F32), h_ref[0, :, f:2 * f].astype(F32), da_ref[0]
        s = jax.nn.sigmoid(gate)
        d_ref[0, :, 0:f] = (dv * up * (s * (1.0 + gate * (1.0 - s)))).astype(hid.dtype)
        d_ref[0, :, f:2 * f] = (dv * (gate * s)).astype(hid.dtype)

    return _group_call(body, "swiglu_bwd", [hid, da], ['tok', 'tok'], [hid.shape], ['tok'],
                       _tile(hid.shape[1], (128, 64, 32, 16)), [hid.dtype])[0]


@jax.custom_vjp
def swiglu_mm(hid, w):
    return _swiglu_mm_fwd(hid, w)[0]


def _swiglu_mm_fwd(hid, w):
    act = _swiglu_fwd(hid)
    y = matmul(_rows(act), w, name="mm_fwd").reshape(hid.shape[:2] + (-1,))
    return y, (hid, act, w)


def _swiglu_mm_bwd(res, dy):
    hid, act, w = res
    da = matmul(_rows(dy), w, tb=True, name="mm_dx").reshape(act.shape)
    dw = matmul(_rows(act), _rows(dy), ta=True, out_dtype=w.dtype, name="mm_dw")
    return _swiglu_bwd(hid, da), dw


swiglu_mm.defvjp(_swiglu_mm_fwd, _swiglu_mm_bwd)


@jax.custom_vjp
def flip_rows(x):
    return _flip_rows_call(x)


def _flip_rows_call(x):
    n, length, c = x.shape
    tb = _tile(length, (256, 128, 64, 32, 16, 8))
    nb = length // tb

    def body(x_ref, o_ref):
        xv = x_ref[0]
        ii = lax.broadcasted_iota(jnp.int32, (tb, tb), 0)
        jj = lax.broadcasted_iota(jnp.int32, (tb, tb), 1)
        rev = (ii + jj == tb - 1).astype(BF16)
        hi = xv.astype(BF16)
        r1 = xv - hi.astype(F32)
        mid = r1.astype(BF16)
        lo = (r1 - mid.astype(F32)).astype(BF16)
        dot = functools.partial(jnp.dot, preferred_element_type=F32)
        o_ref[0] = (dot(rev, hi) + dot(rev, mid)) + dot(rev, lo)

    return pl.pallas_call(
        body, name="flip_rows",
        out_shape=jax.ShapeDtypeStruct(x.shape, F32),
        grid=(n, nb),
        in_specs=[pl.BlockSpec((1, tb, c), lambda i, j: (i, j, 0))],
        out_specs=pl.BlockSpec((1, tb, c), lambda i, j: (i, nb - 1 - j, 0)),
        compiler_params=pltpu.CompilerParams(dimension_semantics=("parallel", "parallel"),
                                             vmem_limit_bytes=VMEM_LIMIT),
    )(x)


flip_rows.defvjp(lambda x: (_flip_rows_call(x), None), lambda _, dy: (_flip_rows_call(dy),))


def _flip_time(t, axis):
    s = t.shape
    lead = math.prod(s[:axis])
    return flip_rows(t.reshape(lead, s[axis], -1)).reshape(s)


def rms_norm(x):
    return x * lax.rsqrt(jnp.mean(x * x, axis=-1, keepdims=True) + EPS)


def raster_to_column(t, rows):
    b, s, d = t.shape
    return t.reshape(b, rows, GRID_W, d).transpose(0, 2, 1, 3).reshape(b, s, d)


def column_to_raster(t, rows):
    b, s, d = t.shape
    return t.reshape(b, GRID_W, rows, d).transpose(0, 2, 1, 3).reshape(b, s, d)


def depthwise_conv(x, w, b):
    pad = A_CONV // 2
    y = lax.conv_general_dilated(x, w[:, None, :], window_strides=(1,), padding=[(pad, pad)],
                                 dimension_numbers=('NWC', 'WIO', 'NWC'), feature_group_count=x.shape[-1])
    return y + b


S5_STATES = B_NGROUPS * B_STATE
S5_ROWS = 8
S5_STEPS_FWD = 64
S5_STEPS_BWD = 32


def _s5_scan_fwd(u2, bd2, cd2, ar8, ai8):
    rows, width = u2.shape
    ns = S5_STATES
    tr = S5_ROWS * S5_STEPS_FWD
    assert rows % tr == 0

    def body(u_ref, bd_ref, cd_ref, ar_ref, ai_ref, y_ref, x_ref, st_ref):
        @pl.when(pl.program_id(0) == 0)
        def _():
            st_ref[...] = jnp.zeros_like(st_ref)

        x_ref[...] = jnp.dot(u_ref[...].astype(BF16), bd_ref[...], preferred_element_type=F32)
        ar, ai = ar_ref[...], ai_ref[...]

        def step(t, carry):
            xr, xi = carry
            r = pl.ds(pl.multiple_of(t * S5_ROWS, S5_ROWS), S5_ROWS)
            nr = ar * xr - ai * xi + x_ref[r, 0:ns]
            ni = ar * xi + ai * xr + x_ref[r, ns:2 * ns]
            x_ref[r, 0:ns] = nr
            x_ref[r, ns:2 * ns] = ni
            return nr, ni

        xr, xi = lax.fori_loop(0, S5_STEPS_FWD, step, (st_ref[:, 0:ns], st_ref[:, ns:2 * ns]), unroll=4)
        st_ref[:, 0:ns] = xr
        st_ref[:, ns:2 * ns] = xi
        y_ref[...] = jnp.dot(x_ref[...].astype(BF16), cd_ref[...], preferred_element_type=F32)

    whole = lambda shape: pl.BlockSpec(shape, lambda i: (0, 0))
    return pl.pallas_call(
        body, name="s5_scan_fwd",
        out_shape=(jax.ShapeDtypeStruct((rows, width), F32), jax.ShapeDtypeStruct((rows, 2 * ns), F32)),
        grid=(rows // tr,),
        in_specs=[pl.BlockSpec((tr, width), lambda i: (i, 0)), whole(bd2.shape), whole(cd2.shape),
                  whole(ar8.shape), whole(ai8.shape)],
        out_specs=(pl.BlockSpec((tr, width), lambda i: (i, 0)), pl.BlockSpec((tr, 2 * ns), lambda i: (i, 0))),
        scratch_shapes=[pltpu.VMEM((S5_ROWS, 2 * ns), F32)],
        compiler_params=pltpu.CompilerParams(dimension_semantics=("arbitrary",), vmem_limit_bytes=VMEM_LIMIT),
    )(u2, bd2, cd2, ar8, ai8)


def _s5_scan_bwd(dy, x, u2, bd2, cd2, ar8, ai8):
    rows, width = u2.shape
    ns = S5_STATES
    steps = S5_STEPS_BWD
    tr = S5_ROWS * steps
    nblk = rows // tr
    assert rows % tr == 0
    nt = (((1,), (1,)), ((), ()))
    tn = (((0,), (0,)), ((), ()))

    def body(dy_ref, x_ref, xp_ref, u_ref, bd_ref, cd_ref, ar_ref, ai_ref,
             du_ref, dbd_ref, dcd_ref, dar_ref, dai_ref, g_ref, st_ref):
        k = pl.program_id(0)

        @pl.when(k == 0)
        def _():
            st_ref[...] = jnp.zeros_like(st_ref)
            dbd_ref[...] = jnp.zeros_like(dbd_ref)
            dcd_ref[...] = jnp.zeros_like(dcd_ref)
            dar_ref[...] = jnp.zeros_like(dar_ref)
            dai_ref[...] = jnp.zeros_like(dai_ref)

        dyb = dy_ref[...].astype(BF16)
        g_ref[...] = lax.dot_general(dyb, cd_ref[...], nt, preferred_element_type=F32)
        ar, ai = ar_ref[...], ai_ref[...]

        def adjoint(r, carry, xpr, xpi):
            gr_n, gi_n, dar, dai = carry
            gr = g_ref[r, 0:ns] + ar * gr_n + ai * gi_n
            gi = g_ref[r, ns:2 * ns] - ai * gr_n + ar * gi_n
            g_ref[r, 0:ns] = gr
            g_ref[r, ns:2 * ns] = gi
            return gr, gi, dar + gr * xpr + gi * xpi, dai + gi * xpr - gr * xpi

        def step(i, carry):
            t = steps - 1 - i
            r = pl.ds(pl.multiple_of(t * S5_ROWS, S5_ROWS), S5_ROWS)
            rp = pl.ds(pl.multiple_of((t - 1) * S5_ROWS, S5_ROWS), S5_ROWS)
            return adjoint(r, carry, x_ref[rp, 0:ns], x_ref[rp, ns:2 * ns])

        zero = jnp.zeros((S5_ROWS, ns), F32)
        carry = lax.fori_loop(0, steps - 1, step, (st_ref[:, 0:ns], st_ref[:, ns:2 * ns], zero, zero), unroll=2)
        has_prev = (k < nblk - 1).astype(F32)
        gr, gi, dar, dai = adjoint(pl.ds(0, S5_ROWS), carry, xp_ref[:, 0:ns] * has_prev, xp_ref[:, ns:2 * ns] * has_prev)
        st_ref[:, 0:ns] = gr
        st_ref[:, ns:2 * ns] = gi
        dar_ref[...] += dar
        dai_ref[...] += dai
        gb = g_ref[...].astype(BF16)
        du_ref[...] = lax.dot_general(gb, bd_ref[...], nt, preferred_element_type=F32)
        dbd_ref[...] += lax.dot_general(u_ref[...].astype(BF16), gb, tn, preferred_element_type=F32)
        dcd_ref[...] += lax.dot_general(x_ref[...].astype(BF16), dyb, tn, preferred_element_type=F32)

    whole = lambda shape: pl.BlockSpec(shape, lambda k: (0, 0))
    rev = lambda k: (nblk - 1 - k, 0)
    prev = lambda k: (jnp.maximum((nblk - 1 - k) * steps - 1, 0), 0)
    return pl.pallas_call(
        body, name="s5_scan_bwd",
        out_shape=(jax.ShapeDtypeStruct((rows, width), F32), jax.ShapeDtypeStruct(bd2.shape, F32),
                   jax.ShapeDtypeStruct(cd2.shape, F32), jax.ShapeDtypeStruct(ar8.shape, F32),
                   jax.ShapeDtypeStruct(ai8.shape, F32)),
        grid=(nblk,),
        in_specs=[pl.BlockSpec((tr, width), rev), pl.BlockSpec((tr, 2 * ns), rev),
                  pl.BlockSpec((S5_ROWS, 2 * ns), prev), pl.BlockSpec((tr, width), rev),
                  whole(bd2.shape), whole(cd2.shape), whole(ar8.shape), whole(ai8.shape)],
        out_specs=(pl.BlockSpec((tr, width), rev), whole(bd2.shape), whole(cd2.shape), whole(ar8.shape),
                   whole(ai8.shape)),
        scratch_shapes=[pltpu.VMEM((tr, 2 * ns), F32), pltpu.VMEM((S5_ROWS, 2 * ns), F32)],
        compiler_params=pltpu.CompilerParams(dimension_semantics=("arbitrary",), vmem_limit_bytes=VMEM_LIMIT),
    )(dy, x, x, u2, bd2, cd2, ar8, ai8)


@jax.custom_vjp
def s5_core(u2, bd2, cd2, ar8, ai8):
    return _s5_scan_fwd(u2, bd2.astype(BF16), cd2.astype(BF16), ar8, ai8)[0]


def _s5_core_fwd(u2, bd2, cd2, ar8, ai8):
    bd2, cd2 = bd2.astype(BF16), cd2.astype(BF16)
    y, x = _s5_scan_fwd(u2, bd2, cd2, ar8, ai8)
    return y, (x, u2, bd2, cd2, ar8, ai8)


def _s5_core_bwd(res, dy):
    return _s5_scan_bwd(dy, *res)


s5_core.defvjp(_s5_core_fwd, _s5_core_bwd)


def s5_mixers(p_ctx, p_lat, lam_re, lam_im, log_step, b_re, b_im, c_re, c_im, d_skip, glu_w, glu_b):
    bsz = p_ctx.shape[0]
    assert 2 * bsz == S5_ROWS
    eye = jnp.eye(B_NGROUPS, dtype=F32)
    bds, cds, ars, ais = [], [], [], []
    for d in range(2):
        step = jnp.exp(log_step[d])[:, None]
        mag = jnp.exp(lam_re[d] * step)
        ar = mag * jnp.cos(lam_im[d] * step)
        ai = mag * jnp.sin(lam_im[d] * step)
        den = lam_re[d] * lam_re[d] + lam_im[d] * lam_im[d]
        nr = ar - 1.0
        kr = (nr * lam_re[d] + ai * lam_im[d]) / den
        ki = (ai * lam_re[d] - nr * lam_im[d]) / den
        br = kr[..., None] * b_re[d] - ki[..., None] * b_im[d]
        bi = kr[..., None] * b_im[d] + ki[..., None] * b_re[d]
        blk = lambda w: jnp.einsum('gnc,gh->gchn', w, eye).reshape(B_WIDTH, S5_STATES)
        bds.append(jnp.concatenate([blk(br), blk(bi)], axis=1))
        blk_c = lambda w: jnp.einsum('gcn,gh->gnhc', w, eye).reshape(S5_STATES, B_WIDTH)
        cds.append(jnp.concatenate([blk_c(c_re[d]), -blk_c(c_im[d])], axis=0))
        ars.append(jnp.broadcast_to(ar.reshape(1, S5_STATES), (bsz, S5_STATES)))
        ais.append(jnp.broadcast_to(ai.reshape(1, S5_STATES), (bsz, S5_STATES)))
    bd2 = jnp.concatenate(bds, axis=0)
    cd2 = jnp.concatenate(cds, axis=1)
    ar8 = jnp.concatenate(ars, axis=0)
    ai8 = jnp.concatenate(ais, axis=0)

    def rows_of(p):
        ut = jnp.swapaxes(p, 0, 1)
        z = jnp.zeros_like(ut)
        return jnp.concatenate([jnp.concatenate([ut, z], axis=-1), jnp.concatenate([z, _flip_time(ut, 0)], axis=-1)], axis=1)

    lc = p_ctx.shape[1]
    u2 = jnp.concatenate([rows_of(p_ctx), rows_of(p_lat)], axis=0)
    y2 = s5_core(u2.reshape(-1, 2 * B_WIDTH), bd2, cd2, ar8, ai8).reshape(u2.shape)

    def finish(y2p, p):
        y = y2p[:, :bsz, :B_WIDTH] + _flip_time(y2p[:, bsz:, B_WIDTH:], 0)
        y = jnp.swapaxes(y, 0, 1) + d_skip * p
        y = jax.nn.gelu(y)
        gate = mm(y.reshape(-1, B_WIDTH), glu_w).reshape(y.shape)
        return y * jax.nn.sigmoid(gate + glu_b)

    return finish(y2[:lc], p_ctx), finish(y2[lc:], p_lat)


GLA_CHUNK = 64
GLA_SUB = 16
NT_DIMS = (((1,), (1,)), ((), ()))
TN_DIMS = (((0,), (0,)), ((), ()))


def _bdot(a, b, dims=(((1,), (0,)), ((), ()))):
    return lax.dot_general(a.astype(BF16), b.astype(BF16), dims, preferred_element_type=F32)


def _hdot(a, b, dims=(((1,), (0,)), ((), ()))):
    ah, bh = a.astype(BF16), b.astype(BF16)
    al, bl = (a - ah.astype(F32)).astype(BF16), (b - bh.astype(F32)).astype(BF16)
    dot = functools.partial(lax.dot_general, dimension_numbers=dims, preferred_element_type=F32)
    return dot(ah, bh) + (dot(ah, bl) + dot(al, bh))


def _sub_block_ref(cum, rows, lo, hi, rev):
    n = cum.shape[0]
    if rev:
        return (cum[hi:hi + 1, :] if hi < n else jnp.zeros_like(cum[0:1, :])), rows >= lo
    return (cum[lo - 1:lo, :] if lo else jnp.zeros_like(cum[0:1, :])), rows < hi


def _chunk_of(step, nc, nc_ctx, rev):
    if not rev:
        return step
    return jnp.where(step < nc_ctx, nc_ctx - 1 - step, nc + nc_ctx - 1 - step)


def _gla_scores(q, k, cum, cumr, tri, rev):
    n = GLA_CHUNK
    if cumr is not None:
        decay = jnp.where(tri, jnp.exp(jnp.where(tri, cum - cumr, 0.0)), 0.0)
        return _bdot(q, k, NT_DIMS) * decay, decay
    rows = lax.broadcasted_iota(jnp.int32, (n, 1), 0)
    parts = []
    for i in range(n // GLA_SUB):
        lo, hi = i * GLA_SUB, (i + 1) * GLA_SUB
        ref, seen = _sub_block_ref(cum, rows, lo, hi, rev)
        qt = q[lo:hi] * jnp.exp(cum[lo:hi] - ref)
        kh = jnp.where(seen, k * jnp.exp(jnp.where(seen, ref - cum, 0.0)), 0.0)
        parts.append(_bdot(qt, kh, NT_DIMS))
    return jnp.where(tri, jnp.concatenate(parts, axis=0), 0.0), None


def _gla_fwd(q, k, cum, cumr, v, rev, nc_ctx):
    bsz, length, width = q.shape
    dk = GLA_CHUNK
    nh = width // dk
    nc = length // GLA_CHUNK
    scalar = cumr is not None

    def body(*refs):
        if scalar:
            q_ref, k_ref, cum_ref, cumr_ref, v_ref, o_ref, s_ref, st_ref = refs
        else:
            q_ref, k_ref, cum_ref, v_ref, o_ref, s_ref, st_ref = refs

        @pl.when(pl.program_id(1) == 0)
        def _():
            st_ref[...] = jnp.zeros_like(st_ref)

        ii = lax.broadcasted_iota(jnp.int32, (GLA_CHUNK, GLA_CHUNK), 0)
        jj = lax.broadcasted_iota(jnp.int32, (GLA_CHUNK, GLA_CHUNK), 1)
        tri = jj >= ii if rev else jj <= ii
        edge = 0 if rev else GLA_CHUNK - 1
        qa, ka, ca, va = q_ref[0], k_ref[0], cum_ref[0], v_ref[0]
        cra = cumr_ref[0] if scalar else None
        outs = []
        for h in range(nh):
            sl = slice(h * dk, (h + 1) * dk)
            qv, kv, cv, vv, st = qa[:, sl], ka[:, sl], ca[:, sl], va[:, sl], st_ref[h]
            s_ref[0, 0, h] = st
            a, _ = _gla_scores(qv, kv, cv, cra[:, sl] if scalar else None, tri, rev)
            outs.append(_bdot(qv * jnp.exp(cv), st, NT_DIMS) + _bdot(a, vv))
            last = cv[edge:edge + 1, :]
            st_ref[h] = st * jnp.exp(last) + _bdot(vv, kv * jnp.exp(last - cv), TN_DIMS)
        o_ref[0] = jnp.concatenate(outs, axis=1)

    seq = pl.BlockSpec((1, GLA_CHUNK, width), lambda n, c: (n, _chunk_of(c, nc, nc_ctx, rev), 0))
    state = pl.BlockSpec((1, 1, nh, dk, dk), lambda n, c: (n, _chunk_of(c, nc, nc_ctx, rev), 0, 0, 0))
    ins = [q, k, cum] + ([cumr] if scalar else []) + [v]
    return pl.pallas_call(
        body, name="gla_fwd_scalar" if scalar else "gla_fwd",
        out_shape=(jax.ShapeDtypeStruct((bsz, length, width), F32), jax.ShapeDtypeStruct((bsz, nc, nh, dk, dk), F32)),
        grid=(bsz, nc),
        in_specs=[seq] * len(ins),
        out_specs=(seq, state),
        scratch_shapes=[pltpu.VMEM((nh, dk, dk), F32)],
        compiler_params=pltpu.CompilerParams(dimension_semantics=("parallel", "arbitrary"),
                                             vmem_limit_bytes=VMEM_LIMIT),
    )(*ins)


def _gla_bwd(do, q, k, cum, cumr, v, states, rev, nc_ctx):
    bsz, length, width = q.shape
    nc = length // GLA_CHUNK
    scalar = cumr is not None
    n = GLA_CHUNK
    dk = GLA_CHUNK
    nh = width // dk

    def body(*refs):
        if scalar:
            do_ref, q_ref, k_ref, cum_ref, cumr_ref, v_ref, s_ref, dq_ref, dk_ref, dc_ref, dcr_ref, dv_ref, dst_ref = refs
        else:
            do_ref, q_ref, k_ref, cum_ref, v_ref, s_ref, dq_ref, dk_ref, dc_ref, dv_ref, dst_ref = refs

        @pl.when(pl.program_id(1) == 0)
        def _():
            dst_ref[...] = jnp.zeros_like(dst_ref)

        ii = lax.broadcasted_iota(jnp.int32, (n, n), 0)
        jj = lax.broadcasted_iota(jnp.int32, (n, n), 1)
        tri = jj >= ii if rev else jj <= ii
        edge = 0 if rev else n - 1
        rows = lax.broadcasted_iota(jnp.int32, (n, 1), 0)
        doa, qa, ka, ca, va = do_ref[0], q_ref[0], k_ref[0], cum_ref[0], v_ref[0]
        cra = cumr_ref[0] if scalar else None
        dqs, dks, dcs, dcrs, dvs = [], [], [], [], []
        for h in range(nh):
            sl = slice(h * dk, (h + 1) * dk)
            dov, qv, kv, cv, vv, st, dst = doa[:, sl], qa[:, sl], ka[:, sl], ca[:, sl], va[:, sl], s_ref[0, 0, h], dst_ref[h]
            e = jnp.exp(cv)
            qe = qv * e
            last = cv[edge:edge + 1, :]
            w = jnp.exp(last - cv)
            kw = kv * w
            el = jnp.exp(last)
            hd = _bdot if scalar else _hdot
            d_qe = hd(dov, st)
            d_kw = hd(vv, dst)
            dv = _bdot(kw, dst, NT_DIMS)
            d_last = jnp.sum(st * dst, axis=0, keepdims=True) * el + jnp.sum(d_kw * kw, axis=0, keepdims=True)
            dst_ref[h] = dst * el + _bdot(dov, qe, TN_DIMS)
            dq = d_qe * e
            dkk = d_kw * w
            dc = d_qe * qe - d_kw * kw + jnp.where(rows == edge, d_last, 0.0)
            da = jnp.where(tri, hd(dov, vv, NT_DIMS), 0.0)
            if scalar:
                a, decay = _gla_scores(qv, kv, cv, cra[:, sl], tri, rev)
                dg = da * decay
                dq = dq + _bdot(dg, kv)
                dkk = dkk + _bdot(dg, qv, TN_DIMS)
                p = da * a
                dc = dc + p
                dcrs.append(-p)
            else:
                a_parts, dq_parts = [], []
                for i in range(n // GLA_SUB):
                    lo, hi = i * GLA_SUB, (i + 1) * GLA_SUB
                    ref, seen = _sub_block_ref(cv, rows, lo, hi, rev)
                    eq = jnp.exp(cv[lo:hi] - ref)
                    qt = qv[lo:hi] * eq
                    ek = jnp.where(seen, jnp.exp(jnp.where(seen, ref - cv, 0.0)), 0.0)
                    kh = kv * ek
                    a_parts.append(_bdot(qt, kh, NT_DIMS))
                    dqt = _hdot(da[lo:hi], kh)
                    dkh = _hdot(da[lo:hi], qt, TN_DIMS)
                    dq_parts.append((dqt * eq, dqt * qt))
                    dkk = dkk + dkh * ek
                    dc = dc - dkh * kh
                a = jnp.where(tri, jnp.concatenate(a_parts, axis=0), 0.0)
                dq = dq + jnp.concatenate([p[0] for p in dq_parts], axis=0)
                dc = dc + jnp.concatenate([p[1] for p in dq_parts], axis=0)
            dvs.append(dv + _bdot(a, dov, TN_DIMS))
            dqs.append(dq)
            dks.append(dkk)
            dcs.append(dc)
        cat = functools.partial(jnp.concatenate, axis=1)
        dq_ref[0], dk_ref[0], dc_ref[0], dv_ref[0] = cat(dqs), cat(dks), cat(dcs), cat(dvs)
        if scalar:
            dcr_ref[0] = cat(dcrs)

    seq = pl.BlockSpec((1, n, width), lambda s, c: (s, _chunk_of(nc - 1 - c, nc, nc_ctx, rev), 0))
    state = pl.BlockSpec((1, 1, nh, dk, dk), lambda s, c: (s, _chunk_of(nc - 1 - c, nc, nc_ctx, rev), 0, 0, 0))
    ins = [do, q, k, cum] + ([cumr] if scalar else []) + [v]
    n_out = 5 if scalar else 4
    return pl.pallas_call(
        body, name="gla_bwd_scalar" if scalar else "gla_bwd",
        out_shape=(jax.ShapeDtypeStruct((bsz, length, width), F32),) * n_out,
        grid=(bsz, nc),
        in_specs=[seq] * len(ins) + [state],
        out_specs=(seq,) * n_out,
        scratch_shapes=[pltpu.VMEM((nh, dk, dk), F32)],
        compiler_params=pltpu.CompilerParams(dimension_semantics=("parallel", "arbitrary"),
                                             vmem_limit_bytes=VMEM_LIMIT),
    )(*ins, states)


@functools.partial(jax.custom_vjp, nondiff_argnums=(4, 5))
def gla(q, k, cum, v, rev, nc_ctx):
    return _gla_fwd(q, k, cum, None, v, rev, nc_ctx)[0]


def _gla_vjp_fwd(q, k, cum, v, rev, nc_ctx):
    o, states = _gla_fwd(q, k, cum, None, v, rev, nc_ctx)
    return o, (q, k, cum, v, states)


def _gla_vjp_bwd(rev, nc_ctx, res, do):
    q, k, cum, v, states = res
    return _gla_bwd(do, q, k, cum, None, v, states, rev, nc_ctx)


gla.defvjp(_gla_vjp_fwd, _gla_vjp_bwd)


@functools.partial(jax.custom_vjp, nondiff_argnums=(5, 6))
def gla_scalar(q, k, cum, cumr, v, rev, nc_ctx):
    return _gla_fwd(q, k, cum, cumr, v, rev, nc_ctx)[0]


def _gla_scalar_vjp_fwd(q, k, cum, cumr, v, rev, nc_ctx):
    o, states = _gla_fwd(q, k, cum, cumr, v, rev, nc_ctx)
    return o, (q, k, cum, cumr, v, states)


def _gla_scalar_vjp_bwd(rev, nc_ctx, res, do):
    q, k, cum, cumr, v, states = res
    return _gla_bwd(do, q, k, cum, cumr, v, states, rev, nc_ctx)


gla_scalar.defvjp(_gla_scalar_vjp_fwd, _gla_scalar_vjp_bwd)


def _chunk_cumsum(g, rev, axis=-2):
    axis = axis % g.ndim
    s = g.shape
    by_chunk = g.reshape(s[:axis] + (s[axis] // GLA_CHUNK, GLA_CHUNK) + s[axis + 1:])
    c = jnp.cumsum(by_chunk, axis=axis + 1)
    if rev:
        c = lax.slice_in_dim(c, GLA_CHUNK - 1, GLA_CHUNK, axis=axis + 1) - c + by_chunk
    return c.reshape(s)


def _both_parts(t_ctx, t_lat):
    return jnp.concatenate([t.reshape(t.shape[:2] + (-1,)) for t in (t_ctx, t_lat)], axis=1)


def _split_parts(o, lc, nh):
    return tuple(t.reshape(t.shape[:2] + (nh, -1)) for t in (o[:, :lc], o[:, lc:]))


def hgrn2_mixers(p_ctx, p_lat, lower, norm_w):
    bsz, lc = p_ctx.shape[:2]
    lower = lower.reshape(2, C_HEADS, C_KEY)

    def heads(p, lo, hi):
        return p[..., lo:hi].reshape(p.shape[:2] + (C_HEADS, -1))

    q_c, q_l = (jax.nn.silu(heads(p, 0, C_WIDTH)) for p in (p_ctx, p_lat))
    v_c, v_l = (heads(p, 3 * C_WIDTH, 4 * C_WIDTH) for p in (p_ctx, p_lat))
    q, v = _both_parts(q_c, q_l), _both_parts(v_c, v_l)
    nc_ctx = lc // GLA_CHUNK
    o = []
    for d in range(2):
        f_c, f_l = (lower[d] + (1.0 - lower[d]) * jax.nn.sigmoid(heads(p, (1 + d) * C_WIDTH, (2 + d) * C_WIDTH))
                    for p in (p_ctx, p_lat))
        cum = jnp.concatenate([_chunk_cumsum(jnp.log(f).reshape(f.shape[:2] + (-1,)), d, axis=1) for f in (f_c, f_l)], axis=1)
        o.append(gla(q, _both_parts(1.0 - f_c, 1.0 - f_l), cum, v, bool(d), nc_ctx))
    f_c, f_l = _split_parts(o[0], lc, C_HEADS)
    b_c, b_l = _split_parts(o[1], lc, C_HEADS)
    outs = []
    for o_sum, p in ((f_c + b_c, p_ctx), (f_l + b_l, p_lat)):
        o_n = rms_norm(o_sum) * norm_w.reshape(C_HEADS, C_VAL)
        outs.append(o_n.reshape(p.shape[:2] + (C_WIDTH,)) * jax.nn.silu(p[..., 4 * C_WIDTH:]))
    return tuple(outs)


def ssd_mixers(p_ctx, p_lat, conv_w, conv_b, dt_bias, a_log, d_skip, norm_w):
    bsz, lc = p_ctx.shape[:2]
    rep = A_HEADS // A_GROUPS
    a = -jnp.exp(a_log)
    xs, bs, cs, dts, zs = [], [], [], [], []
    for p in (p_ctx, p_lat):
        z, xbc, dt_raw = jnp.split(p, [A_INNER, A_INNER + A_CONV_DIM], axis=-1)
        xbc = jax.nn.silu(depthwise_conv(xbc, conv_w, conv_b))
        x_, b_, c_ = jnp.split(xbc, [A_INNER, A_INNER + A_GROUPS * A_STATE], axis=-1)
        shp = p.shape[:2]
        xs.append(x_.reshape(shp + (A_HEADS, A_HEAD_DIM)))
        bs.append(jnp.repeat(b_.reshape(shp + (A_GROUPS, A_STATE)), rep, axis=2))
        cs.append(jnp.repeat(c_.reshape(shp + (A_GROUPS, A_STATE)), rep, axis=2))
        dts.append(jax.nn.softplus(dt_raw.reshape(shp + (2, A_HEADS)) + dt_bias))
        zs.append(z)
    q, v = _both_parts(cs[0], cs[1]), _both_parts(xs[0], xs[1])
    nc_ctx = lc // GLA_CHUNK
    o = []
    for d in range(2):
        k = _both_parts(bs[0] * dts[0][:, :, d, :, None], bs[1] * dts[1][:, :, d, :, None])
        adt = jnp.concatenate([_chunk_cumsum(dt[:, :, d, :] * a[d], d, axis=1) for dt in dts], axis=1)
        nb, lt = adt.shape[:2]
        cum = jnp.broadcast_to(adt[..., None], (nb, lt, A_HEADS, A_STATE)).reshape(nb, lt, -1)
        along = jnp.swapaxes(adt.reshape(nb, lt // GLA_CHUNK, GLA_CHUNK, A_HEADS), 2, 3)[:, :, None]
        cumr = jnp.broadcast_to(along, (nb, lt // GLA_CHUNK, GLA_CHUNK, A_HEADS, GLA_CHUNK)).reshape(nb, lt, -1)
        o.append(gla_scalar(q, k, cum, cumr, v, bool(d), nc_ctx))
    f_c, f_l = _split_parts(o[0], lc, A_HEADS)
    b_c, b_l = _split_parts(o[1], lc, A_HEADS)
    outs = []
    for y, x_, z in ((f_c + b_c, xs[0], zs[0]), (f_l + b_l, xs[1], zs[1])):
        y = y + d_skip[:, None] * x_
        y = y.reshape(z.shape) * jax.nn.silu(z)
        outs.append(rms_norm(y) * norm_w)
    return tuple(outs)


def token_mixers(p_ctx, p_lat, W, l, lower):
    def cut(p):
        return p[..., :A_COLS], p[..., 1408:1408 + B_COLS], p[..., 1664:1664 + C_COLS]

    pa_c, pb_c, pc_c = cut(p_ctx)
    pa_l, pb_l, pc_l = cut(p_lat)
    ya_c, ya_l = ssd_mixers(pa_c, pa_l, W['a_conv_w'][l], W['a_conv_b'][l], W['a_dt_bias'][l], W['a_log'][l],
                            W['a_d'][l], W['a_norm_w'][l])
    yb_c, yb_l = s5_mixers(pb_c, pb_l, W['s5_lam_re'][l], W['s5_lam_im'][l], W['s5_log_step'][l], W['s5_b_re'][l],
                           W['s5_b_im'][l], W['s5_c_re'][l], W['s5_c_im'][l], W['s5_d'][l], W['s5_glu_w'][l],
                           W['s5_glu_b'][l])
    yc_c, yc_l = hgrn2_mixers(pc_c, pc_l, lower, W['hg_norm_w'][l])
    return (jnp.concatenate([ya_c, yb_c, yc_c], axis=-1), jnp.concatenate([ya_l, yb_l, yc_l], axis=-1))


def _pad_w_in(w):
    z = functools.partial(jnp.zeros, dtype=w.dtype)
    return jnp.concatenate([w[:, :A_COLS], z((D_MODEL, 1408 - A_COLS)), w[:, A_COLS:], z((D_MODEL, IN_PAD - 2944))],
                           axis=1)


def _mm3(t, w):
    g, tt, k = t.shape
    return mm(t.reshape(g * tt, k), w).reshape(g, tt, -1)


def _ffn(h, mg, first, w_in, w_out):
    hid = modmm(h, mg[:, first:first + 1], mg[:, first + 1:first + 2], w_in, BF16)
    return gated_add(h, swiglu_mm(hid, w_out), mg[:, first + 2:first + 3], 0.5)


def local_loss(x, W, m_lat, m_ctx, ctx, target):
    bsz, seq, dm = x.shape
    lc = ctx.shape[1]
    tg = bsz * lc
    assert seq % tg == 0
    gl = seq // tg
    ng = bsz * gl
    rows = seq // GRID_W
    p_lb = jax.nn.softmax(W['hg_lb_logits'], axis=0)
    lower_bounds = jnp.cumsum(p_lb, axis=0) - p_lb[:1]
    h = jnp.concatenate([x.reshape(ng, tg, dm), ctx.reshape(1, tg, dm)], axis=0)
    for l in range(DEPTH):
        last = l == DEPTH - 1
        col_major = l % 2 == 1
        mg = jnp.concatenate([jnp.repeat(m_lat[l], gl, axis=0), m_ctx[l][None]], axis=0)
        h = _ffn(h, mg, 0, W['ffn_w_in'][l][0], W['ffn_w_out'][l][0])
        hp = h
        if col_major:
            h_lat = raster_to_column(h[:ng].reshape(bsz, seq, dm), rows)
            hp = jnp.concatenate([h_lat.reshape(ng, tg, dm), h[ng:]], axis=0)
        p = modmm(hp, mg[:, 3:4], mg[:, 4:5], _pad_w_in(W['w_in'][l]), F32)
        mix_ctx, mix_lat = token_mixers(p[ng].reshape(bsz, lc, -1), p[:ng].reshape(bsz, seq, -1), W, l,
                                        lower_bounds[l])
        if last:
            h, mg = h[:ng], mg[:ng]
            y_lat = _mm3(mix_lat.reshape(ng, tg, dm), W['w_out'][l])
            y_ctx = None
        else:
            y = _mm3(jnp.concatenate([mix_lat.reshape(ng, tg, dm), mix_ctx.reshape(1, tg, dm)], axis=0), W['w_out'][l])
            y_lat, y_ctx = y[:ng], y[ng:]
        if col_major:
            y_lat = column_to_raster(y_lat.reshape(bsz, seq, dm), rows).reshape(ng, tg, dm)
        y = y_lat if y_ctx is None else jnp.concatenate([y_lat, y_ctx], axis=0)
        h = gated_add(h, y, mg[:, 5:6], 1.0)
        h = _ffn(h, mg, 6, W['ffn_w_in'][l][1], W['ffn_w_out'][l][1])
    y = rms_norm(h[:ng].reshape(bsz, seq, dm)) * W['final_norm_w']
    err = jnp.square(y - target)
    return 0.5 * jnp.sum(jnp.mean(err, axis=-1))


def _pad_rows(a, rows):
    return jnp.concatenate([a, jnp.zeros((rows - a.shape[0],) + a.shape[1:], a.dtype)], axis=0)


def kernel(x, c, ctx, c_ctx, mod_w, mod_b, ffn_w_in, ffn_w_out, w_in, w_out, a_conv_w, a_conv_b, a_dt_bias, a_log, a_d, a_norm_w, s5_lam_re, s5_lam_im, s5_log_step, s5_b_re, s5_b_im, s5_c_re, s5_c_im, s5_d, s5_glu_w, s5_glu_b, hg_lb_logits, hg_norm_w, final_norm_w, loss_target, m_c_ctx, m_mod_w, m_mod_b, m_ffn_w_in, m_ffn_w_out, m_w_in, m_w_out, m_a_conv_w, m_a_conv_b, m_a_dt_bias, m_a_log, m_a_d, m_a_norm_w, m_s5_lam_re, m_s5_lam_im, m_s5_log_step, m_s5_b_re, m_s5_b_im, m_s5_c_re, m_s5_c_im, m_s5_d, m_s5_glu_w, m_s5_glu_b, m_hg_lb_logits, m_hg_norm_w, m_final_norm_w, v_c_ctx, v_mod_w, v_mod_b, v_ffn_w_in, v_ffn_w_out, v_w_in, v_w_out, v_a_conv_w, v_a_conv_b, v_a_dt_bias, v_a_log, v_a_d, v_a_norm_w, v_s5_lam_re, v_s5_lam_im, v_s5_log_step, v_s5_b_re, v_s5_b_im, v_s5_c_re, v_s5_c_im, v_s5_d, v_s5_glu_w, v_s5_glu_b, v_hg_lb_logits, v_hg_norm_w, v_final_norm_w):
    given = dict(locals())
    w_loc = {n: given[n] for n in WEIGHTS}
    m_loc = {n: given["m_" + n] for n in WEIGHTS}
    v_loc = {n: given["v_" + n] for n in WEIGHTS}
    bsz = x.shape[0]
    me = 4 * lax.axis_index("x") + 2 * lax.axis_index("y") + lax.axis_index("c")

    small_sh = [c] + [w_loc[n] for n in SMALL_SHARDED]
    g1 = _unpack(all_gather([_pack(small_sh, 128, 8)], "gather_small")[0], [a.shape for a in small_sh])
    c_all = g1[0].reshape(N_DEV * bsz, D_MODEL)
    gathered = dict(zip(BIG, all_gather([w_loc[n].astype(BF16) for n in BIG], "gather_weights")))
    W = {'ffn_w_in': [[_assemble(gathered['ffn_w_in'][:, l, i], 1) for i in range(2)] for l in range(DEPTH)],
         'ffn_w_out': [[_assemble(gathered['ffn_w_out'][:, l, i], 0) for i in range(2)] for l in range(DEPTH)],
         'w_in': [_assemble(gathered['w_in'][:, l], 1) for l in range(DEPTH)],
         'w_out': [_assemble(gathered['w_out'][:, l], 0) for l in range(DEPTH)]}
    for (n, ax), t in zip(SMALL_SHARDED.items(), g1[1:]):
        W[n] = _assemble(t, ax)
    for n in SMALL:
        if n not in SMALL_SHARDED and n not in ('c_ctx', 'mod_b'):
            W[n] = w_loc[n]

    n_rows = N_DEV * bsz + 1
    pad_rows = 8 * ((n_rows + 7) // 8)
    c_rows = _pad_rows(jnp.concatenate([c_all, c_ctx[None]], axis=0), pad_rows)
    sc = jax.nn.silu(c_rows)
    mods_sh = jnp.stack([matmul(sc, mod_w[l], name="mod_fwd") for l in range(DEPTH)])
    mods = _assemble(all_gather([mods_sh], "gather_mods")[0], 2) + mod_b[:, None, :]
    m_lat = lax.dynamic_slice_in_dim(mods, me * bsz, bsz, axis=1).reshape(DEPTH, bsz, N_MOD, D_MODEL)
    m_ctx = mods[:, n_rows - 1].reshape(DEPTH, N_MOD, D_MODEL)

    loss_loc, (grad_x, gW, gm_lat, gm_ctx) = jax.value_and_grad(local_loss, argnums=(0, 1, 2, 3))(
        x, W, m_lat, m_ctx, ctx, loss_target)
    loss = lax.psum(loss_loc, MESH_AXES)

    dm_loc = jnp.concatenate([gm_lat.reshape(DEPTH, bsz, -1), gm_ctx.reshape(DEPTH, 1, -1)], axis=1)
    (dm_all,) = all_gather([dm_loc], "gather_dmods")
    dm_ex = jnp.moveaxis(dm_all[:, :, :bsz], 0, 1).reshape(DEPTH, N_DEV * bsz, -1)
    ncol = N_MOD * D_MODEL
    dm_cx = sum_leading(dm_all[:, :, bsz].reshape(N_DEV, DEPTH * ncol // 128, 128), "sum_dmods_ctx")
    dm_cx = dm_cx.reshape(DEPTH, 1, ncol)
    dm_rows = jnp.concatenate([dm_ex, dm_cx, jnp.zeros((DEPTH, pad_rows - n_rows, ncol), F32)], axis=1)
    grad_mod_b = sum_leading(jnp.moveaxis(dm_rows, 1, 0).reshape(pad_rows, DEPTH * ncol // 128, 128),
                             "sum_mod_b").reshape(DEPTH, ncol)
    my_cols = ncol // N_DEV
    dm_mine = lax.dynamic_slice_in_dim(dm_rows, me * my_cols, my_cols, axis=2)
    grad_mod_w = jnp.stack([matmul(sc, dm_mine[l], ta=True, name="mod_dw") for l in range(DEPTH)])
    dm_cx_mine = lax.dynamic_slice_in_dim(dm_cx, me * my_cols, my_cols, axis=2)
    g_sc_ctx = sum(matmul(_pad_rows(dm_cx_mine[l], 8), mod_w[l], tb=True, name="mod_dc")[0] for l in range(DEPTH))

    small_full = [n for n in SMALL if n not in ('c_ctx', 'mod_b')]
    part = [gW[n] for n in small_full] + [g_sc_ctx]
    red = sum_leading(all_gather([_pack(part, 128, 8)], "gather_small_grads")[0], "sum_small_grads")
    red = _unpack(red, [a.shape for a in part])
    grads = dict(zip(small_full, red[:-1]))
    sig = jax.nn.sigmoid(c_ctx)
    grads['c_ctx'] = red[-1] * (sig * (1.0 + c_ctx * (1.0 - sig)))
    grads['mod_b'] = grad_mod_b
    for n, ax in SMALL_SHARDED.items():
        size = w_loc[n].shape[ax]
        grads[n] = lax.dynamic_slice_in_dim(grads[n], me * size, size, axis=ax)
    grads['mod_w'] = grad_mod_w

    by_dev = {'ffn_w_in': jnp.stack([jnp.stack([_split(g, 1) for g in gl], axis=1) for gl in gW['ffn_w_in']], axis=1),
              'ffn_w_out': jnp.stack([jnp.stack([_split(g, 0) for g in gl], axis=1) for gl in gW['ffn_w_out']], axis=1),
              'w_in': jnp.stack([_split(g, 1) for g in gW['w_in']], axis=1),
              'w_out': jnp.stack([_split(g, 0) for g in gW['w_out']], axis=1)}
    for n, t in zip(BIG, all_to_all([by_dev[n] for n in BIG], "exchange_grads")):
        grads[n] = sum_leading(t.reshape((N_DEV,) + _as_2d(t.shape[1:])), "sum_grads").reshape(t.shape[1:])

    delta, new_m, new_v = {}, {}, {}

    for n in list(BIG) + ['mod_w']:
        outs = adamw(*[d[n].reshape(_as_2d(d[n].shape)) for d in (w_loc, grads, m_loc, v_loc)], name="adamw_" + n)
        delta[n], new_m[n], new_v[n] = (o.reshape(w_loc[n].shape) for o in outs)

    def update(names, width, row_mult, tag):
        packed = [_pack([d[n] for n in names], width, row_mult) for d in (w_loc, grads, m_loc, v_loc)]
        outs = adamw(*packed, name="adamw_" + tag)
        shapes = [w_loc[n].shape for n in names]
        for res, out in zip((delta, new_m, new_v), outs):
            res.update(zip(names, _unpack(out, shapes)))

    update(SMALL, 128, 256, "small")
    return (loss, grad_x, *[grads[n] for n in WEIGHTS], *[delta[n] for n in WEIGHTS],
            *[new_m[n] for n in WEIGHTS], *[new_v[n] for n in WEIGHTS])
```

```python
import functools
import math

import jax
import jax.numpy as jnp
from jax import lax
from jax.experimental import pallas as pl
from jax.experimental.pallas import tpu as pltpu

F32 = jnp.float32
BF16 = jnp.bfloat16
N_DEV = 8
MESH_ID = pl.DeviceIdType.MESH
VMEM_LIMIT = 48 * 1024 * 1024

D_MODEL = 1024
DEPTH = 2
GRID_W = 64
EPS = 1e-6
N_MOD = 9
D_FF = 2816
A_INNER = 512
A_HEADS = 8
A_HEAD_DIM = 64
A_GROUPS = 2
A_STATE = 64
A_CONV = 5
A_CONV_DIM = A_INNER + 2 * A_GROUPS * A_STATE
A_COLS = A_INNER + A_CONV_DIM + 2 * A_HEADS
B_WIDTH = 256
B_GROUP = 16
B_NGROUPS = 16
B_STATE = 64
B_COLS = B_WIDTH
C_WIDTH = 256
C_HEADS = 4
C_KEY = 64
C_VAL = 64
C_COLS = 5 * C_WIDTH
IN_PAD = 3072

ADAM_LR = 0.001
ADAM_B1 = 0.9
ADAM_B2 = 0.999
ADAM_EPS = 1e-08
ADAM_WD = 0.01
ADAM_STEP = 10

WEIGHTS = ['c_ctx', 'mod_w', 'mod_b', 'ffn_w_in', 'ffn_w_out', 'w_in', 'w_out', 'a_conv_w', 'a_conv_b', 'a_dt_bias',
           'a_log', 'a_d', 'a_norm_w', 's5_lam_re', 's5_lam_im', 's5_log_step', 's5_b_re', 's5_b_im', 's5_c_re',
           's5_c_im', 's5_d', 's5_glu_w', 's5_glu_b', 'hg_lb_logits', 'hg_norm_w', 'final_norm_w']
BIG = {'ffn_w_in': 3, 'ffn_w_out': 2, 'w_in': 2, 'w_out': 1}
SMALL_SHARDED = {'a_conv_w': 2, 's5_glu_w': 1, 'hg_lb_logits': 2}
SMALL = [n for n in WEIGHTS if n not in BIG and n != 'mod_w']


def _tile(d, prefs):
    for p in prefs:
        if d % p == 0:
            return p
    return d


def _pack(arrs, width, row_mult):
    flat = jnp.concatenate([a.reshape(-1) for a in arrs])
    pad = (-flat.shape[0]) % (width * row_mult)
    if pad:
        flat = jnp.concatenate([flat, jnp.zeros((pad,), flat.dtype)])
    return flat.reshape(-1, width)


def _as_2d(shape):
    return (math.prod(shape[:-1]), shape[-1])


def _unpack(buf, shapes):
    lead = buf.shape[:-2]
    flat = buf.reshape(lead + (-1,))
    out, off = [], 0
    for s in shapes:
        n = math.prod(s)
        out.append(flat[..., off:off + n].reshape(lead + tuple(s)))
        off += n
    return out


def _assemble(g, axis):
    t = jnp.moveaxis(g, 0, axis)
    s = t.shape
    return t.reshape(s[:axis] + (s[axis] * s[axis + 1],) + s[axis + 2:])


def _split(full, axis):
    s = full.shape
    t = full.reshape(s[:axis] + (N_DEV, s[axis] // N_DEV) + s[axis + 1:])
    return jnp.moveaxis(t, axis, 0)


def all_gather(xs, name):
    nt = len(xs)

    def body(*refs):
        x_refs, out_refs = refs[:nt], refs[nt:2 * nt]
        send_sems, recv_sems, local_sems = refs[2 * nt:]
        ax, ay, ac = lax.axis_index("x"), lax.axis_index("y"), lax.axis_index("c")
        me, sibling = (ax, ay, ac), (ax, ay, 1 - ac)
        chips = [(1 - ax, ay), (ax, 1 - ay), (1 - ax, 1 - ay)]

        def slot(t, px, py, pc):
            return out_refs[t].at[4 * px + 2 * py + pc]

        def copy(t, k, block, to, src=None):
            return pltpu.make_async_remote_copy(
                src_ref=slot(t, *block) if src is None else src, dst_ref=slot(t, *block),
                send_sem=send_sems.at[t, k], recv_sem=recv_sems.at[t, k], device_id=to, device_id_type=MESH_ID)

        mine = [pltpu.make_async_copy(x_refs[t], slot(t, *me), local_sems.at[t]) for t in range(nt)]
        for cp in mine:
            cp.start()
        first = []
        for t in range(nt):
            first.append(copy(t, 0, me, sibling, src=x_refs[t]))
            first += [copy(t, 1 + j, me, (*chip, ac), src=x_refs[t]) for j, chip in enumerate(chips)]
        for cp in first:
            cp.start()
        passed = []
        for j, chip in enumerate(chips):
            for t in range(nt):
                copy(t, 1 + j, (*chip, ac), me).wait_recv()
                passed.append(copy(t, 4 + j, (*chip, ac), sibling))
                passed[-1].start()
        for t in range(nt):
            copy(t, 0, sibling, me).wait_recv()
            for j, chip in enumerate(chips):
                copy(t, 4 + j, (*chip, 1 - ac), me).wait_recv()
        for cp in first + passed:
            cp.wait_send()
        for cp in mine:
            cp.wait()

    return pl.pallas_call(
        body, name=name,
        out_shape=tuple(jax.ShapeDtypeStruct((N_DEV,) + x.shape, x.dtype) for x in xs),
        in_specs=[pl.BlockSpec(memory_space=pl.ANY)] * nt,
        out_specs=tuple(pl.BlockSpec(memory_space=pl.ANY) for _ in xs),
        scratch_shapes=[pltpu.SemaphoreType.DMA((nt, 7)), pltpu.SemaphoreType.DMA((nt, 7)),
                        pltpu.SemaphoreType.DMA((nt,))],
    )(*xs)


def all_to_all(gs, name):
    nt = len(gs)

    def body(*refs):
        g_refs, out_refs = refs[:nt], refs[nt:2 * nt]
        send_sems, recv_sems, local_sems = refs[2 * nt:]
        ax, ay, ac = lax.axis_index("x"), lax.axis_index("y"), lax.axis_index("c")
        my = 4 * ax + 2 * ay + ac
        local = [pltpu.make_async_copy(g_refs[t].at[my], out_refs[t].at[my], local_sems.at[t]) for t in range(nt)]
        for cp in local:
            cp.start()
        peers = []
        for r in range(1, N_DEV):
            px = 1 - ax if r & 4 else ax
            py = 1 - ay if r & 2 else ay
            pc = 1 - ac if r & 1 else ac
            peers.append((px, py, pc))

        def copy(t, k, peer):
            return pltpu.make_async_remote_copy(
                src_ref=g_refs[t].at[4 * peer[0] + 2 * peer[1] + peer[2]], dst_ref=out_refs[t].at[my],
                send_sem=send_sems.at[t, k], recv_sem=recv_sems.at[t, k], device_id=peer, device_id_type=MESH_ID)

        def arrival(t, k, peer):
            slot = 4 * peer[0] + 2 * peer[1] + peer[2]
            return pltpu.make_async_remote_copy(
                src_ref=g_refs[t].at[slot], dst_ref=out_refs[t].at[slot],
                send_sem=send_sems.at[t, k], recv_sem=recv_sems.at[t, k], device_id=peer, device_id_type=MESH_ID)

        sends = [copy(t, k, p) for t in range(nt) for k, p in enumerate(peers)]
        for cp in sends:
            cp.start()
        for t in range(nt):
            for k, p in enumerate(peers):
                arrival(t, k, p).wait_recv()
        for cp in sends:
            cp.wait_send()
        for cp in local:
            cp.wait()

    return pl.pallas_call(
        body, name=name,
        out_shape=tuple(jax.ShapeDtypeStruct(g.shape, g.dtype) for g in gs),
        in_specs=[pl.BlockSpec(memory_space=pl.ANY)] * nt,
        out_specs=tuple(pl.BlockSpec(memory_space=pl.ANY) for _ in gs),
        scratch_shapes=[pltpu.SemaphoreType.DMA((nt, 7)), pltpu.SemaphoreType.DMA((nt, 7)),
                        pltpu.SemaphoreType.DMA((nt,))],
    )(*gs)


def matmul(a, b, *, ta=False, tb=False, out_dtype=F32, name="mm"):
    m, k = (a.shape[1], a.shape[0]) if ta else a.shape
    n = b.shape[0] if tb else b.shape[1]
    assert (b.shape[1] if tb else b.shape[0]) == k, (a.shape, b.shape, ta, tb)
    tm = _tile(m, (1024, 512, 256, 128))
    tn = _tile(n, (1408, 1024, 512, 384, 256, 128))
    tk = _tile(k, (1024, 512, 256, 128))
    nk = k // tk
    dims = (((0 if ta else 1,), (1 if tb else 0,)), ((), ()))

    def body(a_ref, b_ref, o_ref, acc_ref):
        step = pl.program_id(2)

        @pl.when(step == 0)
        def _():
            acc_ref[...] = jnp.zeros_like(acc_ref)

        acc_ref[...] += lax.dot_general(a_ref[...].astype(BF16), b_ref[...].astype(BF16), dims,
                                        preferred_element_type=F32)

        @pl.when(step == nk - 1)
        def _():
            o_ref[...] = acc_ref[...].astype(out_dtype)

    a_spec = pl.BlockSpec((tk, tm), lambda i, j, s: (s, i)) if ta else pl.BlockSpec((tm, tk), lambda i, j, s: (i, s))
    b_spec = pl.BlockSpec((tn, tk), lambda i, j, s: (j, s)) if tb else pl.BlockSpec((tk, tn), lambda i, j, s: (s, j))
    return pl.pallas_call(
        body, name=name,
        out_shape=jax.ShapeDtypeStruct((m, n), out_dtype),
        grid=(m // tm, n // tn, nk),
        in_specs=[a_spec, b_spec],
        out_specs=pl.BlockSpec((tm, tn), lambda i, j, s: (i, j)),
        scratch_shapes=[pltpu.VMEM((tm, tn), F32)],
        compiler_params=pltpu.CompilerParams(dimension_semantics=("parallel", "parallel", "arbitrary"),
                                             vmem_limit_bytes=VMEM_LIMIT),
    )(a, b)


@jax.custom_vjp
def mm(x, w):
    return matmul(x, w, name="mm_fwd")


def _mm_fwd(x, w):
    return matmul(x, w, name="mm_fwd"), (x, w)


def _mm_bwd(res, dy):
    x, w = res
    dx = matmul(dy, w, tb=True, out_dtype=x.dtype, name="mm_dx")
    dw = matmul(x, dy, ta=True, out_dtype=w.dtype, name="mm_dw")
    return dx, dw


mm.defvjp(_mm_fwd, _mm_bwd)


def sum_leading(x, name):
    n, r, c = x.shape
    tr = _tile(r, (256, 128, 64, 32, 16, 8))

    def body(x_ref, o_ref):
        acc = x_ref[0].astype(F32)
        for i in range(1, n):
            acc = acc + x_ref[i].astype(F32)
        o_ref[...] = acc

    return pl.pallas_call(
        body, name=name,
        out_shape=jax.ShapeDtypeStruct((r, c), F32),
        grid=(r // tr,),
        in_specs=[pl.BlockSpec((n, tr, c), lambda i: (0, i, 0))],
        out_specs=pl.BlockSpec((tr, c), lambda i: (i, 0)),
        compiler_params=pltpu.CompilerParams(dimension_semantics=("parallel",), vmem_limit_bytes=VMEM_LIMIT),
    )(x)


def adamw(w, g, m, v, name):
    r, c = w.shape
    tr = _tile(r, (256, 128, 64, 32, 16, 8))

    def body(w_ref, g_ref, m_ref, v_ref, d_ref, mo_ref, vo_ref):
        gv = g_ref[...]
        mv = ADAM_B1 * m_ref[...] + (1.0 - ADAM_B1) * gv
        vv = ADAM_B2 * v_ref[...] + (1.0 - ADAM_B2) * jnp.square(gv)
        m_hat = mv / (1.0 - ADAM_B1 ** ADAM_STEP)
        v_hat = vv / (1.0 - ADAM_B2 ** ADAM_STEP)
        d_ref[...] = -ADAM_LR * (m_hat / (jnp.sqrt(v_hat) + ADAM_EPS) + ADAM_WD * w_ref[...])
        mo_ref[...] = mv
        vo_ref[...] = vv

    spec = pl.BlockSpec((tr, c), lambda i: (i, 0))
    return pl.pallas_call(
        body, name=name,
        out_shape=(jax.ShapeDtypeStruct((r, c), F32),) * 3,
        grid=(r // tr,),
        in_specs=[spec] * 4,
        out_specs=(spec,) * 3,
        compiler_params=pltpu.CompilerParams(dimension_semantics=("parallel",), vmem_limit_bytes=VMEM_LIMIT),
    )(w, g, m, v)


ROW_TILE = 256


def _row_tile(t):
    return _tile(t, (ROW_TILE, 128, 64, 32, 16, 8))


def _group_call(body, name, ins, in_kinds, out_shapes, out_kinds, tt, out_dtypes=None):
    g, t = ins[0].shape[:2]

    def spec(kind, shape):
        if kind == 'tok':
            return pl.BlockSpec((1, tt, shape[-1]), lambda i, j: (i, j, 0))
        return pl.BlockSpec((1, 1, shape[-1]), lambda i, j: (i, 0, 0))

    return pl.pallas_call(
        body, name=name,
        out_shape=tuple(jax.ShapeDtypeStruct(s, d) for s, d in zip(out_shapes, out_dtypes or [F32] * len(out_shapes))),
        grid=(g, t // tt),
        in_specs=[spec(k, a.shape) for k, a in zip(in_kinds, ins)],
        out_specs=tuple(spec(k, s) for k, s in zip(out_kinds, out_shapes)),
        compiler_params=pltpu.CompilerParams(dimension_semantics=("parallel", "arbitrary"),
                                             vmem_limit_bytes=VMEM_LIMIT),
    )(*ins)


def _accumulate(ref, val):
    @pl.when(pl.program_id(1) == 0)
    def _():
        ref[...] = jnp.zeros_like(ref)

    ref[0] += jnp.sum(val, axis=0, keepdims=True)


def _modulate_fwd(h, shift, scale, out_dtype):
    def body(h_ref, sh_ref, sc_ref, o_ref):
        hv = h_ref[0]
        r = lax.rsqrt(jnp.mean(hv * hv, axis=-1, keepdims=True) + EPS)
        o_ref[0] = (hv * r * (1.0 + sc_ref[0]) + sh_ref[0]).astype(out_dtype)

    return _group_call(body, "modulate_fwd", [h, shift, scale], ['tok', 'vec', 'vec'], [h.shape], ['tok'],
                       _row_tile(h.shape[1]), [out_dtype])[0]


def _modulate_bwd(h, scale, du):
    def body(h_ref, sc_ref, du_ref, dh_ref, dsh_ref, dsc_ref):
        hv, dv = h_ref[0], du_ref[0]
        r = lax.rsqrt(jnp.mean(hv * hv, axis=-1, keepdims=True) + EPS)
        hn = hv * r
        dn = dv * (1.0 + sc_ref[0])
        dh_ref[0] = r * (dn - hn * jnp.mean(dn * hn, axis=-1, keepdims=True))
        _accumulate(dsh_ref, dv)
        _accumulate(dsc_ref, dv * hn)

    return _group_call(body, "modulate_bwd", [h, scale, du], ['tok', 'vec', 'tok'],
                       [h.shape, scale.shape, scale.shape], ['tok', 'acc', 'acc'], _row_tile(h.shape[1]))


def _rows(t):
    return t.reshape(-1, t.shape[-1])


@functools.partial(jax.custom_vjp, nondiff_argnums=(4,))
def modmm(h, shift, scale, w, out_dtype):
    return _modmm_fwd(h, shift, scale, w, out_dtype)[0]


def _modmm_fwd(h, shift, scale, w, out_dtype):
    u = _modulate_fwd(h, shift, scale, BF16)
    y = matmul(_rows(u), w, out_dtype=out_dtype, name="mm_fwd").reshape(h.shape[:2] + (-1,))
    return y, (h, scale, u, w)


def _modmm_bwd(out_dtype, res, dy):
    h, scale, u, w = res
    du = matmul(_rows(dy), w, tb=True, name="mm_dx").reshape(h.shape)
    dw = matmul(_rows(u), _rows(dy), ta=True, out_dtype=w.dtype, name="mm_dw")
    dh, dsh, dsc = _modulate_bwd(h, scale, du)
    return dh, dsh, dsc, dw


modmm.defvjp(_modmm_fwd, _modmm_bwd)


def _gated_add_call(h, y, gate, coef):
    def body(h_ref, y_ref, g_ref, o_ref):
        o_ref[0] = h_ref[0] + coef * g_ref[0] * y_ref[0]

    return _group_call(body, "gated_add_fwd", [h, y, gate], ['tok', 'tok', 'vec'], [h.shape], ['tok'],
                       _row_tile(h.shape[1]))[0]


def _gated_add_bwd_call(y, gate, dout, coef):
    def body(y_ref, g_ref, d_ref, dy_ref, dg_ref):
        dv = d_ref[0]
        dy_ref[0] = coef * g_ref[0] * dv
        _accumulate(dg_ref, coef * dv * y_ref[0])

    return _group_call(body, "gated_add_bwd", [y, gate, dout], ['tok', 'vec', 'tok'], [y.shape, gate.shape],
                       ['tok', 'acc'], _row_tile(y.shape[1]))


@functools.partial(jax.custom_vjp, nondiff_argnums=(3,))
def gated_add(h, y, gate, coef):
    return _gated_add_call(h, y, gate, coef)


def _gated_add_vjp_fwd(h, y, gate, coef):
    return _gated_add_call(h, y, gate, coef), (y, gate)


def _gated_add_vjp_bwd(coef, res, dout):
    y, gate = res
    dy, dg = _gated_add_bwd_call(y, gate, dout, coef)
    return dout, dy, dg


gated_add.defvjp(_gated_add_vjp_fwd, _gated_add_vjp_bwd)


def _swiglu_fwd(hid):
    f = hid.shape[-1] // 2

    def body(h_ref, o_ref):
        gate, up = h_ref[0, :, 0:f].astype(F32), h_ref[0, :, f:2 * f].astype(F32)
        o_ref[0] = (gate * jax.nn.sigmoid(gate) * up).astype(BF16)

    return _group_call(body, "swiglu_fwd", [hid], ['tok'], [hid.shape[:2] + (f,)], ['tok'],
                       _tile(hid.shape[1], (128, 64, 32, 16)), [BF16])[0]


def _swiglu_bwd(hid, da):
    f = hid.shape[-1] // 2

    def body(h_ref, da_ref, d_ref):
        gate, up, dv = h_ref[0, :, 0:f].astype(F32), h_ref[0, :, f:2 * f].astype(F32), da_ref[0]
        s = jax.nn.sigmoid(gate)
        d_ref[0, :, 0:f] = (dv * up * (s * (1.0 + gate * (1.0 - s)))).astype(hid.dtype)
        d_ref[0, :, f:2 * f] = (dv * (gate * s)).astype(hid.dtype)

    return _group_call(body, "swiglu_bwd", [hid, da], ['tok', 'tok'], [hid.shape], ['tok'],
                       _tile(hid.shape[1], (128, 64, 32, 16)), [hid.dtype])[0]


@jax.custom_vjp
def swiglu_mm(hid, w):
    return _swiglu_mm_fwd(hid, w)[0]


def _swiglu_mm_fwd(hid, w):
    act = _swiglu_fwd(hid)
    y = matmul(_rows(act), w, name="mm_fwd").reshape(hid.shape[:2] + (-1,))
    return y, (hid, act, w)


def _swiglu_mm_bwd(res, dy):
    hid, act, w = res
    da = matmul(_rows(dy), w, tb=True, name="mm_dx").reshape(act.shape)
    dw = matmul(_rows(act), _rows(dy), ta=True, out_dtype=w.dtype, name="mm_dw")
    return _swiglu_bwd(hid, da), dw


swiglu_mm.defvjp(_swiglu_mm_fwd, _swiglu_mm_bwd)


@jax.custom_vjp
def flip_rows(x):
    return _flip_rows_call(x)


def _flip_rows_call(x):
    n, length, c = x.shape
    tb = _tile(length, (256, 128, 64, 32, 16, 8))
    nb = length // tb

    def body(x_ref, o_ref):
        xv = x_ref[0]
        ii = lax.broadcasted_iota(jnp.int32, (tb, tb), 0)
        jj = lax.broadcasted_iota(jnp.int32, (tb, tb), 1)
        rev = (ii + jj == tb - 1).astype(BF16)
        hi = xv.astype(BF16)
        r1 = xv - hi.astype(F32)
        mid = r1.astype(BF16)
        lo = (r1 - mid.astype(F32)).astype(BF16)
        dot = functools.partial(jnp.dot, preferred_element_type=F32)
        o_ref[0] = (dot(rev, hi) + dot(rev, mid)) + dot(rev, lo)

    return pl.pallas_call(
        body, name="flip_rows",
        out_shape=jax.ShapeDtypeStruct(x.shape, F32),
        grid=(n, nb),
        in_specs=[pl.BlockSpec((1, tb, c), lambda i, j: (i, j, 0))],
        out_specs=pl.BlockSpec((1, tb, c), lambda i, j: (i, nb - 1 - j, 0)),
        compiler_params=pltpu.CompilerParams(dimension_semantics=("parallel", "parallel"),
                                             vmem_limit_bytes=VMEM_LIMIT),
    )(x)


flip_rows.defvjp(lambda x: (_flip_rows_call(x), None), lambda _, dy: (_flip_rows_call(dy),))


def _flip_time(t, axis):
    s = t.shape
    lead = math.prod(s[:axis])
    return flip_rows(t.reshape(lead, s[axis], -1)).reshape(s)


def rms_norm(x):
    return x * lax.rsqrt(jnp.mean(x * x, axis=-1, keepdims=True) + EPS)


def raster_to_column(t, rows):
    b, s, d = t.shape
    return t.reshape(b, rows, GRID_W, d).transpose(0, 2, 1, 3).reshape(b, s, d)


def column_to_raster(t, rows):
    b, s, d = t.shape
    return t.reshape(b, GRID_W, rows, d).transpose(0, 2, 1, 3).reshape(b, s, d)


def depthwise_conv(x, w, b):
    pad = A_CONV // 2
    y = lax.conv_general_dilated(x, w[:, None, :], window_strides=(1,), padding=[(pad, pad)],
                                 dimension_numbers=('NWC', 'WIO', 'NWC'), feature_group_count=x.shape[-1])
    return y + b


S5_STATES = B_NGROUPS * B_STATE
S5_ROWS = 8
S5_STEPS_FWD = 64
S5_STEPS_BWD = 32


def _s5_scan_fwd(u2, bd2, cd2, ar8, ai8):
    rows, width = u2.shape
    ns = S5_STATES
    tr = S5_ROWS * S5_STEPS_FWD
    assert rows % tr == 0

    def body(u_ref, bd_ref, cd_ref, ar_ref, ai_ref, y_ref, x_ref, st_ref):
        @pl.when(pl.program_id(0) == 0)
        def _():
            st_ref[...] = jnp.zeros_like(st_ref)

        x_ref[...] = jnp.dot(u_ref[...].astype(BF16), bd_ref[...], preferred_element_type=F32)
        ar, ai = ar_ref[...], ai_ref[...]

        def step(t, carry):
            xr, xi = carry
            r = pl.ds(pl.multiple_of(t * S5_ROWS, S5_ROWS), S5_ROWS)
            nr = ar * xr - ai * xi + x_ref[r, 0:ns]
            ni = ar * xi + ai * xr + x_ref[r, ns:2 * ns]
            x_ref[r, 0:ns] = nr
            x_ref[r, ns:2 * ns] = ni
            return nr, ni

        xr, xi = lax.fori_loop(0, S5_STEPS_FWD, step, (st_ref[:, 0:ns], st_ref[:, ns:2 * ns]), unroll=4)
        st_ref[:, 0:ns] = xr
        st_ref[:, ns:2 * ns] = xi
        y_ref[...] = jnp.dot(x_ref[...].astype(BF16), cd_ref[...], preferred_element_type=F32)

    whole = lambda shape: pl.BlockSpec(shape, lambda i: (0, 0))
    return pl.pallas_call(
        body, name="s5_scan_fwd",
        out_shape=(jax.ShapeDtypeStruct((rows, width), F32), jax.ShapeDtypeStruct((rows, 2 * ns), F32)),
        grid=(rows // tr,),
        in_specs=[pl.BlockSpec((tr, width), lambda i: (i, 0)), whole(bd2.shape), whole(cd2.shape),
                  whole(ar8.shape), whole(ai8.shape)],
        out_specs=(pl.BlockSpec((tr, width), lambda i: (i, 0)), pl.BlockSpec((tr, 2 * ns), lambda i: (i, 0))),
        scratch_shapes=[pltpu.VMEM((S5_ROWS, 2 * ns), F32)],
        compiler_params=pltpu.CompilerParams(dimension_semantics=("arbitrary",), vmem_limit_bytes=VMEM_LIMIT),
    )(u2, bd2, cd2, ar8, ai8)


def _s5_scan_bwd(dy, x, u2, bd2, cd2, ar8, ai8):
    rows, width = u2.shape
    ns = S5_STATES
    steps = S5_STEPS_BWD
    tr = S5_ROWS * steps
    nblk = rows // tr
    assert rows % tr == 0
    nt = (((1,), (1,)), ((), ()))
    tn = (((0,), (0,)), ((), ()))

    def body(dy_ref, x_ref, xp_ref, u_ref, bd_ref, cd_ref, ar_ref, ai_ref,
             du_ref, dbd_ref, dcd_ref, dar_ref, dai_ref, g_ref, st_ref):
        k = pl.program_id(0)

        @pl.when(k == 0)
        def _():
            st_ref[...] = jnp.zeros_like(st_ref)
            dbd_ref[...] = jnp.zeros_like(dbd_ref)
            dcd_ref[...] = jnp.zeros_like(dcd_ref)
            dar_ref[...] = jnp.zeros_like(dar_ref)
            dai_ref[...] = jnp.zeros_like(dai_ref)

        dyb = dy_ref[...].astype(BF16)
        g_ref[...] = lax.dot_general(dyb, cd_ref[...], nt, preferred_element_type=F32)
        ar, ai = ar_ref[...], ai_ref[...]

        def adjoint(r, carry, xpr, xpi):
            gr_n, gi_n, dar, dai = carry
            gr = g_ref[r, 0:ns] + ar * gr_n + ai * gi_n
            gi = g_ref[r, ns:2 * ns] - ai * gr_n + ar * gi_n
            g_ref[r, 0:ns] = gr
            g_ref[r, ns:2 * ns] = gi
            return gr, gi, dar + gr * xpr + gi * xpi, dai + gi * xpr - gr * xpi

        def step(i, carry):
            t = steps - 1 - i
            r = pl.ds(pl.multiple_of(t * S5_ROWS, S5_ROWS), S5_ROWS)
            rp = pl.ds(pl.multiple_of((t - 1) * S5_ROWS, S5_ROWS), S5_ROWS)
            return adjoint(r, carry, x_ref[rp, 0:ns], x_ref[rp, ns:2 * ns])

        zero = jnp.zeros((S5_ROWS, ns), F32)
        carry = lax.fori_loop(0, steps - 1, step, (st_ref[:, 0:ns], st_ref[:, ns:2 * ns], zero, zero), unroll=2)
        has_prev = (k < nblk - 1).astype(F32)
        gr, gi, dar, dai = adjoint(pl.ds(0, S5_ROWS), carry, xp_ref[:, 0:ns] * has_prev, xp_ref[:, ns:2 * ns] * has_prev)
        st_ref[:, 0:ns] = gr
        st_ref[:, ns:2 * ns] = gi
        dar_ref[...] += dar
        dai_ref[...] += dai
        gb = g_ref[...].astype(BF16)
        du_ref[...] = lax.dot_general(gb, bd_ref[...], nt, preferred_element_type=F32)
        dbd_ref[...] += lax.dot_general(u_ref[...].astype(BF16), gb, tn, preferred_element_type=F32)
        dcd_ref[...] += lax.dot_general(x_ref[...].astype(BF16), dyb, tn, preferred_element_type=F32)

    whole = lambda shape: pl.BlockSpec(shape, lambda k: (0, 0))
    rev = lambda k: (nblk - 1 - k, 0)
    prev = lambda k: (jnp.maximum((nblk - 1 - k) * steps - 1, 0), 0)
    return pl.pallas_call(
        body, name="s5_scan_bwd",
        out_shape=(jax.ShapeDtypeStruct((rows, width), F32), jax.ShapeDtypeStruct(bd2.shape, F32),
                   jax.ShapeDtypeStruct(cd2.shape, F32), jax.ShapeDtypeStruct(ar8.shape, F32),
                   jax.ShapeDtypeStruct(ai8.shape, F32)),
        grid=(nblk,),
        in_specs=[pl.BlockSpec((tr, width), rev), pl.BlockSpec((tr, 2 * ns), rev),
                  pl.BlockSpec((S5_ROWS, 2 * ns), prev), pl.BlockSpec((tr, width), rev),
                  whole(bd2.shape), whole(cd2.shape), whole(ar8.shape), whole(ai8.shape)],
        out_specs=(pl.BlockSpec((tr, width), rev), whole(bd2.shape), whole(cd2.shape), whole(ar8.shape),
                   whole(ai8.shape)),
        scratch_shapes=[pltpu.VMEM((tr, 2 * ns), F32), pltpu.VMEM((S5_ROWS, 2 * ns), F32)],
        compiler_params=pltpu.CompilerParams(dimension_semantics=("arbitrary",), vmem_limit_bytes=VMEM_LIMIT),
    )(dy, x, x, u2, bd2, cd2, ar8, ai8)


@jax.custom_vjp
def s5_core(u2, bd2, cd2, ar8, ai8):
    return _s5_scan_fwd(u2, bd2.astype(BF16), cd2.astype(BF16), ar8, ai8)[0]


def _s5_core_fwd(u2, bd2, cd2, ar8, ai8):
    bd2, cd2 = bd2.astype(BF16), cd2.astype(BF16)
    y, x = _s5_scan_fwd(u2, bd2, cd2, ar8, ai8)
    return y, (x, u2, bd2, cd2, ar8, ai8)


def _s5_core_bwd(res, dy):
    return _s5_scan_bwd(dy, *res)


s5_core.defvjp(_s5_core_fwd, _s5_core_bwd)


def s5_mixers(p_ctx, p_lat, lam_re, lam_im, log_step, b_re, b_im, c_re, c_im, d_skip, glu_w, glu_b):
    bsz = p_ctx.shape[0]
    assert 2 * bsz == S5_ROWS
    eye = jnp.eye(B_NGROUPS, dtype=F32)
    bds, cds, ars, ais = [], [], [], []
    for d in range(2):
        step = jnp.exp(log_step[d])[:, None]
        mag = jnp.exp(lam_re[d] * step)
        ar = mag * jnp.cos(lam_im[d] * step)
        ai = mag * jnp.sin(lam_im[d] * step)
        den = lam_re[d] * lam_re[d] + lam_im[d] * lam_im[d]
        nr = ar - 1.0
        kr = (nr * lam_re[d] + ai * lam_im[d]) / den
        ki = (ai * lam_re[d] - nr * lam_im[d]) / den
        br = kr[..., None] * b_re[d] - ki[..., None] * b_im[d]
        bi = kr[..., None] * b_im[d] + ki[..., None] * b_re[d]
        blk = lambda w: jnp.einsum('gnc,gh->gchn', w, eye).reshape(B_WIDTH, S5_STATES)
        bds.append(jnp.concatenate([blk(br), blk(bi)], axis=1))
        blk_c = lambda w: jnp.einsum('gcn,gh->gnhc', w, eye).reshape(S5_STATES, B_WIDTH)
        cds.append(jnp.concatenate([blk_c(c_re[d]), -blk_c(c_im[d])], axis=0))
        ars.append(jnp.broadcast_to(ar.reshape(1, S5_STATES), (bsz, S5_STATES)))
        ais.append(jnp.broadcast_to(ai.reshape(1, S5_STATES), (bsz, S5_STATES)))
    bd2 = jnp.concatenate(bds, axis=0)
    cd2 = jnp.concatenate(cds, axis=1)
    ar8 = jnp.concatenate(ars, axis=0)
    ai8 = jnp.concatenate(ais, axis=0)

    def rows_of(p):
        ut = jnp.swapaxes(p, 0, 1)
        z = jnp.zeros_like(ut)
        return jnp.concatenate([jnp.concatenate([ut, z], axis=-1), jnp.concatenate([z, _flip_time(ut, 0)], axis=-1)], axis=1)

    lc = p_ctx.shape[1]
    u2 = jnp.concatenate([rows_of(p_ctx), rows_of(p_lat)], axis=0)
    y2 = s5_core(u2.reshape(-1, 2 * B_WIDTH), bd2, cd2, ar8, ai8).reshape(u2.shape)

    def finish(y2p, p):
        y = y2p[:, :bsz, :B_WIDTH] + _flip_time(y2p[:, bsz:, B_WIDTH:], 0)
        y = jnp.swapaxes(y, 0, 1) + d_skip * p
        y = jax.nn.gelu(y)
        gate = mm(y.reshape(-1, B_WIDTH), glu_w).reshape(y.shape)
        return y * jax.nn.sigmoid(gate + glu_b)

    return finish(y2[:lc], p_ctx), finish(y2[lc:], p_lat)


GLA_CHUNK = 64
GLA_SUB = 16
NT_DIMS = (((1,), (1,)), ((), ()))
TN_DIMS = (((0,), (0,)), ((), ()))


def _bdot(a, b, dims=(((1,), (0,)), ((), ()))):
    return lax.dot_general(a.astype(BF16), b.astype(BF16), dims, preferred_element_type=F32)


def _hdot(a, b, dims=(((1,), (0,)), ((), ()))):
    ah, bh = a.astype(BF16), b.astype(BF16)
    al, bl = (a - ah.astype(F32)).astype(BF16), (b - bh.astype(F32)).astype(BF16)
    dot = functools.partial(lax.dot_general, dimension_numbers=dims, preferred_element_type=F32)
    return dot(ah, bh) + (dot(ah, bl) + dot(al, bh))


def _sub_block_ref(cum, rows, lo, hi, rev):
    n = cum.shape[0]
    if rev:
        return (cum[hi:hi + 1, :] if hi < n else jnp.zeros_like(cum[0:1, :])), rows >= lo
    return (cum[lo - 1:lo, :] if lo else jnp.zeros_like(cum[0:1, :])), rows < hi


def _chunk_of(step, nc, nc_ctx, rev):
    if not rev:
        return step
    return jnp.where(step < nc_ctx, nc_ctx - 1 - step, nc + nc_ctx - 1 - step)


def _gla_scores(q, k, cum, cumr, tri, rev):
    n = GLA_CHUNK
    if cumr is not None:
        decay = jnp.where(tri, jnp.exp(jnp.where(tri, cum - cumr, 0.0)), 0.0)
        return _bdot(q, k, NT_DIMS) * decay, decay
    rows = lax.broadcasted_iota(jnp.int32, (n, 1), 0)
    parts = []
    for i in range(n // GLA_SUB):
        lo, hi = i * GLA_SUB, (i + 1) * GLA_SUB
        ref, seen = _sub_block_ref(cum, rows, lo, hi, rev)
        qt = q[lo:hi] * jnp.exp(cum[lo:hi] - ref)
        kh = jnp.where(seen, k * jnp.exp(jnp.where(seen, ref - cum, 0.0)), 0.0)
        parts.append(_bdot(qt, kh, NT_DIMS))
    return jnp.where(tri, jnp.concatenate(parts, axis=0), 0.0), None


def _gla_fwd(q, k, cum, cumr, v, rev, nc_ctx):
    bsz, length, width = q.shape
    dk = GLA_CHUNK
    nh = width // dk
    nc = length // GLA_CHUNK
    scalar = cumr is not None

    def body(*refs):
        if scalar:
            q_ref, k_ref, cum_ref, cumr_ref, v_ref, o_ref, s_ref, st_ref = refs
        else:
            q_ref, k_ref, cum_ref, v_ref, o_ref, s_ref, st_ref = refs

        @pl.when(pl.program_id(1) == 0)
        def _():
            st_ref[...] = jnp.zeros_like(st_ref)

        ii = lax.broadcasted_iota(jnp.int32, (GLA_CHUNK, GLA_CHUNK), 0)
        jj = lax.broadcasted_iota(jnp.int32, (GLA_CHUNK, GLA_CHUNK), 1)
        tri = jj >= ii if rev else jj <= ii
        edge = 0 if rev else GLA_CHUNK - 1
        qa, ka, ca, va = q_ref[0], k_ref[0], cum_ref[0], v_ref[0]
        cra = cumr_ref[0] if scalar else None
        outs = []
        for h in range(nh):
            sl = slice(h * dk, (h + 1) * dk)
            qv, kv, cv, vv, st = qa[:, sl], ka[:, sl], ca[:, sl], va[:, sl], st_ref[h]
            s_ref[0, 0, h] = st
            a, _ = _gla_scores(qv, kv, cv, cra[:, sl] if scalar else None, tri, rev)
            outs.append(_bdot(qv * jnp.exp(cv), st, NT_DIMS) + _bdot(a, vv))
            last = cv[edge:edge + 1, :]
            st_ref[h] = st * jnp.exp(last) + _bdot(vv, kv * jnp.exp(last - cv), TN_DIMS)
        o_ref[0] = jnp.concatenate(outs, axis=1)

    seq = pl.BlockSpec((1, GLA_CHUNK, width), lambda n, c: (n, _chunk_of(c, nc, nc_ctx, rev), 0))
    state = pl.BlockSpec((1, 1, nh, dk, dk), lambda n, c: (n, _chunk_of(c, nc, nc_ctx, rev), 0, 0, 0))
    ins = [q, k, cum] + ([cumr] if scalar else []) + [v]
    return pl.pallas_call(
        body, name="gla_fwd_scalar" if scalar else "gla_fwd",
        out_shape=(jax.ShapeDtypeStruct((bsz, length, width), F32), jax.ShapeDtypeStruct((bsz, nc, nh, dk, dk), F32)),
        grid=(bsz, nc),
        in_specs=[seq] * len(ins),
        out_specs=(seq, state),
        scratch_shapes=[pltpu.VMEM((nh, dk, dk), F32)],
        compiler_params=pltpu.CompilerParams(dimension_semantics=("parallel", "arbitrary"),
                                             vmem_limit_bytes=VMEM_LIMIT),
    )(*ins)


def _gla_bwd(do, q, k, cum, cumr, v, states, rev, nc_ctx):
    bsz, length, width = q.shape
    nc = length // GLA_CHUNK
    scalar = cumr is not None
    n = GLA_CHUNK
    dk = GLA_CHUNK
    nh = width // dk

    def body(*refs):
        if scalar:
            do_ref, q_ref, k_ref, cum_ref, cumr_ref, v_ref, s_ref, dq_ref, dk_ref, dc_ref, dcr_ref, dv_ref, dst_ref = refs
        else:
            do_ref, q_ref, k_ref, cum_ref, v_ref, s_ref, dq_ref, dk_ref, dc_ref, dv_ref, dst_ref = refs

        @pl.when(pl.program_id(1) == 0)
        def _():
            dst_ref[...] = jnp.zeros_like(dst_ref)

        ii = lax.broadcasted_iota(jnp.int32, (n, n), 0)
        jj = lax.broadcasted_iota(jnp.int32, (n, n), 1)
        tri = jj >= ii if rev else jj <= ii
        edge = 0 if rev else n - 1
        rows = lax.broadcasted_iota(jnp.int32, (n, 1), 0)
        doa, qa, ka, ca, va = do_ref[0], q_ref[0], k_ref[0], cum_ref[0], v_ref[0]
        cra = cumr_ref[0] if scalar else None
        dqs, dks, dcs, dcrs, dvs = [], [], [], [], []
        for h in range(nh):
            sl = slice(h * dk, (h + 1) * dk)
            dov, qv, kv, cv, vv, st, dst = doa[:, sl], qa[:, sl], ka[:, sl], ca[:, sl], va[:, sl], s_ref[0, 0, h], dst_ref[h]
            e = jnp.exp(cv)
            qe = qv * e
            last = cv[edge:edge + 1, :]
            w = jnp.exp(last - cv)
            kw = kv * w
            el = jnp.exp(last)
            hd = _bdot if scalar else _hdot
            d_qe = hd(dov, st)
            d_kw = hd(vv, dst)
            dv = _bdot(kw, dst, NT_DIMS)
            d_last = jnp.sum(st * dst, axis=0, keepdims=True) * el + jnp.sum(d_kw * kw, axis=0, keepdims=True)
            dst_ref[h] = dst * el + _bdot(dov, qe, TN_DIMS)
            dq = d_qe * e
            dkk = d_kw * w
            dc = d_qe * qe - d_kw * kw + jnp.where(rows == edge, d_last, 0.0)
            da = jnp.where(tri, hd(dov, vv, NT_DIMS), 0.0)
            if scalar:
                a, decay = _gla_scores(qv, kv, cv, cra[:, sl], tri, rev)
                dg = da * decay
                dq = dq + _bdot(dg, kv)
                dkk = dkk + _bdot(dg, qv, TN_DIMS)
                p = da * a
                dc = dc + p
                dcrs.append(-p)
            else:
                a_parts, dq_parts = [], []
                for i in range(n // GLA_SUB):
                    lo, hi = i * GLA_SUB, (i + 1) * GLA_SUB
                    ref, seen = _sub_block_ref(cv, rows, lo, hi, rev)
                    eq = jnp.exp(cv[lo:hi] - ref)
                    qt = qv[lo:hi] * eq
                    ek = jnp.where(seen, jnp.exp(jnp.where(seen, ref - cv, 0.0)), 0.0)
                    kh = kv * ek
                    a_parts.append(_bdot(qt, kh, NT_DIMS))
                    dqt = _hdot(da[lo:hi], kh)
                    dkh = _hdot(da[lo:hi], qt, TN_DIMS)
                    dq_parts.append((dqt * eq, dqt * qt))
                    dkk = dkk + dkh * ek
                    dc = dc - dkh * kh
                a = jnp.where(tri, jnp.concatenate(a_parts, axis=0), 0.0)
                dq = dq + jnp.concatenate([p[0] for p in dq_parts], axis=0)
                dc = dc + jnp.concatenate([p[1] for p in dq_parts], axis=0)
            dvs.append(dv + _bdot(a, dov, TN_DIMS))
            dqs.append(dq)
            dks.append(dkk)
            dcs.append(dc)
        cat = functools.partial(jnp.concatenate, axis=1)
        dq_ref[0], dk_ref[0], dc_ref[0], dv_ref[0] = cat(dqs), cat(dks), cat(dcs), cat(dvs)
        if scalar:
            dcr_ref[0] = cat(dcrs)

    seq = pl.BlockSpec((1, n, width), lambda s, c: (s, _chunk_of(nc - 1 - c, nc, nc_ctx, rev), 0))
    state = pl.BlockSpec((1, 1, nh, dk, dk), lambda s, c: (s, _chunk_of(nc - 1 - c, nc, nc_ctx, rev), 0, 0, 0))
    ins = [do, q, k, cum] + ([cumr] if scalar else []) + [v]
    n_out = 5 if scalar else 4
    return pl.pallas_call(
        body, name="gla_bwd_scalar" if scalar else "gla_bwd",
        out_shape=(jax.ShapeDtypeStruct((bsz, length, width), F32),) * n_out,
        grid=(bsz, nc),
        in_specs=[seq] * len(ins) + [state],
        out_specs=(seq,) * n_out,
        scratch_shapes=[pltpu.VMEM((nh, dk, dk), F32)],
        compiler_params=pltpu.CompilerParams(dimension_semantics=("parallel", "arbitrary"),
                                             vmem_limit_bytes=VMEM_LIMIT),
    )(*ins, states)


@functools.partial(jax.custom_vjp, nondiff_argnums=(4, 5))
def gla(q, k, cum, v, rev, nc_ctx):
    return _gla_fwd(q, k, cum, None, v, rev, nc_ctx)[0]


def _gla_vjp_fwd(q, k, cum, v, rev, nc_ctx):
    o, states = _gla_fwd(q, k, cum, None, v, rev, nc_ctx)
    return o, (q, k, cum, v, states)


def _gla_vjp_bwd(rev, nc_ctx, res, do):
    q, k, cum, v, states = res
    return _gla_bwd(do, q, k, cum, None, v, states, rev, nc_ctx)


gla.defvjp(_gla_vjp_fwd, _gla_vjp_bwd)


@functools.partial(jax.custom_vjp, nondiff_argnums=(5, 6))
def gla_scalar(q, k, cum, cumr, v, rev, nc_ctx):
    return _gla_fwd(q, k, cum, cumr, v, rev, nc_ctx)[0]


def _gla_scalar_vjp_fwd(q, k, cum, cumr, v, rev, nc_ctx):
    o, states = _gla_fwd(q, k, cum, cumr, v, rev, nc_ctx)
    return o, (q, k, cum, cumr, v, states)


def _gla_scalar_vjp_bwd(rev, nc_ctx, res, do):
    q, k, cum, cumr, v, states = res
    return _gla_bwd(do, q, k, cum, cumr, v, states, rev, nc_ctx)


gla_scalar.defvjp(_gla_scalar_vjp_fwd, _gla_scalar_vjp_bwd)


def _chunk_cumsum(g, rev, axis=-2):
    axis = axis % g.ndim
    s = g.shape
    by_chunk = g.reshape(s[:axis] + (s[axis] // GLA_CHUNK, GLA_CHUNK) + s[axis + 1:])
    c = jnp.cumsum(by_chunk, axis=axis + 1)
    if rev:
        c = lax.slice_in_dim(c, GLA_CHUNK - 1, GLA_CHUNK, axis=axis + 1) - c + by_chunk
    return c.reshape(s)


def _both_parts(t_ctx, t_lat):
    return jnp.concatenate([t.reshape(t.shape[:2] + (-1,)) for t in (t_ctx, t_lat)], axis=1)


def _split_parts(o, lc, nh):
    return tuple(t.reshape(t.shape[:2] + (nh, -1)) for t in (o[:, :lc], o[:, lc:]))


def hgrn2_mixers(p_ctx, p_lat, lower, norm_w):
    bsz, lc = p_ctx.shape[:2]
    lower = lower.reshape(2, C_HEADS, C_KEY)

    def heads(p, lo, hi):
        return p[..., lo:hi].reshape(p.shape[:2] + (C_HEADS, -1))

    q_c, q_l = (jax.nn.silu(heads(p, 0, C_WIDTH)) for p in (p_ctx, p_lat))
    v_c, v_l = (heads(p, 3 * C_WIDTH, 4 * C_WIDTH) for p in (p_ctx, p_lat))
    q, v = _both_parts(q_c, q_l), _both_parts(v_c, v_l)
    nc_ctx = lc // GLA_CHUNK
    o = []
    for d in range(2):
        f_c, f_l = (lower[d] + (1.0 - lower[d]) * jax.nn.sigmoid(heads(p, (1 + d) * C_WIDTH, (2 + d) * C_WIDTH))
                    for p in (p_ctx, p_lat))
        cum = jnp.concatenate([_chunk_cumsum(jnp.log(f).reshape(f.shape[:2] + (-1,)), d, axis=1) for f in (f_c, f_l)], axis=1)
        o.append(gla(q, _both_parts(1.0 - f_c, 1.0 - f_l), cum, v, bool(d), nc_ctx))
    f_c, f_l = _split_parts(o[0], lc, C_HEADS)
    b_c, b_l = _split_parts(o[1], lc, C_HEADS)
    outs = []
    for o_sum, p in ((f_c + b_c, p_ctx), (f_l + b_l, p_lat)):
        o_n = rms_norm(o_sum) * norm_w.reshape(C_HEADS, C_VAL)
        outs.append(o_n.reshape(p.shape[:2] + (C_WIDTH,)) * jax.nn.silu(p[..., 4 * C_WIDTH:]))
    return tuple(outs)


def ssd_mixers(p_ctx, p_lat, conv_w, conv_b, dt_bias, a_log, d_skip, norm_w):
    bsz, lc = p_ctx.shape[:2]
    rep = A_HEADS // A_GROUPS
    a = -jnp.exp(a_log)
    xs, bs, cs, dts, zs = [], [], [], [], []
    for p in (p_ctx, p_lat):
        z, xbc, dt_raw = jnp.split(p, [A_INNER, A_INNER + A_CONV_DIM], axis=-1)
        xbc = jax.nn.silu(depthwise_conv(xbc, conv_w, conv_b))
        x_, b_, c_ = jnp.split(xbc, [A_INNER, A_INNER + A_GROUPS * A_STATE], axis=-1)
        shp = p.shape[:2]
        xs.append(x_.reshape(shp + (A_HEADS, A_HEAD_DIM)))
        bs.append(jnp.repeat(b_.reshape(shp + (A_GROUPS, A_STATE)), rep, axis=2))
        cs.append(jnp.repeat(c_.reshape(shp + (A_GROUPS, A_STATE)), rep, axis=2))
        dts.append([jax.nn.softplus(dt_raw[..., d * A_HEADS:(d + 1) * A_HEADS] + dt_bias[d]) for d in range(2)])
        zs.append(z)
    q, v = _both_parts(cs[0], cs[1]), _both_parts(xs[0], xs[1])
    nc_ctx = lc // GLA_CHUNK
    o = []
    for d in range(2):
        k = _both_parts(bs[0] * dts[0][d][..., None], bs[1] * dts[1][d][..., None])
        adt = jnp.concatenate([_chunk_cumsum(dt[d] * a[d], d, axis=1) for dt in dts], axis=1)
        nb, lt = adt.shape[:2]
        cum = jnp.broadcast_to(adt[..., None], (nb, lt, A_HEADS, A_STATE)).reshape(nb, lt, -1)
        along = jnp.swapaxes(adt.reshape(nb, lt // GLA_CHUNK, GLA_CHUNK, A_HEADS), 2, 3)[:, :, None]
        cumr = jnp.broadcast_to(along, (nb, lt // GLA_CHUNK, GLA_CHUNK, A_HEADS, GLA_CHUNK)).reshape(nb, lt, -1)
        o.append(gla_scalar(q, k, cum, cumr, v, bool(d), nc_ctx))
    f_c, f_l = _split_parts(o[0], lc, A_HEADS)
    b_c, b_l = _split_parts(o[1], lc, A_HEADS)
    outs = []
    for y, x_, z in ((f_c + b_c, xs[0], zs[0]), (f_l + b_l, xs[1], zs[1])):
        y = y + d_skip[:, None] * x_
        y = y.reshape(z.shape) * jax.nn.silu(z)
        outs.append(rms_norm(y) * norm_w)
    return tuple(outs)


def token_mixers(p_ctx, p_lat, W, l, lower):
    def cut(p):
        return p[..., :A_COLS], p[..., 1408:1408 + B_COLS], p[..., 1664:1664 + C_COLS]

    pa_c, pb_c, pc_c = cut(p_ctx)
    pa_l, pb_l, pc_l = cut(p_lat)
    ya_c, ya_l = ssd_mixers(pa_c, pa_l, W['a_conv_w'][l], W['a_conv_b'][l], W['a_dt_bias'][l], W['a_log'][l],
                            W['a_d'][l], W['a_norm_w'][l])
    yb_c, yb_l = s5_mixers(pb_c, pb_l, W['s5_lam_re'][l], W['s5_lam_im'][l], W['s5_log_step'][l], W['s5_b_re'][l],
                           W['s5_b_im'][l], W['s5_c_re'][l], W['s5_c_im'][l], W['s5_d'][l], W['s5_glu_w'][l],
                           W['s5_glu_b'][l])
    yc_c, yc_l = hgrn2_mixers(pc_c, pc_l, lower, W['hg_norm_w'][l])
    return (jnp.concatenate([ya_c, yb_c, yc_c], axis=-1), jnp.concatenate([ya_l, yb_l, yc_l], axis=-1))


def _pad_w_in(w):
    z = functools.partial(jnp.zeros, dtype=w.dtype)
    return jnp.concatenate([w[:, :A_COLS], z((D_MODEL, 1408 - A_COLS)), w[:, A_COLS:], z((D_MODEL, IN_PAD - 2944))],
                           axis=1)


def _mm3(t, w):
    g, tt, k = t.shape
    return mm(t.reshape(g * tt, k), w).reshape(g, tt, -1)


def _ffn(h, mg, first, w_in, w_out):
    hid = modmm(h, mg[:, first:first + 1], mg[:, first + 1:first + 2], w_in, BF16)
    return gated_add(h, swiglu_mm(hid, w_out), mg[:, first + 2:first + 3], 0.5)


def local_loss(x, W, m_lat, m_ctx, ctx, target):
    bsz, seq, dm = x.shape
    lc = ctx.shape[1]
    tg = bsz * lc
    assert seq % tg == 0
    gl = seq // tg
    ng = bsz * gl
    rows = seq // GRID_W
    p_lb = jax.nn.softmax(W['hg_lb_logits'], axis=0)
    lower_bounds = jnp.cumsum(p_lb, axis=0) - p_lb[:1]
    h = jnp.concatenate([x.reshape(ng, tg, dm), ctx.reshape(1, tg, dm)], axis=0)
    for l in range(DEPTH):
        last = l == DEPTH - 1
        col_major = l % 2 == 1
        mg = jnp.concatenate([jnp.repeat(m_lat[l], gl, axis=0), m_ctx[l][None]], axis=0)
        h = _ffn(h, mg, 0, W['ffn_w_in'][l][0], W['ffn_w_out'][l][0])
        hp = h
        if col_major:
            h_lat = raster_to_column(h[:ng].reshape(bsz, seq, dm), rows)
            hp = jnp.concatenate([h_lat.reshape(ng, tg, dm), h[ng:]], axis=0)
        p = modmm(hp, mg[:, 3:4], mg[:, 4:5], _pad_w_in(W['w_in'][l]), F32)
        mix_ctx, mix_lat = token_mixers(p[ng].reshape(bsz, lc, -1), p[:ng].reshape(bsz, seq, -1), W, l,
                                        lower_bounds[l])
        if last:
            h, mg = h[:ng], mg[:ng]
            y_lat = _mm3(mix_lat.reshape(ng, tg, dm), W['w_out'][l])
            y_ctx = None
        else:
            y = _mm3(jnp.concatenate([mix_lat.reshape(ng, tg, dm), mix_ctx.reshape(1, tg, dm)], axis=0), W['w_out'][l])
            y_lat, y_ctx = y[:ng], y[ng:]
        if col_major:
            y_lat = column_to_raster(y_lat.reshape(bsz, seq, dm), rows).reshape(ng, tg, dm)
        y = y_lat if y_ctx is None else jnp.concatenate([y_lat, y_ctx], axis=0)
        h = gated_add(h, y, mg[:, 5:6], 1.0)
        h = _ffn(h, mg, 6, W['ffn_w_in'][l][1], W['ffn_w_out'][l][1])
    y = rms_norm(h[:ng].reshape(bsz, seq, dm)) * W['final_norm_w']
    err = jnp.square(y - target)
    return 0.5 * jnp.sum(jnp.mean(err, axis=-1))


def _pad_rows(a, rows):
    return jnp.concatenate([a, jnp.zeros((rows - a.shape[0],) + a.shape[1:], a.dtype)], axis=0)


def kernel(x, c, ctx, c_ctx, mod_w, mod_b, ffn_w_in, ffn_w_out, w_in, w_out, a_conv_w, a_conv_b, a_dt_bias, a_log, a_d, a_norm_w, s5_lam_re, s5_lam_im, s5_log_step, s5_b_re, s5_b_im, s5_c_re, s5_c_im, s5_d, s5_glu_w, s5_glu_b, hg_lb_logits, hg_norm_w, final_norm_w, loss_target, m_c_ctx, m_mod_w, m_mod_b, m_ffn_w_in, m_ffn_w_out, m_w_in, m_w_out, m_a_conv_w, m_a_conv_b, m_a_dt_bias, m_a_log, m_a_d, m_a_norm_w, m_s5_lam_re, m_s5_lam_im, m_s5_log_step, m_s5_b_re, m_s5_b_im, m_s5_c_re, m_s5_c_im, m_s5_d, m_s5_glu_w, m_s5_glu_b, m_hg_lb_logits, m_hg_norm_w, m_final_norm_w, v_c_ctx, v_mod_w, v_mod_b, v_ffn_w_in, v_ffn_w_out, v_w_in, v_w_out, v_a_conv_w, v_a_conv_b, v_a_dt_bias, v_a_log, v_a_d, v_a_norm_w, v_s5_lam_re, v_s5_lam_im, v_s5_log_step, v_s5_b_re, v_s5_b_im, v_s5_c_re, v_s5_c_im, v_s5_d, v_s5_glu_w, v_s5_glu_b, v_hg_lb_logits, v_hg_norm_w, v_final_norm_w):
    given = dict(locals())
    w_loc = {n: given[n] for n in WEIGHTS}
    m_loc = {n: given["m_" + n] for n in WEIGHTS}
    v_loc = {n: given["v_" + n] for n in WEIGHTS}
    bsz = x.shape[0]
    me = 4 * lax.axis_index("x") + 2 * lax.axis_index("y") + lax.axis_index("c")

    small_sh = [c] + [w_loc[n] for n in SMALL_SHARDED]
    g1 = _unpack(all_gather([_pack(small_sh, 128, 8)], "gather_small")[0], [a.shape for a in small_sh])
    c_all = g1[0].reshape(N_DEV * bsz, D_MODEL)
    gathered = dict(zip(BIG, all_gather([w_loc[n].astype(BF16) for n in BIG], "gather_weights")))
    W = {'ffn_w_in': [[_assemble(gathered['ffn_w_in'][:, l, i], 1) for i in range(2)] for l in range(DEPTH)],
         'ffn_w_out': [[_assemble(gathered['ffn_w_out'][:, l, i], 0) for i in range(2)] for l in range(DEPTH)],
         'w_in': [_assemble(gathered['w_in'][:, l], 1) for l in range(DEPTH)],
         'w_out': [_assemble(gathered['w_out'][:, l], 0) for l in range(DEPTH)]}
    for (n, ax), t in zip(SMALL_SHARDED.items(), g1[1:]):
        W[n] = _assemble(t, ax)
    for n in SMALL:
        if n not in SMALL_SHARDED and n not in ('c_ctx', 'mod_b'):
            W[n] = w_loc[n]

    n_rows = N_DEV * bsz + 1
    pad_rows = 8 * ((n_rows + 7) // 8)
    c_rows = _pad_rows(jnp.concatenate([c_all, c_ctx[None]], axis=0), pad_rows)
    sc = jax.nn.silu(c_rows)
    mods_sh = jnp.stack([matmul(sc, mod_w[l], name="mod_fwd") for l in range(DEPTH)])
    mods = _assemble(all_gather([mods_sh], "gather_mods")[0], 2) + mod_b[:, None, :]
    m_lat = lax.dynamic_slice_in_dim(mods, me * bsz, bsz, axis=1).reshape(DEPTH, bsz, N_MOD, D_MODEL)
    m_ctx = mods[:, n_rows - 1].reshape(DEPTH, N_MOD, D_MODEL)

    loss_loc, (grad_x, gW, gm_lat, gm_ctx) = jax.value_and_grad(local_loss, argnums=(0, 1, 2, 3))(
        x, W, m_lat, m_ctx, ctx, loss_target)

    dm_loc = jnp.concatenate([gm_lat.reshape(DEPTH, bsz, -1), gm_ctx.reshape(DEPTH, 1, -1)], axis=1)
    (dm_all,) = all_gather([dm_loc], "gather_dmods")
    dm_ex = jnp.moveaxis(dm_all[:, :, :bsz], 0, 1).reshape(DEPTH, N_DEV * bsz, -1)
    ncol = N_MOD * D_MODEL
    dm_cx = sum_leading(dm_all[:, :, bsz].reshape(N_DEV, DEPTH * ncol // 128, 128), "sum_dmods_ctx")
    dm_cx = dm_cx.reshape(DEPTH, 1, ncol)
    dm_rows = jnp.concatenate([dm_ex, dm_cx, jnp.zeros((DEPTH, pad_rows - n_rows, ncol), F32)], axis=1)
    grad_mod_b = sum_leading(jnp.moveaxis(dm_rows, 1, 0).reshape(pad_rows, DEPTH * ncol // 128, 128),
                             "sum_mod_b").reshape(DEPTH, ncol)
    my_cols = ncol // N_DEV
    dm_mine = lax.dynamic_slice_in_dim(dm_rows, me * my_cols, my_cols, axis=2)
    grad_mod_w = jnp.stack([matmul(sc, dm_mine[l], ta=True, name="mod_dw") for l in range(DEPTH)])
    dm_cx_mine = lax.dynamic_slice_in_dim(dm_cx, me * my_cols, my_cols, axis=2)
    g_sc_ctx = sum(matmul(_pad_rows(dm_cx_mine[l], 8), mod_w[l], tb=True, name="mod_dc")[0] for l in range(DEPTH))

    small_full = [n for n in SMALL if n not in ('c_ctx', 'mod_b')]
    part = [gW[n] for n in small_full] + [g_sc_ctx, loss_loc.reshape(1)]
    red = sum_leading(all_gather([_pack(part, 128, ROW_TILE)], "gather_small_grads")[0], "sum_small_grads")
    red = _unpack(red, [a.shape for a in part])
    grads = dict(zip(small_full, red[:-2]))
    loss = red[-1].reshape(())
    sig = jax.nn.sigmoid(c_ctx)
    grads['c_ctx'] = red[-2] * (sig * (1.0 + c_ctx * (1.0 - sig)))
    grads['mod_b'] = grad_mod_b
    for n, ax in SMALL_SHARDED.items():
        size = w_loc[n].shape[ax]
        grads[n] = lax.dynamic_slice_in_dim(grads[n], me * size, size, axis=ax)
    grads['mod_w'] = grad_mod_w

    by_dev = {'ffn_w_in': jnp.stack([jnp.stack([_split(g, 1) for g in gl], axis=1) for gl in gW['ffn_w_in']], axis=1),
              'ffn_w_out': jnp.stack([jnp.stack([_split(g, 0) for g in gl], axis=1) for gl in gW['ffn_w_out']], axis=1),
              'w_in': jnp.stack([_split(g, 1) for g in gW['w_in']], axis=1),
              'w_out': jnp.stack([_split(g, 0) for g in gW['w_out']], axis=1)}
    for n, t in zip(BIG, all_to_all([by_dev[n] for n in BIG], "exchange_grads")):
        grads[n] = sum_leading(t.reshape((N_DEV,) + _as_2d(t.shape[1:])), "sum_grads").reshape(t.shape[1:])

    delta, new_m, new_v = {}, {}, {}

    for n in list(BIG) + ['mod_w']:
        outs = adamw(*[d[n].reshape(_as_2d(d[n].shape)) for d in (w_loc, grads, m_loc, v_loc)], name="adamw_" + n)
        delta[n], new_m[n], new_v[n] = (o.reshape(w_loc[n].shape) for o in outs)

    def update(names, width, row_mult, tag):
        packed = [_pack([d[n] for n in names], width, row_mult) for d in (w_loc, grads, m_loc, v_loc)]
        outs = adamw(*packed, name="adamw_" + tag)
        shapes = [w_loc[n].shape for n in names]
        for res, out in zip((delta, new_m, new_v), outs):
            res.update(zip(names, _unpack(out, shapes)))

    update(SMALL, 128, 256, "small")
    return (loss, grad_x, *[grads[n] for n in WEIGHTS], *[delta[n] for n in WEIGHTS],
            *[new_m[n] for n in WEIGHTS], *[new_v[n] for n in WEIGHTS])
```

```python
import functools
import math

import jax
import jax.numpy as jnp
from jax import lax
from jax.experimental import pallas as pl
from jax.experimental.pallas import tpu as pltpu

F32 = jnp.float32
BF16 = jnp.bfloat16
N_DEV = 8
MESH_ID = pl.DeviceIdType.MESH
VMEM_LIMIT = 48 * 1024 * 1024

D_MODEL = 1024
DEPTH = 2
GRID_W = 64
EPS = 1e-6
N_MOD = 9
D_FF = 2816
A_INNER = 512
A_HEADS = 8
A_HEAD_DIM = 64
A_GROUPS = 2
A_STATE = 64
A_CONV = 5
A_CONV_DIM = A_INNER + 2 * A_GROUPS * A_STATE
A_COLS = A_INNER + A_CONV_DIM + 2 * A_HEADS
B_WIDTH = 256
B_GROUP = 16
B_NGROUPS = 16
B_STATE = 64
B_COLS = B_WIDTH
C_WIDTH = 256
C_HEADS = 4
C_KEY = 64
C_VAL = 64
C_COLS = 5 * C_WIDTH
IN_PAD = 3072

ADAM_LR = 0.001
ADAM_B1 = 0.9
ADAM_B2 = 0.999
ADAM_EPS = 1e-08
ADAM_WD = 0.01
ADAM_STEP = 10

WEIGHTS = ['c_ctx', 'mod_w', 'mod_b', 'ffn_w_in', 'ffn_w_out', 'w_in', 'w_out', 'a_conv_w', 'a_conv_b', 'a_dt_bias',
           'a_log', 'a_d', 'a_norm_w', 's5_lam_re', 's5_lam_im', 's5_log_step', 's5_b_re', 's5_b_im', 's5_c_re',
           's5_c_im', 's5_d', 's5_glu_w', 's5_glu_b', 'hg_lb_logits', 'hg_norm_w', 'final_norm_w']
BIG = {'ffn_w_in': 3, 'ffn_w_out': 2, 'w_in': 2, 'w_out': 1}
SMALL_SHARDED = {'a_conv_w': 2, 's5_glu_w': 1, 'hg_lb_logits': 2}
SMALL = [n for n in WEIGHTS if n not in BIG and n != 'mod_w']


def _tile(d, prefs):
    for p in prefs:
        if d % p == 0:
            return p
    return d


def _pack(arrs, width, row_mult):
    flat = jnp.concatenate([a.reshape(-1) for a in arrs])
    pad = (-flat.shape[0]) % (width * row_mult)
    if pad:
        flat = jnp.concatenate([flat, jnp.zeros((pad,), flat.dtype)])
    return flat.reshape(-1, width)


def _as_2d(shape):
    return (math.prod(shape[:-1]), shape[-1])


def _unpack(buf, shapes):
    lead = buf.shape[:-2]
    flat = buf.reshape(lead + (-1,))
    out, off = [], 0
    for s in shapes:
        n = math.prod(s)
        out.append(flat[..., off:off + n].reshape(lead + tuple(s)))
        off += n
    return out


def _assemble(g, axis):
    t = jnp.moveaxis(g, 0, axis)
    s = t.shape
    return t.reshape(s[:axis] + (s[axis] * s[axis + 1],) + s[axis + 2:])


def _split(full, axis):
    s = full.shape
    t = full.reshape(s[:axis] + (N_DEV, s[axis] // N_DEV) + s[axis + 1:])
    return jnp.moveaxis(t, axis, 0)


def all_gather(xs, name):
    nt = len(xs)

    def body(*refs):
        x_refs, out_refs = refs[:nt], refs[nt:2 * nt]
        send_sems, recv_sems, local_sems = refs[2 * nt:]
        ax, ay, ac = lax.axis_index("x"), lax.axis_index("y"), lax.axis_index("c")
        me, sibling = (ax, ay, ac), (ax, ay, 1 - ac)
        chips = [(1 - ax, ay), (ax, 1 - ay), (1 - ax, 1 - ay)]

        def slot(t, px, py, pc):
            return out_refs[t].at[4 * px + 2 * py + pc]

        def copy(t, k, block, to, src=None):
            return pltpu.make_async_remote_copy(
                src_ref=slot(t, *block) if src is None else src, dst_ref=slot(t, *block),
                send_sem=send_sems.at[t, k], recv_sem=recv_sems.at[t, k], device_id=to, device_id_type=MESH_ID)

        mine = [pltpu.make_async_copy(x_refs[t], slot(t, *me), local_sems.at[t]) for t in range(nt)]
        for cp in mine:
            cp.start()
        first = []
        for t in range(nt):
            first.append(copy(t, 0, me, sibling, src=x_refs[t]))
            first += [copy(t, 1 + j, me, (*chip, ac), src=x_refs[t]) for j, chip in enumerate(chips)]
        for cp in first:
            cp.start()
        passed = []
        for j, chip in enumerate(chips):
            for t in range(nt):
                copy(t, 1 + j, (*chip, ac), me).wait_recv()
                passed.append(copy(t, 4 + j, (*chip, ac), sibling))
                passed[-1].start()
        for t in range(nt):
            copy(t, 0, sibling, me).wait_recv()
            for j, chip in enumerate(chips):
                copy(t, 4 + j, (*chip, 1 - ac), me).wait_recv()
        for cp in first + passed:
            cp.wait_send()
        for cp in mine:
            cp.wait()

    return pl.pallas_call(
        body, name=name,
        out_shape=tuple(jax.ShapeDtypeStruct((N_DEV,) + x.shape, x.dtype) for x in xs),
        in_specs=[pl.BlockSpec(memory_space=pl.ANY)] * nt,
        out_specs=tuple(pl.BlockSpec(memory_space=pl.ANY) for _ in xs),
        scratch_shapes=[pltpu.SemaphoreType.DMA((nt, 7)), pltpu.SemaphoreType.DMA((nt, 7)),
                        pltpu.SemaphoreType.DMA((nt,))],
    )(*xs)


def all_to_all(gs, name):
    nt = len(gs)

    def body(*refs):
        g_refs, out_refs = refs[:nt], refs[nt:2 * nt]
        send_sems, recv_sems, local_sems = refs[2 * nt:]
        ax, ay, ac = lax.axis_index("x"), lax.axis_index("y"), lax.axis_index("c")
        my = 4 * ax + 2 * ay + ac
        local = [pltpu.make_async_copy(g_refs[t].at[my], out_refs[t].at[my], local_sems.at[t]) for t in range(nt)]
        for cp in local:
            cp.start()
        peers = []
        for r in range(1, N_DEV):
            px = 1 - ax if r & 4 else ax
            py = 1 - ay if r & 2 else ay
            pc = 1 - ac if r & 1 else ac
            peers.append((px, py, pc))

        def copy(t, k, peer):
            return pltpu.make_async_remote_copy(
                src_ref=g_refs[t].at[4 * peer[0] + 2 * peer[1] + peer[2]], dst_ref=out_refs[t].at[my],
                send_sem=send_sems.at[t, k], recv_sem=recv_sems.at[t, k], device_id=peer, device_id_type=MESH_ID)

        def arrival(t, k, peer):
            slot = 4 * peer[0] + 2 * peer[1] + peer[2]
            return pltpu.make_async_remote_copy(
                src_ref=g_refs[t].at[slot], dst_ref=out_refs[t].at[slot],
                send_sem=send_sems.at[t, k], recv_sem=recv_sems.at[t, k], device_id=peer, device_id_type=MESH_ID)

        sends = [copy(t, k, p) for t in range(nt) for k, p in enumerate(peers)]
        for cp in sends:
            cp.start()
        for t in range(nt):
            for k, p in enumerate(peers):
                arrival(t, k, p).wait_recv()
        for cp in sends:
            cp.wait_send()
        for cp in local:
            cp.wait()

    return pl.pallas_call(
        body, name=name,
        out_shape=tuple(jax.ShapeDtypeStruct(g.shape, g.dtype) for g in gs),
        in_specs=[pl.BlockSpec(memory_space=pl.ANY)] * nt,
        out_specs=tuple(pl.BlockSpec(memory_space=pl.ANY) for _ in gs),
        scratch_shapes=[pltpu.SemaphoreType.DMA((nt, 7)), pltpu.SemaphoreType.DMA((nt, 7)),
                        pltpu.SemaphoreType.DMA((nt,))],
    )(*gs)


def matmul(a, b, *, ta=False, tb=False, out_dtype=F32, name="mm"):
    m, k = (a.shape[1], a.shape[0]) if ta else a.shape
    n = b.shape[0] if tb else b.shape[1]
    assert (b.shape[1] if tb else b.shape[0]) == k, (a.shape, b.shape, ta, tb)
    tm = _tile(m, (1024, 512, 256, 128))
    tn = _tile(n, (1408, 1024, 512, 384, 256, 128))
    tk = _tile(k, (1408, 1024, 512, 256, 128))
    nk = k // tk
    dims = (((0 if ta else 1,), (1 if tb else 0,)), ((), ()))

    def body(a_ref, b_ref, o_ref, acc_ref):
        step = pl.program_id(2)

        @pl.when(step == 0)
        def _():
            acc_ref[...] = jnp.zeros_like(acc_ref)

        acc_ref[...] += lax.dot_general(a_ref[...].astype(BF16), b_ref[...].astype(BF16), dims,
                                        preferred_element_type=F32)

        @pl.when(step == nk - 1)
        def _():
            o_ref[...] = acc_ref[...].astype(out_dtype)

    a_spec = pl.BlockSpec((tk, tm), lambda i, j, s: (s, i)) if ta else pl.BlockSpec((tm, tk), lambda i, j, s: (i, s))
    b_spec = pl.BlockSpec((tn, tk), lambda i, j, s: (j, s)) if tb else pl.BlockSpec((tk, tn), lambda i, j, s: (s, j))
    return pl.pallas_call(
        body, name=name,
        out_shape=jax.ShapeDtypeStruct((m, n), out_dtype),
        grid=(m // tm, n // tn, nk),
        in_specs=[a_spec, b_spec],
        out_specs=pl.BlockSpec((tm, tn), lambda i, j, s: (i, j)),
        scratch_shapes=[pltpu.VMEM((tm, tn), F32)],
        compiler_params=pltpu.CompilerParams(dimension_semantics=("parallel", "parallel", "arbitrary"),
                                             vmem_limit_bytes=VMEM_LIMIT),
    )(a, b)


@jax.custom_vjp
def mm(x, w):
    return matmul(x, w, name="mm_fwd")


def _mm_fwd(x, w):
    return matmul(x, w, name="mm_fwd"), (x, w)


def _mm_bwd(res, dy):
    x, w = res
    dx = matmul(dy, w, tb=True, out_dtype=x.dtype, name="mm_dx")
    dw = matmul(x, dy, ta=True, out_dtype=w.dtype, name="mm_dw")
    return dx, dw


mm.defvjp(_mm_fwd, _mm_bwd)


def sum_leading(x, name):
    n, r, c = x.shape
    tr = _tile(r, (256, 128, 64, 32, 16, 8))

    def body(x_ref, o_ref):
        acc = x_ref[0].astype(F32)
        for i in range(1, n):
            acc = acc + x_ref[i].astype(F32)
        o_ref[...] = acc

    return pl.pallas_call(
        body, name=name,
        out_shape=jax.ShapeDtypeStruct((r, c), F32),
        grid=(r // tr,),
        in_specs=[pl.BlockSpec((n, tr, c), lambda i: (0, i, 0))],
        out_specs=pl.BlockSpec((tr, c), lambda i: (i, 0)),
        compiler_params=pltpu.CompilerParams(dimension_semantics=("parallel",), vmem_limit_bytes=VMEM_LIMIT),
    )(x)


def adamw(w, g, m, v, name):
    r, c = w.shape
    tr = _tile(r, (256, 128, 64, 32, 16, 8))

    def body(w_ref, g_ref, m_ref, v_ref, d_ref, mo_ref, vo_ref):
        gv = g_ref[...]
        mv = ADAM_B1 * m_ref[...] + (1.0 - ADAM_B1) * gv
        vv = ADAM_B2 * v_ref[...] + (1.0 - ADAM_B2) * jnp.square(gv)
        m_hat = mv / (1.0 - ADAM_B1 ** ADAM_STEP)
        v_hat = vv / (1.0 - ADAM_B2 ** ADAM_STEP)
        d_ref[...] = -ADAM_LR * (m_hat / (jnp.sqrt(v_hat) + ADAM_EPS) + ADAM_WD * w_ref[...])
        mo_ref[...] = mv
        vo_ref[...] = vv

    spec = pl.BlockSpec((tr, c), lambda i: (i, 0))
    return pl.pallas_call(
        body, name=name,
        out_shape=(jax.ShapeDtypeStruct((r, c), F32),) * 3,
        grid=(r // tr,),
        in_specs=[spec] * 4,
        out_specs=(spec,) * 3,
        compiler_params=pltpu.CompilerParams(dimension_semantics=("parallel",), vmem_limit_bytes=VMEM_LIMIT),
    )(w, g, m, v)


ROW_TILE = 256


def _row_tile(t):
    return _tile(t, (ROW_TILE, 128, 64, 32, 16, 8))


def _group_call(body, name, ins, in_kinds, out_shapes, out_kinds, tt, out_dtypes=None):
    g, t = ins[0].shape[:2]

    def spec(kind, shape):
        if kind == 'tok':
            return pl.BlockSpec((1, tt, shape[-1]), lambda i, j: (i, j, 0))
        return pl.BlockSpec((1, 1, shape[-1]), lambda i, j: (i, 0, 0))

    return pl.pallas_call(
        body, name=name,
        out_shape=tuple(jax.ShapeDtypeStruct(s, d) for s, d in zip(out_shapes, out_dtypes or [F32] * len(out_shapes))),
        grid=(g, t // tt),
        in_specs=[spec(k, a.shape) for k, a in zip(in_kinds, ins)],
        out_specs=tuple(spec(k, s) for k, s in zip(out_kinds, out_shapes)),
        compiler_params=pltpu.CompilerParams(dimension_semantics=("parallel", "arbitrary"),
                                             vmem_limit_bytes=VMEM_LIMIT),
    )(*ins)


def _accumulate(ref, val):
    @pl.when(pl.program_id(1) == 0)
    def _():
        ref[...] = jnp.zeros_like(ref)

    ref[0] += jnp.sum(val, axis=0, keepdims=True)


def _modulate_fwd(h, shift, scale, out_dtype):
    def body(h_ref, sh_ref, sc_ref, o_ref):
        hv = h_ref[0]
        r = lax.rsqrt(jnp.mean(hv * hv, axis=-1, keepdims=True) + EPS)
        o_ref[0] = (hv * r * (1.0 + sc_ref[0]) + sh_ref[0]).astype(out_dtype)

    return _group_call(body, "modulate_fwd", [h, shift, scale], ['tok', 'vec', 'vec'], [h.shape], ['tok'],
                       _row_tile(h.shape[1]), [out_dtype])[0]


def _modulate_bwd(h, scale, du):
    def body(h_ref, sc_ref, du_ref, dh_ref, dsh_ref, dsc_ref):
        hv, dv = h_ref[0], du_ref[0]
        r = lax.rsqrt(jnp.mean(hv * hv, axis=-1, keepdims=True) + EPS)
        hn = hv * r
        dn = dv * (1.0 + sc_ref[0])
        dh_ref[0] = r * (dn - hn * jnp.mean(dn * hn, axis=-1, keepdims=True))
        _accumulate(dsh_ref, dv)
        _accumulate(dsc_ref, dv * hn)

    return _group_call(body, "modulate_bwd", [h, scale, du], ['tok', 'vec', 'tok'],
                       [h.shape, scale.shape, scale.shape], ['tok', 'acc', 'acc'], _row_tile(h.shape[1]))


def _rows(t):
    return t.reshape(-1, t.shape[-1])


@functools.partial(jax.custom_vjp, nondiff_argnums=(4,))
def modmm(h, shift, scale, w, out_dtype):
    return _modmm_fwd(h, shift, scale, w, out_dtype)[0]


def _modmm_fwd(h, shift, scale, w, out_dtype):
    u = _modulate_fwd(h, shift, scale, BF16)
    y = matmul(_rows(u), w, out_dtype=out_dtype, name="mm_fwd").reshape(h.shape[:2] + (-1,))
    return y, (h, scale, u, w)


def _modmm_bwd(out_dtype, res, dy):
    h, scale, u, w = res
    du = matmul(_rows(dy), w, tb=True, name="mm_dx").reshape(h.shape)
    dw = matmul(_rows(u), _rows(dy), ta=True, out_dtype=w.dtype, name="mm_dw")
    dh, dsh, dsc = _modulate_bwd(h, scale, du)
    return dh, dsh, dsc, dw


modmm.defvjp(_modmm_fwd, _modmm_bwd)


def _gated_add_call(h, y, gate, coef):
    def body(h_ref, y_ref, g_ref, o_ref):
        o_ref[0] = h_ref[0] + coef * g_ref[0] * y_ref[0]

    return _group_call(body, "gated_add_fwd", [h, y, gate], ['tok', 'tok', 'vec'], [h.shape], ['tok'],
                       _row_tile(h.shape[1]))[0]


def _gated_add_bwd_call(y, gate, dout, coef):
    def body(y_ref, g_ref, d_ref, dy_ref, dg_ref):
        dv = d_ref[0]
        dy_ref[0] = coef * g_ref[0] * dv
        _accumulate(dg_ref, coef * dv * y_ref[0])

    return _group_call(body, "gated_add_bwd", [y, gate, dout], ['tok', 'vec', 'tok'], [y.shape, gate.shape],
                       ['tok', 'acc'], _row_tile(y.shape[1]))


@functools.partial(jax.custom_vjp, nondiff_argnums=(3,))
def gated_add(h, y, gate, coef):
    return _gated_add_call(h, y, gate, coef)


def _gated_add_vjp_fwd(h, y, gate, coef):
    return _gated_add_call(h, y, gate, coef), (y, gate)


def _gated_add_vjp_bwd(coef, res, dout):
    y, gate = res
    dy, dg = _gated_add_bwd_call(y, gate, dout, coef)
    return dout, dy, dg


gated_add.defvjp(_gated_add_vjp_fwd, _gated_add_vjp_bwd)


def _swiglu_fwd(hid):
    f = hid.shape[-1] // 2

    def body(h_ref, o_ref):
        gate, up = h_ref[0, :, 0:f].astype(F32), h_ref[0, :, f:2 * f].astype(F32)
        o_ref[0] = (gate * jax.nn.sigmoid(gate) * up).astype(BF16)

    return _group_call(body, "swiglu_fwd", [hid], ['tok'], [hid.shape[:2] + (f,)], ['tok'],
                       _tile(hid.shape[1], (128, 64, 32, 16)), [BF16])[0]


def _swiglu_bwd(hid, da):
    f = hid.shape[-1] // 2

    def body(h_ref, da_ref, d_ref):
        gate, up, dv = h_ref[0, :, 0:f].astype(F32), h_ref[0, :, f:2 * f].astype(F32), da_ref[0]
        s = jax.nn.sigmoid(gate)
        d_ref[0, :, 0:f] = (dv * up * (s * (1.0 + gate * (1.0 - s)))).astype(hid.dtype)
        d_ref[0, :, f:2 * f] = (dv * (gate * s)).astype(hid.dtype)

    return _group_call(body, "swiglu_bwd", [hid, da], ['tok', 'tok'], [hid.shape], ['tok'],
                       _tile(hid.shape[1], (128, 64, 32, 16)), [hid.dtype])[0]


@jax.custom_vjp
def swiglu_mm(hid, w):
    return _swiglu_mm_fwd(hid, w)[0]


def _swiglu_mm_fwd(hid, w):
    act = _swiglu_fwd(hid)
    y = matmul(_rows(act), w, name="mm_fwd").reshape(hid.shape[:2] + (-1,))
    return y, (hid, act, w)


def _swiglu_mm_bwd(res, dy):
    hid, act, w = res
    da = matmul(_rows(dy), w, tb=True, name="mm_dx").reshape(act.shape)
    dw = matmul(_rows(act), _rows(dy), ta=True, out_dtype=w.dtype, name="mm_dw")
    return _swiglu_bwd(hid, da), dw


swiglu_mm.defvjp(_swiglu_mm_fwd, _swiglu_mm_bwd)


@jax.custom_vjp
def flip_rows(x):
    return _flip_rows_call(x)


def _flip_rows_call(x):
    n, length, c = x.shape
    tb = _tile(length, (256, 128, 64, 32, 16, 8))
    nb = length // tb

    def body(x_ref, o_ref):
        xv = x_ref[0]
        ii = lax.broadcasted_iota(jnp.int32, (tb, tb), 0)
        jj = lax.broadcasted_iota(jnp.int32, (tb, tb), 1)
        rev = (ii + jj == tb - 1).astype(BF16)
        hi = xv.astype(BF16)
        r1 = xv - hi.astype(F32)
        mid = r1.astype(BF16)
        lo = (r1 - mid.astype(F32)).astype(BF16)
        dot = functools.partial(jnp.dot, preferred_element_type=F32)
        o_ref[0] = (dot(rev, hi) + dot(rev, mid)) + dot(rev, lo)

    return pl.pallas_call(
        body, name="flip_rows",
        out_shape=jax.ShapeDtypeStruct(x.shape, F32),
        grid=(n, nb),
        in_specs=[pl.BlockSpec((1, tb, c), lambda i, j: (i, j, 0))],
        out_specs=pl.BlockSpec((1, tb, c), lambda i, j: (i, nb - 1 - j, 0)),
        compiler_params=pltpu.CompilerParams(dimension_semantics=("parallel", "parallel"),
                                             vmem_limit_bytes=VMEM_LIMIT),
    )(x)


flip_rows.defvjp(lambda x: (_flip_rows_call(x), None), lambda _, dy: (_flip_rows_call(dy),))


def _flip_time(t, axis):
    s = t.shape
    lead = math.prod(s[:axis])
    return flip_rows(t.reshape(lead, s[axis], -1)).reshape(s)


def rms_norm(x):
    return x * lax.rsqrt(jnp.mean(x * x, axis=-1, keepdims=True) + EPS)


def raster_to_column(t, rows):
    b, s, d = t.shape
    return t.reshape(b, rows, GRID_W, d).transpose(0, 2, 1, 3).reshape(b, s, d)


def column_to_raster(t, rows):
    b, s, d = t.shape
    return t.reshape(b, GRID_W, rows, d).transpose(0, 2, 1, 3).reshape(b, s, d)


def depthwise_conv(x, w, b):
    pad = A_CONV // 2
    y = lax.conv_general_dilated(x, w[:, None, :], window_strides=(1,), padding=[(pad, pad)],
                                 dimension_numbers=('NWC', 'WIO', 'NWC'), feature_group_count=x.shape[-1])
    return y + b


S5_STATES = B_NGROUPS * B_STATE
S5_ROWS = 8
S5_STEPS_FWD = 64
S5_STEPS_BWD = 32


def _s5_scan_fwd(u2, bd2, cd2, ar8, ai8):
    rows, width = u2.shape
    ns = S5_STATES
    tr = S5_ROWS * S5_STEPS_FWD
    assert rows % tr == 0

    def body(u_ref, bd_ref, cd_ref, ar_ref, ai_ref, y_ref, x_ref, st_ref):
        @pl.when(pl.program_id(0) == 0)
        def _():
            st_ref[...] = jnp.zeros_like(st_ref)

        x_ref[...] = jnp.dot(u_ref[...].astype(BF16), bd_ref[...], preferred_element_type=F32)
        ar, ai = ar_ref[...], ai_ref[...]

        def step(t, carry):
            xr, xi = carry
            r = pl.ds(pl.multiple_of(t * S5_ROWS, S5_ROWS), S5_ROWS)
            nr = ar * xr - ai * xi + x_ref[r, 0:ns]
            ni = ar * xi + ai * xr + x_ref[r, ns:2 * ns]
            x_ref[r, 0:ns] = nr
            x_ref[r, ns:2 * ns] = ni
            return nr, ni

        xr, xi = lax.fori_loop(0, S5_STEPS_FWD, step, (st_ref[:, 0:ns], st_ref[:, ns:2 * ns]), unroll=4)
        st_ref[:, 0:ns] = xr
        st_ref[:, ns:2 * ns] = xi
        y_ref[...] = jnp.dot(x_ref[...].astype(BF16), cd_ref[...], preferred_element_type=F32)

    whole = lambda shape: pl.BlockSpec(shape, lambda i: (0, 0))
    return pl.pallas_call(
        body, name="s5_scan_fwd",
        out_shape=(jax.ShapeDtypeStruct((rows, width), F32), jax.ShapeDtypeStruct((rows, 2 * ns), F32)),
        grid=(rows // tr,),
        in_specs=[pl.BlockSpec((tr, width), lambda i: (i, 0)), whole(bd2.shape), whole(cd2.shape),
                  whole(ar8.shape), whole(ai8.shape)],
        out_specs=(pl.BlockSpec((tr, width), lambda i: (i, 0)), pl.BlockSpec((tr, 2 * ns), lambda i: (i, 0))),
        scratch_shapes=[pltpu.VMEM((S5_ROWS, 2 * ns), F32)],
        compiler_params=pltpu.CompilerParams(dimension_semantics=("arbitrary",), vmem_limit_bytes=VMEM_LIMIT),
    )(u2, bd2, cd2, ar8, ai8)


def _s5_scan_bwd(dy, x, u2, bd2, cd2, ar8, ai8):
    rows, width = u2.shape
    ns = S5_STATES
    steps = S5_STEPS_BWD
    tr = S5_ROWS * steps
    nblk = rows // tr
    assert rows % tr == 0
    nt = (((1,), (1,)), ((), ()))
    tn = (((0,), (0,)), ((), ()))

    def body(dy_ref, x_ref, xp_ref, u_ref, bd_ref, cd_ref, ar_ref, ai_ref,
             du_ref, dbd_ref, dcd_ref, dar_ref, dai_ref, g_ref, st_ref):
        k = pl.program_id(0)

        @pl.when(k == 0)
        def _():
            st_ref[...] = jnp.zeros_like(st_ref)
            dbd_ref[...] = jnp.zeros_like(dbd_ref)
            dcd_ref[...] = jnp.zeros_like(dcd_ref)
            dar_ref[...] = jnp.zeros_like(dar_ref)
            dai_ref[...] = jnp.zeros_like(dai_ref)

        dyb = dy_ref[...].astype(BF16)
        g_ref[...] = lax.dot_general(dyb, cd_ref[...], nt, preferred_element_type=F32)
        ar, ai = ar_ref[...], ai_ref[...]

        def adjoint(r, carry, xpr, xpi):
            gr_n, gi_n, dar, dai = carry
            gr = g_ref[r, 0:ns] + ar * gr_n + ai * gi_n
            gi = g_ref[r, ns:2 * ns] - ai * gr_n + ar * gi_n
            g_ref[r, 0:ns] = gr
            g_ref[r, ns:2 * ns] = gi
            return gr, gi, dar + gr * xpr + gi * xpi, dai + gi * xpr - gr * xpi

        def step(i, carry):
            t = steps - 1 - i
            r = pl.ds(pl.multiple_of(t * S5_ROWS, S5_ROWS), S5_ROWS)
            rp = pl.ds(pl.multiple_of((t - 1) * S5_ROWS, S5_ROWS), S5_ROWS)
            return adjoint(r, carry, x_ref[rp, 0:ns], x_ref[rp, ns:2 * ns])

        zero = jnp.zeros((S5_ROWS, ns), F32)
        carry = lax.fori_loop(0, steps - 1, step, (st_ref[:, 0:ns], st_ref[:, ns:2 * ns], zero, zero), unroll=2)
        has_prev = (k < nblk - 1).astype(F32)
        gr, gi, dar, dai = adjoint(pl.ds(0, S5_ROWS), carry, xp_ref[:, 0:ns] * has_prev, xp_ref[:, ns:2 * ns] * has_prev)
        st_ref[:, 0:ns] = gr
        st_ref[:, ns:2 * ns] = gi
        dar_ref[...] += dar
        dai_ref[...] += dai
        gb = g_ref[...].astype(BF16)
        du_ref[...] = lax.dot_general(gb, bd_ref[...], nt, preferred_element_type=F32)
        dbd_ref[...] += lax.dot_general(u_ref[...].astype(BF16), gb, tn, preferred_element_type=F32)
        dcd_ref[...] += lax.dot_general(x_ref[...].astype(BF16), dyb, tn, preferred_element_type=F32)

    whole = lambda shape: pl.BlockSpec(shape, lambda k: (0, 0))
    rev = lambda k: (nblk - 1 - k, 0)
    prev = lambda k: (jnp.maximum((nblk - 1 - k) * steps - 1, 0), 0)
    return pl.pallas_call(
        body, name="s5_scan_bwd",
        out_shape=(jax.ShapeDtypeStruct((rows, width), F32), jax.ShapeDtypeStruct(bd2.shape, F32),
                   jax.ShapeDtypeStruct(cd2.shape, F32), jax.ShapeDtypeStruct(ar8.shape, F32),
                   jax.ShapeDtypeStruct(ai8.shape, F32)),
        grid=(nblk,),
        in_specs=[pl.BlockSpec((tr, width), rev), pl.BlockSpec((tr, 2 * ns), rev),
                  pl.BlockSpec((S5_ROWS, 2 * ns), prev), pl.BlockSpec((tr, width), rev),
                  whole(bd2.shape), whole(cd2.shape), whole(ar8.shape), whole(ai8.shape)],
        out_specs=(pl.BlockSpec((tr, width), rev), whole(bd2.shape), whole(cd2.shape), whole(ar8.shape),
                   whole(ai8.shape)),
        scratch_shapes=[pltpu.VMEM((tr, 2 * ns), F32), pltpu.VMEM((S5_ROWS, 2 * ns), F32)],
        compiler_params=pltpu.CompilerParams(dimension_semantics=("arbitrary",), vmem_limit_bytes=VMEM_LIMIT),
    )(dy, x, x, u2, bd2, cd2, ar8, ai8)


@jax.custom_vjp
def s5_core(u2, bd2, cd2, ar8, ai8):
    return _s5_scan_fwd(u2, bd2.astype(BF16), cd2.astype(BF16), ar8, ai8)[0]


def _s5_core_fwd(u2, bd2, cd2, ar8, ai8):
    bd2, cd2 = bd2.astype(BF16), cd2.astype(BF16)
    y, x = _s5_scan_fwd(u2, bd2, cd2, ar8, ai8)
    return y, (x, u2, bd2, cd2, ar8, ai8)


def _s5_core_bwd(res, dy):
    return _s5_scan_bwd(dy, *res)


s5_core.defvjp(_s5_core_fwd, _s5_core_bwd)


def s5_mixers(p_ctx, p_lat, lam_re, lam_im, log_step, b_re, b_im, c_re, c_im, d_skip, glu_w, glu_b):
    bsz = p_ctx.shape[0]
    assert 2 * bsz == S5_ROWS
    eye = jnp.eye(B_NGROUPS, dtype=F32)
    bds, cds, ars, ais = [], [], [], []
    for d in range(2):
        step = jnp.exp(log_step[d])[:, None]
        mag = jnp.exp(lam_re[d] * step)
        ar = mag * jnp.cos(lam_im[d] * step)
        ai = mag * jnp.sin(lam_im[d] * step)
        den = lam_re[d] * lam_re[d] + lam_im[d] * lam_im[d]
        nr = ar - 1.0
        kr = (nr * lam_re[d] + ai * lam_im[d]) / den
        ki = (ai * lam_re[d] - nr * lam_im[d]) / den
        br = kr[..., None] * b_re[d] - ki[..., None] * b_im[d]
        bi = kr[..., None] * b_im[d] + ki[..., None] * b_re[d]
        blk = lambda w: jnp.einsum('gnc,gh->gchn', w, eye).reshape(B_WIDTH, S5_STATES)
        bds.append(jnp.concatenate([blk(br), blk(bi)], axis=1))
        blk_c = lambda w: jnp.einsum('gcn,gh->gnhc', w, eye).reshape(S5_STATES, B_WIDTH)
        cds.append(jnp.concatenate([blk_c(c_re[d]), -blk_c(c_im[d])], axis=0))
        ars.append(jnp.broadcast_to(ar.reshape(1, S5_STATES), (bsz, S5_STATES)))
        ais.append(jnp.broadcast_to(ai.reshape(1, S5_STATES), (bsz, S5_STATES)))
    bd2 = jnp.concatenate(bds, axis=0)
    cd2 = jnp.concatenate(cds, axis=1)
    ar8 = jnp.concatenate(ars, axis=0)
    ai8 = jnp.concatenate(ais, axis=0)

    def rows_of(p):
        ut = jnp.swapaxes(p, 0, 1)
        z = jnp.zeros_like(ut)
        return jnp.concatenate([jnp.concatenate([ut, z], axis=-1), jnp.concatenate([z, _flip_time(ut, 0)], axis=-1)], axis=1)

    lc = p_ctx.shape[1]
    u2 = jnp.concatenate([rows_of(p_ctx), rows_of(p_lat)], axis=0)
    y2 = s5_core(u2.reshape(-1, 2 * B_WIDTH), bd2, cd2, ar8, ai8).reshape(u2.shape)

    def finish(y2p, p):
        y = y2p[:, :bsz, :B_WIDTH] + _flip_time(y2p[:, bsz:, B_WIDTH:], 0)
        y = jnp.swapaxes(y, 0, 1) + d_skip * p
        y = jax.nn.gelu(y)
        gate = mm(y.reshape(-1, B_WIDTH), glu_w).reshape(y.shape)
        return y * jax.nn.sigmoid(gate + glu_b)

    return finish(y2[:lc], p_ctx), finish(y2[lc:], p_lat)


GLA_CHUNK = 64
GLA_SUB = 16
NT_DIMS = (((1,), (1,)), ((), ()))
TN_DIMS = (((0,), (0,)), ((), ()))


def _bdot(a, b, dims=(((1,), (0,)), ((), ()))):
    return lax.dot_general(a.astype(BF16), b.astype(BF16), dims, preferred_element_type=F32)


def _hdot(a, b, dims=(((1,), (0,)), ((), ()))):
    ah, bh = a.astype(BF16), b.astype(BF16)
    al, bl = (a - ah.astype(F32)).astype(BF16), (b - bh.astype(F32)).astype(BF16)
    dot = functools.partial(lax.dot_general, dimension_numbers=dims, preferred_element_type=F32)
    return dot(ah, bh) + (dot(ah, bl) + dot(al, bh))


def _sub_block_ref(cum, rows, lo, hi, rev):
    n = cum.shape[0]
    if rev:
        return (cum[hi:hi + 1, :] if hi < n else jnp.zeros_like(cum[0:1, :])), rows >= lo
    return (cum[lo - 1:lo, :] if lo else jnp.zeros_like(cum[0:1, :])), rows < hi


def _chunk_of(step, nc, nc_ctx, rev):
    if not rev:
        return step
    return jnp.where(step < nc_ctx, nc_ctx - 1 - step, nc + nc_ctx - 1 - step)


def _gla_scores(q, k, cum, cumr, tri, rev):
    n = GLA_CHUNK
    if cumr is not None:
        decay = jnp.where(tri, jnp.exp(jnp.where(tri, cum - cumr, 0.0)), 0.0)
        return _bdot(q, k, NT_DIMS) * decay, decay
    rows = lax.broadcasted_iota(jnp.int32, (n, 1), 0)
    parts = []
    for i in range(n // GLA_SUB):
        lo, hi = i * GLA_SUB, (i + 1) * GLA_SUB
        ref, seen = _sub_block_ref(cum, rows, lo, hi, rev)
        qt = q[lo:hi] * jnp.exp(cum[lo:hi] - ref)
        kh = jnp.where(seen, k * jnp.exp(jnp.where(seen, ref - cum, 0.0)), 0.0)
        parts.append(_bdot(qt, kh, NT_DIMS))
    return jnp.where(tri, jnp.concatenate(parts, axis=0), 0.0), None


def _gla_fwd(q, k, cum, cumr, v, rev, nc_ctx):
    bsz, length, width = q.shape
    dk = GLA_CHUNK
    nh = width // dk
    nc = length // GLA_CHUNK
    scalar = cumr is not None

    def body(*refs):
        if scalar:
            q_ref, k_ref, cum_ref, cumr_ref, v_ref, o_ref, s_ref, st_ref = refs
        else:
            q_ref, k_ref, cum_ref, v_ref, o_ref, s_ref, st_ref = refs

        @pl.when(pl.program_id(1) == 0)
        def _():
            st_ref[...] = jnp.zeros_like(st_ref)

        ii = lax.broadcasted_iota(jnp.int32, (GLA_CHUNK, GLA_CHUNK), 0)
        jj = lax.broadcasted_iota(jnp.int32, (GLA_CHUNK, GLA_CHUNK), 1)
        tri = jj >= ii if rev else jj <= ii
        edge = 0 if rev else GLA_CHUNK - 1
        qa, ka, ca, va = q_ref[0], k_ref[0], cum_ref[0], v_ref[0]
        cra = cumr_ref[0] if scalar else None
        outs = []
        for h in range(nh):
            sl = slice(h * dk, (h + 1) * dk)
            qv, kv, cv, vv, st = qa[:, sl], ka[:, sl], ca[:, sl], va[:, sl], st_ref[h]
            s_ref[0, 0, h] = st
            a, _ = _gla_scores(qv, kv, cv, cra[:, sl] if scalar else None, tri, rev)
            outs.append(_bdot(qv * jnp.exp(cv), st, NT_DIMS) + _bdot(a, vv))
            last = cv[edge:edge + 1, :]
            st_ref[h] = st * jnp.exp(last) + _bdot(vv, kv * jnp.exp(last - cv), TN_DIMS)
        o_ref[0] = jnp.concatenate(outs, axis=1)

    seq = pl.BlockSpec((1, GLA_CHUNK, width), lambda n, c: (n, _chunk_of(c, nc, nc_ctx, rev), 0))
    state = pl.BlockSpec((1, 1, nh, dk, dk), lambda n, c: (n, _chunk_of(c, nc, nc_ctx, rev), 0, 0, 0))
    ins = [q, k, cum] + ([cumr] if scalar else []) + [v]
    return pl.pallas_call(
        body, name="gla_fwd_scalar" if scalar else "gla_fwd",
        out_shape=(jax.ShapeDtypeStruct((bsz, length, width), F32), jax.ShapeDtypeStruct((bsz, nc, nh, dk, dk), F32)),
        grid=(bsz, nc),
        in_specs=[seq] * len(ins),
        out_specs=(seq, state),
        scratch_shapes=[pltpu.VMEM((nh, dk, dk), F32)],
        compiler_params=pltpu.CompilerParams(dimension_semantics=("parallel", "arbitrary"),
                                             vmem_limit_bytes=VMEM_LIMIT),
    )(*ins)


def _gla_bwd(do, q, k, cum, cumr, v, states, rev, nc_ctx):
    bsz, length, width = q.shape
    nc = length // GLA_CHUNK
    scalar = cumr is not None
    n = GLA_CHUNK
    dk = GLA_CHUNK
    nh = width // dk

    def body(*refs):
        if scalar:
            do_ref, q_ref, k_ref, cum_ref, cumr_ref, v_ref, s_ref, dq_ref, dk_ref, dc_ref, dcr_ref, dv_ref, dst_ref = refs
        else:
            do_ref, q_ref, k_ref, cum_ref, v_ref, s_ref, dq_ref, dk_ref, dc_ref, dv_ref, dst_ref = refs

        @pl.when(pl.program_id(1) == 0)
        def _():
            dst_ref[...] = jnp.zeros_like(dst_ref)

        ii = lax.broadcasted_iota(jnp.int32, (n, n), 0)
        jj = lax.broadcasted_iota(jnp.int32, (n, n), 1)
        tri = jj >= ii if rev else jj <= ii
        edge = 0 if rev else n - 1
        rows = lax.broadcasted_iota(jnp.int32, (n, 1), 0)
        doa, qa, ka, ca, va = do_ref[0], q_ref[0], k_ref[0], cum_ref[0], v_ref[0]
        cra = cumr_ref[0] if scalar else None
        dqs, dks, dcs, dcrs, dvs = [], [], [], [], []
        for h in range(nh):
            sl = slice(h * dk, (h + 1) * dk)
            dov, qv, kv, cv, vv, st, dst = doa[:, sl], qa[:, sl], ka[:, sl], ca[:, sl], va[:, sl], s_ref[0, 0, h], dst_ref[h]
            e = jnp.exp(cv)
            qe = qv * e
            last = cv[edge:edge + 1, :]
            w = jnp.exp(last - cv)
            kw = kv * w
            el = jnp.exp(last)
            hd = _bdot if scalar else _hdot
            d_qe = hd(dov, st)
            d_kw = hd(vv, dst)
            dv = _bdot(kw, dst, NT_DIMS)
            d_last = jnp.sum(st * dst, axis=0, keepdims=True) * el + jnp.sum(d_kw * kw, axis=0, keepdims=True)
            dst_ref[h] = dst * el + _bdot(dov, qe, TN_DIMS)
            dq = d_qe * e
            dkk = d_kw * w
            dc = d_qe * qe - d_kw * kw + jnp.where(rows == edge, d_last, 0.0)
            da = jnp.where(tri, hd(dov, vv, NT_DIMS), 0.0)
            if scalar:
                a, decay = _gla_scores(qv, kv, cv, cra[:, sl], tri, rev)
                dg = da * decay
                dq = dq + _bdot(dg, kv)
                dkk = dkk + _bdot(dg, qv, TN_DIMS)
                p = da * a
                dc = dc + p
                dcrs.append(-p)
            else:
                a_parts, dq_parts = [], []
                for i in range(n // GLA_SUB):
                    lo, hi = i * GLA_SUB, (i + 1) * GLA_SUB
                    ref, seen = _sub_block_ref(cv, rows, lo, hi, rev)
                    eq = jnp.exp(cv[lo:hi] - ref)
                    qt = qv[lo:hi] * eq
                    ek = jnp.where(seen, jnp.exp(jnp.where(seen, ref - cv, 0.0)), 0.0)
                    kh = kv * ek
                    a_parts.append(_bdot(qt, kh, NT_DIMS))
                    dqt = _hdot(da[lo:hi], kh)
                    dkh = _hdot(da[lo:hi], qt, TN_DIMS)
                    dq_parts.append((dqt * eq, dqt * qt))
                    dkk = dkk + dkh * ek
                    dc = dc - dkh * kh
                a = jnp.where(tri, jnp.concatenate(a_parts, axis=0), 0.0)
                dq = dq + jnp.concatenate([p[0] for p in dq_parts], axis=0)
                dc = dc + jnp.concatenate([p[1] for p in dq_parts], axis=0)
            dvs.append(dv + _bdot(a, dov, TN_DIMS))
            dqs.append(dq)
            dks.append(dkk)
            dcs.append(dc)
        cat = functools.partial(jnp.concatenate, axis=1)
        dq_ref[0], dk_ref[0], dc_ref[0], dv_ref[0] = cat(dqs), cat(dks), cat(dcs), cat(dvs)
        if scalar:
            dcr_ref[0] = cat(dcrs)

    seq = pl.BlockSpec((1, n, width), lambda s, c: (s, _chunk_of(nc - 1 - c, nc, nc_ctx, rev), 0))
    state = pl.BlockSpec((1, 1, nh, dk, dk), lambda s, c: (s, _chunk_of(nc - 1 - c, nc, nc_ctx, rev), 0, 0, 0))
    ins = [do, q, k, cum] + ([cumr] if scalar else []) + [v]
    n_out = 5 if scalar else 4
    return pl.pallas_call(
        body, name="gla_bwd_scalar" if scalar else "gla_bwd",
        out_shape=(jax.ShapeDtypeStruct((bsz, length, width), F32),) * n_out,
        grid=(bsz, nc),
        in_specs=[seq] * len(ins) + [state],
        out_specs=(seq,) * n_out,
        scratch_shapes=[pltpu.VMEM((nh, dk, dk), F32)],
        compiler_params=pltpu.CompilerParams(dimension_semantics=("parallel", "arbitrary"),
                                             vmem_limit_bytes=VMEM_LIMIT),
    )(*ins, states)


@functools.partial(jax.custom_vjp, nondiff_argnums=(4, 5))
def gla(q, k, cum, v, rev, nc_ctx):
    return _gla_fwd(q, k, cum, None, v, rev, nc_ctx)[0]


def _gla_vjp_fwd(q, k, cum, v, rev, nc_ctx):
    o, states = _gla_fwd(q, k, cum, None, v, rev, nc_ctx)
    return o, (q, k, cum, v, states)


def _gla_vjp_bwd(rev, nc_ctx, res, do):
    q, k, cum, v, states = res
    return _gla_bwd(do, q, k, cum, None, v, states, rev, nc_ctx)


gla.defvjp(_gla_vjp_fwd, _gla_vjp_bwd)


@functools.partial(jax.custom_vjp, nondiff_argnums=(5, 6))
def gla_scalar(q, k, cum, cumr, v, rev, nc_ctx):
    return _gla_fwd(q, k, cum, cumr, v, rev, nc_ctx)[0]


def _gla_scalar_vjp_fwd(q, k, cum, cumr, v, rev, nc_ctx):
    o, states = _gla_fwd(q, k, cum, cumr, v, rev, nc_ctx)
    return o, (q, k, cum, cumr, v, states)


def _gla_scalar_vjp_bwd(rev, nc_ctx, res, do):
    q, k, cum, cumr, v, states = res
    return _gla_bwd(do, q, k, cum, cumr, v, states, rev, nc_ctx)


gla_scalar.defvjp(_gla_scalar_vjp_fwd, _gla_scalar_vjp_bwd)


def _chunk_cumsum(g, rev, axis=-2):
    axis = axis % g.ndim
    s = g.shape
    by_chunk = g.reshape(s[:axis] + (s[axis] // GLA_CHUNK, GLA_CHUNK) + s[axis + 1:])
    c = jnp.cumsum(by_chunk, axis=axis + 1)
    if rev:
        c = lax.slice_in_dim(c, GLA_CHUNK - 1, GLA_CHUNK, axis=axis + 1) - c + by_chunk
    return c.reshape(s)


def _both_parts(t_ctx, t_lat):
    return jnp.concatenate([t.reshape(t.shape[:2] + (-1,)) for t in (t_ctx, t_lat)], axis=1)


def _split_parts(o, lc, nh):
    return tuple(t.reshape(t.shape[:2] + (nh, -1)) for t in (o[:, :lc], o[:, lc:]))


def hgrn2_mixers(p_ctx, p_lat, lower, norm_w):
    bsz, lc = p_ctx.shape[:2]
    lower = lower.reshape(2, C_HEADS, C_KEY)

    def heads(p, lo, hi):
        return p[..., lo:hi].reshape(p.shape[:2] + (C_HEADS, -1))

    q_c, q_l = (jax.nn.silu(heads(p, 0, C_WIDTH)) for p in (p_ctx, p_lat))
    v_c, v_l = (heads(p, 3 * C_WIDTH, 4 * C_WIDTH) for p in (p_ctx, p_lat))
    q, v = _both_parts(q_c, q_l), _both_parts(v_c, v_l)
    nc_ctx = lc // GLA_CHUNK
    o = []
    for d in range(2):
        f_c, f_l = (lower[d] + (1.0 - lower[d]) * jax.nn.sigmoid(heads(p, (1 + d) * C_WIDTH, (2 + d) * C_WIDTH))
                    for p in (p_ctx, p_lat))
        cum = jnp.concatenate([_chunk_cumsum(jnp.log(f).reshape(f.shape[:2] + (-1,)), d, axis=1) for f in (f_c, f_l)], axis=1)
        o.append(gla(q, _both_parts(1.0 - f_c, 1.0 - f_l), cum, v, bool(d), nc_ctx))
    f_c, f_l = _split_parts(o[0], lc, C_HEADS)
    b_c, b_l = _split_parts(o[1], lc, C_HEADS)
    outs = []
    for o_sum, p in ((f_c + b_c, p_ctx), (f_l + b_l, p_lat)):
        o_n = rms_norm(o_sum) * norm_w.reshape(C_HEADS, C_VAL)
        outs.append(o_n.reshape(p.shape[:2] + (C_WIDTH,)) * jax.nn.silu(p[..., 4 * C_WIDTH:]))
    return tuple(outs)


def ssd_mixers(p_ctx, p_lat, conv_w, conv_b, dt_bias, a_log, d_skip, norm_w):
    bsz, lc = p_ctx.shape[:2]
    rep = A_HEADS // A_GROUPS
    a = -jnp.exp(a_log)
    xs, bs, cs, dts, zs = [], [], [], [], []
    for p in (p_ctx, p_lat):
        z, xbc, dt_raw = jnp.split(p, [A_INNER, A_INNER + A_CONV_DIM], axis=-1)
        xbc = jax.nn.silu(depthwise_conv(xbc, conv_w, conv_b))
        x_, b_, c_ = jnp.split(xbc, [A_INNER, A_INNER + A_GROUPS * A_STATE], axis=-1)
        shp = p.shape[:2]
        xs.append(x_.reshape(shp + (A_HEADS, A_HEAD_DIM)))
        bs.append(jnp.repeat(b_.reshape(shp + (A_GROUPS, A_STATE)), rep, axis=2))
        cs.append(jnp.repeat(c_.reshape(shp + (A_GROUPS, A_STATE)), rep, axis=2))
        dts.append([jax.nn.softplus(dt_raw[..., d * A_HEADS:(d + 1) * A_HEADS] + dt_bias[d]) for d in range(2)])
        zs.append(z)
    q, v = _both_parts(cs[0], cs[1]), _both_parts(xs[0], xs[1])
    nc_ctx = lc // GLA_CHUNK
    o = []
    for d in range(2):
        k = _both_parts(bs[0] * dts[0][d][..., None], bs[1] * dts[1][d][..., None])
        adt = jnp.concatenate([_chunk_cumsum(dt[d] * a[d], d, axis=1) for dt in dts], axis=1)
        nb, lt = adt.shape[:2]
        cum = jnp.broadcast_to(adt[..., None], (nb, lt, A_HEADS, A_STATE)).reshape(nb, lt, -1)
        along = jnp.swapaxes(adt.reshape(nb, lt // GLA_CHUNK, GLA_CHUNK, A_HEADS), 2, 3)[:, :, None]
        cumr = jnp.broadcast_to(along, (nb, lt // GLA_CHUNK, GLA_CHUNK, A_HEADS, GLA_CHUNK)).reshape(nb, lt, -1)
        o.append(gla_scalar(q, k, cum, cumr, v, bool(d), nc_ctx))
    f_c, f_l = _split_parts(o[0], lc, A_HEADS)
    b_c, b_l = _split_parts(o[1], lc, A_HEADS)
    outs = []
    for y, x_, z in ((f_c + b_c, xs[0], zs[0]), (f_l + b_l, xs[1], zs[1])):
        y = y + d_skip[:, None] * x_
        y = y.reshape(z.shape) * jax.nn.silu(z)
        outs.append(rms_norm(y) * norm_w)
    return tuple(outs)


def token_mixers(p_ctx, p_lat, W, l, lower):
    def cut(p):
        return p[..., :A_COLS], p[..., 1408:1408 + B_COLS], p[..., 1664:1664 + C_COLS]

    pa_c, pb_c, pc_c = cut(p_ctx)
    pa_l, pb_l, pc_l = cut(p_lat)
    ya_c, ya_l = ssd_mixers(pa_c, pa_l, W['a_conv_w'][l], W['a_conv_b'][l], W['a_dt_bias'][l], W['a_log'][l],
                            W['a_d'][l], W['a_norm_w'][l])
    yb_c, yb_l = s5_mixers(pb_c, pb_l, W['s5_lam_re'][l], W['s5_lam_im'][l], W['s5_log_step'][l], W['s5_b_re'][l],
                           W['s5_b_im'][l], W['s5_c_re'][l], W['s5_c_im'][l], W['s5_d'][l], W['s5_glu_w'][l],
                           W['s5_glu_b'][l])
    yc_c, yc_l = hgrn2_mixers(pc_c, pc_l, lower, W['hg_norm_w'][l])
    return (jnp.concatenate([ya_c, yb_c, yc_c], axis=-1), jnp.concatenate([ya_l, yb_l, yc_l], axis=-1))


def _pad_w_in(w):
    z = functools.partial(jnp.zeros, dtype=w.dtype)
    return jnp.concatenate([w[:, :A_COLS], z((D_MODEL, 1408 - A_COLS)), w[:, A_COLS:], z((D_MODEL, IN_PAD - 2944))],
                           axis=1)


def _mm3(t, w):
    g, tt, k = t.shape
    return mm(t.reshape(g * tt, k), w).reshape(g, tt, -1)


def _ffn(h, mg, first, w_in, w_out):
    hid = modmm(h, mg[:, first:first + 1], mg[:, first + 1:first + 2], w_in, BF16)
    return gated_add(h, swiglu_mm(hid, w_out), mg[:, first + 2:first + 3], 0.5)


def local_loss(x, W, m_lat, m_ctx, ctx, target):
    bsz, seq, dm = x.shape
    lc = ctx.shape[1]
    tg = bsz * lc
    assert seq % tg == 0
    gl = seq // tg
    ng = bsz * gl
    rows = seq // GRID_W
    p_lb = jax.nn.softmax(W['hg_lb_logits'], axis=0)
    lower_bounds = jnp.cumsum(p_lb, axis=0) - p_lb[:1]
    h = jnp.concatenate([x.reshape(ng, tg, dm), ctx.reshape(1, tg, dm)], axis=0)
    for l in range(DEPTH):
        last = l == DEPTH - 1
        col_major = l % 2 == 1
        mg = jnp.concatenate([jnp.repeat(m_lat[l], gl, axis=0), m_ctx[l][None]], axis=0)
        h = _ffn(h, mg, 0, W['ffn_w_in'][l][0], W['ffn_w_out'][l][0])
        hp = h
        if col_major:
            h_lat = raster_to_column(h[:ng].reshape(bsz, seq, dm), rows)
            hp = jnp.concatenate([h_lat.reshape(ng, tg, dm), h[ng:]], axis=0)
        p = modmm(hp, mg[:, 3:4], mg[:, 4:5], _pad_w_in(W['w_in'][l]), F32)
        mix_ctx, mix_lat = token_mixers(p[ng].reshape(bsz, lc, -1), p[:ng].reshape(bsz, seq, -1), W, l,
                                        lower_bounds[l])
        if last:
            h, mg = h[:ng], mg[:ng]
            y_lat = _mm3(mix_lat.reshape(ng, tg, dm), W['w_out'][l])
            y_ctx = None
        else:
            y = _mm3(jnp.concatenate([mix_lat.reshape(ng, tg, dm), mix_ctx.reshape(1, tg, dm)], axis=0), W['w_out'][l])
            y_lat, y_ctx = y[:ng], y[ng:]
        if col_major:
            y_lat = column_to_raster(y_lat.reshape(bsz, seq, dm), rows).reshape(ng, tg, dm)
        y = y_lat if y_ctx is None else jnp.concatenate([y_lat, y_ctx], axis=0)
        h = gated_add(h, y, mg[:, 5:6], 1.0)
        h = _ffn(h, mg, 6, W['ffn_w_in'][l][1], W['ffn_w_out'][l][1])
    y = rms_norm(h[:ng].reshape(bsz, seq, dm)) * W['final_norm_w']
    err = jnp.square(y - target)
    return 0.5 * jnp.sum(jnp.mean(err, axis=-1))


def _pad_rows(a, rows):
    return jnp.concatenate([a, jnp.zeros((rows - a.shape[0],) + a.shape[1:], a.dtype)], axis=0)


def kernel(x, c, ctx, c_ctx, mod_w, mod_b, ffn_w_in, ffn_w_out, w_in, w_out, a_conv_w, a_conv_b, a_dt_bias, a_log, a_d, a_norm_w, s5_lam_re, s5_lam_im, s5_log_step, s5_b_re, s5_b_im, s5_c_re, s5_c_im, s5_d, s5_glu_w, s5_glu_b, hg_lb_logits, hg_norm_w, final_norm_w, loss_target, m_c_ctx, m_mod_w, m_mod_b, m_ffn_w_in, m_ffn_w_out, m_w_in, m_w_out, m_a_conv_w, m_a_conv_b, m_a_dt_bias, m_a_log, m_a_d, m_a_norm_w, m_s5_lam_re, m_s5_lam_im, m_s5_log_step, m_s5_b_re, m_s5_b_im, m_s5_c_re, m_s5_c_im, m_s5_d, m_s5_glu_w, m_s5_glu_b, m_hg_lb_logits, m_hg_norm_w, m_final_norm_w, v_c_ctx, v_mod_w, v_mod_b, v_ffn_w_in, v_ffn_w_out, v_w_in, v_w_out, v_a_conv_w, v_a_conv_b, v_a_dt_bias, v_a_log, v_a_d, v_a_norm_w, v_s5_lam_re, v_s5_lam_im, v_s5_log_step, v_s5_b_re, v_s5_b_im, v_s5_c_re, v_s5_c_im, v_s5_d, v_s5_glu_w, v_s5_glu_b, v_hg_lb_logits, v_hg_norm_w, v_final_norm_w):
    given = dict(locals())
    w_loc = {n: given[n] for n in WEIGHTS}
    m_loc = {n: given["m_" + n] for n in WEIGHTS}
    v_loc = {n: given["v_" + n] for n in WEIGHTS}
    bsz = x.shape[0]
    me = 4 * lax.axis_index("x") + 2 * lax.axis_index("y") + lax.axis_index("c")

    small_sh = [c] + [w_loc[n] for n in SMALL_SHARDED]
    g1 = _unpack(all_gather([_pack(small_sh, 128, 8)], "gather_small")[0], [a.shape for a in small_sh])
    c_all = g1[0].reshape(N_DEV * bsz, D_MODEL)
    gathered = dict(zip(BIG, all_gather([w_loc[n].astype(BF16) for n in BIG], "gather_weights")))
    W = {'ffn_w_in': [[_assemble(gathered['ffn_w_in'][:, l, i], 1) for i in range(2)] for l in range(DEPTH)],
         'ffn_w_out': [[_assemble(gathered['ffn_w_out'][:, l, i], 0) for i in range(2)] for l in range(DEPTH)],
         'w_in': [_assemble(gathered['w_in'][:, l], 1) for l in range(DEPTH)],
         'w_out': [_assemble(gathered['w_out'][:, l], 0) for l in range(DEPTH)]}
    for (n, ax), t in zip(SMALL_SHARDED.items(), g1[1:]):
        W[n] = _assemble(t, ax)
    for n in SMALL:
        if n not in SMALL_SHARDED and n not in ('c_ctx', 'mod_b'):
            W[n] = w_loc[n]

    n_rows = N_DEV * bsz + 1
    pad_rows = 8 * ((n_rows + 7) // 8)
    c_rows = _pad_rows(jnp.concatenate([c_all, c_ctx[None]], axis=0), pad_rows)
    sc = jax.nn.silu(c_rows)
    mods_sh = jnp.stack([matmul(sc, mod_w[l], name="mod_fwd") for l in range(DEPTH)])
    mods = _assemble(all_gather([mods_sh], "gather_mods")[0], 2) + mod_b[:, None, :]
    m_lat = lax.dynamic_slice_in_dim(mods, me * bsz, bsz, axis=1).reshape(DEPTH, bsz, N_MOD, D_MODEL)
    m_ctx = mods[:, n_rows - 1].reshape(DEPTH, N_MOD, D_MODEL)

    loss_loc, (grad_x, gW, gm_lat, gm_ctx) = jax.value_and_grad(local_loss, argnums=(0, 1, 2, 3))(
        x, W, m_lat, m_ctx, ctx, loss_target)

    dm_loc = jnp.concatenate([gm_lat.reshape(DEPTH, bsz, -1), gm_ctx.reshape(DEPTH, 1, -1)], axis=1)
    (dm_all,) = all_gather([dm_loc], "gather_dmods")
    dm_ex = jnp.moveaxis(dm_all[:, :, :bsz], 0, 1).reshape(DEPTH, N_DEV * bsz, -1)
    ncol = N_MOD * D_MODEL
    dm_cx = sum_leading(dm_all[:, :, bsz].reshape(N_DEV, DEPTH * ncol // 128, 128), "sum_dmods_ctx")
    dm_cx = dm_cx.reshape(DEPTH, 1, ncol)
    dm_rows = jnp.concatenate([dm_ex, dm_cx, jnp.zeros((DEPTH, pad_rows - n_rows, ncol), F32)], axis=1)
    grad_mod_b = sum_leading(jnp.moveaxis(dm_rows, 1, 0).reshape(pad_rows, DEPTH * ncol // 128, 128),
                             "sum_mod_b").reshape(DEPTH, ncol)
    my_cols = ncol // N_DEV
    dm_mine = lax.dynamic_slice_in_dim(dm_rows, me * my_cols, my_cols, axis=2)
    grad_mod_w = jnp.stack([matmul(sc, dm_mine[l], ta=True, name="mod_dw") for l in range(DEPTH)])
    dm_cx_mine = lax.dynamic_slice_in_dim(dm_cx, me * my_cols, my_cols, axis=2)
    g_sc_ctx = sum(matmul(_pad_rows(dm_cx_mine[l], 8), mod_w[l], tb=True, name="mod_dc")[0] for l in range(DEPTH))

    small_full = [n for n in SMALL if n not in ('c_ctx', 'mod_b')]
    part = [gW[n] for n in small_full] + [g_sc_ctx, loss_loc.reshape(1)]
    red = sum_leading(all_gather([_pack(part, 128, ROW_TILE)], "gather_small_grads")[0], "sum_small_grads")
    red = _unpack(red, [a.shape for a in part])
    grads = dict(zip(small_full, red[:-2]))
    loss = red[-1].reshape(())
    sig = jax.nn.sigmoid(c_ctx)
    grads['c_ctx'] = red[-2] * (sig * (1.0 + c_ctx * (1.0 - sig)))
    grads['mod_b'] = grad_mod_b
    for n, ax in SMALL_SHARDED.items():
        size = w_loc[n].shape[ax]
        grads[n] = lax.dynamic_slice_in_dim(grads[n], me * size, size, axis=ax)
    grads['mod_w'] = grad_mod_w

    by_dev = {'ffn_w_in': jnp.stack([jnp.stack([_split(g, 1) for g in gl], axis=1) for gl in gW['ffn_w_in']], axis=1),
              'ffn_w_out': jnp.stack([jnp.stack([_split(g, 0) for g in gl], axis=1) for gl in gW['ffn_w_out']], axis=1),
              'w_in': jnp.stack([_split(g, 1) for g in gW['w_in']], axis=1),
              'w_out': jnp.stack([_split(g, 0) for g in gW['w_out']], axis=1)}
    for n, t in zip(BIG, all_to_all([by_dev[n] for n in BIG], "exchange_grads")):
        grads[n] = sum_leading(t.reshape((N_DEV,) + _as_2d(t.shape[1:])), "sum_grads").reshape(t.shape[1:])

    delta, new_m, new_v = {}, {}, {}

    for n in list(BIG) + ['mod_w']:
        outs = adamw(*[d[n].reshape(_as_2d(d[n].shape)) for d in (w_loc, grads, m_loc, v_loc)], name="adamw_" + n)
        delta[n], new_m[n], new_v[n] = (o.reshape(w_loc[n].shape) for o in outs)

    def update(names, width, row_mult, tag):
        packed = [_pack([d[n] for n in names], width, row_mult) for d in (w_loc, grads, m_loc, v_loc)]
        outs = adamw(*packed, name="adamw_" + tag)
        shapes = [w_loc[n].shape for n in names]
        for res, out in zip((delta, new_m, new_v), outs):
            res.update(zip(names, _unpack(out, shapes)))

    update(SMALL, 128, 256, "small")
    return (loss, grad_x, *[grads[n] for n in WEIGHTS], *[delta[n] for n in WEIGHTS],
            *[new_m[n] for n in WEIGHTS], *[new_v[n] for n in WEIGHTS])
```

```python
import functools
import math

import jax
import jax.numpy as jnp
from jax import lax
from jax.experimental import pallas as pl
from jax.experimental.pallas import tpu as pltpu

F32 = jnp.float32
BF16 = jnp.bfloat16
N_DEV = 8
MESH_ID = pl.DeviceIdType.MESH
VMEM_LIMIT = 48 * 1024 * 1024

D_MODEL = 1024
DEPTH = 2
GRID_W = 64
EPS = 1e-6
N_MOD = 9
D_FF = 2816
A_INNER = 512
A_HEADS = 8
A_HEAD_DIM = 64
A_GROUPS = 2
A_STATE = 64
A_CONV = 5
A_CONV_DIM = A_INNER + 2 * A_GROUPS * A_STATE
A_COLS = A_INNER + A_CONV_DIM + 2 * A_HEADS
B_WIDTH = 256
B_GROUP = 16
B_NGROUPS = 16
B_STATE = 64
B_COLS = B_WIDTH
C_WIDTH = 256
C_HEADS = 4
C_KEY = 64
C_VAL = 64
C_COLS = 5 * C_WIDTH
IN_PAD = 3072

ADAM_LR = 0.001
ADAM_B1 = 0.9
ADAM_B2 = 0.999
ADAM_EPS = 1e-08
ADAM_WD = 0.01
ADAM_STEP = 10

WEIGHTS = ['c_ctx', 'mod_w', 'mod_b', 'ffn_w_in', 'ffn_w_out', 'w_in', 'w_out', 'a_conv_w', 'a_conv_b', 'a_dt_bias',
           'a_log', 'a_d', 'a_norm_w', 's5_lam_re', 's5_lam_im', 's5_log_step', 's5_b_re', 's5_b_im', 's5_c_re',
           's5_c_im', 's5_d', 's5_glu_w', 's5_glu_b', 'hg_lb_logits', 'hg_norm_w', 'final_norm_w']
BIG = {'ffn_w_in': 3, 'ffn_w_out': 2, 'w_in': 2, 'w_out': 1}
SMALL_SHARDED = {'a_conv_w': 2, 's5_glu_w': 1, 'hg_lb_logits': 2}
SMALL = [n for n in WEIGHTS if n not in BIG and n != 'mod_w']


def _tile(d, prefs):
    for p in prefs:
        if d % p == 0:
            return p
    return d


def _pack(arrs, width, row_mult):
    flat = jnp.concatenate([a.reshape(-1) for a in arrs])
    pad = (-flat.shape[0]) % (width * row_mult)
    if pad:
        flat = jnp.concatenate([flat, jnp.zeros((pad,), flat.dtype)])
    return flat.reshape(-1, width)


def _as_2d(shape):
    return (math.prod(shape[:-1]), shape[-1])


def _unpack(buf, shapes):
    lead = buf.shape[:-2]
    flat = buf.reshape(lead + (-1,))
    out, off = [], 0
    for s in shapes:
        n = math.prod(s)
        out.append(flat[..., off:off + n].reshape(lead + tuple(s)))
        off += n
    return out


def _assemble(g, axis):
    t = jnp.moveaxis(g, 0, axis)
    s = t.shape
    return t.reshape(s[:axis] + (s[axis] * s[axis + 1],) + s[axis + 2:])


def _split(full, axis):
    s = full.shape
    t = full.reshape(s[:axis] + (N_DEV, s[axis] // N_DEV) + s[axis + 1:])
    return jnp.moveaxis(t, axis, 0)


def all_gather(xs, name):
    nt = len(xs)

    def body(*refs):
        x_refs, out_refs = refs[:nt], refs[nt:2 * nt]
        send_sems, recv_sems, local_sems = refs[2 * nt:]
        ax, ay, ac = lax.axis_index("x"), lax.axis_index("y"), lax.axis_index("c")
        me, sibling = (ax, ay, ac), (ax, ay, 1 - ac)
        chips = [(1 - ax, ay), (ax, 1 - ay), (1 - ax, 1 - ay)]

        def slot(t, px, py, pc):
            return out_refs[t].at[4 * px + 2 * py + pc]

        def copy(t, k, block, to, src=None):
            return pltpu.make_async_remote_copy(
                src_ref=slot(t, *block) if src is None else src, dst_ref=slot(t, *block),
                send_sem=send_sems.at[t, k], recv_sem=recv_sems.at[t, k], device_id=to, device_id_type=MESH_ID)

        mine = [pltpu.make_async_copy(x_refs[t], slot(t, *me), local_sems.at[t]) for t in range(nt)]
        for cp in mine:
            cp.start()
        first = []
        for t in range(nt):
            first.append(copy(t, 0, me, sibling, src=x_refs[t]))
            first += [copy(t, 1 + j, me, (*chip, ac), src=x_refs[t]) for j, chip in enumerate(chips)]
        for cp in first:
            cp.start()
        passed = []
        for j, chip in enumerate(chips):
            for t in range(nt):
                copy(t, 1 + j, (*chip, ac), me).wait_recv()
                passed.append(copy(t, 4 + j, (*chip, ac), sibling))
                passed[-1].start()
        for t in range(nt):
            copy(t, 0, sibling, me).wait_recv()
            for j, chip in enumerate(chips):
                copy(t, 4 + j, (*chip, 1 - ac), me).wait_recv()
        for cp in first + passed:
            cp.wait_send()
        for cp in mine:
            cp.wait()

    return pl.pallas_call(
        body, name=name,
        out_shape=tuple(jax.ShapeDtypeStruct((N_DEV,) + x.shape, x.dtype) for x in xs),
        in_specs=[pl.BlockSpec(memory_space=pl.ANY)] * nt,
        out_specs=tuple(pl.BlockSpec(memory_space=pl.ANY) for _ in xs),
        scratch_shapes=[pltpu.SemaphoreType.DMA((nt, 7)), pltpu.SemaphoreType.DMA((nt, 7)),
                        pltpu.SemaphoreType.DMA((nt,))],
    )(*xs)


def all_to_all(gs, name):
    nt = len(gs)

    def body(*refs):
        g_refs, out_refs = refs[:nt], refs[nt:2 * nt]
        send_sems, recv_sems, local_sems = refs[2 * nt:]
        ax, ay, ac = lax.axis_index("x"), lax.axis_index("y"), lax.axis_index("c")
        my = 4 * ax + 2 * ay + ac
        local = [pltpu.make_async_copy(g_refs[t].at[my], out_refs[t].at[my], local_sems.at[t]) for t in range(nt)]
        for cp in local:
            cp.start()
        peers = []
        for r in range(1, N_DEV):
            px = 1 - ax if r & 4 else ax
            py = 1 - ay if r & 2 else ay
            pc = 1 - ac if r & 1 else ac
            peers.append((px, py, pc))

        def copy(t, k, peer):
            return pltpu.make_async_remote_copy(
                src_ref=g_refs[t].at[4 * peer[0] + 2 * peer[1] + peer[2]], dst_ref=out_refs[t].at[my],
                send_sem=send_sems.at[t, k], recv_sem=recv_sems.at[t, k], device_id=peer, device_id_type=MESH_ID)

        def arrival(t, k, peer):
            slot = 4 * peer[0] + 2 * peer[1] + peer[2]
            return pltpu.make_async_remote_copy(
                src_ref=g_refs[t].at[slot], dst_ref=out_refs[t].at[slot],
                send_sem=send_sems.at[t, k], recv_sem=recv_sems.at[t, k], device_id=peer, device_id_type=MESH_ID)

        sends = [copy(t, k, p) for t in range(nt) for k, p in enumerate(peers)]
        for cp in sends:
            cp.start()
        for t in range(nt):
            for k, p in enumerate(peers):
                arrival(t, k, p).wait_recv()
        for cp in sends:
            cp.wait_send()
        for cp in local:
            cp.wait()

    return pl.pallas_call(
        body, name=name,
        out_shape=tuple(jax.ShapeDtypeStruct(g.shape, g.dtype) for g in gs),
        in_specs=[pl.BlockSpec(memory_space=pl.ANY)] * nt,
        out_specs=tuple(pl.BlockSpec(memory_space=pl.ANY) for _ in gs),
        scratch_shapes=[pltpu.SemaphoreType.DMA((nt, 7)), pltpu.SemaphoreType.DMA((nt, 7)),
                        pltpu.SemaphoreType.DMA((nt,))],
    )(*gs)


def pair_exchange(gs, name):
    nt = len(gs)

    def body(*refs):
        g_refs, out_refs = refs[:nt], refs[nt:2 * nt]
        send_sems, recv_sems = refs[2 * nt:]
        ax, ay, ac = lax.axis_index("x"), lax.axis_index("y"), lax.axis_index("c")
        copies = [pltpu.make_async_remote_copy(
            src_ref=g_refs[t].at[2 * j + (1 - ac)], dst_ref=out_refs[t].at[j],
            send_sem=send_sems.at[t, j], recv_sem=recv_sems.at[t, j],
            device_id=(ax, ay, 1 - ac), device_id_type=MESH_ID) for t in range(nt) for j in range(4)]
        for cp in copies:
            cp.start()
        for cp in copies:
            cp.wait()

    return pl.pallas_call(
        body, name=name,
        out_shape=tuple(jax.ShapeDtypeStruct((4,) + g.shape[1:], g.dtype) for g in gs),
        in_specs=[pl.BlockSpec(memory_space=pl.ANY)] * nt,
        out_specs=tuple(pl.BlockSpec(memory_space=pl.ANY) for _ in gs),
        scratch_shapes=[pltpu.SemaphoreType.DMA((nt, 4)), pltpu.SemaphoreType.DMA((nt, 4))],
    )(*gs)


def chip_exchange(ps, name):
    nt = len(ps)

    def body(*refs):
        p_refs, out_refs = refs[:nt], refs[nt:2 * nt]
        send_sems, recv_sems, local_sems = refs[2 * nt:]
        ax, ay, ac = lax.axis_index("x"), lax.axis_index("y"), lax.axis_index("c")
        my = 2 * ax + ay
        local = [pltpu.make_async_copy(p_refs[t].at[my], out_refs[t].at[my], local_sems.at[t]) for t in range(nt)]
        for cp in local:
            cp.start()
        chips = [(1 - ax, ay), (ax, 1 - ay), (1 - ax, 1 - ay)]

        def copy(t, k, chip, slot):
            return pltpu.make_async_remote_copy(
                src_ref=p_refs[t].at[2 * chip[0] + chip[1]], dst_ref=out_refs[t].at[slot],
                send_sem=send_sems.at[t, k], recv_sem=recv_sems.at[t, k], device_id=(*chip, ac), device_id_type=MESH_ID)

        sends = [copy(t, k, chip, my) for t in range(nt) for k, chip in enumerate(chips)]
        for cp in sends:
            cp.start()
        for t in range(nt):
            for k, chip in enumerate(chips):
                copy(t, k, chip, 2 * chip[0] + chip[1]).wait_recv()
        for cp in sends:
            cp.wait_send()
        for cp in local:
            cp.wait()

    return pl.pallas_call(
        body, name=name,
        out_shape=tuple(jax.ShapeDtypeStruct(p.shape, p.dtype) for p in ps),
        in_specs=[pl.BlockSpec(memory_space=pl.ANY)] * nt,
        out_specs=tuple(pl.BlockSpec(memory_space=pl.ANY) for _ in ps),
        scratch_shapes=[pltpu.SemaphoreType.DMA((nt, 3)), pltpu.SemaphoreType.DMA((nt, 3)),
                        pltpu.SemaphoreType.DMA((nt,))],
    )(*ps)


def matmul(a, b, *, ta=False, tb=False, out_dtype=F32, name="mm"):
    m, k = (a.shape[1], a.shape[0]) if ta else a.shape
    n = b.shape[0] if tb else b.shape[1]
    assert (b.shape[1] if tb else b.shape[0]) == k, (a.shape, b.shape, ta, tb)
    tm = _tile(m, (1024, 512, 256, 128))
    tn = _tile(n, (1408, 1024, 512, 384, 256, 128))
    tk = _tile(k, (1408, 1024, 512, 256, 128))
    nk = k // tk
    dims = (((0 if ta else 1,), (1 if tb else 0,)), ((), ()))

    def body(a_ref, b_ref, o_ref, acc_ref):
        step = pl.program_id(2)

        @pl.when(step == 0)
        def _():
            acc_ref[...] = jnp.zeros_like(acc_ref)

        acc_ref[...] += lax.dot_general(a_ref[...].astype(BF16), b_ref[...].astype(BF16), dims,
                                        preferred_element_type=F32)

        @pl.when(step == nk - 1)
        def _():
            o_ref[...] = acc_ref[...].astype(out_dtype)

    a_spec = pl.BlockSpec((tk, tm), lambda i, j, s: (s, i)) if ta else pl.BlockSpec((tm, tk), lambda i, j, s: (i, s))
    b_spec = pl.BlockSpec((tn, tk), lambda i, j, s: (j, s)) if tb else pl.BlockSpec((tk, tn), lambda i, j, s: (s, j))
    return pl.pallas_call(
        body, name=name,
        out_shape=jax.ShapeDtypeStruct((m, n), out_dtype),
        grid=(m // tm, n // tn, nk),
        in_specs=[a_spec, b_spec],
        out_specs=pl.BlockSpec((tm, tn), lambda i, j, s: (i, j)),
        scratch_shapes=[pltpu.VMEM((tm, tn), F32)],
        compiler_params=pltpu.CompilerParams(dimension_semantics=("parallel", "parallel", "arbitrary"),
                                             vmem_limit_bytes=VMEM_LIMIT),
    )(a, b)


@jax.custom_vjp
def mm(x, w):
    return matmul(x, w, name="mm_fwd")


def _mm_fwd(x, w):
    return matmul(x, w, name="mm_fwd"), (x, w)


def _mm_bwd(res, dy):
    x, w = res
    dx = matmul(dy, w, tb=True, out_dtype=x.dtype, name="mm_dx")
    dw = matmul(x, dy, ta=True, out_dtype=w.dtype, name="mm_dw")
    return dx, dw


mm.defvjp(_mm_fwd, _mm_bwd)


def sum_leading(x, name):
    n, r, c = x.shape
    tr = _tile(r, (256, 128, 64, 32, 16, 8))

    def body(x_ref, o_ref):
        acc = x_ref[0].astype(F32)
        for i in range(1, n):
            acc = acc + x_ref[i].astype(F32)
        o_ref[...] = acc

    return pl.pallas_call(
        body, name=name,
        out_shape=jax.ShapeDtypeStruct((r, c), F32),
        grid=(r // tr,),
        in_specs=[pl.BlockSpec((n, tr, c), lambda i: (0, i, 0))],
        out_specs=pl.BlockSpec((tr, c), lambda i: (i, 0)),
        compiler_params=pltpu.CompilerParams(dimension_semantics=("parallel",), vmem_limit_bytes=VMEM_LIMIT),
    )(x)


def add_pairs(a, b, name):
    n, r, c = a.shape
    tr = _tile(r, (256, 128, 64, 32, 16))

    def body(a_ref, b_ref, o_ref):
        o_ref[...] = (a_ref[...].astype(F32) + b_ref[...].astype(F32)).astype(a.dtype)

    spec = pl.BlockSpec((1, tr, c), lambda i, j: (i, j, 0))
    return pl.pallas_call(
        body, name=name,
        out_shape=jax.ShapeDtypeStruct(a.shape, a.dtype),
        grid=(n, r // tr),
        in_specs=[spec, spec],
        out_specs=spec,
        compiler_params=pltpu.CompilerParams(dimension_semantics=("parallel", "parallel"),
                                             vmem_limit_bytes=VMEM_LIMIT),
    )(a, b)


def adamw(w, g, m, v, name):
    r, c = w.shape
    tr = _tile(r, (256, 128, 64, 32, 16, 8))

    def body(w_ref, g_ref, m_ref, v_ref, d_ref, mo_ref, vo_ref):
        gv = g_ref[...]
        mv = ADAM_B1 * m_ref[...] + (1.0 - ADAM_B1) * gv
        vv = ADAM_B2 * v_ref[...] + (1.0 - ADAM_B2) * jnp.square(gv)
        m_hat = mv / (1.0 - ADAM_B1 ** ADAM_STEP)
        v_hat = vv / (1.0 - ADAM_B2 ** ADAM_STEP)
        d_ref[...] = -ADAM_LR * (m_hat / (jnp.sqrt(v_hat) + ADAM_EPS) + ADAM_WD * w_ref[...])
        mo_ref[...] = mv
        vo_ref[...] = vv

    spec = pl.BlockSpec((tr, c), lambda i: (i, 0))
    return pl.pallas_call(
        body, name=name,
        out_shape=(jax.ShapeDtypeStruct((r, c), F32),) * 3,
        grid=(r // tr,),
        in_specs=[spec] * 4,
        out_specs=(spec,) * 3,
        compiler_params=pltpu.CompilerParams(dimension_semantics=("parallel",), vmem_limit_bytes=VMEM_LIMIT),
    )(w, g, m, v)


ROW_TILE = 256


def _row_tile(t):
    return _tile(t, (ROW_TILE, 128, 64, 32, 16, 8))


def _group_call(body, name, ins, in_kinds, out_shapes, out_kinds, tt, out_dtypes=None):
    g, t = ins[0].shape[:2]

    def spec(kind, shape):
        if kind == 'tok':
            return pl.BlockSpec((1, tt, shape[-1]), lambda i, j: (i, j, 0))
        return pl.BlockSpec((1, 1, shape[-1]), lambda i, j: (i, 0, 0))

    return pl.pallas_call(
        body, name=name,
        out_shape=tuple(jax.ShapeDtypeStruct(s, d) for s, d in zip(out_shapes, out_dtypes or [F32] * len(out_shapes))),
        grid=(g, t // tt),
        in_specs=[spec(k, a.shape) for k, a in zip(in_kinds, ins)],
        out_specs=tuple(spec(k, s) for k, s in zip(out_kinds, out_shapes)),
        compiler_params=pltpu.CompilerParams(dimension_semantics=("parallel", "arbitrary"),
                                             vmem_limit_bytes=VMEM_LIMIT),
    )(*ins)


def _accumulate(ref, val):
    @pl.when(pl.program_id(1) == 0)
    def _():
        ref[...] = jnp.zeros_like(ref)

    ref[0] += jnp.sum(val, axis=0, keepdims=True)


def _modulate_fwd(h, shift, scale, out_dtype):
    def body(h_ref, sh_ref, sc_ref, o_ref):
        hv = h_ref[0]
        r = lax.rsqrt(jnp.mean(hv * hv, axis=-1, keepdims=True) + EPS)
        o_ref[0] = (hv * r * (1.0 + sc_ref[0]) + sh_ref[0]).astype(out_dtype)

    return _group_call(body, "modulate_fwd", [h, shift, scale], ['tok', 'vec', 'vec'], [h.shape], ['tok'],
                       _row_tile(h.shape[1]), [out_dtype])[0]


def _modulate_bwd(h, scale, du):
    def body(h_ref, sc_ref, du_ref, dh_ref, dsh_ref, dsc_ref):
        hv, dv = h_ref[0], du_ref[0]
        r = lax.rsqrt(jnp.mean(hv * hv, axis=-1, keepdims=True) + EPS)
        hn = hv * r
        dn = dv * (1.0 + sc_ref[0])
        dh_ref[0] = r * (dn - hn * jnp.mean(dn * hn, axis=-1, keepdims=True))
        _accumulate(dsh_ref, dv)
        _accumulate(dsc_ref, dv * hn)

    return _group_call(body, "modulate_bwd", [h, scale, du], ['tok', 'vec', 'tok'],
                       [h.shape, scale.shape, scale.shape], ['tok', 'acc', 'acc'], _row_tile(h.shape[1]))


def _rows(t):
    return t.reshape(-1, t.shape[-1])


@functools.partial(jax.custom_vjp, nondiff_argnums=(4,))
def modmm(h, shift, scale, w, out_dtype):
    return _modmm_fwd(h, shift, scale, w, out_dtype)[0]


def _modmm_fwd(h, shift, scale, w, out_dtype):
    u = _modulate_fwd(h, shift, scale, BF16)
    y = matmul(_rows(u), w, out_dtype=out_dtype, name="mm_fwd").reshape(h.shape[:2] + (-1,))
    return y, (h, scale, u, w)


def _modmm_bwd(out_dtype, res, dy):
    h, scale, u, w = res
    du = matmul(_rows(dy), w, tb=True, name="mm_dx").reshape(h.shape)
    dw = matmul(_rows(u), _rows(dy), ta=True, out_dtype=w.dtype, name="mm_dw")
    dh, dsh, dsc = _modulate_bwd(h, scale, du)
    return dh, dsh, dsc, dw


modmm.defvjp(_modmm_fwd, _modmm_bwd)


def _gated_add_call(h, y, gate, coef):
    def body(h_ref, y_ref, g_ref, o_ref):
        o_ref[0] = h_ref[0] + coef * g_ref[0] * y_ref[0]

    return _group_call(body, "gated_add_fwd", [h, y, gate], ['tok', 'tok', 'vec'], [h.shape], ['tok'],
                       _row_tile(h.shape[1]))[0]


def _gated_add_bwd_call(y, gate, dout, coef):
    def body(y_ref, g_ref, d_ref, dy_ref, dg_ref):
        dv = d_ref[0]
        dy_ref[0] = coef * g_ref[0] * dv
        _accumulate(dg_ref, coef * dv * y_ref[0])

    return _group_call(body, "gated_add_bwd", [y, gate, dout], ['tok', 'vec', 'tok'], [y.shape, gate.shape],
                       ['tok', 'acc'], _row_tile(y.shape[1]))


@functools.partial(jax.custom_vjp, nondiff_argnums=(3,))
def gated_add(h, y, gate, coef):
    return _gated_add_call(h, y, gate, coef)


def _gated_add_vjp_fwd(h, y, gate, coef):
    return _gated_add_call(h, y, gate, coef), (y, gate)


def _gated_add_vjp_bwd(coef, res, dout):
    y, gate = res
    dy, dg = _gated_add_bwd_call(y, gate, dout, coef)
    return dout, dy, dg


gated_add.defvjp(_gated_add_vjp_fwd, _gated_add_vjp_bwd)


def _swiglu_fwd(hid):
    f = hid.shape[-1] // 2

    def body(h_ref, o_ref):
        gate, up = h_ref[0, :, 0:f].astype(F32), h_ref[0, :, f:2 * f].astype(F32)
        o_ref[0] = (gate * jax.nn.sigmoid(gate) * up).astype(BF16)

    return _group_call(body, "swiglu_fwd", [hid], ['tok'], [hid.shape[:2] + (f,)], ['tok'],
                       _tile(hid.shape[1], (128, 64, 32, 16)), [BF16])[0]


def _swiglu_bwd(hid, da):
    f = hid.shape[-1] // 2

    def body(h_ref, da_ref, d_ref):
        gate, up, dv = h_ref[0, :, 0:f].astype(F32), h_ref[0, :, f:2 * f].astype(F32), da_ref[0]
        s = jax.nn.sigmoid(gate)
        d_ref[0, :, 0:f] = (dv * up * (s * (1.0 + gate * (1.0 - s)))).astype(hid.dtype)
        d_ref[0, :, f:2 * f] = (dv * (gate * s)).astype(hid.dtype)

    return _group_call(body, "swiglu_bwd", [hid, da], ['tok', 'tok'], [hid.shape], ['tok'],
                       _tile(hid.shape[1], (128, 64, 32, 16)), [hid.dtype])[0]


@jax.custom_vjp
def swiglu_mm(hid, w):
    return _swiglu_mm_fwd(hid, w)[0]


def _swiglu_mm_fwd(hid, w):
    act = _swiglu_fwd(hid)
    y = matmul(_rows(act), w, name="mm_fwd").reshape(hid.shape[:2] + (-1,))
    return y, (hid, act, w)


def _swiglu_mm_bwd(res, dy):
    hid, act, w = res
    da = matmul(_rows(dy), w, tb=True, name="mm_dx").reshape(act.shape)
    dw = matmul(_rows(act), _rows(dy), ta=True, out_dtype=w.dtype, name="mm_dw")
    return _swiglu_bwd(hid, da), dw


swiglu_mm.defvjp(_swiglu_mm_fwd, _swiglu_mm_bwd)


@jax.custom_vjp
def flip_rows(x):
    return _flip_rows_call(x)


def _flip_rows_call(x):
    n, length, c = x.shape
    tb = _tile(length, (256, 128, 64, 32, 16, 8))
    nb = length // tb

    def body(x_ref, o_ref):
        xv = x_ref[0]
        ii = lax.broadcasted_iota(jnp.int32, (tb, tb), 0)
        jj = lax.broadcasted_iota(jnp.int32, (tb, tb), 1)
        rev = (ii + jj == tb - 1).astype(BF16)
        hi = xv.astype(BF16)
        r1 = xv - hi.astype(F32)
        mid = r1.astype(BF16)
        lo = (r1 - mid.astype(F32)).astype(BF16)
        dot = functools.partial(jnp.dot, preferred_element_type=F32)
        o_ref[0] = (dot(rev, hi) + dot(rev, mid)) + dot(rev, lo)

    return pl.pallas_call(
        body, name="flip_rows",
        out_shape=jax.ShapeDtypeStruct(x.shape, F32),
        grid=(n, nb),
        in_specs=[pl.BlockSpec((1, tb, c), lambda i, j: (i, j, 0))],
        out_specs=pl.BlockSpec((1, tb, c), lambda i, j: (i, nb - 1 - j, 0)),
        compiler_params=pltpu.CompilerParams(dimension_semantics=("parallel", "parallel"),
                                             vmem_limit_bytes=VMEM_LIMIT),
    )(x)


flip_rows.defvjp(lambda x: (_flip_rows_call(x), None), lambda _, dy: (_flip_rows_call(dy),))


def _flip_time(t, axis):
    s = t.shape
    lead = math.prod(s[:axis])
    return flip_rows(t.reshape(lead, s[axis], -1)).reshape(s)


def rms_norm(x):
    return x * lax.rsqrt(jnp.mean(x * x, axis=-1, keepdims=True) + EPS)


def raster_to_column(t, rows):
    b, s, d = t.shape
    return t.reshape(b, rows, GRID_W, d).transpose(0, 2, 1, 3).reshape(b, s, d)


def column_to_raster(t, rows):
    b, s, d = t.shape
    return t.reshape(b, GRID_W, rows, d).transpose(0, 2, 1, 3).reshape(b, s, d)


def depthwise_conv(x, w, b):
    pad = A_CONV // 2
    y = lax.conv_general_dilated(x, w[:, None, :], window_strides=(1,), padding=[(pad, pad)],
                                 dimension_numbers=('NWC', 'WIO', 'NWC'), feature_group_count=x.shape[-1])
    return y + b


S5_STATES = B_NGROUPS * B_STATE
S5_ROWS = 8
S5_STEPS_FWD = 64
S5_STEPS_BWD = 32


def _s5_scan_fwd(u2, bd2, cd2, ar8, ai8):
    rows, width = u2.shape
    ns = S5_STATES
    tr = S5_ROWS * S5_STEPS_FWD
    assert rows % tr == 0

    def body(u_ref, bd_ref, cd_ref, ar_ref, ai_ref, y_ref, x_ref, st_ref):
        @pl.when(pl.program_id(0) == 0)
        def _():
            st_ref[...] = jnp.zeros_like(st_ref)

        x_ref[...] = jnp.dot(u_ref[...].astype(BF16), bd_ref[...], preferred_element_type=F32)
        ar, ai = ar_ref[...], ai_ref[...]

        def step(t, carry):
            xr, xi = carry
            r = pl.ds(pl.multiple_of(t * S5_ROWS, S5_ROWS), S5_ROWS)
            nr = ar * xr - ai * xi + x_ref[r, 0:ns]
            ni = ar * xi + ai * xr + x_ref[r, ns:2 * ns]
            x_ref[r, 0:ns] = nr
            x_ref[r, ns:2 * ns] = ni
            return nr, ni

        xr, xi = lax.fori_loop(0, S5_STEPS_FWD, step, (st_ref[:, 0:ns], st_ref[:, ns:2 * ns]), unroll=4)
        st_ref[:, 0:ns] = xr
        st_ref[:, ns:2 * ns] = xi
        y_ref[...] = jnp.dot(x_ref[...].astype(BF16), cd_ref[...], preferred_element_type=F32)

    whole = lambda shape: pl.BlockSpec(shape, lambda i: (0, 0))
    return pl.pallas_call(
        body, name="s5_scan_fwd",
        out_shape=(jax.ShapeDtypeStruct((rows, width), F32), jax.ShapeDtypeStruct((rows, 2 * ns), F32)),
        grid=(rows // tr,),
        in_specs=[pl.BlockSpec((tr, width), lambda i: (i, 0)), whole(bd2.shape), whole(cd2.shape),
                  whole(ar8.shape), whole(ai8.shape)],
        out_specs=(pl.BlockSpec((tr, width), lambda i: (i, 0)), pl.BlockSpec((tr, 2 * ns), lambda i: (i, 0))),
        scratch_shapes=[pltpu.VMEM((S5_ROWS, 2 * ns), F32)],
        compiler_params=pltpu.CompilerParams(dimension_semantics=("arbitrary",), vmem_limit_bytes=VMEM_LIMIT),
    )(u2, bd2, cd2, ar8, ai8)


def _s5_scan_bwd(dy, x, u2, bd2, cd2, ar8, ai8):
    rows, width = u2.shape
    ns = S5_STATES
    steps = S5_STEPS_BWD
    tr = S5_ROWS * steps
    nblk = rows // tr
    assert rows % tr == 0
    nt = (((1,), (1,)), ((), ()))
    tn = (((0,), (0,)), ((), ()))

    def body(dy_ref, x_ref, xp_ref, u_ref, bd_ref, cd_ref, ar_ref, ai_ref,
             du_ref, dbd_ref, dcd_ref, dar_ref, dai_ref, g_ref, st_ref):
        k = pl.program_id(0)

        @pl.when(k == 0)
        def _():
            st_ref[...] = jnp.zeros_like(st_ref)
            dbd_ref[...] = jnp.zeros_like(dbd_ref)
            dcd_ref[...] = jnp.zeros_like(dcd_ref)
            dar_ref[...] = jnp.zeros_like(dar_ref)
            dai_ref[...] = jnp.zeros_like(dai_ref)

        dyb = dy_ref[...].astype(BF16)
        g_ref[...] = lax.dot_general(dyb, cd_ref[...], nt, preferred_element_type=F32)
        ar, ai = ar_ref[...], ai_ref[...]

        def adjoint(r, carry, xpr, xpi):
            gr_n, gi_n, dar, dai = carry
            gr = g_ref[r, 0:ns] + ar * gr_n + ai * gi_n
            gi = g_ref[r, ns:2 * ns] - ai * gr_n + ar * gi_n
            g_ref[r, 0:ns] = gr
            g_ref[r, ns:2 * ns] = gi
            return gr, gi, dar + gr * xpr + gi * xpi, dai + gi * xpr - gr * xpi

        def step(i, carry):
            t = steps - 1 - i
            r = pl.ds(pl.multiple_of(t * S5_ROWS, S5_ROWS), S5_ROWS)
            rp = pl.ds(pl.multiple_of((t - 1) * S5_ROWS, S5_ROWS), S5_ROWS)
            return adjoint(r, carry, x_ref[rp, 0:ns], x_ref[rp, ns:2 * ns])

        zero = jnp.zeros((S5_ROWS, ns), F32)
        carry = lax.fori_loop(0, steps - 1, step, (st_ref[:, 0:ns], st_ref[:, ns:2 * ns], zero, zero), unroll=2)
        has_prev = (k < nblk - 1).astype(F32)
        gr, gi, dar, dai = adjoint(pl.ds(0, S5_ROWS), carry, xp_ref[:, 0:ns] * has_prev, xp_ref[:, ns:2 * ns] * has_prev)
        st_ref[:, 0:ns] = gr
        st_ref[:, ns:2 * ns] = gi
        dar_ref[...] += dar
        dai_ref[...] += dai
        gb = g_ref[...].astype(BF16)
        du_ref[...] = lax.dot_general(gb, bd_ref[...], nt, preferred_element_type=F32)
        dbd_ref[...] += lax.dot_general(u_ref[...].astype(BF16), gb, tn, preferred_element_type=F32)
        dcd_ref[...] += lax.dot_general(x_ref[...].astype(BF16), dyb, tn, preferred_element_type=F32)

    whole = lambda shape: pl.BlockSpec(shape, lambda k: (0, 0))
    rev = lambda k: (nblk - 1 - k, 0)
    prev = lambda k: (jnp.maximum((nblk - 1 - k) * steps - 1, 0), 0)
    return pl.pallas_call(
        body, name="s5_scan_bwd",
        out_shape=(jax.ShapeDtypeStruct((rows, width), F32), jax.ShapeDtypeStruct(bd2.shape, F32),
                   jax.ShapeDtypeStruct(cd2.shape, F32), jax.ShapeDtypeStruct(ar8.shape, F32),
                   jax.ShapeDtypeStruct(ai8.shape, F32)),
        grid=(nblk,),
        in_specs=[pl.BlockSpec((tr, width), rev), pl.BlockSpec((tr, 2 * ns), rev),
                  pl.BlockSpec((S5_ROWS, 2 * ns), prev), pl.BlockSpec((tr, width), rev),
                  whole(bd2.shape), whole(cd2.shape), whole(ar8.shape), whole(ai8.shape)],
        out_specs=(pl.BlockSpec((tr, width), rev), whole(bd2.shape), whole(cd2.shape), whole(ar8.shape),
                   whole(ai8.shape)),
        scratch_shapes=[pltpu.VMEM((tr, 2 * ns), F32), pltpu.VMEM((S5_ROWS, 2 * ns), F32)],
        compiler_params=pltpu.CompilerParams(dimension_semantics=("arbitrary",), vmem_limit_bytes=VMEM_LIMIT),
    )(dy, x, x, u2, bd2, cd2, ar8, ai8)


@jax.custom_vjp
def s5_core(u2, bd2, cd2, ar8, ai8):
    return _s5_scan_fwd(u2, bd2.astype(BF16), cd2.astype(BF16), ar8, ai8)[0]


def _s5_core_fwd(u2, bd2, cd2, ar8, ai8):
    bd2, cd2 = bd2.astype(BF16), cd2.astype(BF16)
    y, x = _s5_scan_fwd(u2, bd2, cd2, ar8, ai8)
    return y, (x, u2, bd2, cd2, ar8, ai8)


def _s5_core_bwd(res, dy):
    return _s5_scan_bwd(dy, *res)


s5_core.defvjp(_s5_core_fwd, _s5_core_bwd)


def s5_mixers(p_ctx, p_lat, lam_re, lam_im, log_step, b_re, b_im, c_re, c_im, d_skip, glu_w, glu_b):
    bsz = p_ctx.shape[0]
    assert 2 * bsz == S5_ROWS
    eye = jnp.eye(B_NGROUPS, dtype=F32)
    bds, cds, ars, ais = [], [], [], []
    for d in range(2):
        step = jnp.exp(log_step[d])[:, None]
        mag = jnp.exp(lam_re[d] * step)
        ar = mag * jnp.cos(lam_im[d] * step)
        ai = mag * jnp.sin(lam_im[d] * step)
        den = lam_re[d] * lam_re[d] + lam_im[d] * lam_im[d]
        nr = ar - 1.0
        kr = (nr * lam_re[d] + ai * lam_im[d]) / den
        ki = (ai * lam_re[d] - nr * lam_im[d]) / den
        br = kr[..., None] * b_re[d] - ki[..., None] * b_im[d]
        bi = kr[..., None] * b_im[d] + ki[..., None] * b_re[d]
        blk = lambda w: jnp.einsum('gnc,gh->gchn', w, eye).reshape(B_WIDTH, S5_STATES)
        bds.append(jnp.concatenate([blk(br), blk(bi)], axis=1))
        blk_c = lambda w: jnp.einsum('gcn,gh->gnhc', w, eye).reshape(S5_STATES, B_WIDTH)
        cds.append(jnp.concatenate([blk_c(c_re[d]), -blk_c(c_im[d])], axis=0))
        ars.append(jnp.broadcast_to(ar.reshape(1, S5_STATES), (bsz, S5_STATES)))
        ais.append(jnp.broadcast_to(ai.reshape(1, S5_STATES), (bsz, S5_STATES)))
    bd2 = jnp.concatenate(bds, axis=0)
    cd2 = jnp.concatenate(cds, axis=1)
    ar8 = jnp.concatenate(ars, axis=0)
    ai8 = jnp.concatenate(ais, axis=0)

    def rows_of(p):
        ut = jnp.swapaxes(p, 0, 1)
        z = jnp.zeros_like(ut)
        return jnp.concatenate([jnp.concatenate([ut, z], axis=-1), jnp.concatenate([z, _flip_time(ut, 0)], axis=-1)], axis=1)

    lc = p_ctx.shape[1]
    u2 = jnp.concatenate([rows_of(p_ctx), rows_of(p_lat)], axis=0)
    y2 = s5_core(u2.reshape(-1, 2 * B_WIDTH), bd2, cd2, ar8, ai8).reshape(u2.shape)

    def finish(y2p, p):
        y = y2p[:, :bsz, :B_WIDTH] + _flip_time(y2p[:, bsz:, B_WIDTH:], 0)
        y = jnp.swapaxes(y, 0, 1) + d_skip * p
        y = jax.nn.gelu(y)
        gate = mm(y.reshape(-1, B_WIDTH), glu_w).reshape(y.shape)
        return y * jax.nn.sigmoid(gate + glu_b)

    return finish(y2[:lc], p_ctx), finish(y2[lc:], p_lat)


GLA_CHUNK = 64
GLA_SUB = 16
NT_DIMS = (((1,), (1,)), ((), ()))
TN_DIMS = (((0,), (0,)), ((), ()))


def _bdot(a, b, dims=(((1,), (0,)), ((), ()))):
    return lax.dot_general(a.astype(BF16), b.astype(BF16), dims, preferred_element_type=F32)


def _hdot(a, b, dims=(((1,), (0,)), ((), ()))):
    ah, bh = a.astype(BF16), b.astype(BF16)
    al, bl = (a - ah.astype(F32)).astype(BF16), (b - bh.astype(F32)).astype(BF16)
    dot = functools.partial(lax.dot_general, dimension_numbers=dims, preferred_element_type=F32)
    return dot(ah, bh) + (dot(ah, bl) + dot(al, bh))


def _sub_block_ref(cum, rows, lo, hi, rev):
    n = cum.shape[0]
    if rev:
        return (cum[hi:hi + 1, :] if hi < n else jnp.zeros_like(cum[0:1, :])), rows >= lo
    return (cum[lo - 1:lo, :] if lo else jnp.zeros_like(cum[0:1, :])), rows < hi


def _chunk_of(step, nc, nc_ctx, rev):
    if not rev:
        return step
    return jnp.where(step < nc_ctx, nc_ctx - 1 - step, nc + nc_ctx - 1 - step)


def _gla_scores(q, k, cum, cumr, tri, rev):
    n = GLA_CHUNK
    if cumr is not None:
        decay = jnp.where(tri, jnp.exp(jnp.where(tri, cum - cumr, 0.0)), 0.0)
        return _bdot(q, k, NT_DIMS) * decay, decay
    rows = lax.broadcasted_iota(jnp.int32, (n, 1), 0)
    parts = []
    for i in range(n // GLA_SUB):
        lo, hi = i * GLA_SUB, (i + 1) * GLA_SUB
        ref, seen = _sub_block_ref(cum, rows, lo, hi, rev)
        qt = q[lo:hi] * jnp.exp(cum[lo:hi] - ref)
        kh = jnp.where(seen, k * jnp.exp(jnp.where(seen, ref - cum, 0.0)), 0.0)
        parts.append(_bdot(qt, kh, NT_DIMS))
    return jnp.where(tri, jnp.concatenate(parts, axis=0), 0.0), None


def _gla_fwd(q, k, cum, cumr, v, rev, nc_ctx):
    bsz, length, width = q.shape
    dk = GLA_CHUNK
    nh = width // dk
    nc = length // GLA_CHUNK
    scalar = cumr is not None

    def body(*refs):
        if scalar:
            q_ref, k_ref, cum_ref, cumr_ref, v_ref, o_ref, s_ref, st_ref = refs
        else:
            q_ref, k_ref, cum_ref, v_ref, o_ref, s_ref, st_ref = refs

        @pl.when(pl.program_id(1) == 0)
        def _():
            st_ref[...] = jnp.zeros_like(st_ref)

        ii = lax.broadcasted_iota(jnp.int32, (GLA_CHUNK, GLA_CHUNK), 0)
        jj = lax.broadcasted_iota(jnp.int32, (GLA_CHUNK, GLA_CHUNK), 1)
        tri = jj >= ii if rev else jj <= ii
        edge = 0 if rev else GLA_CHUNK - 1
        qa, ka, ca, va = q_ref[0], k_ref[0], cum_ref[0], v_ref[0]
        cra = cumr_ref[0] if scalar else None
        outs = []
        for h in range(nh):
            sl = slice(h * dk, (h + 1) * dk)
            qv, kv, cv, vv, st = qa[:, sl], ka[:, sl], ca[:, sl], va[:, sl], st_ref[h]
            s_ref[0, 0, h] = st
            a, _ = _gla_scores(qv, kv, cv, cra[:, sl] if scalar else None, tri, rev)
            outs.append(_bdot(qv * jnp.exp(cv), st, NT_DIMS) + _bdot(a, vv))
            last = cv[edge:edge + 1, :]
            st_ref[h] = st * jnp.exp(last) + _bdot(vv, kv * jnp.exp(last - cv), TN_DIMS)
        o_ref[0] = jnp.concatenate(outs, axis=1)

    seq = pl.BlockSpec((1, GLA_CHUNK, width), lambda n, c: (n, _chunk_of(c, nc, nc_ctx, rev), 0))
    state = pl.BlockSpec((1, 1, nh, dk, dk), lambda n, c: (n, _chunk_of(c, nc, nc_ctx, rev), 0, 0, 0))
    ins = [q, k, cum] + ([cumr] if scalar else []) + [v]
    return pl.pallas_call(
        body, name="gla_fwd_scalar" if scalar else "gla_fwd",
        out_shape=(jax.ShapeDtypeStruct((bsz, length, width), F32), jax.ShapeDtypeStruct((bsz, nc, nh, dk, dk), F32)),
        grid=(bsz, nc),
        in_specs=[seq] * len(ins),
        out_specs=(seq, state),
        scratch_shapes=[pltpu.VMEM((nh, dk, dk), F32)],
        compiler_params=pltpu.CompilerParams(dimension_semantics=("parallel", "arbitrary"),
                                             vmem_limit_bytes=VMEM_LIMIT),
    )(*ins)


def _gla_bwd(do, q, k, cum, cumr, v, states, rev, nc_ctx):
    bsz, length, width = q.shape
    nc = length // GLA_CHUNK
    scalar = cumr is not None
    n = GLA_CHUNK
    dk = GLA_CHUNK
    nh = width // dk

    def body(*refs):
        if scalar:
            do_ref, q_ref, k_ref, cum_ref, cumr_ref, v_ref, s_ref, dq_ref, dk_ref, dc_ref, dcr_ref, dv_ref, dst_ref = refs
        else:
            do_ref, q_ref, k_ref, cum_ref, v_ref, s_ref, dq_ref, dk_ref, dc_ref, dv_ref, dst_ref = refs

        @pl.when(pl.program_id(1) == 0)
        def _():
            dst_ref[...] = jnp.zeros_like(dst_ref)

        ii = lax.broadcasted_iota(jnp.int32, (n, n), 0)
        jj = lax.broadcasted_iota(jnp.int32, (n, n), 1)
        tri = jj >= ii if rev else jj <= ii
        edge = 0 if rev else n - 1
        rows = lax.broadcasted_iota(jnp.int32, (n, 1), 0)
        doa, qa, ka, ca, va = do_ref[0], q_ref[0], k_ref[0], cum_ref[0], v_ref[0]
        cra = cumr_ref[0] if scalar else None
        dqs, dks, dcs, dcrs, dvs = [], [], [], [], []
        for h in range(nh):
            sl = slice(h * dk, (h + 1) * dk)
            dov, qv, kv, cv, vv, st, dst = doa[:, sl], qa[:, sl], ka[:, sl], ca[:, sl], va[:, sl], s_ref[0, 0, h], dst_ref[h]
            e = jnp.exp(cv)
            qe = qv * e
            last = cv[edge:edge + 1, :]
            w = jnp.exp(last - cv)
            kw = kv * w
            el = jnp.exp(last)
            hd = _bdot if scalar else _hdot
            d_qe = hd(dov, st)
            d_kw = hd(vv, dst)
            dv = _bdot(kw, dst, NT_DIMS)
            d_last = jnp.sum(st * dst, axis=0, keepdims=True) * el + jnp.sum(d_kw * kw, axis=0, keepdims=True)
            dst_ref[h] = dst * el + _bdot(dov, qe, TN_DIMS)
            dq = d_qe * e
            dkk = d_kw * w
            dc = d_qe * qe - d_kw * kw + jnp.where(rows == edge, d_last, 0.0)
            da = jnp.where(tri, hd(dov, vv, NT_DIMS), 0.0)
            if scalar:
                a, decay = _gla_scores(qv, kv, cv, cra[:, sl], tri, rev)
                dg = da * decay
                dq = dq + _bdot(dg, kv)
                dkk = dkk + _bdot(dg, qv, TN_DIMS)
                p = da * a
                dc = dc + p
                dcrs.append(-p)
            else:
                a_parts, dq_parts = [], []
                for i in range(n // GLA_SUB):
                    lo, hi = i * GLA_SUB, (i + 1) * GLA_SUB
                    ref, seen = _sub_block_ref(cv, rows, lo, hi, rev)
                    eq = jnp.exp(cv[lo:hi] - ref)
                    qt = qv[lo:hi] * eq
                    ek = jnp.where(seen, jnp.exp(jnp.where(seen, ref - cv, 0.0)), 0.0)
                    kh = kv * ek
                    a_parts.append(_bdot(qt, kh, NT_DIMS))
                    dqt = _hdot(da[lo:hi], kh)
                    dkh = _hdot(da[lo:hi], qt, TN_DIMS)
                    dq_parts.append((dqt * eq, dqt * qt))
                    dkk = dkk + dkh * ek
                    dc = dc - dkh * kh
                a = jnp.where(tri, jnp.concatenate(a_parts, axis=0), 0.0)
                dq = dq + jnp.concatenate([p[0] for p in dq_parts], axis=0)
                dc = dc + jnp.concatenate([p[1] for p in dq_parts], axis=0)
            dvs.append(dv + _bdot(a, dov, TN_DIMS))
            dqs.append(dq)
            dks.append(dkk)
            dcs.append(dc)
        cat = functools.partial(jnp.concatenate, axis=1)
        dq_ref[0], dk_ref[0], dc_ref[0], dv_ref[0] = cat(dqs), cat(dks), cat(dcs), cat(dvs)
        if scalar:
            dcr_ref[0] = cat(dcrs)

    seq = pl.BlockSpec((1, n, width), lambda s, c: (s, _chunk_of(nc - 1 - c, nc, nc_ctx, rev), 0))
    state = pl.BlockSpec((1, 1, nh, dk, dk), lambda s, c: (s, _chunk_of(nc - 1 - c, nc, nc_ctx, rev), 0, 0, 0))
    ins = [do, q, k, cum] + ([cumr] if scalar else []) + [v]
    n_out = 5 if scalar else 4
    return pl.pallas_call(
        body, name="gla_bwd_scalar" if scalar else "gla_bwd",
        out_shape=(jax.ShapeDtypeStruct((bsz, length, width), F32),) * n_out,
        grid=(bsz, nc),
        in_specs=[seq] * len(ins) + [state],
        out_specs=(seq,) * n_out,
        scratch_shapes=[pltpu.VMEM((nh, dk, dk), F32)],
        compiler_params=pltpu.CompilerParams(dimension_semantics=("parallel", "arbitrary"),
                                             vmem_limit_bytes=VMEM_LIMIT),
    )(*ins, states)


@functools.partial(jax.custom_vjp, nondiff_argnums=(4, 5))
def gla(q, k, cum, v, rev, nc_ctx):
    return _gla_fwd(q, k, cum, None, v, rev, nc_ctx)[0]


def _gla_vjp_fwd(q, k, cum, v, rev, nc_ctx):
    o, states = _gla_fwd(q, k, cum, None, v, rev, nc_ctx)
    return o, (q, k, cum, v, states)


def _gla_vjp_bwd(rev, nc_ctx, res, do):
    q, k, cum, v, states = res
    return _gla_bwd(do, q, k, cum, None, v, states, rev, nc_ctx)


gla.defvjp(_gla_vjp_fwd, _gla_vjp_bwd)


@functools.partial(jax.custom_vjp, nondiff_argnums=(5, 6))
def gla_scalar(q, k, cum, cumr, v, rev, nc_ctx):
    return _gla_fwd(q, k, cum, cumr, v, rev, nc_ctx)[0]


def _gla_scalar_vjp_fwd(q, k, cum, cumr, v, rev, nc_ctx):
    o, states = _gla_fwd(q, k, cum, cumr, v, rev, nc_ctx)
    return o, (q, k, cum, cumr, v, states)


def _gla_scalar_vjp_bwd(rev, nc_ctx, res, do):
    q, k, cum, cumr, v, states = res
    return _gla_bwd(do, q, k, cum, cumr, v, states, rev, nc_ctx)


gla_scalar.defvjp(_gla_scalar_vjp_fwd, _gla_scalar_vjp_bwd)


def _chunk_cumsum(g, rev, axis=-2):
    axis = axis % g.ndim
    s = g.shape
    by_chunk = g.reshape(s[:axis] + (s[axis] // GLA_CHUNK, GLA_CHUNK) + s[axis + 1:])
    c = jnp.cumsum(by_chunk, axis=axis + 1)
    if rev:
        c = lax.slice_in_dim(c, GLA_CHUNK - 1, GLA_CHUNK, axis=axis + 1) - c + by_chunk
    return c.reshape(s)


def _both_parts(t_ctx, t_lat):
    return jnp.concatenate([t.reshape(t.shape[:2] + (-1,)) for t in (t_ctx, t_lat)], axis=1)


def _split_parts(o, lc, nh):
    return tuple(t.reshape(t.shape[:2] + (nh, -1)) for t in (o[:, :lc], o[:, lc:]))


def hgrn2_mixers(p_ctx, p_lat, lower, norm_w):
    bsz, lc = p_ctx.shape[:2]
    lower = lower.reshape(2, C_HEADS, C_KEY)

    def heads(p, lo, hi):
        return p[..., lo:hi].reshape(p.shape[:2] + (C_HEADS, -1))

    q_c, q_l = (jax.nn.silu(heads(p, 0, C_WIDTH)) for p in (p_ctx, p_lat))
    v_c, v_l = (heads(p, 3 * C_WIDTH, 4 * C_WIDTH) for p in (p_ctx, p_lat))
    q, v = _both_parts(q_c, q_l), _both_parts(v_c, v_l)
    nc_ctx = lc // GLA_CHUNK
    o = []
    for d in range(2):
        f_c, f_l = (lower[d] + (1.0 - lower[d]) * jax.nn.sigmoid(heads(p, (1 + d) * C_WIDTH, (2 + d) * C_WIDTH))
                    for p in (p_ctx, p_lat))
        cum = jnp.concatenate([_chunk_cumsum(jnp.log(f).reshape(f.shape[:2] + (-1,)), d, axis=1) for f in (f_c, f_l)], axis=1)
        o.append(gla(q, _both_parts(1.0 - f_c, 1.0 - f_l), cum, v, bool(d), nc_ctx))
    f_c, f_l = _split_parts(o[0], lc, C_HEADS)
    b_c, b_l = _split_parts(o[1], lc, C_HEADS)
    outs = []
    for o_sum, p in ((f_c + b_c, p_ctx), (f_l + b_l, p_lat)):
        o_n = rms_norm(o_sum) * norm_w.reshape(C_HEADS, C_VAL)
        outs.append(o_n.reshape(p.shape[:2] + (C_WIDTH,)) * jax.nn.silu(p[..., 4 * C_WIDTH:]))
    return tuple(outs)


def ssd_mixers(p_ctx, p_lat, conv_w, conv_b, dt_bias, a_log, d_skip, norm_w):
    bsz, lc = p_ctx.shape[:2]
    rep = A_HEADS // A_GROUPS
    a = -jnp.exp(a_log)
    xs, bs, cs, dts, zs = [], [], [], [], []
    for p in (p_ctx, p_lat):
        z, xbc, dt_raw = jnp.split(p, [A_INNER, A_INNER + A_CONV_DIM], axis=-1)
        xbc = jax.nn.silu(depthwise_conv(xbc, conv_w, conv_b))
        x_, b_, c_ = jnp.split(xbc, [A_INNER, A_INNER + A_GROUPS * A_STATE], axis=-1)
        shp = p.shape[:2]
        xs.append(x_.reshape(shp + (A_HEADS, A_HEAD_DIM)))
        bs.append(jnp.repeat(b_.reshape(shp + (A_GROUPS, A_STATE)), rep, axis=2))
        cs.append(jnp.repeat(c_.reshape(shp + (A_GROUPS, A_STATE)), rep, axis=2))
        dts.append([jax.nn.softplus(dt_raw[..., d * A_HEADS:(d + 1) * A_HEADS] + dt_bias[d]) for d in range(2)])
        zs.append(z)
    q, v = _both_parts(cs[0], cs[1]), _both_parts(xs[0], xs[1])
    nc_ctx = lc // GLA_CHUNK
    o = []
    for d in range(2):
        k = _both_parts(bs[0] * dts[0][d][..., None], bs[1] * dts[1][d][..., None])
        adt = jnp.concatenate([_chunk_cumsum(dt[d] * a[d], d, axis=1) for dt in dts], axis=1)
        nb, lt = adt.shape[:2]
        cum = jnp.broadcast_to(adt[..., None], (nb, lt, A_HEADS, A_STATE)).reshape(nb, lt, -1)
        along = jnp.swapaxes(adt.reshape(nb, lt // GLA_CHUNK, GLA_CHUNK, A_HEADS), 2, 3)[:, :, None]
        cumr = jnp.broadcast_to(along, (nb, lt // GLA_CHUNK, GLA_CHUNK, A_HEADS, GLA_CHUNK)).reshape(nb, lt, -1)
        o.append(gla_scalar(q, k, cum, cumr, v, bool(d), nc_ctx))
    f_c, f_l = _split_parts(o[0], lc, A_HEADS)
    b_c, b_l = _split_parts(o[1], lc, A_HEADS)
    outs = []
    for y, x_, z in ((f_c + b_c, xs[0], zs[0]), (f_l + b_l, xs[1], zs[1])):
        y = y + d_skip[:, None] * x_
        y = y.reshape(z.shape) * jax.nn.silu(z)
        outs.append(rms_norm(y) * norm_w)
    return tuple(outs)


def token_mixers(p_ctx, p_lat, W, l, lower):
    def cut(p):
        return p[..., :A_COLS], p[..., 1408:1408 + B_COLS], p[..., 1664:1664 + C_COLS]

    pa_c, pb_c, pc_c = cut(p_ctx)
    pa_l, pb_l, pc_l = cut(p_lat)
    ya_c, ya_l = ssd_mixers(pa_c, pa_l, W['a_conv_w'][l], W['a_conv_b'][l], W['a_dt_bias'][l], W['a_log'][l],
                            W['a_d'][l], W['a_norm_w'][l])
    yb_c, yb_l = s5_mixers(pb_c, pb_l, W['s5_lam_re'][l], W['s5_lam_im'][l], W['s5_log_step'][l], W['s5_b_re'][l],
                           W['s5_b_im'][l], W['s5_c_re'][l], W['s5_c_im'][l], W['s5_d'][l], W['s5_glu_w'][l],
                           W['s5_glu_b'][l])
    yc_c, yc_l = hgrn2_mixers(pc_c, pc_l, lower, W['hg_norm_w'][l])
    return (jnp.concatenate([ya_c, yb_c, yc_c], axis=-1), jnp.concatenate([ya_l, yb_l, yc_l], axis=-1))


def _pad_w_in(w):
    z = functools.partial(jnp.zeros, dtype=w.dtype)
    return jnp.concatenate([w[:, :A_COLS], z((D_MODEL, 1408 - A_COLS)), w[:, A_COLS:], z((D_MODEL, IN_PAD - 2944))],
                           axis=1)


def _mm3(t, w):
    g, tt, k = t.shape
    return mm(t.reshape(g * tt, k), w).reshape(g, tt, -1)


def _ffn(h, mg, first, w_in, w_out):
    hid = modmm(h, mg[:, first:first + 1], mg[:, first + 1:first + 2], w_in, BF16)
    return gated_add(h, swiglu_mm(hid, w_out), mg[:, first + 2:first + 3], 0.5)


def local_loss(x, W, m_lat, m_ctx, ctx, target):
    bsz, seq, dm = x.shape
    lc = ctx.shape[1]
    tg = bsz * lc
    assert seq % tg == 0
    gl = seq // tg
    ng = bsz * gl
    rows = seq // GRID_W
    p_lb = jax.nn.softmax(W['hg_lb_logits'], axis=0)
    lower_bounds = jnp.cumsum(p_lb, axis=0) - p_lb[:1]
    h = jnp.concatenate([x.reshape(ng, tg, dm), ctx.reshape(1, tg, dm)], axis=0)
    for l in range(DEPTH):
        last = l == DEPTH - 1
        col_major = l % 2 == 1
        mg = jnp.concatenate([jnp.repeat(m_lat[l], gl, axis=0), m_ctx[l][None]], axis=0)
        h = _ffn(h, mg, 0, W['ffn_w_in'][l][0], W['ffn_w_out'][l][0])
        hp = h
        if col_major:
            h_lat = raster_to_column(h[:ng].reshape(bsz, seq, dm), rows)
            hp = jnp.concatenate([h_lat.reshape(ng, tg, dm), h[ng:]], axis=0)
        p = modmm(hp, mg[:, 3:4], mg[:, 4:5], _pad_w_in(W['w_in'][l]), F32)
        mix_ctx, mix_lat = token_mixers(p[ng].reshape(bsz, lc, -1), p[:ng].reshape(bsz, seq, -1), W, l,
                                        lower_bounds[l])
        if last:
            h, mg = h[:ng], mg[:ng]
            y_lat = _mm3(mix_lat.reshape(ng, tg, dm), W['w_out'][l])
            y_ctx = None
        else:
            y = _mm3(jnp.concatenate([mix_lat.reshape(ng, tg, dm), mix_ctx.reshape(1, tg, dm)], axis=0), W['w_out'][l])
            y_lat, y_ctx = y[:ng], y[ng:]
        if col_major:
            y_lat = column_to_raster(y_lat.reshape(bsz, seq, dm), rows).reshape(ng, tg, dm)
        y = y_lat if y_ctx is None else jnp.concatenate([y_lat, y_ctx], axis=0)
        h = gated_add(h, y, mg[:, 5:6], 1.0)
        h = _ffn(h, mg, 6, W['ffn_w_in'][l][1], W['ffn_w_out'][l][1])
    y = rms_norm(h[:ng].reshape(bsz, seq, dm)) * W['final_norm_w']
    err = jnp.square(y - target)
    return 0.5 * jnp.sum(jnp.mean(err, axis=-1))


def _pad_rows(a, rows):
    return jnp.concatenate([a, jnp.zeros((rows - a.shape[0],) + a.shape[1:], a.dtype)], axis=0)


def kernel(x, c, ctx, c_ctx, mod_w, mod_b, ffn_w_in, ffn_w_out, w_in, w_out, a_conv_w, a_conv_b, a_dt_bias, a_log, a_d, a_norm_w, s5_lam_re, s5_lam_im, s5_log_step, s5_b_re, s5_b_im, s5_c_re, s5_c_im, s5_d, s5_glu_w, s5_glu_b, hg_lb_logits, hg_norm_w, final_norm_w, loss_target, m_c_ctx, m_mod_w, m_mod_b, m_ffn_w_in, m_ffn_w_out, m_w_in, m_w_out, m_a_conv_w, m_a_conv_b, m_a_dt_bias, m_a_log, m_a_d, m_a_norm_w, m_s5_lam_re, m_s5_lam_im, m_s5_log_step, m_s5_b_re, m_s5_b_im, m_s5_c_re, m_s5_c_im, m_s5_d, m_s5_glu_w, m_s5_glu_b, m_hg_lb_logits, m_hg_norm_w, m_final_norm_w, v_c_ctx, v_mod_w, v_mod_b, v_ffn_w_in, v_ffn_w_out, v_w_in, v_w_out, v_a_conv_w, v_a_conv_b, v_a_dt_bias, v_a_log, v_a_d, v_a_norm_w, v_s5_lam_re, v_s5_lam_im, v_s5_log_step, v_s5_b_re, v_s5_b_im, v_s5_c_re, v_s5_c_im, v_s5_d, v_s5_glu_w, v_s5_glu_b, v_hg_lb_logits, v_hg_norm_w, v_final_norm_w):
    given = dict(locals())
    w_loc = {n: given[n] for n in WEIGHTS}
    m_loc = {n: given["m_" + n] for n in WEIGHTS}
    v_loc = {n: given["v_" + n] for n in WEIGHTS}
    bsz = x.shape[0]
    me = 4 * lax.axis_index("x") + 2 * lax.axis_index("y") + lax.axis_index("c")

    small_sh = [c] + [w_loc[n] for n in SMALL_SHARDED]
    g1 = _unpack(all_gather([_pack(small_sh, 128, 8)], "gather_small")[0], [a.shape for a in small_sh])
    c_all = g1[0].reshape(N_DEV * bsz, D_MODEL)
    gathered = dict(zip(BIG, all_gather([w_loc[n].astype(BF16) for n in BIG], "gather_weights")))
    W = {'ffn_w_in': [[_assemble(gathered['ffn_w_in'][:, l, i], 1) for i in range(2)] for l in range(DEPTH)],
         'ffn_w_out': [[_assemble(gathered['ffn_w_out'][:, l, i], 0) for i in range(2)] for l in range(DEPTH)],
         'w_in': [_assemble(gathered['w_in'][:, l], 1) for l in range(DEPTH)],
         'w_out': [_assemble(gathered['w_out'][:, l], 0) for l in range(DEPTH)]}
    for (n, ax), t in zip(SMALL_SHARDED.items(), g1[1:]):
        W[n] = _assemble(t, ax)
    for n in SMALL:
        if n not in SMALL_SHARDED and n not in ('c_ctx', 'mod_b'):
            W[n] = w_loc[n]

    n_rows = N_DEV * bsz + 1
    pad_rows = 8 * ((n_rows + 7) // 8)
    c_rows = _pad_rows(jnp.concatenate([c_all, c_ctx[None]], axis=0), pad_rows)
    sc = jax.nn.silu(c_rows)
    mods_sh = jnp.stack([matmul(sc, mod_w[l], name="mod_fwd") for l in range(DEPTH)])
    mods = _assemble(all_gather([mods_sh], "gather_mods")[0], 2) + mod_b[:, None, :]
    m_lat = lax.dynamic_slice_in_dim(mods, me * bsz, bsz, axis=1).reshape(DEPTH, bsz, N_MOD, D_MODEL)
    m_ctx = mods[:, n_rows - 1].reshape(DEPTH, N_MOD, D_MODEL)

    loss_loc, (grad_x, gW, gm_lat, gm_ctx) = jax.value_and_grad(local_loss, argnums=(0, 1, 2, 3))(
        x, W, m_lat, m_ctx, ctx, loss_target)

    dm_loc = jnp.concatenate([gm_lat.reshape(DEPTH, bsz, -1), gm_ctx.reshape(DEPTH, 1, -1)], axis=1)
    (dm_all,) = all_gather([dm_loc], "gather_dmods")
    dm_ex = jnp.moveaxis(dm_all[:, :, :bsz], 0, 1).reshape(DEPTH, N_DEV * bsz, -1)
    ncol = N_MOD * D_MODEL
    dm_cx = sum_leading(dm_all[:, :, bsz].reshape(N_DEV, DEPTH * ncol // 128, 128), "sum_dmods_ctx")
    dm_cx = dm_cx.reshape(DEPTH, 1, ncol)
    dm_rows = jnp.concatenate([dm_ex, dm_cx, jnp.zeros((DEPTH, pad_rows - n_rows, ncol), F32)], axis=1)
    grad_mod_b = sum_leading(jnp.moveaxis(dm_rows, 1, 0).reshape(pad_rows, DEPTH * ncol // 128, 128),
                             "sum_mod_b").reshape(DEPTH, ncol)
    my_cols = ncol // N_DEV
    dm_mine = lax.dynamic_slice_in_dim(dm_rows, me * my_cols, my_cols, axis=2)
    grad_mod_w = jnp.stack([matmul(sc, dm_mine[l], ta=True, name="mod_dw") for l in range(DEPTH)])
    dm_cx_mine = lax.dynamic_slice_in_dim(dm_cx, me * my_cols, my_cols, axis=2)
    g_sc_ctx = sum(matmul(_pad_rows(dm_cx_mine[l], 8), mod_w[l], tb=True, name="mod_dc")[0] for l in range(DEPTH))

    small_full = [n for n in SMALL if n not in ('c_ctx', 'mod_b')]
    part = [gW[n] for n in small_full] + [g_sc_ctx, loss_loc.reshape(1)]
    red = sum_leading(all_gather([_pack(part, 128, ROW_TILE)], "gather_small_grads")[0], "sum_small_grads")
    red = _unpack(red, [a.shape for a in part])
    grads = dict(zip(small_full, red[:-2]))
    loss = red[-1].reshape(())
    sig = jax.nn.sigmoid(c_ctx)
    grads['c_ctx'] = red[-2] * (sig * (1.0 + c_ctx * (1.0 - sig)))
    grads['mod_b'] = grad_mod_b
    for n, ax in SMALL_SHARDED.items():
        size = w_loc[n].shape[ax]
        grads[n] = lax.dynamic_slice_in_dim(grads[n], me * size, size, axis=ax)
    grads['mod_w'] = grad_mod_w

    by_dev = {'ffn_w_in': jnp.stack([jnp.stack([_split(g, 1) for g in gl], axis=1) for gl in gW['ffn_w_in']], axis=1),
              'ffn_w_out': jnp.stack([jnp.stack([_split(g, 0) for g in gl], axis=1) for gl in gW['ffn_w_out']], axis=1),
              'w_in': jnp.stack([_split(g, 1) for g in gW['w_in']], axis=1),
              'w_out': jnp.stack([_split(g, 0) for g in gW['w_out']], axis=1)}
    ac = lax.axis_index("c")
    sends = [by_dev[n] for n in BIG]
    from_sibling = pair_exchange(sends, "exchange_grads_pair")
    pair = []
    for g, r in zip(sends, from_sibling):
        mine = lax.dynamic_index_in_dim(g.reshape((4, 2) + g.shape[1:]), ac, axis=1, keepdims=False)
        shape3 = (4,) + _as_2d(g.shape[1:])
        pair.append(add_pairs(mine.reshape(shape3), r.reshape(shape3), "sum_grads_pair").reshape(r.shape))
    for n, t in zip(BIG, chip_exchange(pair, "exchange_grads_chips")):
        grads[n] = sum_leading(t.reshape((4,) + _as_2d(t.shape[1:])), "sum_grads").reshape(t.shape[1:])

    delta, new_m, new_v = {}, {}, {}

    for n in list(BIG) + ['mod_w']:
        outs = adamw(*[d[n].reshape(_as_2d(d[n].shape)) for d in (w_loc, grads, m_loc, v_loc)], name="adamw_" + n)
        delta[n], new_m[n], new_v[n] = (o.reshape(w_loc[n].shape) for o in outs)

    def update(names, width, row_mult, tag):
        packed = [_pack([d[n] for n in names], width, row_mult) for d in (w_loc, grads, m_loc, v_loc)]
        outs = adamw(*packed, name="adamw_" + tag)
        shapes = [w_loc[n].shape for n in names]
        for res, out in zip((delta, new_m, new_v), outs):
            res.update(zip(names, _unpack(out, shapes)))

    update(SMALL, 128, 256, "small")
    return (loss, grad_x, *[grads[n] for n in WEIGHTS], *[delta[n] for n in WEIGHTS],
            *[new_m[n] for n in WEIGHTS], *[new_v[n] for n in WEIGHTS])
```

```python
import functools
import math

import jax
import jax.numpy as jnp
from jax import lax
from jax.experimental import pallas as pl
from jax.experimental.pallas import tpu as pltpu

F32 = jnp.float32
BF16 = jnp.bfloat16
N_DEV = 8
MESH_ID = pl.DeviceIdType.MESH
VMEM_LIMIT = 48 * 1024 * 1024

D_MODEL = 1024
DEPTH = 2
GRID_W = 64
EPS = 1e-6
N_MOD = 9
D_FF = 2816
A_INNER = 512
A_HEADS = 8
A_HEAD_DIM = 64
A_GROUPS = 2
A_STATE = 64
A_CONV = 5
A_CONV_DIM = A_INNER + 2 * A_GROUPS * A_STATE
A_COLS = A_INNER + A_CONV_DIM + 2 * A_HEADS
B_WIDTH = 256
B_GROUP = 16
B_NGROUPS = 16
B_STATE = 64
B_COLS = B_WIDTH
C_WIDTH = 256
C_HEADS = 4
C_KEY = 64
C_VAL = 64
C_COLS = 5 * C_WIDTH
IN_PAD = 3072

ADAM_LR = 0.001
ADAM_B1 = 0.9
ADAM_B2 = 0.999
ADAM_EPS = 1e-08
ADAM_WD = 0.01
ADAM_STEP = 10

WEIGHTS = ['c_ctx', 'mod_w', 'mod_b', 'ffn_w_in', 'ffn_w_out', 'w_in', 'w_out', 'a_conv_w', 'a_conv_b', 'a_dt_bias',
           'a_log', 'a_d', 'a_norm_w', 's5_lam_re', 's5_lam_im', 's5_log_step', 's5_b_re', 's5_b_im', 's5_c_re',
           's5_c_im', 's5_d', 's5_glu_w', 's5_glu_b', 'hg_lb_logits', 'hg_norm_w', 'final_norm_w']
BIG = {'ffn_w_in': 3, 'ffn_w_out': 2, 'w_in': 2, 'w_out': 1}
SMALL_SHARDED = {'a_conv_w': 2, 's5_glu_w': 1, 'hg_lb_logits': 2}
SMALL = [n for n in WEIGHTS if n not in BIG and n != 'mod_w']


def _tile(d, prefs):
    for p in prefs:
        if d % p == 0:
            return p
    return d


def _pack(arrs, width, row_mult):
    flat = jnp.concatenate([a.reshape(-1) for a in arrs])
    pad = (-flat.shape[0]) % (width * row_mult)
    if pad:
        flat = jnp.concatenate([flat, jnp.zeros((pad,), flat.dtype)])
    return flat.reshape(-1, width)


def _as_2d(shape):
    return (math.prod(shape[:-1]), shape[-1])


def _unpack(buf, shapes):
    lead = buf.shape[:-2]
    flat = buf.reshape(lead + (-1,))
    out, off = [], 0
    for s in shapes:
        n = math.prod(s)
        out.append(flat[..., off:off + n].reshape(lead + tuple(s)))
        off += n
    return out


def _assemble(g, axis):
    t = jnp.moveaxis(g, 0, axis)
    s = t.shape
    return t.reshape(s[:axis] + (s[axis] * s[axis + 1],) + s[axis + 2:])


def _split(full, axis):
    s = full.shape
    t = full.reshape(s[:axis] + (N_DEV, s[axis] // N_DEV) + s[axis + 1:])
    return jnp.moveaxis(t, axis, 0)


def all_gather(xs, name):
    nt = len(xs)

    def body(*refs):
        x_refs, out_refs = refs[:nt], refs[nt:2 * nt]
        send_sems, recv_sems, local_sems = refs[2 * nt:]
        ax, ay, ac = lax.axis_index("x"), lax.axis_index("y"), lax.axis_index("c")
        me, sibling = (ax, ay, ac), (ax, ay, 1 - ac)
        chips = [(1 - ax, ay), (ax, 1 - ay), (1 - ax, 1 - ay)]

        def slot(t, px, py, pc):
            return out_refs[t].at[4 * px + 2 * py + pc]

        def copy(t, k, block, to, src=None):
            return pltpu.make_async_remote_copy(
                src_ref=slot(t, *block) if src is None else src, dst_ref=slot(t, *block),
                send_sem=send_sems.at[t, k], recv_sem=recv_sems.at[t, k], device_id=to, device_id_type=MESH_ID)

        mine = [pltpu.make_async_copy(x_refs[t], slot(t, *me), local_sems.at[t]) for t in range(nt)]
        for cp in mine:
            cp.start()
        first = []
        for t in range(nt):
            first.append(copy(t, 0, me, sibling, src=x_refs[t]))
            first += [copy(t, 1 + j, me, (*chip, ac), src=x_refs[t]) for j, chip in enumerate(chips)]
        for cp in first:
            cp.start()
        passed = []
        for j, chip in enumerate(chips):
            for t in range(nt):
                copy(t, 1 + j, (*chip, ac), me).wait_recv()
                passed.append(copy(t, 4 + j, (*chip, ac), sibling))
                passed[-1].start()
        for t in range(nt):
            copy(t, 0, sibling, me).wait_recv()
            for j, chip in enumerate(chips):
                copy(t, 4 + j, (*chip, 1 - ac), me).wait_recv()
        for cp in first + passed:
            cp.wait_send()
        for cp in mine:
            cp.wait()

    return pl.pallas_call(
        body, name=name,
        out_shape=tuple(jax.ShapeDtypeStruct((N_DEV,) + x.shape, x.dtype) for x in xs),
        in_specs=[pl.BlockSpec(memory_space=pl.ANY)] * nt,
        out_specs=tuple(pl.BlockSpec(memory_space=pl.ANY) for _ in xs),
        scratch_shapes=[pltpu.SemaphoreType.DMA((nt, 7)), pltpu.SemaphoreType.DMA((nt, 7)),
                        pltpu.SemaphoreType.DMA((nt,))],
    )(*xs)


def all_to_all(gs, name):
    nt = len(gs)

    def body(*refs):
        g_refs, out_refs = refs[:nt], refs[nt:2 * nt]
        send_sems, recv_sems, local_sems = refs[2 * nt:]
        ax, ay, ac = lax.axis_index("x"), lax.axis_index("y"), lax.axis_index("c")
        my = 4 * ax + 2 * ay + ac
        local = [pltpu.make_async_copy(g_refs[t].at[my], out_refs[t].at[my], local_sems.at[t]) for t in range(nt)]
        for cp in local:
            cp.start()
        peers = []
        for r in range(1, N_DEV):
            px = 1 - ax if r & 4 else ax
            py = 1 - ay if r & 2 else ay
            pc = 1 - ac if r & 1 else ac
            peers.append((px, py, pc))

        def copy(t, k, peer):
            return pltpu.make_async_remote_copy(
                src_ref=g_refs[t].at[4 * peer[0] + 2 * peer[1] + peer[2]], dst_ref=out_refs[t].at[my],
                send_sem=send_sems.at[t, k], recv_sem=recv_sems.at[t, k], device_id=peer, device_id_type=MESH_ID)

        def arrival(t, k, peer):
            slot = 4 * peer[0] + 2 * peer[1] + peer[2]
            return pltpu.make_async_remote_copy(
                src_ref=g_refs[t].at[slot], dst_ref=out_refs[t].at[slot],
                send_sem=send_sems.at[t, k], recv_sem=recv_sems.at[t, k], device_id=peer, device_id_type=MESH_ID)

        sends = [copy(t, k, p) for t in range(nt) for k, p in enumerate(peers)]
        for cp in sends:
            cp.start()
        for t in range(nt):
            for k, p in enumerate(peers):
                arrival(t, k, p).wait_recv()
        for cp in sends:
            cp.wait_send()
        for cp in local:
            cp.wait()

    return pl.pallas_call(
        body, name=name,
        out_shape=tuple(jax.ShapeDtypeStruct(g.shape, g.dtype) for g in gs),
        in_specs=[pl.BlockSpec(memory_space=pl.ANY)] * nt,
        out_specs=tuple(pl.BlockSpec(memory_space=pl.ANY) for _ in gs),
        scratch_shapes=[pltpu.SemaphoreType.DMA((nt, 7)), pltpu.SemaphoreType.DMA((nt, 7)),
                        pltpu.SemaphoreType.DMA((nt,))],
    )(*gs)


def pair_exchange(gs, name):
    nt = len(gs)

    def body(*refs):
        g_refs, out_refs = refs[:nt], refs[nt:2 * nt]
        send_sems, recv_sems = refs[2 * nt:]
        ax, ay, ac = lax.axis_index("x"), lax.axis_index("y"), lax.axis_index("c")
        copies = [pltpu.make_async_remote_copy(
            src_ref=g_refs[t].at[2 * j + (1 - ac)], dst_ref=out_refs[t].at[j],
            send_sem=send_sems.at[t, j], recv_sem=recv_sems.at[t, j],
            device_id=(ax, ay, 1 - ac), device_id_type=MESH_ID) for t in range(nt) for j in range(4)]
        for cp in copies:
            cp.start()
        for cp in copies:
            cp.wait()

    return pl.pallas_call(
        body, name=name,
        out_shape=tuple(jax.ShapeDtypeStruct((4,) + g.shape[1:], g.dtype) for g in gs),
        in_specs=[pl.BlockSpec(memory_space=pl.ANY)] * nt,
        out_specs=tuple(pl.BlockSpec(memory_space=pl.ANY) for _ in gs),
        scratch_shapes=[pltpu.SemaphoreType.DMA((nt, 4)), pltpu.SemaphoreType.DMA((nt, 4))],
    )(*gs)


def chip_exchange(ps, name):
    nt = len(ps)

    def body(*refs):
        p_refs, out_refs = refs[:nt], refs[nt:2 * nt]
        send_sems, recv_sems, local_sems = refs[2 * nt:]
        ax, ay, ac = lax.axis_index("x"), lax.axis_index("y"), lax.axis_index("c")
        my = 2 * ax + ay
        local = [pltpu.make_async_copy(p_refs[t].at[my], out_refs[t].at[my], local_sems.at[t]) for t in range(nt)]
        for cp in local:
            cp.start()
        chips = [(1 - ax, ay), (ax, 1 - ay), (1 - ax, 1 - ay)]

        def copy(t, k, chip, slot):
            return pltpu.make_async_remote_copy(
                src_ref=p_refs[t].at[2 * chip[0] + chip[1]], dst_ref=out_refs[t].at[slot],
                send_sem=send_sems.at[t, k], recv_sem=recv_sems.at[t, k], device_id=(*chip, ac), device_id_type=MESH_ID)

        sends = [copy(t, k, chip, my) for t in range(nt) for k, chip in enumerate(chips)]
        for cp in sends:
            cp.start()
        for t in range(nt):
            for k, chip in enumerate(chips):
                copy(t, k, chip, 2 * chip[0] + chip[1]).wait_recv()
        for cp in sends:
            cp.wait_send()
        for cp in local:
            cp.wait()

    return pl.pallas_call(
        body, name=name,
        out_shape=tuple(jax.ShapeDtypeStruct(p.shape, p.dtype) for p in ps),
        in_specs=[pl.BlockSpec(memory_space=pl.ANY)] * nt,
        out_specs=tuple(pl.BlockSpec(memory_space=pl.ANY) for _ in ps),
        scratch_shapes=[pltpu.SemaphoreType.DMA((nt, 3)), pltpu.SemaphoreType.DMA((nt, 3)),
                        pltpu.SemaphoreType.DMA((nt,))],
    )(*ps)


def matmul(a, b, *, ta=False, tb=False, out_dtype=F32, name="mm"):
    m, k = (a.shape[1], a.shape[0]) if ta else a.shape
    n = b.shape[0] if tb else b.shape[1]
    assert (b.shape[1] if tb else b.shape[0]) == k, (a.shape, b.shape, ta, tb)
    tm = _tile(m, (1024, 512, 256, 128))
    tn = _tile(n, (1408, 1024, 512, 384, 256, 128))
    tk = _tile(k, (1408, 1024, 512, 256, 128))
    nk = k // tk
    dims = (((0 if ta else 1,), (1 if tb else 0,)), ((), ()))

    def body(a_ref, b_ref, o_ref, acc_ref):
        step = pl.program_id(2)

        @pl.when(step == 0)
        def _():
            acc_ref[...] = jnp.zeros_like(acc_ref)

        acc_ref[...] += lax.dot_general(a_ref[...].astype(BF16), b_ref[...].astype(BF16), dims,
                                        preferred_element_type=F32)

        @pl.when(step == nk - 1)
        def _():
            o_ref[...] = acc_ref[...].astype(out_dtype)

    a_spec = pl.BlockSpec((tk, tm), lambda i, j, s: (s, i)) if ta else pl.BlockSpec((tm, tk), lambda i, j, s: (i, s))
    b_spec = pl.BlockSpec((tn, tk), lambda i, j, s: (j, s)) if tb else pl.BlockSpec((tk, tn), lambda i, j, s: (s, j))
    return pl.pallas_call(
        body, name=name,
        out_shape=jax.ShapeDtypeStruct((m, n), out_dtype),
        grid=(m // tm, n // tn, nk),
        in_specs=[a_spec, b_spec],
        out_specs=pl.BlockSpec((tm, tn), lambda i, j, s: (i, j)),
        scratch_shapes=[pltpu.VMEM((tm, tn), F32)],
        compiler_params=pltpu.CompilerParams(dimension_semantics=("parallel", "parallel", "arbitrary"),
                                             vmem_limit_bytes=VMEM_LIMIT),
    )(a, b)


@jax.custom_vjp
def mm(x, w):
    return matmul(x, w, name="mm_fwd")


def _mm_fwd(x, w):
    return matmul(x, w, name="mm_fwd"), (x, w)


def _mm_bwd(res, dy):
    x, w = res
    dx = matmul(dy, w, tb=True, out_dtype=x.dtype, name="mm_dx")
    dw = matmul(x, dy, ta=True, out_dtype=w.dtype, name="mm_dw")
    return dx, dw


mm.defvjp(_mm_fwd, _mm_bwd)


def sum_leading(x, name):
    n, r, c = x.shape
    tr = _tile(r, (256, 128, 64, 32, 16, 8))

    def body(x_ref, o_ref):
        acc = x_ref[0].astype(F32)
        for i in range(1, n):
            acc = acc + x_ref[i].astype(F32)
        o_ref[...] = acc

    return pl.pallas_call(
        body, name=name,
        out_shape=jax.ShapeDtypeStruct((r, c), F32),
        grid=(r // tr,),
        in_specs=[pl.BlockSpec((n, tr, c), lambda i: (0, i, 0))],
        out_specs=pl.BlockSpec((tr, c), lambda i: (i, 0)),
        compiler_params=pltpu.CompilerParams(dimension_semantics=("parallel",), vmem_limit_bytes=VMEM_LIMIT),
    )(x)


def add_pairs(a, b, name):
    n, r, c = a.shape
    tr = _tile(r, (256, 128, 64, 32, 16))

    def body(a_ref, b_ref, o_ref):
        o_ref[...] = (a_ref[...].astype(F32) + b_ref[...].astype(F32)).astype(a.dtype)

    spec = pl.BlockSpec((1, tr, c), lambda i, j: (i, j, 0))
    return pl.pallas_call(
        body, name=name,
        out_shape=jax.ShapeDtypeStruct(a.shape, a.dtype),
        grid=(n, r // tr),
        in_specs=[spec, spec],
        out_specs=spec,
        compiler_params=pltpu.CompilerParams(dimension_semantics=("parallel", "parallel"),
                                             vmem_limit_bytes=VMEM_LIMIT),
    )(a, b)


def adamw(w, g, m, v, name):
    r, c = w.shape
    tr = _tile(r, (256, 128, 64, 32, 16, 8))

    def body(w_ref, g_ref, m_ref, v_ref, d_ref, mo_ref, vo_ref):
        gv = g_ref[...]
        mv = ADAM_B1 * m_ref[...] + (1.0 - ADAM_B1) * gv
        vv = ADAM_B2 * v_ref[...] + (1.0 - ADAM_B2) * jnp.square(gv)
        m_hat = mv / (1.0 - ADAM_B1 ** ADAM_STEP)
        v_hat = vv / (1.0 - ADAM_B2 ** ADAM_STEP)
        d_ref[...] = -ADAM_LR * (m_hat / (jnp.sqrt(v_hat) + ADAM_EPS) + ADAM_WD * w_ref[...])
        mo_ref[...] = mv
        vo_ref[...] = vv

    spec = pl.BlockSpec((tr, c), lambda i: (i, 0))
    return pl.pallas_call(
        body, name=name,
        out_shape=(jax.ShapeDtypeStruct((r, c), F32),) * 3,
        grid=(r // tr,),
        in_specs=[spec] * 4,
        out_specs=(spec,) * 3,
        compiler_params=pltpu.CompilerParams(dimension_semantics=("parallel",), vmem_limit_bytes=VMEM_LIMIT),
    )(w, g, m, v)


ROW_TILE = 512


def _row_tile(t):
    return _tile(t, (ROW_TILE, 128, 64, 32, 16, 8))


def _group_call(body, name, ins, in_kinds, out_shapes, out_kinds, tt, out_dtypes=None):
    g, t = ins[0].shape[:2]

    def spec(kind, shape):
        if kind == 'tok':
            return pl.BlockSpec((1, tt, shape[-1]), lambda i, j: (i, j, 0))
        return pl.BlockSpec((1, 1, shape[-1]), lambda i, j: (i, 0, 0))

    return pl.pallas_call(
        body, name=name,
        out_shape=tuple(jax.ShapeDtypeStruct(s, d) for s, d in zip(out_shapes, out_dtypes or [F32] * len(out_shapes))),
        grid=(g, t // tt),
        in_specs=[spec(k, a.shape) for k, a in zip(in_kinds, ins)],
        out_specs=tuple(spec(k, s) for k, s in zip(out_kinds, out_shapes)),
        compiler_params=pltpu.CompilerParams(dimension_semantics=("parallel", "arbitrary"),
                                             vmem_limit_bytes=VMEM_LIMIT),
    )(*ins)


def _accumulate(ref, val):
    @pl.when(pl.program_id(1) == 0)
    def _():
        ref[...] = jnp.zeros_like(ref)

    ref[0] += jnp.sum(val, axis=0, keepdims=True)


def _modulate_fwd(h, shift, scale, out_dtype):
    def body(h_ref, sh_ref, sc_ref, o_ref):
        hv = h_ref[0]
        r = lax.rsqrt(jnp.mean(hv * hv, axis=-1, keepdims=True) + EPS)
        o_ref[0] = (hv * r * (1.0 + sc_ref[0]) + sh_ref[0]).astype(out_dtype)

    return _group_call(body, "modulate_fwd", [h, shift, scale], ['tok', 'vec', 'vec'], [h.shape], ['tok'],
                       _row_tile(h.shape[1]), [out_dtype])[0]


def _modulate_bwd(h, scale, du):
    def body(h_ref, sc_ref, du_ref, dh_ref, dsh_ref, dsc_ref):
        hv, dv = h_ref[0], du_ref[0]
        r = lax.rsqrt(jnp.mean(hv * hv, axis=-1, keepdims=True) + EPS)
        hn = hv * r
        dn = dv * (1.0 + sc_ref[0])
        dh_ref[0] = r * (dn - hn * jnp.mean(dn * hn, axis=-1, keepdims=True))
        _accumulate(dsh_ref, dv)
        _accumulate(dsc_ref, dv * hn)

    return _group_call(body, "modulate_bwd", [h, scale, du], ['tok', 'vec', 'tok'],
                       [h.shape, scale.shape, scale.shape], ['tok', 'acc', 'acc'], _row_tile(h.shape[1]))


def _rows(t):
    return t.reshape(-1, t.shape[-1])


@functools.partial(jax.custom_vjp, nondiff_argnums=(4,))
def modmm(h, shift, scale, w, out_dtype):
    return _modmm_fwd(h, shift, scale, w, out_dtype)[0]


def _modmm_fwd(h, shift, scale, w, out_dtype):
    u = _modulate_fwd(h, shift, scale, BF16)
    y = matmul(_rows(u), w, out_dtype=out_dtype, name="mm_fwd").reshape(h.shape[:2] + (-1,))
    return y, (h, scale, u, w)


def _modmm_bwd(out_dtype, res, dy):
    h, scale, u, w = res
    du = matmul(_rows(dy), w, tb=True, name="mm_dx").reshape(h.shape)
    dw = matmul(_rows(u), _rows(dy), ta=True, out_dtype=w.dtype, name="mm_dw")
    dh, dsh, dsc = _modulate_bwd(h, scale, du)
    return dh, dsh, dsc, dw


modmm.defvjp(_modmm_fwd, _modmm_bwd)


def _gated_add_call(h, y, gate, coef):
    def body(h_ref, y_ref, g_ref, o_ref):
        o_ref[0] = h_ref[0] + coef * g_ref[0] * y_ref[0]

    return _group_call(body, "gated_add_fwd", [h, y, gate], ['tok', 'tok', 'vec'], [h.shape], ['tok'],
                       _row_tile(h.shape[1]))[0]


def _gated_add_bwd_call(y, gate, dout, coef):
    def body(y_ref, g_ref, d_ref, dy_ref, dg_ref):
        dv = d_ref[0]
        dy_ref[0] = coef * g_ref[0] * dv
        _accumulate(dg_ref, coef * dv * y_ref[0])

    return _group_call(body, "gated_add_bwd", [y, gate, dout], ['tok', 'vec', 'tok'], [y.shape, gate.shape],
                       ['tok', 'acc'], _row_tile(y.shape[1]))


@functools.partial(jax.custom_vjp, nondiff_argnums=(3,))
def gated_add(h, y, gate, coef):
    return _gated_add_call(h, y, gate, coef)


def _gated_add_vjp_fwd(h, y, gate, coef):
    return _gated_add_call(h, y, gate, coef), (y, gate)


def _gated_add_vjp_bwd(coef, res, dout):
    y, gate = res
    dy, dg = _gated_add_bwd_call(y, gate, dout, coef)
    return dout, dy, dg


gated_add.defvjp(_gated_add_vjp_fwd, _gated_add_vjp_bwd)


def _swiglu_fwd(hid):
    f = hid.shape[-1] // 2

    def body(h_ref, o_ref):
        gate, up = h_ref[0, :, 0:f].astype(F32), h_ref[0, :, f:2 * f].astype(F32)
        o_ref[0] = (gate * jax.nn.sigmoid(gate) * up).astype(BF16)

    return _group_call(body, "swiglu_fwd", [hid], ['tok'], [hid.shape[:2] + (f,)], ['tok'],
                       _tile(hid.shape[1], (128, 64, 32, 16)), [BF16])[0]


def _swiglu_bwd(hid, da):
    f = hid.shape[-1] // 2

    def body(h_ref, da_ref, d_ref):
        gate, up, dv = h_ref[0, :, 0:f].astype(F32), h_ref[0, :, f:2 * f].astype(F32), da_ref[0]
        s = jax.nn.sigmoid(gate)
        d_ref[0, :, 0:f] = (dv * up * (s * (1.0 + gate * (1.0 - s)))).astype(hid.dtype)
        d_ref[0, :, f:2 * f] = (dv * (gate * s)).astype(hid.dtype)

    return _group_call(body, "swiglu_bwd", [hid, da], ['tok', 'tok'], [hid.shape], ['tok'],
                       _tile(hid.shape[1], (128, 64, 32, 16)), [hid.dtype])[0]


@jax.custom_vjp
def swiglu_mm(hid, w):
    return _swiglu_mm_fwd(hid, w)[0]


def _swiglu_mm_fwd(hid, w):
    act = _swiglu_fwd(hid)
    y = matmul(_rows(act), w, name="mm_fwd").reshape(hid.shape[:2] + (-1,))
    return y, (hid, act, w)


def _swiglu_mm_bwd(res, dy):
    hid, act, w = res
    da = matmul(_rows(dy), w, tb=True, name="mm_dx").reshape(act.shape)
    dw = matmul(_rows(act), _rows(dy), ta=True, out_dtype=w.dtype, name="mm_dw")
    return _swiglu_bwd(hid, da), dw


swiglu_mm.defvjp(_swiglu_mm_fwd, _swiglu_mm_bwd)


@jax.custom_vjp
def flip_rows(x):
    return _flip_rows_call(x)


def _flip_rows_call(x):
    n, length, c = x.shape
    tb = _tile(length, (256, 128, 64, 32, 16, 8))
    nb = length // tb

    def body(x_ref, o_ref):
        xv = x_ref[0]
        ii = lax.broadcasted_iota(jnp.int32, (tb, tb), 0)
        jj = lax.broadcasted_iota(jnp.int32, (tb, tb), 1)
        rev = (ii + jj == tb - 1).astype(BF16)
        hi = xv.astype(BF16)
        r1 = xv - hi.astype(F32)
        mid = r1.astype(BF16)
        lo = (r1 - mid.astype(F32)).astype(BF16)
        dot = functools.partial(jnp.dot, preferred_element_type=F32)
        o_ref[0] = (dot(rev, hi) + dot(rev, mid)) + dot(rev, lo)

    return pl.pallas_call(
        body, name="flip_rows",
        out_shape=jax.ShapeDtypeStruct(x.shape, F32),
        grid=(n, nb),
        in_specs=[pl.BlockSpec((1, tb, c), lambda i, j: (i, j, 0))],
        out_specs=pl.BlockSpec((1, tb, c), lambda i, j: (i, nb - 1 - j, 0)),
        compiler_params=pltpu.CompilerParams(dimension_semantics=("parallel", "parallel"),
                                             vmem_limit_bytes=VMEM_LIMIT),
    )(x)


flip_rows.defvjp(lambda x: (_flip_rows_call(x), None), lambda _, dy: (_flip_rows_call(dy),))


def _flip_time(t, axis):
    s = t.shape
    lead = math.prod(s[:axis])
    return flip_rows(t.reshape(lead, s[axis], -1)).reshape(s)


def rms_norm(x):
    return x * lax.rsqrt(jnp.mean(x * x, axis=-1, keepdims=True) + EPS)


def raster_to_column(t, rows):
    b, s, d = t.shape
    return t.reshape(b, rows, GRID_W, d).transpose(0, 2, 1, 3).reshape(b, s, d)


def column_to_raster(t, rows):
    b, s, d = t.shape
    return t.reshape(b, GRID_W, rows, d).transpose(0, 2, 1, 3).reshape(b, s, d)


def depthwise_conv(x, w, b):
    pad = A_CONV // 2
    y = lax.conv_general_dilated(x, w[:, None, :], window_strides=(1,), padding=[(pad, pad)],
                                 dimension_numbers=('NWC', 'WIO', 'NWC'), feature_group_count=x.shape[-1])
    return y + b


S5_STATES = B_NGROUPS * B_STATE
S5_ROWS = 8
S5_STEPS_FWD = 64
S5_STEPS_BWD = 32


def _s5_scan_fwd(u2, bd2, cd2, ar8, ai8):
    rows, width = u2.shape
    ns = S5_STATES
    tr = S5_ROWS * S5_STEPS_FWD
    assert rows % tr == 0

    def body(u_ref, bd_ref, cd_ref, ar_ref, ai_ref, y_ref, x_ref, st_ref):
        @pl.when(pl.program_id(0) == 0)
        def _():
            st_ref[...] = jnp.zeros_like(st_ref)

        x_ref[...] = jnp.dot(u_ref[...].astype(BF16), bd_ref[...], preferred_element_type=F32)
        ar, ai = ar_ref[...], ai_ref[...]

        def step(t, carry):
            xr, xi = carry
            r = pl.ds(pl.multiple_of(t * S5_ROWS, S5_ROWS), S5_ROWS)
            nr = ar * xr - ai * xi + x_ref[r, 0:ns]
            ni = ar * xi + ai * xr + x_ref[r, ns:2 * ns]
            x_ref[r, 0:ns] = nr
            x_ref[r, ns:2 * ns] = ni
            return nr, ni

        xr, xi = lax.fori_loop(0, S5_STEPS_FWD, step, (st_ref[:, 0:ns], st_ref[:, ns:2 * ns]), unroll=4)
        st_ref[:, 0:ns] = xr
        st_ref[:, ns:2 * ns] = xi
        y_ref[...] = jnp.dot(x_ref[...].astype(BF16), cd_ref[...], preferred_element_type=F32)

    whole = lambda shape: pl.BlockSpec(shape, lambda i: (0, 0))
    return pl.pallas_call(
        body, name="s5_scan_fwd",
        out_shape=(jax.ShapeDtypeStruct((rows, width), F32), jax.ShapeDtypeStruct((rows, 2 * ns), F32)),
        grid=(rows // tr,),
        in_specs=[pl.BlockSpec((tr, width), lambda i: (i, 0)), whole(bd2.shape), whole(cd2.shape),
                  whole(ar8.shape), whole(ai8.shape)],
        out_specs=(pl.BlockSpec((tr, width), lambda i: (i, 0)), pl.BlockSpec((tr, 2 * ns), lambda i: (i, 0))),
        scratch_shapes=[pltpu.VMEM((S5_ROWS, 2 * ns), F32)],
        compiler_params=pltpu.CompilerParams(dimension_semantics=("arbitrary",), vmem_limit_bytes=VMEM_LIMIT),
    )(u2, bd2, cd2, ar8, ai8)


def _s5_scan_bwd(dy, x, u2, bd2, cd2, ar8, ai8):
    rows, width = u2.shape
    ns = S5_STATES
    steps = S5_STEPS_BWD
    tr = S5_ROWS * steps
    nblk = rows // tr
    assert rows % tr == 0
    nt = (((1,), (1,)), ((), ()))
    tn = (((0,), (0,)), ((), ()))

    def body(dy_ref, x_ref, xp_ref, u_ref, bd_ref, cd_ref, ar_ref, ai_ref,
             du_ref, dbd_ref, dcd_ref, dar_ref, dai_ref, g_ref, st_ref):
        k = pl.program_id(0)

        @pl.when(k == 0)
        def _():
            st_ref[...] = jnp.zeros_like(st_ref)
            dbd_ref[...] = jnp.zeros_like(dbd_ref)
            dcd_ref[...] = jnp.zeros_like(dcd_ref)
            dar_ref[...] = jnp.zeros_like(dar_ref)
            dai_ref[...] = jnp.zeros_like(dai_ref)

        dyb = dy_ref[...].astype(BF16)
        g_ref[...] = lax.dot_general(dyb, cd_ref[...], nt, preferred_element_type=F32)
        ar, ai = ar_ref[...], ai_ref[...]

        def adjoint(r, carry, xpr, xpi):
            gr_n, gi_n, dar, dai = carry
            gr = g_ref[r, 0:ns] + ar * gr_n + ai * gi_n
            gi = g_ref[r, ns:2 * ns] - ai * gr_n + ar * gi_n
            g_ref[r, 0:ns] = gr
            g_ref[r, ns:2 * ns] = gi
            return gr, gi, dar + gr * xpr + gi * xpi, dai + gi * xpr - gr * xpi

        def step(i, carry):
            t = steps - 1 - i
            r = pl.ds(pl.multiple_of(t * S5_ROWS, S5_ROWS), S5_ROWS)
            rp = pl.ds(pl.multiple_of((t - 1) * S5_ROWS, S5_ROWS), S5_ROWS)
            return adjoint(r, carry, x_ref[rp, 0:ns], x_ref[rp, ns:2 * ns])

        zero = jnp.zeros((S5_ROWS, ns), F32)
        carry = lax.fori_loop(0, steps - 1, step, (st_ref[:, 0:ns], st_ref[:, ns:2 * ns], zero, zero), unroll=2)
        has_prev = (k < nblk - 1).astype(F32)
        gr, gi, dar, dai = adjoint(pl.ds(0, S5_ROWS), carry, xp_ref[:, 0:ns] * has_prev, xp_ref[:, ns:2 * ns] * has_prev)
        st_ref[:, 0:ns] = gr
        st_ref[:, ns:2 * ns] = gi
        dar_ref[...] += dar
        dai_ref[...] += dai
        gb = g_ref[...].astype(BF16)
        du_ref[...] = lax.dot_general(gb, bd_ref[...], nt, preferred_element_type=F32)
        dbd_ref[...] += lax.dot_general(u_ref[...].astype(BF16), gb, tn, preferred_element_type=F32)
        dcd_ref[...] += lax.dot_general(x_ref[...].astype(BF16), dyb, tn, preferred_element_type=F32)

    whole = lambda shape: pl.BlockSpec(shape, lambda k: (0, 0))
    rev = lambda k: (nblk - 1 - k, 0)
    prev = lambda k: (jnp.maximum((nblk - 1 - k) * steps - 1, 0), 0)
    return pl.pallas_call(
        body, name="s5_scan_bwd",
        out_shape=(jax.ShapeDtypeStruct((rows, width), F32), jax.ShapeDtypeStruct(bd2.shape, F32),
                   jax.ShapeDtypeStruct(cd2.shape, F32), jax.ShapeDtypeStruct(ar8.shape, F32),
                   jax.ShapeDtypeStruct(ai8.shape, F32)),
        grid=(nblk,),
        in_specs=[pl.BlockSpec((tr, width), rev), pl.BlockSpec((tr, 2 * ns), rev),
                  pl.BlockSpec((S5_ROWS, 2 * ns), prev), pl.BlockSpec((tr, width), rev),
                  whole(bd2.shape), whole(cd2.shape), whole(ar8.shape), whole(ai8.shape)],
        out_specs=(pl.BlockSpec((tr, width), rev), whole(bd2.shape), whole(cd2.shape), whole(ar8.shape),
                   whole(ai8.shape)),
        scratch_shapes=[pltpu.VMEM((tr, 2 * ns), F32), pltpu.VMEM((S5_ROWS, 2 * ns), F32)],
        compiler_params=pltpu.CompilerParams(dimension_semantics=("arbitrary",), vmem_limit_bytes=VMEM_LIMIT),
    )(dy, x, x, u2, bd2, cd2, ar8, ai8)


@jax.custom_vjp
def s5_core(u2, bd2, cd2, ar8, ai8):
    return _s5_scan_fwd(u2, bd2.astype(BF16), cd2.astype(BF16), ar8, ai8)[0]


def _s5_core_fwd(u2, bd2, cd2, ar8, ai8):
    bd2, cd2 = bd2.astype(BF16), cd2.astype(BF16)
    y, x = _s5_scan_fwd(u2, bd2, cd2, ar8, ai8)
    return y, (x, u2, bd2, cd2, ar8, ai8)


def _s5_core_bwd(res, dy):
    return _s5_scan_bwd(dy, *res)


s5_core.defvjp(_s5_core_fwd, _s5_core_bwd)


def s5_mixers(p_ctx, p_lat, lam_re, lam_im, log_step, b_re, b_im, c_re, c_im, d_skip, glu_w, glu_b):
    bsz = p_ctx.shape[0]
    assert 2 * bsz == S5_ROWS
    eye = jnp.eye(B_NGROUPS, dtype=F32)
    bds, cds, ars, ais = [], [], [], []
    for d in range(2):
        step = jnp.exp(log_step[d])[:, None]
        mag = jnp.exp(lam_re[d] * step)
        ar = mag * jnp.cos(lam_im[d] * step)
        ai = mag * jnp.sin(lam_im[d] * step)
        den = lam_re[d] * lam_re[d] + lam_im[d] * lam_im[d]
        nr = ar - 1.0
        kr = (nr * lam_re[d] + ai * lam_im[d]) / den
        ki = (ai * lam_re[d] - nr * lam_im[d]) / den
        br = kr[..., None] * b_re[d] - ki[..., None] * b_im[d]
        bi = kr[..., None] * b_im[d] + ki[..., None] * b_re[d]
        blk = lambda w: jnp.einsum('gnc,gh->gchn', w, eye).reshape(B_WIDTH, S5_STATES)
        bds.append(jnp.concatenate([blk(br), blk(bi)], axis=1))
        blk_c = lambda w: jnp.einsum('gcn,gh->gnhc', w, eye).reshape(S5_STATES, B_WIDTH)
        cds.append(jnp.concatenate([blk_c(c_re[d]), -blk_c(c_im[d])], axis=0))
        ars.append(jnp.broadcast_to(ar.reshape(1, S5_STATES), (bsz, S5_STATES)))
        ais.append(jnp.broadcast_to(ai.reshape(1, S5_STATES), (bsz, S5_STATES)))
    bd2 = jnp.concatenate(bds, axis=0)
    cd2 = jnp.concatenate(cds, axis=1)
    ar8 = jnp.concatenate(ars, axis=0)
    ai8 = jnp.concatenate(ais, axis=0)

    def rows_of(p):
        ut = jnp.swapaxes(p, 0, 1)
        z = jnp.zeros_like(ut)
        return jnp.concatenate([jnp.concatenate([ut, z], axis=-1), jnp.concatenate([z, _flip_time(ut, 0)], axis=-1)], axis=1)

    lc = p_ctx.shape[1]
    u2 = jnp.concatenate([rows_of(p_ctx), rows_of(p_lat)], axis=0)
    y2 = s5_core(u2.reshape(-1, 2 * B_WIDTH), bd2, cd2, ar8, ai8).reshape(u2.shape)

    def finish(y2p, p):
        y = y2p[:, :bsz, :B_WIDTH] + _flip_time(y2p[:, bsz:, B_WIDTH:], 0)
        y = jnp.swapaxes(y, 0, 1) + d_skip * p
        y = jax.nn.gelu(y)
        gate = mm(y.reshape(-1, B_WIDTH), glu_w).reshape(y.shape)
        return y * jax.nn.sigmoid(gate + glu_b)

    return finish(y2[:lc], p_ctx), finish(y2[lc:], p_lat)


GLA_CHUNK = 64
GLA_SUB = 16
NT_DIMS = (((1,), (1,)), ((), ()))
TN_DIMS = (((0,), (0,)), ((), ()))


def _bdot(a, b, dims=(((1,), (0,)), ((), ()))):
    return lax.dot_general(a.astype(BF16), b.astype(BF16), dims, preferred_element_type=F32)


def _hdot(a, b, dims=(((1,), (0,)), ((), ()))):
    ah, bh = a.astype(BF16), b.astype(BF16)
    al, bl = (a - ah.astype(F32)).astype(BF16), (b - bh.astype(F32)).astype(BF16)
    dot = functools.partial(lax.dot_general, dimension_numbers=dims, preferred_element_type=F32)
    return dot(ah, bh) + (dot(ah, bl) + dot(al, bh))


def _sub_block_ref(cum, rows, lo, hi, rev):
    n = cum.shape[0]
    if rev:
        return (cum[hi:hi + 1, :] if hi < n else jnp.zeros_like(cum[0:1, :])), rows >= lo
    return (cum[lo - 1:lo, :] if lo else jnp.zeros_like(cum[0:1, :])), rows < hi


def _chunk_of(step, nc, nc_ctx, rev):
    if not rev:
        return step
    return jnp.where(step < nc_ctx, nc_ctx - 1 - step, nc + nc_ctx - 1 - step)


def _gla_scores(q, k, cum, cumr, tri, rev):
    n = GLA_CHUNK
    if cumr is not None:
        decay = jnp.where(tri, jnp.exp(jnp.where(tri, cum - cumr, 0.0)), 0.0)
        return _bdot(q, k, NT_DIMS) * decay, decay
    rows = lax.broadcasted_iota(jnp.int32, (n, 1), 0)
    parts = []
    for i in range(n // GLA_SUB):
        lo, hi = i * GLA_SUB, (i + 1) * GLA_SUB
        ref, seen = _sub_block_ref(cum, rows, lo, hi, rev)
        qt = q[lo:hi] * jnp.exp(cum[lo:hi] - ref)
        kh = jnp.where(seen, k * jnp.exp(jnp.where(seen, ref - cum, 0.0)), 0.0)
        parts.append(_bdot(qt, kh, NT_DIMS))
    return jnp.where(tri, jnp.concatenate(parts, axis=0), 0.0), None


def _gla_fwd(q, k, cum, cumr, v, rev, nc_ctx):
    bsz, length, width = q.shape
    dk = GLA_CHUNK
    nh = width // dk
    nc = length // GLA_CHUNK
    scalar = cumr is not None

    def body(*refs):
        if scalar:
            q_ref, k_ref, cum_ref, cumr_ref, v_ref, o_ref, s_ref, st_ref = refs
        else:
            q_ref, k_ref, cum_ref, v_ref, o_ref, s_ref, st_ref = refs

        @pl.when(pl.program_id(1) == 0)
        def _():
            st_ref[...] = jnp.zeros_like(st_ref)

        ii = lax.broadcasted_iota(jnp.int32, (GLA_CHUNK, GLA_CHUNK), 0)
        jj = lax.broadcasted_iota(jnp.int32, (GLA_CHUNK, GLA_CHUNK), 1)
        tri = jj >= ii if rev else jj <= ii
        edge = 0 if rev else GLA_CHUNK - 1
        qa, ka, ca, va = q_ref[0], k_ref[0], cum_ref[0], v_ref[0]
        cra = cumr_ref[0] if scalar else None
        outs = []
        for h in range(nh):
            sl = slice(h * dk, (h + 1) * dk)
            qv, kv, cv, vv, st = qa[:, sl], ka[:, sl], ca[:, sl], va[:, sl], st_ref[h]
            s_ref[0, 0, h] = st
            a, _ = _gla_scores(qv, kv, cv, cra[:, sl] if scalar else None, tri, rev)
            outs.append(_bdot(qv * jnp.exp(cv), st, NT_DIMS) + _bdot(a, vv))
            last = cv[edge:edge + 1, :]
            st_ref[h] = st * jnp.exp(last) + _bdot(vv, kv * jnp.exp(last - cv), TN_DIMS)
        o_ref[0] = jnp.concatenate(outs, axis=1)

    seq = pl.BlockSpec((1, GLA_CHUNK, width), lambda n, c: (n, _chunk_of(c, nc, nc_ctx, rev), 0))
    state = pl.BlockSpec((1, 1, nh, dk, dk), lambda n, c: (n, _chunk_of(c, nc, nc_ctx, rev), 0, 0, 0))
    ins = [q, k, cum] + ([cumr] if scalar else []) + [v]
    return pl.pallas_call(
        body, name="gla_fwd_scalar" if scalar else "gla_fwd",
        out_shape=(jax.ShapeDtypeStruct((bsz, length, width), F32), jax.ShapeDtypeStruct((bsz, nc, nh, dk, dk), F32)),
        grid=(bsz, nc),
        in_specs=[seq] * len(ins),
        out_specs=(seq, state),
        scratch_shapes=[pltpu.VMEM((nh, dk, dk), F32)],
        compiler_params=pltpu.CompilerParams(dimension_semantics=("parallel", "arbitrary"),
                                             vmem_limit_bytes=VMEM_LIMIT),
    )(*ins)


def _gla_bwd(do, q, k, cum, cumr, v, states, rev, nc_ctx):
    bsz, length, width = q.shape
    nc = length // GLA_CHUNK
    scalar = cumr is not None
    n = GLA_CHUNK
    dk = GLA_CHUNK
    nh = width // dk

    def body(*refs):
        if scalar:
            do_ref, q_ref, k_ref, cum_ref, cumr_ref, v_ref, s_ref, dq_ref, dk_ref, dc_ref, dcr_ref, dv_ref, dst_ref = refs
        else:
            do_ref, q_ref, k_ref, cum_ref, v_ref, s_ref, dq_ref, dk_ref, dc_ref, dv_ref, dst_ref = refs

        @pl.when(pl.program_id(1) == 0)
        def _():
            dst_ref[...] = jnp.zeros_like(dst_ref)

        ii = lax.broadcasted_iota(jnp.int32, (n, n), 0)
        jj = lax.broadcasted_iota(jnp.int32, (n, n), 1)
        tri = jj >= ii if rev else jj <= ii
        edge = 0 if rev else n - 1
        rows = lax.broadcasted_iota(jnp.int32, (n, 1), 0)
        doa, qa, ka, ca, va = do_ref[0], q_ref[0], k_ref[0], cum_ref[0], v_ref[0]
        cra = cumr_ref[0] if scalar else None
        dqs, dks, dcs, dcrs, dvs = [], [], [], [], []
        for h in range(nh):
            sl = slice(h * dk, (h + 1) * dk)
            dov, qv, kv, cv, vv, st, dst = doa[:, sl], qa[:, sl], ka[:, sl], ca[:, sl], va[:, sl], s_ref[0, 0, h], dst_ref[h]
            e = jnp.exp(cv)
            qe = qv * e
            last = cv[edge:edge + 1, :]
            w = jnp.exp(last - cv)
            kw = kv * w
            el = jnp.exp(last)
            hd = _bdot if scalar else _hdot
            d_qe = hd(dov, st)
            d_kw = hd(vv, dst)
            dv = _bdot(kw, dst, NT_DIMS)
            d_last = jnp.sum(st * dst, axis=0, keepdims=True) * el + jnp.sum(d_kw * kw, axis=0, keepdims=True)
            dst_ref[h] = dst * el + _bdot(dov, qe, TN_DIMS)
            dq = d_qe * e
            dkk = d_kw * w
            dc = d_qe * qe - d_kw * kw + jnp.where(rows == edge, d_last, 0.0)
            da = jnp.where(tri, hd(dov, vv, NT_DIMS), 0.0)
            if scalar:
                a, decay = _gla_scores(qv, kv, cv, cra[:, sl], tri, rev)
                dg = da * decay
                dq = dq + _bdot(dg, kv)
                dkk = dkk + _bdot(dg, qv, TN_DIMS)
                p = da * a
                dc = dc + p
                dcrs.append(-p)
            else:
                a_parts, dq_parts = [], []
                for i in range(n // GLA_SUB):
                    lo, hi = i * GLA_SUB, (i + 1) * GLA_SUB
                    ref, seen = _sub_block_ref(cv, rows, lo, hi, rev)
                    eq = jnp.exp(cv[lo:hi] - ref)
                    qt = qv[lo:hi] * eq
                    ek = jnp.where(seen, jnp.exp(jnp.where(seen, ref - cv, 0.0)), 0.0)
                    kh = kv * ek
                    a_parts.append(_bdot(qt, kh, NT_DIMS))
                    dqt = _hdot(da[lo:hi], kh)
                    dkh = _hdot(da[lo:hi], qt, TN_DIMS)
                    dq_parts.append((dqt * eq, dqt * qt))
                    dkk = dkk + dkh * ek
                    dc = dc - dkh * kh
                a = jnp.where(tri, jnp.concatenate(a_parts, axis=0), 0.0)
                dq = dq + jnp.concatenate([p[0] for p in dq_parts], axis=0)
                dc = dc + jnp.concatenate([p[1] for p in dq_parts], axis=0)
            dvs.append(dv + _bdot(a, dov, TN_DIMS))
            dqs.append(dq)
            dks.append(dkk)
            dcs.append(dc)
        cat = functools.partial(jnp.concatenate, axis=1)
        dq_ref[0], dk_ref[0], dc_ref[0], dv_ref[0] = cat(dqs), cat(dks), cat(dcs), cat(dvs)
        if scalar:
            dcr_ref[0] = cat(dcrs)

    seq = pl.BlockSpec((1, n, width), lambda s, c: (s, _chunk_of(nc - 1 - c, nc, nc_ctx, rev), 0))
    state = pl.BlockSpec((1, 1, nh, dk, dk), lambda s, c: (s, _chunk_of(nc - 1 - c, nc, nc_ctx, rev), 0, 0, 0))
    ins = [do, q, k, cum] + ([cumr] if scalar else []) + [v]
    n_out = 5 if scalar else 4
    return pl.pallas_call(
        body, name="gla_bwd_scalar" if scalar else "gla_bwd",
        out_shape=(jax.ShapeDtypeStruct((bsz, length, width), F32),) * n_out,
        grid=(bsz, nc),
        in_specs=[seq] * len(ins) + [state],
        out_specs=(seq,) * n_out,
        scratch_shapes=[pltpu.VMEM((nh, dk, dk), F32)],
        compiler_params=pltpu.CompilerParams(dimension_semantics=("parallel", "arbitrary"),
                                             vmem_limit_bytes=VMEM_LIMIT),
    )(*ins, states)


@functools.partial(jax.custom_vjp, nondiff_argnums=(4, 5))
def gla(q, k, cum, v, rev, nc_ctx):
    return _gla_fwd(q, k, cum, None, v, rev, nc_ctx)[0]


def _gla_vjp_fwd(q, k, cum, v, rev, nc_ctx):
    o, states = _gla_fwd(q, k, cum, None, v, rev, nc_ctx)
    return o, (q, k, cum, v, states)


def _gla_vjp_bwd(rev, nc_ctx, res, do):
    q, k, cum, v, states = res
    return _gla_bwd(do, q, k, cum, None, v, states, rev, nc_ctx)


gla.defvjp(_gla_vjp_fwd, _gla_vjp_bwd)


@functools.partial(jax.custom_vjp, nondiff_argnums=(5, 6))
def gla_scalar(q, k, cum, cumr, v, rev, nc_ctx):
    return _gla_fwd(q, k, cum, cumr, v, rev, nc_ctx)[0]


def _gla_scalar_vjp_fwd(q, k, cum, cumr, v, rev, nc_ctx):
    o, states = _gla_fwd(q, k, cum, cumr, v, rev, nc_ctx)
    return o, (q, k, cum, cumr, v, states)


def _gla_scalar_vjp_bwd(rev, nc_ctx, res, do):
    q, k, cum, cumr, v, states = res
    return _gla_bwd(do, q, k, cum, cumr, v, states, rev, nc_ctx)


gla_scalar.defvjp(_gla_scalar_vjp_fwd, _gla_scalar_vjp_bwd)


def _chunk_cumsum(g, rev, axis=-2):
    axis = axis % g.ndim
    s = g.shape
    by_chunk = g.reshape(s[:axis] + (s[axis] // GLA_CHUNK, GLA_CHUNK) + s[axis + 1:])
    c = jnp.cumsum(by_chunk, axis=axis + 1)
    if rev:
        c = lax.slice_in_dim(c, GLA_CHUNK - 1, GLA_CHUNK, axis=axis + 1) - c + by_chunk
    return c.reshape(s)


def _both_parts(t_ctx, t_lat):
    return jnp.concatenate([t.reshape(t.shape[:2] + (-1,)) for t in (t_ctx, t_lat)], axis=1)


def _split_parts(o, lc, nh):
    return tuple(t.reshape(t.shape[:2] + (nh, -1)) for t in (o[:, :lc], o[:, lc:]))


def hgrn2_mixers(p_ctx, p_lat, lower, norm_w):
    bsz, lc = p_ctx.shape[:2]
    lower = lower.reshape(2, C_HEADS, C_KEY)

    def heads(p, lo, hi):
        return p[..., lo:hi].reshape(p.shape[:2] + (C_HEADS, -1))

    q_c, q_l = (jax.nn.silu(heads(p, 0, C_WIDTH)) for p in (p_ctx, p_lat))
    v_c, v_l = (heads(p, 3 * C_WIDTH, 4 * C_WIDTH) for p in (p_ctx, p_lat))
    q, v = _both_parts(q_c, q_l), _both_parts(v_c, v_l)
    nc_ctx = lc // GLA_CHUNK
    o = []
    for d in range(2):
        f_c, f_l = (lower[d] + (1.0 - lower[d]) * jax.nn.sigmoid(heads(p, (1 + d) * C_WIDTH, (2 + d) * C_WIDTH))
                    for p in (p_ctx, p_lat))
        cum = jnp.concatenate([_chunk_cumsum(jnp.log(f).reshape(f.shape[:2] + (-1,)), d, axis=1) for f in (f_c, f_l)], axis=1)
        o.append(gla(q, _both_parts(1.0 - f_c, 1.0 - f_l), cum, v, bool(d), nc_ctx))
    f_c, f_l = _split_parts(o[0], lc, C_HEADS)
    b_c, b_l = _split_parts(o[1], lc, C_HEADS)
    outs = []
    for o_sum, p in ((f_c + b_c, p_ctx), (f_l + b_l, p_lat)):
        o_n = rms_norm(o_sum) * norm_w.reshape(C_HEADS, C_VAL)
        outs.append(o_n.reshape(p.shape[:2] + (C_WIDTH,)) * jax.nn.silu(p[..., 4 * C_WIDTH:]))
    return tuple(outs)


def ssd_mixers(p_ctx, p_lat, conv_w, conv_b, dt_bias, a_log, d_skip, norm_w):
    bsz, lc = p_ctx.shape[:2]
    rep = A_HEADS // A_GROUPS
    a = -jnp.exp(a_log)
    xs, bs, cs, dts, zs = [], [], [], [], []
    for p in (p_ctx, p_lat):
        z, xbc, dt_raw = jnp.split(p, [A_INNER, A_INNER + A_CONV_DIM], axis=-1)
        xbc = jax.nn.silu(depthwise_conv(xbc, conv_w, conv_b))
        x_, b_, c_ = jnp.split(xbc, [A_INNER, A_INNER + A_GROUPS * A_STATE], axis=-1)
        shp = p.shape[:2]
        xs.append(x_.reshape(shp + (A_HEADS, A_HEAD_DIM)))
        bs.append(jnp.repeat(b_.reshape(shp + (A_GROUPS, A_STATE)), rep, axis=2))
        cs.append(jnp.repeat(c_.reshape(shp + (A_GROUPS, A_STATE)), rep, axis=2))
        dts.append([jax.nn.softplus(dt_raw[..., d * A_HEADS:(d + 1) * A_HEADS] + dt_bias[d]) for d in range(2)])
        zs.append(z)
    q, v = _both_parts(cs[0], cs[1]), _both_parts(xs[0], xs[1])
    nc_ctx = lc // GLA_CHUNK
    o = []
    for d in range(2):
        k = _both_parts(bs[0] * dts[0][d][..., None], bs[1] * dts[1][d][..., None])
        adt = jnp.concatenate([_chunk_cumsum(dt[d] * a[d], d, axis=1) for dt in dts], axis=1)
        nb, lt = adt.shape[:2]
        cum = jnp.broadcast_to(adt[..., None], (nb, lt, A_HEADS, A_STATE)).reshape(nb, lt, -1)
        along = jnp.swapaxes(adt.reshape(nb, lt // GLA_CHUNK, GLA_CHUNK, A_HEADS), 2, 3)[:, :, None]
        cumr = jnp.broadcast_to(along, (nb, lt // GLA_CHUNK, GLA_CHUNK, A_HEADS, GLA_CHUNK)).reshape(nb, lt, -1)
        o.append(gla_scalar(q, k, cum, cumr, v, bool(d), nc_ctx))
    f_c, f_l = _split_parts(o[0], lc, A_HEADS)
    b_c, b_l = _split_parts(o[1], lc, A_HEADS)
    outs = []
    for y, x_, z in ((f_c + b_c, xs[0], zs[0]), (f_l + b_l, xs[1], zs[1])):
        y = y + d_skip[:, None] * x_
        y = y.reshape(z.shape) * jax.nn.silu(z)
        outs.append(rms_norm(y) * norm_w)
    return tuple(outs)


def token_mixers(p_ctx, p_lat, W, l, lower):
    def cut(p):
        return p[..., :A_COLS], p[..., 1408:1408 + B_COLS], p[..., 1664:1664 + C_COLS]

    pa_c, pb_c, pc_c = cut(p_ctx)
    pa_l, pb_l, pc_l = cut(p_lat)
    ya_c, ya_l = ssd_mixers(pa_c, pa_l, W['a_conv_w'][l], W['a_conv_b'][l], W['a_dt_bias'][l], W['a_log'][l],
                            W['a_d'][l], W['a_norm_w'][l])
    yb_c, yb_l = s5_mixers(pb_c, pb_l, W['s5_lam_re'][l], W['s5_lam_im'][l], W['s5_log_step'][l], W['s5_b_re'][l],
                           W['s5_b_im'][l], W['s5_c_re'][l], W['s5_c_im'][l], W['s5_d'][l], W['s5_glu_w'][l],
                           W['s5_glu_b'][l])
    yc_c, yc_l = hgrn2_mixers(pc_c, pc_l, lower, W['hg_norm_w'][l])
    return (jnp.concatenate([ya_c, yb_c, yc_c], axis=-1), jnp.concatenate([ya_l, yb_l, yc_l], axis=-1))


def _pad_w_in(w):
    z = functools.partial(jnp.zeros, dtype=w.dtype)
    return jnp.concatenate([w[:, :A_COLS], z((D_MODEL, 1408 - A_COLS)), w[:, A_COLS:], z((D_MODEL, IN_PAD - 2944))],
                           axis=1)


def _mm3(t, w):
    g, tt, k = t.shape
    return mm(t.reshape(g * tt, k), w).reshape(g, tt, -1)


def _ffn(h, mg, first, w_in, w_out):
    hid = modmm(h, mg[:, first:first + 1], mg[:, first + 1:first + 2], w_in, BF16)
    return gated_add(h, swiglu_mm(hid, w_out), mg[:, first + 2:first + 3], 0.5)


def local_loss(x, W, m_lat, m_ctx, ctx, target):
    bsz, seq, dm = x.shape
    lc = ctx.shape[1]
    tg = bsz * lc
    assert seq % tg == 0
    gl = seq // tg
    ng = bsz * gl
    rows = seq // GRID_W
    p_lb = jax.nn.softmax(W['hg_lb_logits'], axis=0)
    lower_bounds = jnp.cumsum(p_lb, axis=0) - p_lb[:1]
    h = jnp.concatenate([x.reshape(ng, tg, dm), ctx.reshape(1, tg, dm)], axis=0)
    for l in range(DEPTH):
        last = l == DEPTH - 1
        col_major = l % 2 == 1
        mg = jnp.concatenate([jnp.repeat(m_lat[l], gl, axis=0), m_ctx[l][None]], axis=0)
        h = _ffn(h, mg, 0, W['ffn_w_in'][l][0], W['ffn_w_out'][l][0])
        hp = h
        if col_major:
            h_lat = raster_to_column(h[:ng].reshape(bsz, seq, dm), rows)
            hp = jnp.concatenate([h_lat.reshape(ng, tg, dm), h[ng:]], axis=0)
        p = modmm(hp, mg[:, 3:4], mg[:, 4:5], _pad_w_in(W['w_in'][l]), F32)
        mix_ctx, mix_lat = token_mixers(p[ng].reshape(bsz, lc, -1), p[:ng].reshape(bsz, seq, -1), W, l,
                                        lower_bounds[l])
        if last:
            h, mg = h[:ng], mg[:ng]
            y_lat = _mm3(mix_lat.reshape(ng, tg, dm), W['w_out'][l])
            y_ctx = None
        else:
            y = _mm3(jnp.concatenate([mix_lat.reshape(ng, tg, dm), mix_ctx.reshape(1, tg, dm)], axis=0), W['w_out'][l])
            y_lat, y_ctx = y[:ng], y[ng:]
        if col_major:
            y_lat = column_to_raster(y_lat.reshape(bsz, seq, dm), rows).reshape(ng, tg, dm)
        y = y_lat if y_ctx is None else jnp.concatenate([y_lat, y_ctx], axis=0)
        h = gated_add(h, y, mg[:, 5:6], 1.0)
        h = _ffn(h, mg, 6, W['ffn_w_in'][l][1], W['ffn_w_out'][l][1])
    y = rms_norm(h[:ng].reshape(bsz, seq, dm)) * W['final_norm_w']
    err = jnp.square(y - target)
    return 0.5 * jnp.sum(jnp.mean(err, axis=-1))


def _pad_rows(a, rows):
    return jnp.concatenate([a, jnp.zeros((rows - a.shape[0],) + a.shape[1:], a.dtype)], axis=0)


def kernel(x, c, ctx, c_ctx, mod_w, mod_b, ffn_w_in, ffn_w_out, w_in, w_out, a_conv_w, a_conv_b, a_dt_bias, a_log, a_d, a_norm_w, s5_lam_re, s5_lam_im, s5_log_step, s5_b_re, s5_b_im, s5_c_re, s5_c_im, s5_d, s5_glu_w, s5_glu_b, hg_lb_logits, hg_norm_w, final_norm_w, loss_target, m_c_ctx, m_mod_w, m_mod_b, m_ffn_w_in, m_ffn_w_out, m_w_in, m_w_out, m_a_conv_w, m_a_conv_b, m_a_dt_bias, m_a_log, m_a_d, m_a_norm_w, m_s5_lam_re, m_s5_lam_im, m_s5_log_step, m_s5_b_re, m_s5_b_im, m_s5_c_re, m_s5_c_im, m_s5_d, m_s5_glu_w, m_s5_glu_b, m_hg_lb_logits, m_hg_norm_w, m_final_norm_w, v_c_ctx, v_mod_w, v_mod_b, v_ffn_w_in, v_ffn_w_out, v_w_in, v_w_out, v_a_conv_w, v_a_conv_b, v_a_dt_bias, v_a_log, v_a_d, v_a_norm_w, v_s5_lam_re, v_s5_lam_im, v_s5_log_step, v_s5_b_re, v_s5_b_im, v_s5_c_re, v_s5_c_im, v_s5_d, v_s5_glu_w, v_s5_glu_b, v_hg_lb_logits, v_hg_norm_w, v_final_norm_w):
    given = dict(locals())
    w_loc = {n: given[n] for n in WEIGHTS}
    m_loc = {n: given["m_" + n] for n in WEIGHTS}
    v_loc = {n: given["v_" + n] for n in WEIGHTS}
    bsz = x.shape[0]
    me = 4 * lax.axis_index("x") + 2 * lax.axis_index("y") + lax.axis_index("c")

    small_sh = [c] + [w_loc[n] for n in SMALL_SHARDED]
    g1 = _unpack(all_gather([_pack(small_sh, 128, 8)], "gather_small")[0], [a.shape for a in small_sh])
    c_all = g1[0].reshape(N_DEV * bsz, D_MODEL)
    gathered = dict(zip(BIG, all_gather([w_loc[n].astype(BF16) for n in BIG], "gather_weights")))
    W = {'ffn_w_in': [[_assemble(gathered['ffn_w_in'][:, l, i], 1) for i in range(2)] for l in range(DEPTH)],
         'ffn_w_out': [[_assemble(gathered['ffn_w_out'][:, l, i], 0) for i in range(2)] for l in range(DEPTH)],
         'w_in': [_assemble(gathered['w_in'][:, l], 1) for l in range(DEPTH)],
         'w_out': [_assemble(gathered['w_out'][:, l], 0) for l in range(DEPTH)]}
    for (n, ax), t in zip(SMALL_SHARDED.items(), g1[1:]):
        W[n] = _assemble(t, ax)
    for n in SMALL:
        if n not in SMALL_SHARDED and n not in ('c_ctx', 'mod_b'):
            W[n] = w_loc[n]

    n_rows = N_DEV * bsz + 1
    pad_rows = 8 * ((n_rows + 7) // 8)
    c_rows = _pad_rows(jnp.concatenate([c_all, c_ctx[None]], axis=0), pad_rows)
    sc = jax.nn.silu(c_rows)
    mods_sh = jnp.stack([matmul(sc, mod_w[l], name="mod_fwd") for l in range(DEPTH)])
    mods = _assemble(all_gather([mods_sh], "gather_mods")[0], 2) + mod_b[:, None, :]
    m_lat = lax.dynamic_slice_in_dim(mods, me * bsz, bsz, axis=1).reshape(DEPTH, bsz, N_MOD, D_MODEL)
    m_ctx = mods[:, n_rows - 1].reshape(DEPTH, N_MOD, D_MODEL)

    loss_loc, (grad_x, gW, gm_lat, gm_ctx) = jax.value_and_grad(local_loss, argnums=(0, 1, 2, 3))(
        x, W, m_lat, m_ctx, ctx, loss_target)

    dm_loc = jnp.concatenate([gm_lat.reshape(DEPTH, bsz, -1), gm_ctx.reshape(DEPTH, 1, -1)], axis=1)
    (dm_all,) = all_gather([dm_loc], "gather_dmods")
    dm_ex = jnp.moveaxis(dm_all[:, :, :bsz], 0, 1).reshape(DEPTH, N_DEV * bsz, -1)
    ncol = N_MOD * D_MODEL
    dm_cx = sum_leading(dm_all[:, :, bsz].reshape(N_DEV, DEPTH * ncol // 128, 128), "sum_dmods_ctx")
    dm_cx = dm_cx.reshape(DEPTH, 1, ncol)
    dm_rows = jnp.concatenate([dm_ex, dm_cx, jnp.zeros((DEPTH, pad_rows - n_rows, ncol), F32)], axis=1)
    grad_mod_b = sum_leading(jnp.moveaxis(dm_rows, 1, 0).reshape(pad_rows, DEPTH * ncol // 128, 128),
                             "sum_mod_b").reshape(DEPTH, ncol)
    my_cols = ncol // N_DEV
    dm_mine = lax.dynamic_slice_in_dim(dm_rows, me * my_cols, my_cols, axis=2)
    grad_mod_w = jnp.stack([matmul(sc, dm_mine[l], ta=True, name="mod_dw") for l in range(DEPTH)])
    dm_cx_mine = lax.dynamic_slice_in_dim(dm_cx, me * my_cols, my_cols, axis=2)
    g_sc_ctx = sum(matmul(_pad_rows(dm_cx_mine[l], 8), mod_w[l], tb=True, name="mod_dc")[0] for l in range(DEPTH))

    small_full = [n for n in SMALL if n not in ('c_ctx', 'mod_b')]
    part = [gW[n] for n in small_full] + [g_sc_ctx, loss_loc.reshape(1)]
    red = sum_leading(all_gather([_pack(part, 128, ROW_TILE)], "gather_small_grads")[0], "sum_small_grads")
    red = _unpack(red, [a.shape for a in part])
    grads = dict(zip(small_full, red[:-2]))
    loss = red[-1].reshape(())
    sig = jax.nn.sigmoid(c_ctx)
    grads['c_ctx'] = red[-2] * (sig * (1.0 + c_ctx * (1.0 - sig)))
    grads['mod_b'] = grad_mod_b
    for n, ax in SMALL_SHARDED.items():
        size = w_loc[n].shape[ax]
        grads[n] = lax.dynamic_slice_in_dim(grads[n], me * size, size, axis=ax)
    grads['mod_w'] = grad_mod_w

    by_dev = {'ffn_w_in': jnp.stack([jnp.stack([_split(g, 1) for g in gl], axis=1) for gl in gW['ffn_w_in']], axis=1),
              'ffn_w_out': jnp.stack([jnp.stack([_split(g, 0) for g in gl], axis=1) for gl in gW['ffn_w_out']], axis=1),
              'w_in': jnp.stack([_split(g, 1) for g in gW['w_in']], axis=1),
              'w_out': jnp.stack([_split(g, 0) for g in gW['w_out']], axis=1)}
    ac = lax.axis_index("c")
    sends = [by_dev[n] for n in BIG]
    from_sibling = pair_exchange(sends, "exchange_grads_pair")
    pair = []
    for g, r in zip(sends, from_sibling):
        mine = lax.dynamic_index_in_dim(g.reshape((4, 2) + g.shape[1:]), ac, axis=1, keepdims=False)
        shape3 = (4,) + _as_2d(g.shape[1:])
        pair.append(add_pairs(mine.reshape(shape3), r.reshape(shape3), "sum_grads_pair").reshape(r.shape))
    for n, t in zip(BIG, chip_exchange(pair, "exchange_grads_chips")):
        grads[n] = sum_leading(t.reshape((4,) + _as_2d(t.shape[1:])), "sum_grads").reshape(t.shape[1:])

    delta, new_m, new_v = {}, {}, {}

    for n in list(BIG) + ['mod_w']:
        outs = adamw(*[d[n].reshape(_as_2d(d[n].shape)) for d in (w_loc, grads, m_loc, v_loc)], name="adamw_" + n)
        delta[n], new_m[n], new_v[n] = (o.reshape(w_loc[n].shape) for o in outs)

    def update(names, width, row_mult, tag):
        packed = [_pack([d[n] for n in names], width, row_mult) for d in (w_loc, grads, m_loc, v_loc)]
        outs = adamw(*packed, name="adamw_" + tag)
        shapes = [w_loc[n].shape for n in names]
        for res, out in zip((delta, new_m, new_v), outs):
            res.update(zip(names, _unpack(out, shapes)))

    update(SMALL, 128, 256, "small")
    return (loss, grad_x, *[grads[n] for n in WEIGHTS], *[delta[n] for n in WEIGHTS],
            *[new_m[n] for n in WEIGHTS], *[new_v[n] for n in WEIGHTS])
```

```python
import functools
import math

import jax
import jax.numpy as jnp
from jax import lax
from jax.experimental import pallas as pl
from jax.experimental.pallas import tpu as pltpu

F32 = jnp.float32
BF16 = jnp.bfloat16
N_DEV = 8
MESH_ID = pl.DeviceIdType.MESH
VMEM_LIMIT = 48 * 1024 * 1024

D_MODEL = 1024
DEPTH = 2
GRID_W = 64
EPS = 1e-6
N_MOD = 9
D_FF = 2816
A_INNER = 512
A_HEADS = 8
A_HEAD_DIM = 64
A_GROUPS = 2
A_STATE = 64
A_CONV = 5
A_CONV_DIM = A_INNER + 2 * A_GROUPS * A_STATE
A_COLS = A_INNER + A_CONV_DIM + 2 * A_HEADS
B_WIDTH = 256
B_GROUP = 16
B_NGROUPS = 16
B_STATE = 64
B_COLS = B_WIDTH
C_WIDTH = 256
C_HEADS = 4
C_KEY = 64
C_VAL = 64
C_COLS = 5 * C_WIDTH
IN_PAD = 3072

ADAM_LR = 0.001
ADAM_B1 = 0.9
ADAM_B2 = 0.999
ADAM_EPS = 1e-08
ADAM_WD = 0.01
ADAM_STEP = 10

WEIGHTS = ['c_ctx', 'mod_w', 'mod_b', 'ffn_w_in', 'ffn_w_out', 'w_in', 'w_out', 'a_conv_w', 'a_conv_b', 'a_dt_bias',
           'a_log', 'a_d', 'a_norm_w', 's5_lam_re', 's5_lam_im', 's5_log_step', 's5_b_re', 's5_b_im', 's5_c_re',
           's5_c_im', 's5_d', 's5_glu_w', 's5_glu_b', 'hg_lb_logits', 'hg_norm_w', 'final_norm_w']
BIG = {'ffn_w_in': 3, 'ffn_w_out': 2, 'w_in': 2, 'w_out': 1}
SMALL_SHARDED = {'a_conv_w': 2, 's5_glu_w': 1, 'hg_lb_logits': 2}
SMALL = [n for n in WEIGHTS if n not in BIG and n != 'mod_w']


def _tile(d, prefs):
    for p in prefs:
        if d % p == 0:
            return p
    return d


def _pack(arrs, width, row_mult):
    flat = jnp.concatenate([a.reshape(-1) for a in arrs])
    pad = (-flat.shape[0]) % (width * row_mult)
    if pad:
        flat = jnp.concatenate([flat, jnp.zeros((pad,), flat.dtype)])
    return flat.reshape(-1, width)


def _as_2d(shape):
    return (math.prod(shape[:-1]), shape[-1])


def _unpack(buf, shapes):
    lead = buf.shape[:-2]
    flat = buf.reshape(lead + (-1,))
    out, off = [], 0
    for s in shapes:
        n = math.prod(s)
        out.append(flat[..., off:off + n].reshape(lead + tuple(s)))
        off += n
    return out


def _assemble(g, axis):
    t = jnp.moveaxis(g, 0, axis)
    s = t.shape
    return t.reshape(s[:axis] + (s[axis] * s[axis + 1],) + s[axis + 2:])


def _split(full, axis):
    s = full.shape
    t = full.reshape(s[:axis] + (N_DEV, s[axis] // N_DEV) + s[axis + 1:])
    return jnp.moveaxis(t, axis, 0)


def all_gather(xs, name):
    nt = len(xs)

    def body(*refs):
        x_refs, out_refs = refs[:nt], refs[nt:2 * nt]
        send_sems, recv_sems, local_sems = refs[2 * nt:]
        ax, ay, ac = lax.axis_index("x"), lax.axis_index("y"), lax.axis_index("c")
        me, sibling = (ax, ay, ac), (ax, ay, 1 - ac)
        chips = [(1 - ax, ay), (ax, 1 - ay), (1 - ax, 1 - ay)]

        def slot(t, px, py, pc):
            return out_refs[t].at[4 * px + 2 * py + pc]

        def copy(t, k, block, to, src=None):
            return pltpu.make_async_remote_copy(
                src_ref=slot(t, *block) if src is None else src, dst_ref=slot(t, *block),
                send_sem=send_sems.at[t, k], recv_sem=recv_sems.at[t, k], device_id=to, device_id_type=MESH_ID)

        mine = [pltpu.make_async_copy(x_refs[t], slot(t, *me), local_sems.at[t]) for t in range(nt)]
        for cp in mine:
            cp.start()
        first = []
        for t in range(nt):
            first.append(copy(t, 0, me, sibling, src=x_refs[t]))
            first += [copy(t, 1 + j, me, (*chip, ac), src=x_refs[t]) for j, chip in enumerate(chips)]
        for cp in first:
            cp.start()
        passed = []
        for j, chip in enumerate(chips):
            for t in range(nt):
                copy(t, 1 + j, (*chip, ac), me).wait_recv()
                passed.append(copy(t, 4 + j, (*chip, ac), sibling))
                passed[-1].start()
        for t in range(nt):
            copy(t, 0, sibling, me).wait_recv()
            for j, chip in enumerate(chips):
                copy(t, 4 + j, (*chip, 1 - ac), me).wait_recv()
        for cp in first + passed:
            cp.wait_send()
        for cp in mine:
            cp.wait()

    return pl.pallas_call(
        body, name=name,
        out_shape=tuple(jax.ShapeDtypeStruct((N_DEV,) + x.shape, x.dtype) for x in xs),
        in_specs=[pl.BlockSpec(memory_space=pl.ANY)] * nt,
        out_specs=tuple(pl.BlockSpec(memory_space=pl.ANY) for _ in xs),
        scratch_shapes=[pltpu.SemaphoreType.DMA((nt, 7)), pltpu.SemaphoreType.DMA((nt, 7)),
                        pltpu.SemaphoreType.DMA((nt,))],
    )(*xs)


def all_to_all(gs, name):
    nt = len(gs)

    def body(*refs):
        g_refs, out_refs = refs[:nt], refs[nt:2 * nt]
        send_sems, recv_sems, local_sems = refs[2 * nt:]
        ax, ay, ac = lax.axis_index("x"), lax.axis_index("y"), lax.axis_index("c")
        my = 4 * ax + 2 * ay + ac
        local = [pltpu.make_async_copy(g_refs[t].at[my], out_refs[t].at[my], local_sems.at[t]) for t in range(nt)]
        for cp in local:
            cp.start()
        peers = []
        for r in range(1, N_DEV):
            px = 1 - ax if r & 4 else ax
            py = 1 - ay if r & 2 else ay
            pc = 1 - ac if r & 1 else ac
            peers.append((px, py, pc))

        def copy(t, k, peer):
            return pltpu.make_async_remote_copy(
                src_ref=g_refs[t].at[4 * peer[0] + 2 * peer[1] + peer[2]], dst_ref=out_refs[t].at[my],
                send_sem=send_sems.at[t, k], recv_sem=recv_sems.at[t, k], device_id=peer, device_id_type=MESH_ID)

        def arrival(t, k, peer):
            slot = 4 * peer[0] + 2 * peer[1] + peer[2]
            return pltpu.make_async_remote_copy(
                src_ref=g_refs[t].at[slot], dst_ref=out_refs[t].at[slot],
                send_sem=send_sems.at[t, k], recv_sem=recv_sems.at[t, k], device_id=peer, device_id_type=MESH_ID)

        sends = [copy(t, k, p) for t in range(nt) for k, p in enumerate(peers)]
        for cp in sends:
            cp.start()
        for t in range(nt):
            for k, p in enumerate(peers):
                arrival(t, k, p).wait_recv()
        for cp in sends:
            cp.wait_send()
        for cp in local:
            cp.wait()

    return pl.pallas_call(
        body, name=name,
        out_shape=tuple(jax.ShapeDtypeStruct(g.shape, g.dtype) for g in gs),
        in_specs=[pl.BlockSpec(memory_space=pl.ANY)] * nt,
        out_specs=tuple(pl.BlockSpec(memory_space=pl.ANY) for _ in gs),
        scratch_shapes=[pltpu.SemaphoreType.DMA((nt, 7)), pltpu.SemaphoreType.DMA((nt, 7)),
                        pltpu.SemaphoreType.DMA((nt,))],
    )(*gs)


def pair_exchange(gs, name):
    nt = len(gs)

    def body(*refs):
        g_refs, out_refs = refs[:nt], refs[nt:2 * nt]
        send_sems, recv_sems = refs[2 * nt:]
        ax, ay, ac = lax.axis_index("x"), lax.axis_index("y"), lax.axis_index("c")
        copies = [pltpu.make_async_remote_copy(
            src_ref=g_refs[t].at[2 * j + (1 - ac)], dst_ref=out_refs[t].at[j],
            send_sem=send_sems.at[t, j], recv_sem=recv_sems.at[t, j],
            device_id=(ax, ay, 1 - ac), device_id_type=MESH_ID) for t in range(nt) for j in range(4)]
        for cp in copies:
            cp.start()
        for cp in copies:
            cp.wait()

    return pl.pallas_call(
        body, name=name,
        out_shape=tuple(jax.ShapeDtypeStruct((4,) + g.shape[1:], g.dtype) for g in gs),
        in_specs=[pl.BlockSpec(memory_space=pl.ANY)] * nt,
        out_specs=tuple(pl.BlockSpec(memory_space=pl.ANY) for _ in gs),
        scratch_shapes=[pltpu.SemaphoreType.DMA((nt, 4)), pltpu.SemaphoreType.DMA((nt, 4))],
    )(*gs)


def chip_exchange(ps, name):
    nt = len(ps)

    def body(*refs):
        p_refs, out_refs = refs[:nt], refs[nt:2 * nt]
        send_sems, recv_sems, local_sems = refs[2 * nt:]
        ax, ay, ac = lax.axis_index("x"), lax.axis_index("y"), lax.axis_index("c")
        my = 2 * ax + ay
        local = [pltpu.make_async_copy(p_refs[t].at[my], out_refs[t].at[my], local_sems.at[t]) for t in range(nt)]
        for cp in local:
            cp.start()
        chips = [(1 - ax, ay), (ax, 1 - ay), (1 - ax, 1 - ay)]

        def copy(t, k, chip, slot):
            return pltpu.make_async_remote_copy(
                src_ref=p_refs[t].at[2 * chip[0] + chip[1]], dst_ref=out_refs[t].at[slot],
                send_sem=send_sems.at[t, k], recv_sem=recv_sems.at[t, k], device_id=(*chip, ac), device_id_type=MESH_ID)

        sends = [copy(t, k, chip, my) for t in range(nt) for k, chip in enumerate(chips)]
        for cp in sends:
            cp.start()
        for t in range(nt):
            for k, chip in enumerate(chips):
                copy(t, k, chip, 2 * chip[0] + chip[1]).wait_recv()
        for cp in sends:
            cp.wait_send()
        for cp in local:
            cp.wait()

    return pl.pallas_call(
        body, name=name,
        out_shape=tuple(jax.ShapeDtypeStruct(p.shape, p.dtype) for p in ps),
        in_specs=[pl.BlockSpec(memory_space=pl.ANY)] * nt,
        out_specs=tuple(pl.BlockSpec(memory_space=pl.ANY) for _ in ps),
        scratch_shapes=[pltpu.SemaphoreType.DMA((nt, 3)), pltpu.SemaphoreType.DMA((nt, 3)),
                        pltpu.SemaphoreType.DMA((nt,))],
    )(*ps)


def matmul(a, b, *, ta=False, tb=False, out_dtype=F32, name="mm"):
    m, k = (a.shape[1], a.shape[0]) if ta else a.shape
    n = b.shape[0] if tb else b.shape[1]
    assert (b.shape[1] if tb else b.shape[0]) == k, (a.shape, b.shape, ta, tb)
    tm = _tile(m, (1024, 512, 256, 128))
    tn = _tile(n, (1408, 1024, 512, 384, 256, 128))
    tk = _tile(k, (1408, 1024, 512, 256, 128))
    nk = k // tk
    dims = (((0 if ta else 1,), (1 if tb else 0,)), ((), ()))

    def body(a_ref, b_ref, o_ref, acc_ref):
        step = pl.program_id(2)

        @pl.when(step == 0)
        def _():
            acc_ref[...] = jnp.zeros_like(acc_ref)

        acc_ref[...] += lax.dot_general(a_ref[...].astype(BF16), b_ref[...].astype(BF16), dims,
                                        preferred_element_type=F32)

        @pl.when(step == nk - 1)
        def _():
            o_ref[...] = acc_ref[...].astype(out_dtype)

    def body_one_step(a_ref, b_ref, o_ref):
        o_ref[...] = lax.dot_general(a_ref[...].astype(BF16), b_ref[...].astype(BF16), dims,
                                     preferred_element_type=F32).astype(out_dtype)

    a_spec = pl.BlockSpec((tk, tm), lambda i, j, s: (s, i)) if ta else pl.BlockSpec((tm, tk), lambda i, j, s: (i, s))
    b_spec = pl.BlockSpec((tn, tk), lambda i, j, s: (j, s)) if tb else pl.BlockSpec((tk, tn), lambda i, j, s: (s, j))
    return pl.pallas_call(
        body_one_step if nk == 1 else body, name=name,
        out_shape=jax.ShapeDtypeStruct((m, n), out_dtype),
        grid=(m // tm, n // tn, nk),
        in_specs=[a_spec, b_spec],
        out_specs=pl.BlockSpec((tm, tn), lambda i, j, s: (i, j)),
        scratch_shapes=[] if nk == 1 else [pltpu.VMEM((tm, tn), F32)],
        compiler_params=pltpu.CompilerParams(dimension_semantics=("parallel", "parallel", "arbitrary"),
                                             vmem_limit_bytes=VMEM_LIMIT),
    )(a, b)


@jax.custom_vjp
def mm(x, w):
    return matmul(x, w, name="mm_fwd")


def _mm_fwd(x, w):
    return matmul(x, w, name="mm_fwd"), (x, w)


def _mm_bwd(res, dy):
    x, w = res
    dx = matmul(dy, w, tb=True, out_dtype=x.dtype, name="mm_dx")
    dw = matmul(x, dy, ta=True, out_dtype=w.dtype, name="mm_dw")
    return dx, dw


mm.defvjp(_mm_fwd, _mm_bwd)


def sum_leading(x, name):
    n, r, c = x.shape
    tr = _tile(r, (256, 128, 64, 32, 16, 8))

    def body(x_ref, o_ref):
        acc = x_ref[0].astype(F32)
        for i in range(1, n):
            acc = acc + x_ref[i].astype(F32)
        o_ref[...] = acc

    return pl.pallas_call(
        body, name=name,
        out_shape=jax.ShapeDtypeStruct((r, c), F32),
        grid=(r // tr,),
        in_specs=[pl.BlockSpec((n, tr, c), lambda i: (0, i, 0))],
        out_specs=pl.BlockSpec((tr, c), lambda i: (i, 0)),
        compiler_params=pltpu.CompilerParams(dimension_semantics=("parallel",), vmem_limit_bytes=VMEM_LIMIT),
    )(x)


def add_pairs(a, b, name):
    n, r, c = a.shape
    tr = _tile(r, (256, 128, 64, 32, 16))

    def body(a_ref, b_ref, o_ref):
        o_ref[...] = (a_ref[...].astype(F32) + b_ref[...].astype(F32)).astype(a.dtype)

    spec = pl.BlockSpec((1, tr, c), lambda i, j: (i, j, 0))
    return pl.pallas_call(
        body, name=name,
        out_shape=jax.ShapeDtypeStruct(a.shape, a.dtype),
        grid=(n, r // tr),
        in_specs=[spec, spec],
        out_specs=spec,
        compiler_params=pltpu.CompilerParams(dimension_semantics=("parallel", "parallel"),
                                             vmem_limit_bytes=VMEM_LIMIT),
    )(a, b)


def adamw(w, g, m, v, name):
    r, c = w.shape
    tr = _tile(r, (256, 128, 64, 32, 16, 8))

    def body(w_ref, g_ref, m_ref, v_ref, d_ref, mo_ref, vo_ref):
        gv = g_ref[...]
        mv = ADAM_B1 * m_ref[...] + (1.0 - ADAM_B1) * gv
        vv = ADAM_B2 * v_ref[...] + (1.0 - ADAM_B2) * jnp.square(gv)
        m_hat = mv / (1.0 - ADAM_B1 ** ADAM_STEP)
        v_hat = vv / (1.0 - ADAM_B2 ** ADAM_STEP)
        d_ref[...] = -ADAM_LR * (m_hat / (jnp.sqrt(v_hat) + ADAM_EPS) + ADAM_WD * w_ref[...])
        mo_ref[...] = mv
        vo_ref[...] = vv

    spec = pl.BlockSpec((tr, c), lambda i: (i, 0))
    return pl.pallas_call(
        body, name=name,
        out_shape=(jax.ShapeDtypeStruct((r, c), F32),) * 3,
        grid=(r // tr,),
        in_specs=[spec] * 4,
        out_specs=(spec,) * 3,
        compiler_params=pltpu.CompilerParams(dimension_semantics=("parallel",), vmem_limit_bytes=VMEM_LIMIT),
    )(w, g, m, v)


ROW_TILE = 512


def _row_tile(t):
    return _tile(t, (ROW_TILE, 128, 64, 32, 16, 8))


def _group_call(body, name, ins, in_kinds, out_shapes, out_kinds, tt, out_dtypes=None):
    g, t = ins[0].shape[:2]

    def spec(kind, shape):
        if kind == 'tok':
            return pl.BlockSpec((1, tt, shape[-1]), lambda i, j: (i, j, 0))
        return pl.BlockSpec((1, 1, shape[-1]), lambda i, j: (i, 0, 0))

    return pl.pallas_call(
        body, name=name,
        out_shape=tuple(jax.ShapeDtypeStruct(s, d) for s, d in zip(out_shapes, out_dtypes or [F32] * len(out_shapes))),
        grid=(g, t // tt),
        in_specs=[spec(k, a.shape) for k, a in zip(in_kinds, ins)],
        out_specs=tuple(spec(k, s) for k, s in zip(out_kinds, out_shapes)),
        compiler_params=pltpu.CompilerParams(dimension_semantics=("parallel", "arbitrary"),
                                             vmem_limit_bytes=VMEM_LIMIT),
    )(*ins)


def _accumulate(ref, val):
    @pl.when(pl.program_id(1) == 0)
    def _():
        ref[...] = jnp.zeros_like(ref)

    ref[0] += jnp.sum(val, axis=0, keepdims=True)


def _modulate_fwd(h, shift, scale, out_dtype):
    def body(h_ref, sh_ref, sc_ref, o_ref):
        hv = h_ref[0]
        r = lax.rsqrt(jnp.mean(hv * hv, axis=-1, keepdims=True) + EPS)
        o_ref[0] = (hv * r * (1.0 + sc_ref[0]) + sh_ref[0]).astype(out_dtype)

    return _group_call(body, "modulate_fwd", [h, shift, scale], ['tok', 'vec', 'vec'], [h.shape], ['tok'],
                       _row_tile(h.shape[1]), [out_dtype])[0]


def _modulate_bwd(h, scale, du):
    def body(h_ref, sc_ref, du_ref, dh_ref, dsh_ref, dsc_ref):
        hv, dv = h_ref[0], du_ref[0]
        r = lax.rsqrt(jnp.mean(hv * hv, axis=-1, keepdims=True) + EPS)
        hn = hv * r
        dn = dv * (1.0 + sc_ref[0])
        dh_ref[0] = r * (dn - hn * jnp.mean(dn * hn, axis=-1, keepdims=True))
        _accumulate(dsh_ref, dv)
        _accumulate(dsc_ref, dv * hn)

    return _group_call(body, "modulate_bwd", [h, scale, du], ['tok', 'vec', 'tok'],
                       [h.shape, scale.shape, scale.shape], ['tok', 'acc', 'acc'], _row_tile(h.shape[1]))


def _rows(t):
    return t.reshape(-1, t.shape[-1])


@functools.partial(jax.custom_vjp, nondiff_argnums=(4,))
def modmm(h, shift, scale, w, out_dtype):
    return _modmm_fwd(h, shift, scale, w, out_dtype)[0]


def _modmm_fwd(h, shift, scale, w, out_dtype):
    u = _modulate_fwd(h, shift, scale, BF16)
    y = matmul(_rows(u), w, out_dtype=out_dtype, name="mm_fwd").reshape(h.shape[:2] + (-1,))
    return y, (h, scale, u, w)


def _modmm_bwd(out_dtype, res, dy):
    h, scale, u, w = res
    du = matmul(_rows(dy), w, tb=True, name="mm_dx").reshape(h.shape)
    dw = matmul(_rows(u), _rows(dy), ta=True, out_dtype=w.dtype, name="mm_dw")
    dh, dsh, dsc = _modulate_bwd(h, scale, du)
    return dh, dsh, dsc, dw


modmm.defvjp(_modmm_fwd, _modmm_bwd)


def _gated_add_call(h, y, gate, coef):
    def body(h_ref, y_ref, g_ref, o_ref):
        o_ref[0] = h_ref[0] + coef * g_ref[0] * y_ref[0]

    return _group_call(body, "gated_add_fwd", [h, y, gate], ['tok', 'tok', 'vec'], [h.shape], ['tok'],
                       _row_tile(h.shape[1]))[0]


def _gated_add_bwd_call(y, gate, dout, coef):
    def body(y_ref, g_ref, d_ref, dy_ref, dg_ref):
        dv = d_ref[0]
        dy_ref[0] = coef * g_ref[0] * dv
        _accumulate(dg_ref, coef * dv * y_ref[0])

    return _group_call(body, "gated_add_bwd", [y, gate, dout], ['tok', 'vec', 'tok'], [y.shape, gate.shape],
                       ['tok', 'acc'], _row_tile(y.shape[1]))


@functools.partial(jax.custom_vjp, nondiff_argnums=(3,))
def gated_add(h, y, gate, coef):
    return _gated_add_call(h, y, gate, coef)


def _gated_add_vjp_fwd(h, y, gate, coef):
    return _gated_add_call(h, y, gate, coef), (y, gate)


def _gated_add_vjp_bwd(coef, res, dout):
    y, gate = res
    dy, dg = _gated_add_bwd_call(y, gate, dout, coef)
    return dout, dy, dg


gated_add.defvjp(_gated_add_vjp_fwd, _gated_add_vjp_bwd)


def _swiglu_fwd(hid):
    f = hid.shape[-1] // 2

    def body(h_ref, o_ref):
        gate, up = h_ref[0, :, 0:f].astype(F32), h_ref[0, :, f:2 * f].astype(F32)
        o_ref[0] = (gate * jax.nn.sigmoid(gate) * up).astype(BF16)

    return _group_call(body, "swiglu_fwd", [hid], ['tok'], [hid.shape[:2] + (f,)], ['tok'],
                       _tile(hid.shape[1], (128, 64, 32, 16)), [BF16])[0]


def _swiglu_bwd(hid, da):
    f = hid.shape[-1] // 2

    def body(h_ref, da_ref, d_ref):
        gate, up, dv = h_ref[0, :, 0:f].astype(F32), h_ref[0, :, f:2 * f].astype(F32), da_ref[0]
        s = jax.nn.sigmoid(gate)
        d_ref[0, :, 0:f] = (dv * up * (s * (1.0 + gate * (1.0 - s)))).astype(hid.dtype)
        d_ref[0, :, f:2 * f] = (dv * (gate * s)).astype(hid.dtype)

    return _group_call(body, "swiglu_bwd", [hid, da], ['tok', 'tok'], [hid.shape], ['tok'],
                       _tile(hid.shape[1], (128, 64, 32, 16)), [hid.dtype])[0]


@jax.custom_vjp
def swiglu_mm(hid, w):
    return _swiglu_mm_fwd(hid, w)[0]


def _swiglu_mm_fwd(hid, w):
    act = _swiglu_fwd(hid)
    y = matmul(_rows(act), w, name="mm_fwd").reshape(hid.shape[:2] + (-1,))
    return y, (hid, act, w)


def _swiglu_mm_bwd(res, dy):
    hid, act, w = res
    da = matmul(_rows(dy), w, tb=True, name="mm_dx").reshape(act.shape)
    dw = matmul(_rows(act), _rows(dy), ta=True, out_dtype=w.dtype, name="mm_dw")
    return _swiglu_bwd(hid, da), dw


swiglu_mm.defvjp(_swiglu_mm_fwd, _swiglu_mm_bwd)


@jax.custom_vjp
def flip_rows(x):
    return _flip_rows_call(x)


def _flip_rows_call(x):
    n, length, c = x.shape
    tb = _tile(length, (256, 128, 64, 32, 16, 8))
    nb = length // tb

    def body(x_ref, o_ref):
        xv = x_ref[0]
        ii = lax.broadcasted_iota(jnp.int32, (tb, tb), 0)
        jj = lax.broadcasted_iota(jnp.int32, (tb, tb), 1)
        rev = (ii + jj == tb - 1).astype(BF16)
        hi = xv.astype(BF16)
        r1 = xv - hi.astype(F32)
        mid = r1.astype(BF16)
        lo = (r1 - mid.astype(F32)).astype(BF16)
        dot = functools.partial(jnp.dot, preferred_element_type=F32)
        o_ref[0] = (dot(rev, hi) + dot(rev, mid)) + dot(rev, lo)

    return pl.pallas_call(
        body, name="flip_rows",
        out_shape=jax.ShapeDtypeStruct(x.shape, F32),
        grid=(n, nb),
        in_specs=[pl.BlockSpec((1, tb, c), lambda i, j: (i, j, 0))],
        out_specs=pl.BlockSpec((1, tb, c), lambda i, j: (i, nb - 1 - j, 0)),
        compiler_params=pltpu.CompilerParams(dimension_semantics=("parallel", "parallel"),
                                             vmem_limit_bytes=VMEM_LIMIT),
    )(x)


flip_rows.defvjp(lambda x: (_flip_rows_call(x), None), lambda _, dy: (_flip_rows_call(dy),))


def _flip_time(t, axis):
    s = t.shape
    lead = math.prod(s[:axis])
    return flip_rows(t.reshape(lead, s[axis], -1)).reshape(s)


def rms_norm(x):
    return x * lax.rsqrt(jnp.mean(x * x, axis=-1, keepdims=True) + EPS)


def raster_to_column(t, rows):
    b, s, d = t.shape
    return t.reshape(b, rows, GRID_W, d).transpose(0, 2, 1, 3).reshape(b, s, d)


def column_to_raster(t, rows):
    b, s, d = t.shape
    return t.reshape(b, GRID_W, rows, d).transpose(0, 2, 1, 3).reshape(b, s, d)


def depthwise_conv(x, w, b):
    pad = A_CONV // 2
    y = lax.conv_general_dilated(x, w[:, None, :], window_strides=(1,), padding=[(pad, pad)],
                                 dimension_numbers=('NWC', 'WIO', 'NWC'), feature_group_count=x.shape[-1])
    return y + b


S5_STATES = B_NGROUPS * B_STATE
S5_ROWS = 8
S5_STEPS_FWD = 64
S5_STEPS_BWD = 32


def _s5_scan_fwd(u2, bd2, cd2, ar8, ai8):
    rows, width = u2.shape
    ns = S5_STATES
    tr = S5_ROWS * S5_STEPS_FWD
    assert rows % tr == 0

    def body(u_ref, bd_ref, cd_ref, ar_ref, ai_ref, y_ref, x_ref, st_ref):
        @pl.when(pl.program_id(0) == 0)
        def _():
            st_ref[...] = jnp.zeros_like(st_ref)

        x_ref[...] = jnp.dot(u_ref[...].astype(BF16), bd_ref[...], preferred_element_type=F32)
        ar, ai = ar_ref[...], ai_ref[...]

        def step(t, carry):
            xr, xi = carry
            r = pl.ds(pl.multiple_of(t * S5_ROWS, S5_ROWS), S5_ROWS)
            nr = ar * xr - ai * xi + x_ref[r, 0:ns]
            ni = ar * xi + ai * xr + x_ref[r, ns:2 * ns]
            x_ref[r, 0:ns] = nr
            x_ref[r, ns:2 * ns] = ni
            return nr, ni

        xr, xi = lax.fori_loop(0, S5_STEPS_FWD, step, (st_ref[:, 0:ns], st_ref[:, ns:2 * ns]), unroll=4)
        st_ref[:, 0:ns] = xr
        st_ref[:, ns:2 * ns] = xi
        y_ref[...] = jnp.dot(x_ref[...].astype(BF16), cd_ref[...], preferred_element_type=F32)

    whole = lambda shape: pl.BlockSpec(shape, lambda i: (0, 0))
    return pl.pallas_call(
        body, name="s5_scan_fwd",
        out_shape=(jax.ShapeDtypeStruct((rows, width), F32), jax.ShapeDtypeStruct((rows, 2 * ns), F32)),
        grid=(rows // tr,),
        in_specs=[pl.BlockSpec((tr, width), lambda i: (i, 0)), whole(bd2.shape), whole(cd2.shape),
                  whole(ar8.shape), whole(ai8.shape)],
        out_specs=(pl.BlockSpec((tr, width), lambda i: (i, 0)), pl.BlockSpec((tr, 2 * ns), lambda i: (i, 0))),
        scratch_shapes=[pltpu.VMEM((S5_ROWS, 2 * ns), F32)],
        compiler_params=pltpu.CompilerParams(dimension_semantics=("arbitrary",), vmem_limit_bytes=VMEM_LIMIT),
    )(u2, bd2, cd2, ar8, ai8)


def _s5_scan_bwd(dy, x, u2, bd2, cd2, ar8, ai8):
    rows, width = u2.shape
    ns = S5_STATES
    steps = S5_STEPS_BWD
    tr = S5_ROWS * steps
    nblk = rows // tr
    assert rows % tr == 0
    nt = (((1,), (1,)), ((), ()))
    tn = (((0,), (0,)), ((), ()))

    def body(dy_ref, x_ref, xp_ref, u_ref, bd_ref, cd_ref, ar_ref, ai_ref,
             du_ref, dbd_ref, dcd_ref, dar_ref, dai_ref, g_ref, st_ref):
        k = pl.program_id(0)

        @pl.when(k == 0)
        def _():
            st_ref[...] = jnp.zeros_like(st_ref)
            dbd_ref[...] = jnp.zeros_like(dbd_ref)
            dcd_ref[...] = jnp.zeros_like(dcd_ref)
            dar_ref[...] = jnp.zeros_like(dar_ref)
            dai_ref[...] = jnp.zeros_like(dai_ref)

        dyb = dy_ref[...].astype(BF16)
        g_ref[...] = lax.dot_general(dyb, cd_ref[...], nt, preferred_element_type=F32)
        ar, ai = ar_ref[...], ai_ref[...]

        def adjoint(r, carry, xpr, xpi):
            gr_n, gi_n, dar, dai = carry
            gr = g_ref[r, 0:ns] + ar * gr_n + ai * gi_n
            gi = g_ref[r, ns:2 * ns] - ai * gr_n + ar * gi_n
            g_ref[r, 0:ns] = gr
            g_ref[r, ns:2 * ns] = gi
            return gr, gi, dar + gr * xpr + gi * xpi, dai + gi * xpr - gr * xpi

        def step(i, carry):
            t = steps - 1 - i
            r = pl.ds(pl.multiple_of(t * S5_ROWS, S5_ROWS), S5_ROWS)
            rp = pl.ds(pl.multiple_of((t - 1) * S5_ROWS, S5_ROWS), S5_ROWS)
            return adjoint(r, carry, x_ref[rp, 0:ns], x_ref[rp, ns:2 * ns])

        zero = jnp.zeros((S5_ROWS, ns), F32)
        carry = lax.fori_loop(0, steps - 1, step, (st_ref[:, 0:ns], st_ref[:, ns:2 * ns], zero, zero), unroll=2)
        has_prev = (k < nblk - 1).astype(F32)
        gr, gi, dar, dai = adjoint(pl.ds(0, S5_ROWS), carry, xp_ref[:, 0:ns] * has_prev, xp_ref[:, ns:2 * ns] * has_prev)
        st_ref[:, 0:ns] = gr
        st_ref[:, ns:2 * ns] = gi
        dar_ref[...] += dar
        dai_ref[...] += dai
        gb = g_ref[...].astype(BF16)
        du_ref[...] = lax.dot_general(gb, bd_ref[...], nt, preferred_element_type=F32)
        dbd_ref[...] += lax.dot_general(u_ref[...].astype(BF16), gb, tn, preferred_element_type=F32)
        dcd_ref[...] += lax.dot_general(x_ref[...].astype(BF16), dyb, tn, preferred_element_type=F32)

    whole = lambda shape: pl.BlockSpec(shape, lambda k: (0, 0))
    rev = lambda k: (nblk - 1 - k, 0)
    prev = lambda k: (jnp.maximum((nblk - 1 - k) * steps - 1, 0), 0)
    return pl.pallas_call(
        body, name="s5_scan_bwd",
        out_shape=(jax.ShapeDtypeStruct((rows, width), F32), jax.ShapeDtypeStruct(bd2.shape, F32),
                   jax.ShapeDtypeStruct(cd2.shape, F32), jax.ShapeDtypeStruct(ar8.shape, F32),
                   jax.ShapeDtypeStruct(ai8.shape, F32)),
        grid=(nblk,),
        in_specs=[pl.BlockSpec((tr, width), rev), pl.BlockSpec((tr, 2 * ns), rev),
                  pl.BlockSpec((S5_ROWS, 2 * ns), prev), pl.BlockSpec((tr, width), rev),
                  whole(bd2.shape), whole(cd2.shape), whole(ar8.shape), whole(ai8.shape)],
        out_specs=(pl.BlockSpec((tr, width), rev), whole(bd2.shape), whole(cd2.shape), whole(ar8.shape),
                   whole(ai8.shape)),
        scratch_shapes=[pltpu.VMEM((tr, 2 * ns), F32), pltpu.VMEM((S5_ROWS, 2 * ns), F32)],
        compiler_params=pltpu.CompilerParams(dimension_semantics=("arbitrary",), vmem_limit_bytes=VMEM_LIMIT),
    )(dy, x, x, u2, bd2, cd2, ar8, ai8)


@jax.custom_vjp
def s5_core(u2, bd2, cd2, ar8, ai8):
    return _s5_scan_fwd(u2, bd2.astype(BF16), cd2.astype(BF16), ar8, ai8)[0]


def _s5_core_fwd(u2, bd2, cd2, ar8, ai8):
    bd2, cd2 = bd2.astype(BF16), cd2.astype(BF16)
    y, x = _s5_scan_fwd(u2, bd2, cd2, ar8, ai8)
    return y, (x, u2, bd2, cd2, ar8, ai8)


def _s5_core_bwd(res, dy):
    return _s5_scan_bwd(dy, *res)


s5_core.defvjp(_s5_core_fwd, _s5_core_bwd)


def s5_mixers(p_ctx, p_lat, lam_re, lam_im, log_step, b_re, b_im, c_re, c_im, d_skip, glu_w, glu_b):
    bsz = p_ctx.shape[0]
    assert 2 * bsz == S5_ROWS
    eye = jnp.eye(B_NGROUPS, dtype=F32)
    bds, cds, ars, ais = [], [], [], []
    for d in range(2):
        step = jnp.exp(log_step[d])[:, None]
        mag = jnp.exp(lam_re[d] * step)
        ar = mag * jnp.cos(lam_im[d] * step)
        ai = mag * jnp.sin(lam_im[d] * step)
        den = lam_re[d] * lam_re[d] + lam_im[d] * lam_im[d]
        nr = ar - 1.0
        kr = (nr * lam_re[d] + ai * lam_im[d]) / den
        ki = (ai * lam_re[d] - nr * lam_im[d]) / den
        br = kr[..., None] * b_re[d] - ki[..., None] * b_im[d]
        bi = kr[..., None] * b_im[d] + ki[..., None] * b_re[d]
        blk = lambda w: jnp.einsum('gnc,gh->gchn', w, eye).reshape(B_WIDTH, S5_STATES)
        bds.append(jnp.concatenate([blk(br), blk(bi)], axis=1))
        blk_c = lambda w: jnp.einsum('gcn,gh->gnhc', w, eye).reshape(S5_STATES, B_WIDTH)
        cds.append(jnp.concatenate([blk_c(c_re[d]), -blk_c(c_im[d])], axis=0))
        ars.append(jnp.broadcast_to(ar.reshape(1, S5_STATES), (bsz, S5_STATES)))
        ais.append(jnp.broadcast_to(ai.reshape(1, S5_STATES), (bsz, S5_STATES)))
    bd2 = jnp.concatenate(bds, axis=0)
    cd2 = jnp.concatenate(cds, axis=1)
    ar8 = jnp.concatenate(ars, axis=0)
    ai8 = jnp.concatenate(ais, axis=0)

    def rows_of(p):
        ut = jnp.swapaxes(p, 0, 1)
        z = jnp.zeros_like(ut)
        return jnp.concatenate([jnp.concatenate([ut, z], axis=-1), jnp.concatenate([z, _flip_time(ut, 0)], axis=-1)], axis=1)

    lc = p_ctx.shape[1]
    u2 = jnp.concatenate([rows_of(p_ctx), rows_of(p_lat)], axis=0)
    y2 = s5_core(u2.reshape(-1, 2 * B_WIDTH), bd2, cd2, ar8, ai8).reshape(u2.shape)

    def finish(y2p, p):
        y = y2p[:, :bsz, :B_WIDTH] + _flip_time(y2p[:, bsz:, B_WIDTH:], 0)
        y = jnp.swapaxes(y, 0, 1) + d_skip * p
        y = jax.nn.gelu(y)
        gate = mm(y.reshape(-1, B_WIDTH), glu_w).reshape(y.shape)
        return y * jax.nn.sigmoid(gate + glu_b)

    return finish(y2[:lc], p_ctx), finish(y2[lc:], p_lat)


GLA_CHUNK = 64
GLA_SUB = 16
NT_DIMS = (((1,), (1,)), ((), ()))
TN_DIMS = (((0,), (0,)), ((), ()))


def _bdot(a, b, dims=(((1,), (0,)), ((), ()))):
    return lax.dot_general(a.astype(BF16), b.astype(BF16), dims, preferred_element_type=F32)


def _hdot(a, b, dims=(((1,), (0,)), ((), ()))):
    ah, bh = a.astype(BF16), b.astype(BF16)
    al, bl = (a - ah.astype(F32)).astype(BF16), (b - bh.astype(F32)).astype(BF16)
    dot = functools.partial(lax.dot_general, dimension_numbers=dims, preferred_element_type=F32)
    return dot(ah, bh) + (dot(ah, bl) + dot(al, bh))


def _sub_block_ref(cum, rows, lo, hi, rev):
    n = cum.shape[0]
    if rev:
        return (cum[hi:hi + 1, :] if hi < n else jnp.zeros_like(cum[0:1, :])), rows >= lo
    return (cum[lo - 1:lo, :] if lo else jnp.zeros_like(cum[0:1, :])), rows < hi


def _chunk_of(step, nc, nc_ctx, rev):
    if not rev:
        return step
    return jnp.where(step < nc_ctx, nc_ctx - 1 - step, nc + nc_ctx - 1 - step)


def _gla_scores(q, k, cum, cumr, tri, rev):
    n = GLA_CHUNK
    if cumr is not None:
        decay = jnp.where(tri, jnp.exp(jnp.where(tri, cum - cumr, 0.0)), 0.0)
        return _bdot(q, k, NT_DIMS) * decay, decay
    rows = lax.broadcasted_iota(jnp.int32, (n, 1), 0)
    parts = []
    for i in range(n // GLA_SUB):
        lo, hi = i * GLA_SUB, (i + 1) * GLA_SUB
        ref, seen = _sub_block_ref(cum, rows, lo, hi, rev)
        qt = q[lo:hi] * jnp.exp(cum[lo:hi] - ref)
        kh = jnp.where(seen, k * jnp.exp(jnp.where(seen, ref - cum, 0.0)), 0.0)
        parts.append(_bdot(qt, kh, NT_DIMS))
    return jnp.where(tri, jnp.concatenate(parts, axis=0), 0.0), None


def _gla_fwd(q, k, cum, cumr, v, rev, nc_ctx):
    bsz, length, width = q.shape
    dk = GLA_CHUNK
    nh = width // dk
    nc = length // GLA_CHUNK
    scalar = cumr is not None

    def body(*refs):
        if scalar:
            q_ref, k_ref, cum_ref, cumr_ref, v_ref, o_ref, s_ref, st_ref = refs
        else:
            q_ref, k_ref, cum_ref, v_ref, o_ref, s_ref, st_ref = refs

        @pl.when(pl.program_id(1) == 0)
        def _():
            st_ref[...] = jnp.zeros_like(st_ref)

        ii = lax.broadcasted_iota(jnp.int32, (GLA_CHUNK, GLA_CHUNK), 0)
        jj = lax.broadcasted_iota(jnp.int32, (GLA_CHUNK, GLA_CHUNK), 1)
        tri = jj >= ii if rev else jj <= ii
        edge = 0 if rev else GLA_CHUNK - 1
        qa, ka, ca, va = q_ref[0], k_ref[0], cum_ref[0], v_ref[0]
        cra = cumr_ref[0] if scalar else None
        outs = []
        for h in range(nh):
            sl = slice(h * dk, (h + 1) * dk)
            qv, kv, cv, vv, st = qa[:, sl], ka[:, sl], ca[:, sl], va[:, sl], st_ref[h]
            s_ref[0, 0, h] = st
            a, _ = _gla_scores(qv, kv, cv, cra[:, sl] if scalar else None, tri, rev)
            outs.append(_bdot(qv * jnp.exp(cv), st, NT_DIMS) + _bdot(a, vv))
            last = cv[edge:edge + 1, :]
            st_ref[h] = st * jnp.exp(last) + _bdot(vv, kv * jnp.exp(last - cv), TN_DIMS)
        o_ref[0] = jnp.concatenate(outs, axis=1)

    seq = pl.BlockSpec((1, GLA_CHUNK, width), lambda n, c: (n, _chunk_of(c, nc, nc_ctx, rev), 0))
    state = pl.BlockSpec((1, 1, nh, dk, dk), lambda n, c: (n, _chunk_of(c, nc, nc_ctx, rev), 0, 0, 0))
    ins = [q, k, cum] + ([cumr] if scalar else []) + [v]
    return pl.pallas_call(
        body, name="gla_fwd_scalar" if scalar else "gla_fwd",
        out_shape=(jax.ShapeDtypeStruct((bsz, length, width), F32), jax.ShapeDtypeStruct((bsz, nc, nh, dk, dk), F32)),
        grid=(bsz, nc),
        in_specs=[seq] * len(ins),
        out_specs=(seq, state),
        scratch_shapes=[pltpu.VMEM((nh, dk, dk), F32)],
        compiler_params=pltpu.CompilerParams(dimension_semantics=("parallel", "arbitrary"),
                                             vmem_limit_bytes=VMEM_LIMIT),
    )(*ins)


def _gla_bwd(do, q, k, cum, cumr, v, states, rev, nc_ctx):
    bsz, length, width = q.shape
    nc = length // GLA_CHUNK
    scalar = cumr is not None
    n = GLA_CHUNK
    dk = GLA_CHUNK
    nh = width // dk

    def body(*refs):
        if scalar:
            do_ref, q_ref, k_ref, cum_ref, cumr_ref, v_ref, s_ref, dq_ref, dk_ref, dc_ref, dcr_ref, dv_ref, dst_ref = refs
        else:
            do_ref, q_ref, k_ref, cum_ref, v_ref, s_ref, dq_ref, dk_ref, dc_ref, dv_ref, dst_ref = refs

        @pl.when(pl.program_id(1) == 0)
        def _():
            dst_ref[...] = jnp.zeros_like(dst_ref)

        ii = lax.broadcasted_iota(jnp.int32, (n, n), 0)
        jj = lax.broadcasted_iota(jnp.int32, (n, n), 1)
        tri = jj >= ii if rev else jj <= ii
        edge = 0 if rev else n - 1
        rows = lax.broadcasted_iota(jnp.int32, (n, 1), 0)
        doa, qa, ka, ca, va = do_ref[0], q_ref[0], k_ref[0], cum_ref[0], v_ref[0]
        cra = cumr_ref[0] if scalar else None
        dqs, dks, dcs, dcrs, dvs = [], [], [], [], []
        for h in range(nh):
            sl = slice(h * dk, (h + 1) * dk)
            dov, qv, kv, cv, vv, st, dst = doa[:, sl], qa[:, sl], ka[:, sl], ca[:, sl], va[:, sl], s_ref[0, 0, h], dst_ref[h]
            e = jnp.exp(cv)
            qe = qv * e
            last = cv[edge:edge + 1, :]
            w = jnp.exp(last - cv)
            kw = kv * w
            el = jnp.exp(last)
            hd = _bdot if scalar else _hdot
            d_qe = hd(dov, st)
            d_kw = hd(vv, dst)
            dv = _bdot(kw, dst, NT_DIMS)
            d_last = jnp.sum(st * dst, axis=0, keepdims=True) * el + jnp.sum(d_kw * kw, axis=0, keepdims=True)
            dst_ref[h] = dst * el + _bdot(dov, qe, TN_DIMS)
            dq = d_qe * e
            dkk = d_kw * w
            dc = d_qe * qe - d_kw * kw + jnp.where(rows == edge, d_last, 0.0)
            da = jnp.where(tri, hd(dov, vv, NT_DIMS), 0.0)
            if scalar:
                a, decay = _gla_scores(qv, kv, cv, cra[:, sl], tri, rev)
                dg = da * decay
                dq = dq + _bdot(dg, kv)
                dkk = dkk + _bdot(dg, qv, TN_DIMS)
                p = da * a
                dc = dc + p
                dcrs.append(-p)
            else:
                a_parts, dq_parts = [], []
                for i in range(n // GLA_SUB):
                    lo, hi = i * GLA_SUB, (i + 1) * GLA_SUB
                    ref, seen = _sub_block_ref(cv, rows, lo, hi, rev)
                    eq = jnp.exp(cv[lo:hi] - ref)
                    qt = qv[lo:hi] * eq
                    ek = jnp.where(seen, jnp.exp(jnp.where(seen, ref - cv, 0.0)), 0.0)
                    kh = kv * ek
                    a_parts.append(_bdot(qt, kh, NT_DIMS))
                    dqt = _hdot(da[lo:hi], kh)
                    dkh = _hdot(da[lo:hi], qt, TN_DIMS)
                    dq_parts.append((dqt * eq, dqt * qt))
                    dkk = dkk + dkh * ek
                    dc = dc - dkh * kh
                a = jnp.where(tri, jnp.concatenate(a_parts, axis=0), 0.0)
                dq = dq + jnp.concatenate([p[0] for p in dq_parts], axis=0)
                dc = dc + jnp.concatenate([p[1] for p in dq_parts], axis=0)
            dvs.append(dv + _bdot(a, dov, TN_DIMS))
            dqs.append(dq)
            dks.append(dkk)
            dcs.append(dc)
        cat = functools.partial(jnp.concatenate, axis=1)
        dq_ref[0], dk_ref[0], dc_ref[0], dv_ref[0] = cat(dqs), cat(dks), cat(dcs), cat(dvs)
        if scalar:
            dcr_ref[0] = cat(dcrs)

    seq = pl.BlockSpec((1, n, width), lambda s, c: (s, _chunk_of(nc - 1 - c, nc, nc_ctx, rev), 0))
    state = pl.BlockSpec((1, 1, nh, dk, dk), lambda s, c: (s, _chunk_of(nc - 1 - c, nc, nc_ctx, rev), 0, 0, 0))
    ins = [do, q, k, cum] + ([cumr] if scalar else []) + [v]
    n_out = 5 if scalar else 4
    return pl.pallas_call(
        body, name="gla_bwd_scalar" if scalar else "gla_bwd",
        out_shape=(jax.ShapeDtypeStruct((bsz, length, width), F32),) * n_out,
        grid=(bsz, nc),
        in_specs=[seq] * len(ins) + [state],
        out_specs=(seq,) * n_out,
        scratch_shapes=[pltpu.VMEM((nh, dk, dk), F32)],
        compiler_params=pltpu.CompilerParams(dimension_semantics=("parallel", "arbitrary"),
                                             vmem_limit_bytes=VMEM_LIMIT),
    )(*ins, states)


@functools.partial(jax.custom_vjp, nondiff_argnums=(4, 5))
def gla(q, k, cum, v, rev, nc_ctx):
    return _gla_fwd(q, k, cum, None, v, rev, nc_ctx)[0]


def _gla_vjp_fwd(q, k, cum, v, rev, nc_ctx):
    o, states = _gla_fwd(q, k, cum, None, v, rev, nc_ctx)
    return o, (q, k, cum, v, states)


def _gla_vjp_bwd(rev, nc_ctx, res, do):
    q, k, cum, v, states = res
    return _gla_bwd(do, q, k, cum, None, v, states, rev, nc_ctx)


gla.defvjp(_gla_vjp_fwd, _gla_vjp_bwd)


@functools.partial(jax.custom_vjp, nondiff_argnums=(5, 6))
def gla_scalar(q, k, cum, cumr, v, rev, nc_ctx):
    return _gla_fwd(q, k, cum, cumr, v, rev, nc_ctx)[0]


def _gla_scalar_vjp_fwd(q, k, cum, cumr, v, rev, nc_ctx):
    o, states = _gla_fwd(q, k, cum, cumr, v, rev, nc_ctx)
    return o, (q, k, cum, cumr, v, states)


def _gla_scalar_vjp_bwd(rev, nc_ctx, res, do):
    q, k, cum, cumr, v, states = res
    return _gla_bwd(do, q, k, cum, cumr, v, states, rev, nc_ctx)


gla_scalar.defvjp(_gla_scalar_vjp_fwd, _gla_scalar_vjp_bwd)


def _chunk_cumsum(g, rev, axis=-2):
    axis = axis % g.ndim
    s = g.shape
    by_chunk = g.reshape(s[:axis] + (s[axis] // GLA_CHUNK, GLA_CHUNK) + s[axis + 1:])
    c = jnp.cumsum(by_chunk, axis=axis + 1)
    if rev:
        c = lax.slice_in_dim(c, GLA_CHUNK - 1, GLA_CHUNK, axis=axis + 1) - c + by_chunk
    return c.reshape(s)


def _both_parts(t_ctx, t_lat):
    return jnp.concatenate([t.reshape(t.shape[:2] + (-1,)) for t in (t_ctx, t_lat)], axis=1)


def _split_parts(o, lc, nh):
    return tuple(t.reshape(t.shape[:2] + (nh, -1)) for t in (o[:, :lc], o[:, lc:]))


def hgrn2_mixers(p_ctx, p_lat, lower, norm_w):
    bsz, lc = p_ctx.shape[:2]
    lower = lower.reshape(2, C_HEADS, C_KEY)

    def heads(p, lo, hi):
        return p[..., lo:hi].reshape(p.shape[:2] + (C_HEADS, -1))

    q_c, q_l = (jax.nn.silu(heads(p, 0, C_WIDTH)) for p in (p_ctx, p_lat))
    v_c, v_l = (heads(p, 3 * C_WIDTH, 4 * C_WIDTH) for p in (p_ctx, p_lat))
    q, v = _both_parts(q_c, q_l), _both_parts(v_c, v_l)
    nc_ctx = lc // GLA_CHUNK
    o = []
    for d in range(2):
        f_c, f_l = (lower[d] + (1.0 - lower[d]) * jax.nn.sigmoid(heads(p, (1 + d) * C_WIDTH, (2 + d) * C_WIDTH))
                    for p in (p_ctx, p_lat))
        cum = jnp.concatenate([_chunk_cumsum(jnp.log(f).reshape(f.shape[:2] + (-1,)), d, axis=1) for f in (f_c, f_l)], axis=1)
        o.append(gla(q, _both_parts(1.0 - f_c, 1.0 - f_l), cum, v, bool(d), nc_ctx))
    f_c, f_l = _split_parts(o[0], lc, C_HEADS)
    b_c, b_l = _split_parts(o[1], lc, C_HEADS)
    outs = []
    for o_sum, p in ((f_c + b_c, p_ctx), (f_l + b_l, p_lat)):
        o_n = rms_norm(o_sum) * norm_w.reshape(C_HEADS, C_VAL)
        outs.append(o_n.reshape(p.shape[:2] + (C_WIDTH,)) * jax.nn.silu(p[..., 4 * C_WIDTH:]))
    return tuple(outs)


def ssd_mixers(p_ctx, p_lat, conv_w, conv_b, dt_bias, a_log, d_skip, norm_w):
    bsz, lc = p_ctx.shape[:2]
    rep = A_HEADS // A_GROUPS
    a = -jnp.exp(a_log)
    xs, bs, cs, dts, zs = [], [], [], [], []
    for p in (p_ctx, p_lat):
        z, xbc, dt_raw = jnp.split(p, [A_INNER, A_INNER + A_CONV_DIM], axis=-1)
        xbc = jax.nn.silu(depthwise_conv(xbc, conv_w, conv_b))
        x_, b_, c_ = jnp.split(xbc, [A_INNER, A_INNER + A_GROUPS * A_STATE], axis=-1)
        shp = p.shape[:2]
        xs.append(x_.reshape(shp + (A_HEADS, A_HEAD_DIM)))
        bs.append(jnp.repeat(b_.reshape(shp + (A_GROUPS, A_STATE)), rep, axis=2))
        cs.append(jnp.repeat(c_.reshape(shp + (A_GROUPS, A_STATE)), rep, axis=2))
        dts.append([jax.nn.softplus(dt_raw[..., d * A_HEADS:(d + 1) * A_HEADS] + dt_bias[d]) for d in range(2)])
        zs.append(z)
    q, v = _both_parts(cs[0], cs[1]), _both_parts(xs[0], xs[1])
    nc_ctx = lc // GLA_CHUNK
    o = []
    for d in range(2):
        k = _both_parts(bs[0] * dts[0][d][..., None], bs[1] * dts[1][d][..., None])
        adt = jnp.concatenate([_chunk_cumsum(dt[d] * a[d], d, axis=1) for dt in dts], axis=1)
        nb, lt = adt.shape[:2]
        cum = jnp.broadcast_to(adt[..., None], (nb, lt, A_HEADS, A_STATE)).reshape(nb, lt, -1)
        along = jnp.swapaxes(adt.reshape(nb, lt // GLA_CHUNK, GLA_CHUNK, A_HEADS), 2, 3)[:, :, None]
        cumr = jnp.broadcast_to(along, (nb, lt // GLA_CHUNK, GLA_CHUNK, A_HEADS, GLA_CHUNK)).reshape(nb, lt, -1)
        o.append(gla_scalar(q, k, cum, cumr, v, bool(d), nc_ctx))
    f_c, f_l = _split_parts(o[0], lc, A_HEADS)
    b_c, b_l = _split_parts(o[1], lc, A_HEADS)
    outs = []
    for y, x_, z in ((f_c + b_c, xs[0], zs[0]), (f_l + b_l, xs[1], zs[1])):
        y = y + d_skip[:, None] * x_
        y = y.reshape(z.shape) * jax.nn.silu(z)
        outs.append(rms_norm(y) * norm_w)
    return tuple(outs)


def token_mixers(p_ctx, p_lat, W, l, lower):
    def cut(p):
        return p[..., :A_COLS], p[..., 1408:1408 + B_COLS], p[..., 1664:1664 + C_COLS]

    pa_c, pb_c, pc_c = cut(p_ctx)
    pa_l, pb_l, pc_l = cut(p_lat)
    ya_c, ya_l = ssd_mixers(pa_c, pa_l, W['a_conv_w'][l], W['a_conv_b'][l], W['a_dt_bias'][l], W['a_log'][l],
                            W['a_d'][l], W['a_norm_w'][l])
    yb_c, yb_l = s5_mixers(pb_c, pb_l, W['s5_lam_re'][l], W['s5_lam_im'][l], W['s5_log_step'][l], W['s5_b_re'][l],
                           W['s5_b_im'][l], W['s5_c_re'][l], W['s5_c_im'][l], W['s5_d'][l], W['s5_glu_w'][l],
                           W['s5_glu_b'][l])
    yc_c, yc_l = hgrn2_mixers(pc_c, pc_l, lower, W['hg_norm_w'][l])
    return (jnp.concatenate([ya_c, yb_c, yc_c], axis=-1), jnp.concatenate([ya_l, yb_l, yc_l], axis=-1))


def _pad_w_in(w):
    z = functools.partial(jnp.zeros, dtype=w.dtype)
    return jnp.concatenate([w[:, :A_COLS], z((D_MODEL, 1408 - A_COLS)), w[:, A_COLS:], z((D_MODEL, IN_PAD - 2944))],
                           axis=1)


def _mm3(t, w):
    g, tt, k = t.shape
    return mm(t.reshape(g * tt, k), w).reshape(g, tt, -1)


def _ffn(h, mg, first, w_in, w_out):
    hid = modmm(h, mg[:, first:first + 1], mg[:, first + 1:first + 2], w_in, BF16)
    return gated_add(h, swiglu_mm(hid, w_out), mg[:, first + 2:first + 3], 0.5)


def local_loss(x, W, m_lat, m_ctx, ctx, target):
    bsz, seq, dm = x.shape
    lc = ctx.shape[1]
    tg = bsz * lc
    assert seq % tg == 0
    gl = seq // tg
    ng = bsz * gl
    rows = seq // GRID_W
    p_lb = jax.nn.softmax(W['hg_lb_logits'], axis=0)
    lower_bounds = jnp.cumsum(p_lb, axis=0) - p_lb[:1]
    h = jnp.concatenate([x.reshape(ng, tg, dm), ctx.reshape(1, tg, dm)], axis=0)
    for l in range(DEPTH):
        last = l == DEPTH - 1
        col_major = l % 2 == 1
        mg = jnp.concatenate([jnp.repeat(m_lat[l], gl, axis=0), m_ctx[l][None]], axis=0)
        h = _ffn(h, mg, 0, W['ffn_w_in'][l][0], W['ffn_w_out'][l][0])
        hp = h
        if col_major:
            h_lat = raster_to_column(h[:ng].reshape(bsz, seq, dm), rows)
            hp = jnp.concatenate([h_lat.reshape(ng, tg, dm), h[ng:]], axis=0)
        p = modmm(hp, mg[:, 3:4], mg[:, 4:5], _pad_w_in(W['w_in'][l]), F32)
        mix_ctx, mix_lat = token_mixers(p[ng].reshape(bsz, lc, -1), p[:ng].reshape(bsz, seq, -1), W, l,
                                        lower_bounds[l])
        if last:
            h, mg = h[:ng], mg[:ng]
            y_lat = _mm3(mix_lat.reshape(ng, tg, dm), W['w_out'][l])
            y_ctx = None
        else:
            y = _mm3(jnp.concatenate([mix_lat.reshape(ng, tg, dm), mix_ctx.reshape(1, tg, dm)], axis=0), W['w_out'][l])
            y_lat, y_ctx = y[:ng], y[ng:]
        if col_major:
            y_lat = column_to_raster(y_lat.reshape(bsz, seq, dm), rows).reshape(ng, tg, dm)
        y = y_lat if y_ctx is None else jnp.concatenate([y_lat, y_ctx], axis=0)
        h = gated_add(h, y, mg[:, 5:6], 1.0)
        h = _ffn(h, mg, 6, W['ffn_w_in'][l][1], W['ffn_w_out'][l][1])
    y = rms_norm(h[:ng].reshape(bsz, seq, dm)) * W['final_norm_w']
    err = jnp.square(y - target)
    return 0.5 * jnp.sum(jnp.mean(err, axis=-1))


def _pad_rows(a, rows):
    return jnp.concatenate([a, jnp.zeros((rows - a.shape[0],) + a.shape[1:], a.dtype)], axis=0)


def kernel(x, c, ctx, c_ctx, mod_w, mod_b, ffn_w_in, ffn_w_out, w_in, w_out, a_conv_w, a_conv_b, a_dt_bias, a_log, a_d, a_norm_w, s5_lam_re, s5_lam_im, s5_log_step, s5_b_re, s5_b_im, s5_c_re, s5_c_im, s5_d, s5_glu_w, s5_glu_b, hg_lb_logits, hg_norm_w, final_norm_w, loss_target, m_c_ctx, m_mod_w, m_mod_b, m_ffn_w_in, m_ffn_w_out, m_w_in, m_w_out, m_a_conv_w, m_a_conv_b, m_a_dt_bias, m_a_log, m_a_d, m_a_norm_w, m_s5_lam_re, m_s5_lam_im, m_s5_log_step, m_s5_b_re, m_s5_b_im, m_s5_c_re, m_s5_c_im, m_s5_d, m_s5_glu_w, m_s5_glu_b, m_hg_lb_logits, m_hg_norm_w, m_final_norm_w, v_c_ctx, v_mod_w, v_mod_b, v_ffn_w_in, v_ffn_w_out, v_w_in, v_w_out, v_a_conv_w, v_a_conv_b, v_a_dt_bias, v_a_log, v_a_d, v_a_norm_w, v_s5_lam_re, v_s5_lam_im, v_s5_log_step, v_s5_b_re, v_s5_b_im, v_s5_c_re, v_s5_c_im, v_s5_d, v_s5_glu_w, v_s5_glu_b, v_hg_lb_logits, v_hg_norm_w, v_final_norm_w):
    given = dict(locals())
    w_loc = {n: given[n] for n in WEIGHTS}
    m_loc = {n: given["m_" + n] for n in WEIGHTS}
    v_loc = {n: given["v_" + n] for n in WEIGHTS}
    bsz = x.shape[0]
    me = 4 * lax.axis_index("x") + 2 * lax.axis_index("y") + lax.axis_index("c")

    small_sh = [c] + [w_loc[n] for n in SMALL_SHARDED]
    g1 = _unpack(all_gather([_pack(small_sh, 128, 8)], "gather_small")[0], [a.shape for a in small_sh])
    c_all = g1[0].reshape(N_DEV * bsz, D_MODEL)
    gathered = dict(zip(BIG, all_gather([w_loc[n].astype(BF16) for n in BIG], "gather_weights")))
    W = {'ffn_w_in': [[_assemble(gathered['ffn_w_in'][:, l, i], 1) for i in range(2)] for l in range(DEPTH)],
         'ffn_w_out': [[_assemble(gathered['ffn_w_out'][:, l, i], 0) for i in range(2)] for l in range(DEPTH)],
         'w_in': [_assemble(gathered['w_in'][:, l], 1) for l in range(DEPTH)],
         'w_out': [_assemble(gathered['w_out'][:, l], 0) for l in range(DEPTH)]}
    for (n, ax), t in zip(SMALL_SHARDED.items(), g1[1:]):
        W[n] = _assemble(t, ax)
    for n in SMALL:
        if n not in SMALL_SHARDED and n not in ('c_ctx', 'mod_b'):
            W[n] = w_loc[n]

    n_rows = N_DEV * bsz + 1
    pad_rows = 8 * ((n_rows + 7) // 8)
    c_rows = _pad_rows(jnp.concatenate([c_all, c_ctx[None]], axis=0), pad_rows)
    sc = jax.nn.silu(c_rows)
    mods_sh = jnp.stack([matmul(sc, mod_w[l], name="mod_fwd") for l in range(DEPTH)])
    mods = _assemble(all_gather([mods_sh], "gather_mods")[0], 2) + mod_b[:, None, :]
    m_lat = lax.dynamic_slice_in_dim(mods, me * bsz, bsz, axis=1).reshape(DEPTH, bsz, N_MOD, D_MODEL)
    m_ctx = mods[:, n_rows - 1].reshape(DEPTH, N_MOD, D_MODEL)

    loss_loc, (grad_x, gW, gm_lat, gm_ctx) = jax.value_and_grad(local_loss, argnums=(0, 1, 2, 3))(
        x, W, m_lat, m_ctx, ctx, loss_target)

    dm_loc = jnp.concatenate([gm_lat.reshape(DEPTH, bsz, -1), gm_ctx.reshape(DEPTH, 1, -1)], axis=1)
    (dm_all,) = all_gather([dm_loc], "gather_dmods")
    dm_ex = jnp.moveaxis(dm_all[:, :, :bsz], 0, 1).reshape(DEPTH, N_DEV * bsz, -1)
    ncol = N_MOD * D_MODEL
    dm_cx = sum_leading(dm_all[:, :, bsz].reshape(N_DEV, DEPTH * ncol // 128, 128), "sum_dmods_ctx")
    dm_cx = dm_cx.reshape(DEPTH, 1, ncol)
    dm_rows = jnp.concatenate([dm_ex, dm_cx, jnp.zeros((DEPTH, pad_rows - n_rows, ncol), F32)], axis=1)
    grad_mod_b = sum_leading(jnp.moveaxis(dm_rows, 1, 0).reshape(pad_rows, DEPTH * ncol // 128, 128),
                             "sum_mod_b").reshape(DEPTH, ncol)
    my_cols = ncol // N_DEV
    dm_mine = lax.dynamic_slice_in_dim(dm_rows, me * my_cols, my_cols, axis=2)
    grad_mod_w = jnp.stack([matmul(sc, dm_mine[l], ta=True, name="mod_dw") for l in range(DEPTH)])
    dm_cx_mine = lax.dynamic_slice_in_dim(dm_cx, me * my_cols, my_cols, axis=2)
    g_sc_ctx = sum(matmul(_pad_rows(dm_cx_mine[l], 8), mod_w[l], tb=True, name="mod_dc")[0] for l in range(DEPTH))

    small_full = [n for n in SMALL if n not in ('c_ctx', 'mod_b')]
    part = [gW[n] for n in small_full] + [g_sc_ctx, loss_loc.reshape(1)]
    red = sum_leading(all_gather([_pack(part, 128, ROW_TILE)], "gather_small_grads")[0], "sum_small_grads")
    red = _unpack(red, [a.shape for a in part])
    grads = dict(zip(small_full, red[:-2]))
    loss = red[-1].reshape(())
    sig = jax.nn.sigmoid(c_ctx)
    grads['c_ctx'] = red[-2] * (sig * (1.0 + c_ctx * (1.0 - sig)))
    grads['mod_b'] = grad_mod_b
    for n, ax in SMALL_SHARDED.items():
        size = w_loc[n].shape[ax]
        grads[n] = lax.dynamic_slice_in_dim(grads[n], me * size, size, axis=ax)
    grads['mod_w'] = grad_mod_w

    by_dev = {'ffn_w_in': jnp.stack([jnp.stack([_split(g, 1) for g in gl], axis=1) for gl in gW['ffn_w_in']], axis=1),
              'ffn_w_out': jnp.stack([jnp.stack([_split(g, 0) for g in gl], axis=1) for gl in gW['ffn_w_out']], axis=1),
              'w_in': jnp.stack([_split(g, 1) for g in gW['w_in']], axis=1),
              'w_out': jnp.stack([_split(g, 0) for g in gW['w_out']], axis=1)}
    ac = lax.axis_index("c")
    sends = [by_dev[n] for n in BIG]
    from_sibling = pair_exchange(sends, "exchange_grads_pair")
    pair = []
    for g, r in zip(sends, from_sibling):
        mine = lax.dynamic_index_in_dim(g.reshape((4, 2) + g.shape[1:]), ac, axis=1, keepdims=False)
        shape3 = (4,) + _as_2d(g.shape[1:])
        pair.append(add_pairs(mine.reshape(shape3), r.reshape(shape3), "sum_grads_pair").reshape(r.shape))
    for n, t in zip(BIG, chip_exchange(pair, "exchange_grads_chips")):
        grads[n] = sum_leading(t.reshape((4,) + _as_2d(t.shape[1:])), "sum_grads").reshape(t.shape[1:])

    delta, new_m, new_v = {}, {}, {}

    for n in list(BIG) + ['mod_w']:
        outs = adamw(*[d[n].reshape(_as_2d(d[n].shape)) for d in (w_loc, grads, m_loc, v_loc)], name="adamw_" + n)
        delta[n], new_m[n], new_v[n] = (o.reshape(w_loc[n].shape) for o in outs)

    def update(names, width, row_mult, tag):
        packed = [_pack([d[n] for n in names], width, row_mult) for d in (w_loc, grads, m_loc, v_loc)]
        outs = adamw(*packed, name="adamw_" + tag)
        shapes = [w_loc[n].shape for n in names]
        for res, out in zip((delta, new_m, new_v), outs):
            res.update(zip(names, _unpack(out, shapes)))

    update(SMALL, 128, 256, "small")
    return (loss, grad_x, *[grads[n] for n in WEIGHTS], *[delta[n] for n in WEIGHTS],
            *[new_m[n] for n in WEIGHTS], *[new_v[n] for n in WEIGHTS])
```
